```python
import jax, jax.numpy as jnp
from jax import lax
import numpy as np

D_MODEL = 1024
BATCH = 8
SEQ = 2048
DEPTH = 1
DEC_BATCH = 128
DEC_SEQ = 1
PAST_LEN = 16384
PAGE_SIZE = 128

N_META = 16
MIX_WIDTH = D_MODEL
HGRN_WIDTH = MIX_WIDTH // 2
HGRN_HEADS = HGRN_WIDTH // 128
HGRN_DK = HGRN_WIDTH // HGRN_HEADS
HGRN_DV = HGRN_WIDTH // HGRN_HEADS
HGRN_CHUNK = 64
POOL_WIDTH = MIX_WIDTH - HGRN_WIDTH
POOL_WINDOWS = (2, 4, 8, 16)
POOL_GROUPS = len(POOL_WINDOWS)
POOL_GROUP_DIM = POOL_WIDTH // POOL_GROUPS
POOL_BUF = max(POOL_WINDOWS) - 1
IN_WIDTH = 4 * HGRN_WIDTH + POOL_WIDTH
N_EXPERTS = 256
TOP_K = 8
D_EXPERT = D_MODEL // 4
D_SHARED = D_EXPERT
ROUTED_SCALE = 2.5
MOE_BLOCK = 64
ALPHA = (2 * DEPTH) ** 0.25
BETA = (8 * DEPTH) ** -0.25
LN_EPS = 1e-5
RMS_EPS = 1e-6

kernel_name = 'hymba_hgrn2_pool_moe_deepnorm_step'


def _layernorm(x, g, b):
    xf = x.astype(jnp.float32)
    mu = xf.mean(-1, keepdims=True)
    var = jnp.square(xf - mu).mean(-1, keepdims=True)
    return ((xf - mu) * lax.rsqrt(var + LN_EPS) * g + b).astype(x.dtype)


def _gla_chunk(S, q, k, v, logf):
    C = q.shape[1]
    b = jnp.cumsum(logf, axis=1)
    o_inter = jnp.einsum('bthk,bhkv->bthv', q * jnp.exp(b), S)
    causal = jnp.tril(jnp.ones((C, C), bool))[None, :, :, None, None]
    diff = b[:, :, None] - b[:, None, :]
    decay = jnp.where(causal, jnp.exp(jnp.where(causal, diff, 0.0)), 0.0)
    scores = jnp.einsum('bthk,bshk,btshk->bhts', q, k, decay)
    o = o_inter + jnp.einsum('bhts,bshv->bthv', scores, v)
    b_last = b[:, -1]
    S_new = jnp.exp(b_last)[..., None] * S + jnp.einsum(
        'bshk,bshv->bhkv', k * jnp.exp(b_last[:, None] - b), v)
    return S_new, o


def _hgrn_prompt(q, k, v, logf):
    B = q.shape[0]
    S0 = jnp.zeros((B, HGRN_HEADS, HGRN_DK, HGRN_DV), jnp.float32)
    S1, o_meta = _gla_chunk(S0, q[:, :N_META], k[:, :N_META], v[:, :N_META], logf[:, :N_META])

    def to_chunks(a):
        rest = a[:, N_META:]
        nc = rest.shape[1] // HGRN_CHUNK
        return rest.reshape(B, nc, HGRN_CHUNK, *a.shape[2:]).swapaxes(0, 1)

    def step(S, xs):
        return _gla_chunk(S, *xs)

    S_fin, o_rest = lax.scan(step, S1, (to_chunks(q), to_chunks(k), to_chunks(v), to_chunks(logf)))
    o_rest = o_rest.swapaxes(0, 1).reshape(B, -1, HGRN_HEADS, HGRN_DV)
    return S_fin, jnp.concatenate([o_meta, o_rest], axis=1)


def _pool(p_ext, n_prev, pos0):
    L_ext = p_ext.shape[1]
    pf = p_ext.astype(jnp.float32)
    csum = jnp.concatenate([jnp.zeros_like(pf[:, :1]), jnp.cumsum(pf, axis=1)], axis=1)
    rows = jnp.arange(n_prev, L_ext)
    hi = rows + 1
    outs = []
    for gi, w in enumerate(POOL_WINDOWS):
        sl = slice(gi * POOL_GROUP_DIM, (gi + 1) * POOL_GROUP_DIM)
        cs = csum[..., sl]
        cnt = jnp.minimum(w, pos0 + rows + 1)
        lo = hi - cnt
        mean = (cs[:, hi] - cs[:, lo]) / cnt[:, None].astype(jnp.float32)
        outs.append(mean - pf[:, n_prev:, sl])
    return jnp.concatenate(outs, axis=-1)


def _token_mixer(h, S_prev, buf_prev, lb, w_in, hgrn_norm_g, w_pool, pool_scale, w_out):
    B, L, _ = h.shape
    proj = h @ w_in
    q, f, i, g, p = jnp.split(proj, [HGRN_WIDTH, 2 * HGRN_WIDTH, 3 * HGRN_WIDTH, 4 * HGRN_WIDTH], axis=-1)
    zf = f.astype(jnp.float32)
    logf = jnp.log(lb + (1.0 - lb) * jax.nn.sigmoid(zf))
    kk = (1.0 - lb) * jax.nn.sigmoid(-zf)
    qa = jax.nn.silu(q.astype(jnp.float32))
    heads = lambda a: a.reshape(B, L, HGRN_HEADS, -1)
    if S_prev is None:
        S_new, o = _hgrn_prompt(heads(qa), heads(kk), heads(i.astype(jnp.float32)), heads(logf))
    else:
        S_new, o = _gla_chunk(S_prev.astype(jnp.float32), heads(qa), heads(kk),
                              heads(i.astype(jnp.float32)), heads(logf))
    o = o * lax.rsqrt(jnp.mean(jnp.square(o), -1, keepdims=True) + RMS_EPS)
    o_a = o.reshape(B, L, HGRN_WIDTH) * hgrn_norm_g * jax.nn.silu(g.astype(jnp.float32))
    if buf_prev is None:
        p_ext, n_prev, pos0 = p, 0, 0
    else:
        p_ext, n_prev, pos0 = jnp.concatenate([buf_prev.astype(p.dtype), p], axis=1), POOL_BUF, PAST_LEN - POOL_BUF
    pooled = _pool(p_ext, n_prev, pos0).reshape(B, L, POOL_GROUPS, POOL_GROUP_DIM)
    o_b = jnp.einsum('btgc,gcd->btgd', pooled, w_pool).reshape(B, L, POOL_WIDTH) * pool_scale
    mix = jnp.concatenate([o_a.astype(h.dtype), o_b.astype(h.dtype)], axis=-1) @ w_out
    return mix, S_new, p_ext[:, -POOL_BUF:]


def _moe(x, w_router, b_router, w_gate_e, w_up_e, w_down_e, w_gate_s, w_up_s, w_down_s):
    lead = x.shape[:-1]
    x2 = x.reshape(-1, D_MODEL)
    T = x2.shape[0]
    shared = (jax.nn.silu(x2 @ w_gate_s) * (x2 @ w_up_s)) @ w_down_s
    scores = jax.nn.sigmoid((x2 @ w_router).astype(jnp.float32))
    _, idx = lax.top_k(scores + b_router.astype(jnp.float32), TOP_K)
    s_sel = jnp.take_along_axis(scores, idx, axis=1)
    gates = s_sel / jnp.sum(s_sel, -1, keepdims=True) * ROUTED_SCALE
    N = T * TOP_K
    e_flat = idx.reshape(-1).astype(jnp.int32)
    tok_flat = jnp.repeat(jnp.arange(T, dtype=jnp.int32), TOP_K)
    g_flat = gates.reshape(-1)
    order = jnp.argsort(e_flat)
    e_sorted = e_flat[order]
    counts = jnp.bincount(e_flat, length=N_EXPERTS)
    padded = (counts + MOE_BLOCK - 1) // MOE_BLOCK * MOE_BLOCK
    pend = jnp.cumsum(padded)
    pstart = pend - padded
    cstart = jnp.cumsum(counts) - counts
    slot = pstart[e_sorted] + (jnp.arange(N) - cstart[e_sorted])
    n_blocks = (N + N_EXPERTS * (MOE_BLOCK - 1) + MOE_BLOCK - 1) // MOE_BLOCK
    n_slots = n_blocks * MOE_BLOCK
    slot_tok = jnp.zeros((n_slots,), jnp.int32).at[slot].set(tok_flat[order])
    slot_gate = jnp.zeros((n_slots,), jnp.float32).at[slot].set(g_flat[order])
    block_exp = jnp.minimum(jnp.searchsorted(pend, jnp.arange(n_blocks) * MOE_BLOCK, side='right'),
                            N_EXPERTS - 1).astype(jnp.int32)

    def step(acc, blk):
        tok, gate, e = blk
        xb = x2[tok]
        hb = jax.nn.silu(xb @ w_gate_e[e]) * (xb @ w_up_e[e])
        yb = (hb @ w_down_e[e]).astype(jnp.float32) * gate[:, None]
        return acc.at[tok].add(yb), None

    routed, _ = lax.scan(step, jnp.zeros((T, D_MODEL), jnp.float32),
                         (slot_tok.reshape(n_blocks, MOE_BLOCK), slot_gate.reshape(n_blocks, MOE_BLOCK), block_exp))
    return (shared.astype(jnp.float32) + routed).astype(x.dtype).reshape(*lead, D_MODEL)


def setup_inputs(seed: int = 0) -> dict:
    key = jax.random.key(seed)
    ks = jax.random.split(key, 28)
    D = D_MODEL

    def n(k, shape, s):
        return jax.random.normal(k, shape, jnp.float32) * s

    col_scale = jnp.concatenate([
        jnp.ones((2 * HGRN_WIDTH,), jnp.float32),
        jnp.full((HGRN_WIDTH,), BETA, jnp.float32),
        jnp.ones((HGRN_WIDTH,), jnp.float32),
        jnp.full((POOL_WIDTH,), BETA, jnp.float32)])
    return {
        'x_prompt': n(ks[0], (BATCH, SEQ, D), 1.0),
        'x_sample': n(ks[1], (DEC_BATCH, DEC_SEQ, D), 1.0),
        'state_hgrn': n(ks[2], (DEPTH, DEC_BATCH, HGRN_HEADS, HGRN_DK, HGRN_DV), 0.3),
        'state_pool': n(ks[3], (DEPTH, DEC_BATCH, POOL_BUF, POOL_WIDTH), 1.0),
        'meta_tokens': n(ks[4], (N_META, D), 1.0),
        'ln_emb_g': 1.0 + n(ks[5], (D,), 0.02),
        'ln_emb_b': n(ks[6], (D,), 0.02),
        'w_in': n(ks[7], (DEPTH, D, IN_WIDTH), D ** -0.5) * col_scale,
        'lb_logits': n(ks[8], (DEPTH + 1, HGRN_WIDTH), 0.5),
        'hgrn_norm_g': 1.0 + n(ks[9], (DEPTH, HGRN_WIDTH), 0.02),
        'w_pool': n(ks[10], (DEPTH, POOL_GROUPS, POOL_GROUP_DIM, POOL_GROUP_DIM), BETA * POOL_GROUP_DIM ** -0.5),
        'pool_scale': 1.0 + n(ks[11], (DEPTH, POOL_WIDTH), 0.02),
        'w_out': n(ks[12], (DEPTH, MIX_WIDTH, D), BETA * MIX_WIDTH ** -0.5),
        'ln1_g': 1.0 + n(ks[13], (DEPTH, D), 0.02),
        'ln1_b': n(ks[14], (DEPTH, D), 0.02),
        'w_router': n(ks[15], (DEPTH, D, N_EXPERTS), D ** -0.5),
        'b_router': n(ks[16], (DEPTH, N_EXPERTS), 0.01),
        'w_gate_e': n(ks[17], (DEPTH, N_EXPERTS, D, D_EXPERT), D ** -0.5),
        'w_up_e': n(ks[18], (DEPTH, N_EXPERTS, D, D_EXPERT), BETA * D ** -0.5),
        'w_down_e': n(ks[19], (DEPTH, N_EXPERTS, D_EXPERT, D), BETA * D_EXPERT ** -0.5),
        'w_gate_s': n(ks[20], (DEPTH, D, D_SHARED), D ** -0.5),
        'w_up_s': n(ks[21], (DEPTH, D, D_SHARED), BETA * D ** -0.5),
        'w_down_s': n(ks[22], (DEPTH, D_SHARED, D), BETA * D_SHARED ** -0.5),
        'ln2_g': 1.0 + n(ks[23], (DEPTH, D), 0.02),
        'ln2_b': n(ks[24], (DEPTH, D), 0.02),
    }


def reference(x_prompt, x_sample, state_hgrn, state_pool, meta_tokens, ln_emb_g, ln_emb_b,
              w_in, lb_logits, hgrn_norm_g, w_pool, pool_scale, w_out, ln1_g, ln1_b,
              w_router, b_router, w_gate_e, w_up_e, w_down_e, w_gate_s, w_up_s, w_down_s,
              ln2_g, ln2_b):
    lb_all = jnp.cumsum(jax.nn.softmax(lb_logits.astype(jnp.float32), axis=0), axis=0)
    B = x_prompt.shape[0]
    meta = jnp.broadcast_to(meta_tokens[None].astype(x_prompt.dtype), (B, N_META, D_MODEL))
    xp = _layernorm(jnp.concatenate([meta, x_prompt], axis=1), ln_emb_g, ln_emb_b)
    xs = _layernorm(x_sample, ln_emb_g, ln_emb_b)
    hp, pp, hs, ps = [], [], [], []
    for l in range(DEPTH):
        mix_w = (lb_all[l], w_in[l], hgrn_norm_g[l], w_pool[l], pool_scale[l], w_out[l])
        mix_p, S_p, buf_p = _token_mixer(xp, None, None, *mix_w)
        mix_s, S_s, buf_s = _token_mixer(xs, state_hgrn[l], state_pool[l], *mix_w)
        hp.append(S_p.astype(x_prompt.dtype))
        pp.append(buf_p.astype(x_prompt.dtype))
        hs.append(S_s.astype(state_hgrn.dtype))
        ps.append(buf_s.astype(state_pool.dtype))
        xp = _layernorm(ALPHA * xp + mix_p, ln1_g[l], ln1_b[l])
        xs = _layernorm(ALPHA * xs + mix_s, ln1_g[l], ln1_b[l])
        moe_w = (w_router[l], b_router[l], w_gate_e[l], w_up_e[l], w_down_e[l],
                 w_gate_s[l], w_up_s[l], w_down_s[l])
        xp = _layernorm(ALPHA * xp + _moe(xp, *moe_w), ln2_g[l], ln2_b[l])
        xs = _layernorm(ALPHA * xs + _moe(xs, *moe_w), ln2_g[l], ln2_b[l])
    y_prompt = xp[:, N_META:]
    y_sample = xs
    state_hgrn_prompt = jnp.stack(hp)
    state_pool_prompt = jnp.stack(pp)
    state_hgrn_sample = jnp.stack(hs)
    state_pool_sample = jnp.stack(ps)
    return (y_prompt, y_sample, state_hgrn_prompt, state_pool_prompt, state_hgrn_sample, state_pool_sample)
```

```python
import functools

import jax
import jax.numpy as jnp
from jax import lax
from jax.experimental import pallas as pl
from jax.experimental.pallas import tpu as pltpu

F32 = jnp.float32
BF16 = jnp.bfloat16
I32 = jnp.int32

D_MODEL = 1024
N_META = 16
HGRN_WIDTH = 512
HGRN_HEADS = 4
HEAD_DIM = 128
POOL_WIDTH = 512
POOL_WINDOWS = (2, 4, 8, 16)
POOL_GROUP_DIM = 128
POOL_BUF = 15
IN_WIDTH = 4 * HGRN_WIDTH + POOL_WIDTH
N_EXPERTS = 256
TOP_K = 8
D_EXPERT = 256
ROUTED_SCALE = 2.5
DEPTH = 1
ALPHA = (2 * DEPTH) ** 0.25
LN_EPS = 1e-5
RMS_EPS = 1e-6

SUB = 16
MIX_BLOCK = 128
MOE_ROWS = 256
TILE_STRIDE = MOE_ROWS + 8
VMEM_LIMIT = 48 * 1024 * 1024


def _ln(x, g, b):
    mu = jnp.mean(x, axis=-1, keepdims=True)
    xc = x - mu
    var = jnp.mean(xc * xc, axis=-1, keepdims=True)
    return xc * lax.rsqrt(var + LN_EPS) * g + b


def _sigmoid(z):
    return 1.0 / (1.0 + jnp.exp(-z))


def _bdot(a, b):
    return jnp.dot(a, b, preferred_element_type=F32)


def _ln_proj_kernel(x_ref, g_ref, b_ref, w_ref, lbl_ref, ng_ref,
                    xn_ref, q_ref, k_ref, gl_ref, v_ref, gs_ref, p_ref):
    xn = _ln(x_ref[...], g_ref[...], b_ref[...])
    xn_ref[...] = xn
    proj = _bdot(xn.astype(BF16), w_ref[...])
    lbl = lbl_ref[...]
    e = jnp.exp(lbl - jnp.max(lbl, axis=0, keepdims=True))
    lb = e[0:1] / jnp.sum(e, axis=0, keepdims=True)
    W = HGRN_WIDTH
    q = proj[:, 0:W]
    f = proj[:, W:2 * W]
    q_ref[...] = q * _sigmoid(q)
    k_ref[...] = (1.0 - lb) * _sigmoid(-f)
    gl_ref[...] = jnp.log(lb + (1.0 - lb) * _sigmoid(f))
    v_ref[...] = proj[:, 2 * W:3 * W]
    g = proj[:, 3 * W:4 * W]
    gs_ref[...] = ng_ref[...] * (g * _sigmoid(g))
    p_ref[...] = proj[:, 4 * W:]


def _row_tile(n_rows):
    for tm in (256, 128, 64, 32, 16, 8):
        if n_rows % tm == 0:
            return tm
    raise ValueError(f"row count {n_rows} is not a multiple of 8")


def _ln_proj(x, ln_g, ln_b, w_in_bf, lb_logits, norm_g):
    T = x.shape[0]
    tm = _row_tile(T)
    row = lambda i: (i, 0)
    const = lambda i: (0, 0)
    outs = [jax.ShapeDtypeStruct((T, D_MODEL), F32)] + [jax.ShapeDtypeStruct((T, HGRN_WIDTH), F32)] * 6
    return pl.pallas_call(
        _ln_proj_kernel,
        out_shape=outs,
        grid=(T // tm,),
        in_specs=[
            pl.BlockSpec((tm, D_MODEL), row),
            pl.BlockSpec((1, D_MODEL), const),
            pl.BlockSpec((1, D_MODEL), const),
            pl.BlockSpec((D_MODEL, IN_WIDTH), const),
            pl.BlockSpec((DEPTH + 1, HGRN_WIDTH), const),
            pl.BlockSpec((1, HGRN_WIDTH), const),
        ],
        out_specs=[pl.BlockSpec((tm, D_MODEL), row)] + [pl.BlockSpec((tm, HGRN_WIDTH), row)] * 6,
        compiler_params=pltpu.CompilerParams(dimension_semantics=("arbitrary",), vmem_limit_bytes=VMEM_LIMIT),
        name="ln_proj",
    )(x, ln_g, ln_b, w_in_bf, lb_logits, norm_g)


def _pool_group(pe, p_cur, gi, w):
    sl = slice(gi * POOL_GROUP_DIM, (gi + 1) * POOL_GROUP_DIM)
    s = pe[:, sl]
    sh = 1
    while sh < w:
        s = s + pltpu.roll(s, sh, 0)
        sh *= 2
    return s[SUB:, :] * (1.0 / w) - p_cur[:, sl]


def _mixer_kernel(q_ref, k_ref, g_ref, v_ref, gs_ref, p_ref, s0_ref, pp0_ref, wpool_ref, pscale_ref,
                  mix_ref, sfin_ref, st_scr, pe_scr):
    i = pl.program_id(1)
    nblk = pl.num_programs(1)
    n = MIX_BLOCK

    @pl.when(i == 0)
    def _():
        for h in range(HGRN_HEADS):
            st_scr[h] = s0_ref[h].T
        pe_scr[0:SUB, :] = pp0_ref[...]

    rows = lax.broadcasted_iota(I32, (n, HEAD_DIM), 0)
    r16 = rows & (SUB - 1)
    t8 = lax.broadcasted_iota(I32, (8, HEAD_DIM), 0)
    zero_bf = jnp.zeros((SUB, HEAD_DIM), BF16)

    o_heads = []
    for h in range(HGRN_HEADS):
        hs = slice(h * HEAD_DIM, (h + 1) * HEAD_DIM)
        Q = q_ref[:, hs]
        K = k_ref[:, hs]
        G = g_ref[:, hs]
        V = v_ref[:, hs]
        bf = G
        br = G
        for sh in (1, 2, 4, 8):
            bf = bf + jnp.where(r16 >= sh, pltpu.roll(bf, sh, 0), 0.0)
            br = br + jnp.where(r16 < SUB - sh, pltpu.roll(br, n - sh, 0), 0.0)
        br = br - G
        qt = (Q * jnp.exp(bf)).astype(BF16)
        kt = (K * jnp.exp(br)).astype(BF16)
        vt = V.T.astype(BF16)
        st = st_scr[h]
        o_parts = []
        for c in range(n // SUB):
            r0 = c * SUB
            b_top, b_bot = bf[r0:r0 + 8], bf[r0 + 8:r0 + 16]
            q_top, q_bot = Q[r0:r0 + 8], Q[r0 + 8:r0 + 16]
            acc_top = jnp.zeros((8, HEAD_DIM), F32)
            acc_bot = jnp.zeros((8, HEAD_DIM), F32)
            for s in range(SUB):
                bs = bf[r0 + s:r0 + s + 1]
                ks = K[r0 + s:r0 + s + 1]
                vs = V[r0 + s:r0 + s + 1]
                if s < 8:
                    col = jnp.sum(q_top * jnp.exp(b_top - bs) * ks, axis=-1, keepdims=True)
                    col = jnp.where(t8[:, 0:1] >= s, col, 0.0)
                    acc_top = acc_top + col * vs
                    col = jnp.sum(q_bot * jnp.exp(b_bot - bs) * ks, axis=-1, keepdims=True)
                    acc_bot = acc_bot + col * vs
                else:
                    col = jnp.sum(q_bot * jnp.exp(b_bot - bs) * ks, axis=-1, keepdims=True)
                    col = jnp.where(t8[:, 0:1] + 8 >= s, col, 0.0)
                    acc_bot = acc_bot + col * vs
            o_diag = jnp.concatenate([acc_top, acc_bot], axis=0)
            o_inter = lax.dot_general(qt[r0:r0 + SUB], st.astype(BF16), (((1,), (1,)), ((), ())),
                                      preferred_element_type=F32)
            o_parts.append(o_inter + o_diag)
            kmask = jnp.concatenate([zero_bf] * c + [kt[r0:r0 + SUB]] + [zero_bf] * (n // SUB - 1 - c), axis=0)
            d_st = _bdot(vt, kmask)
            st = st * jnp.exp(bf[r0 + SUB - 1:r0 + SUB]) + d_st
        st_scr[h] = st
        o = jnp.concatenate(o_parts, axis=0)
        o = o * lax.rsqrt(jnp.mean(o * o, axis=-1, keepdims=True) + RMS_EPS)
        o_heads.append(o * gs_ref[:, hs])

    p_cur = p_ref[...]
    pe_scr[SUB:SUB + n, :] = p_cur
    pe = pe_scr[...]
    ob = []
    for gi, w in enumerate(POOL_WINDOWS):
        pooled = _pool_group(pe, p_cur, gi, w)
        sl = slice(gi * POOL_GROUP_DIM, (gi + 1) * POOL_GROUP_DIM)
        ob.append(_bdot(pooled.astype(BF16), wpool_ref[gi]) * pscale_ref[:, sl])
    pe_scr[0:SUB, :] = p_cur[n - SUB:, :]

    mix_ref[...] = jnp.concatenate(o_heads + ob, axis=1).astype(BF16)

    @pl.when(i == nblk - 1)
    def _():
        for h in range(HGRN_HEADS):
            sfin_ref[0, h] = st_scr[h].T


def _mixer(q, k, g, v, gs, p, s0, pp0, wpool_bf, pscale, nseq, seqlen):
    nblk = seqlen // MIX_BLOCK
    tok = lambda b, i: (b * nblk + i, 0)
    tspec = pl.BlockSpec((MIX_BLOCK, HGRN_WIDTH), tok)
    return pl.pallas_call(
        _mixer_kernel,
        out_shape=[jax.ShapeDtypeStruct((nseq * seqlen, D_MODEL), BF16),
                   jax.ShapeDtypeStruct((nseq, HGRN_HEADS, HEAD_DIM, HEAD_DIM), F32)],
        grid=(nseq, nblk),
        in_specs=[tspec] * 6 + [
            pl.BlockSpec((HGRN_HEADS, HEAD_DIM, HEAD_DIM), lambda b, i: (0, 0, 0)),
            pl.BlockSpec((SUB, POOL_WIDTH), lambda b, i: (0, 0)),
            pl.BlockSpec((len(POOL_WINDOWS), POOL_GROUP_DIM, POOL_GROUP_DIM), lambda b, i: (0, 0, 0)),
            pl.BlockSpec((1, POOL_WIDTH), lambda b, i: (0, 0)),
        ],
        out_specs=[pl.BlockSpec((MIX_BLOCK, D_MODEL), tok),
                   pl.BlockSpec((1, HGRN_HEADS, HEAD_DIM, HEAD_DIM), lambda b, i: (b, 0, 0, 0))],
        scratch_shapes=[pltpu.VMEM((HGRN_HEADS, HEAD_DIM, HEAD_DIM), F32),
                        pltpu.VMEM((SUB + MIX_BLOCK, POOL_WIDTH), F32)],
        compiler_params=pltpu.CompilerParams(dimension_semantics=("arbitrary", "arbitrary"),
                                             vmem_limit_bytes=VMEM_LIMIT),
        name="mixer",
    )(q, k, g, v, gs, p, s0, pp0, wpool_bf, pscale)


STEP_SEQS = 32


def _mixer_step_kernel(qt_ref, kt_ref, gt_ref, v_ref, gs_ref, s_ref, snew_ref, oa_ref):
    qt = qt_ref[0, 0]
    kt = kt_ref[0, 0]
    dt = jnp.exp(gt_ref[0, 0])
    rows = []
    for bb in range(STEP_SEQS):
        sn = s_ref[bb, 0] * dt[:, bb:bb + 1] + kt[:, bb:bb + 1] * v_ref[bb:bb + 1, :]
        snew_ref[bb, 0] = sn
        rows.append(jnp.sum(sn * qt[:, bb:bb + 1], axis=0, keepdims=True))
    o = jnp.concatenate(rows, axis=0)
    o = o * lax.rsqrt(jnp.mean(o * o, axis=-1, keepdims=True) + RMS_EPS)
    oa_ref[...] = o * gs_ref[...]


def _mixer_step(qT, kT, gT, v, gs, state):
    nseq = v.shape[0]
    nbc = nseq // STEP_SEQS
    cspec = pl.BlockSpec((1, 1, HEAD_DIM, STEP_SEQS), lambda h, c: (h, c, 0, 0))
    rspec = pl.BlockSpec((STEP_SEQS, HEAD_DIM), lambda h, c: (c, h))
    sspec = pl.BlockSpec((STEP_SEQS, 1, HEAD_DIM, HEAD_DIM), lambda h, c: (c, h, 0, 0))
    return pl.pallas_call(
        _mixer_step_kernel,
        out_shape=[jax.ShapeDtypeStruct(state.shape, F32), jax.ShapeDtypeStruct((nseq, HGRN_WIDTH), F32)],
        grid=(HGRN_HEADS, nbc),
        in_specs=[cspec, cspec, cspec, rspec, rspec, sspec],
        out_specs=[sspec, rspec],
        compiler_params=pltpu.CompilerParams(dimension_semantics=("arbitrary", "arbitrary"),
                                             vmem_limit_bytes=VMEM_LIMIT),
        name="mixer_step",
    )(qT, kT, gT, v, gs, state)


def _pool_step_kernel(sp_ref, p_ref, wpool_ref, pscale_ref, ob_ref):
    p_cur = p_ref[...]
    outs = []
    for gi, w in enumerate(POOL_WINDOWS):
        sl = slice(gi * POOL_GROUP_DIM, (gi + 1) * POOL_GROUP_DIM)
        s = p_cur[:, sl]
        for r in range(POOL_BUF - (w - 1), POOL_BUF):
            s = s + sp_ref[r][:, sl]
        pooled = s * (1.0 / w) - p_cur[:, sl]
        outs.append(_bdot(pooled.astype(BF16), wpool_ref[gi]) * pscale_ref[:, sl])
    ob_ref[...] = jnp.concatenate(outs, axis=1)


def _pool_step(spT, p, wpool_bf, pscale):
    nseq = p.shape[0]
    return pl.pallas_call(
        _pool_step_kernel,
        out_shape=jax.ShapeDtypeStruct((nseq, POOL_WIDTH), F32),
        name="pool_step",
    )(spT, p, wpool_bf, pscale)


def _outproj_kernel(mix_ref, xn_ref, wout_ref, g1_ref, b1_ref, wr_ref, br_ref,
                    x1_ref, x1p_ref, idx_ref, gate_ref):
    tm = mix_ref.shape[0]
    mix = _bdot(mix_ref[...], wout_ref[...])
    x1 = _ln(ALPHA * xn_ref[...] + mix, g1_ref[...], b1_ref[...])
    x1_ref[...] = x1
    xb = x1.astype(BF16)
    xr = xb.astype(F32)
    half = D_MODEL // 2
    lo = lax.shift_right_logical(lax.bitcast_convert_type(xr[:, :half], jnp.uint32), jnp.uint32(16))
    hi = lax.bitcast_convert_type(xr[:, half:], jnp.uint32) & jnp.uint32(0xFFFF0000)
    x1p_ref[...] = lax.bitcast_convert_type(hi | lo, I32)

    scores = _sigmoid(_bdot(xb, wr_ref[...]))
    sel = scores + br_ref[...]
    lane = lax.broadcasted_iota(I32, (tm, N_EXPERTS), 1).astype(F32)
    lane_o = lax.broadcasted_iota(I32, (tm, 128), 1)
    idx_o = jnp.zeros((tm, 128), F32)
    s_o = jnp.zeros((tm, 128), F32)
    ssum = jnp.zeros((tm, 1), F32)
    for j in range(TOP_K):
        m = jnp.max(sel, axis=-1, keepdims=True)
        am = jnp.min(jnp.where(sel == m, lane, float(N_EXPERTS)), axis=-1, keepdims=True)
        hit = lane == am
        sj = jnp.sum(jnp.where(hit, scores, 0.0), axis=-1, keepdims=True)
        sel = jnp.where(hit, -jnp.inf, sel)
        idx_o = jnp.where(lane_o == j, am, idx_o)
        s_o = jnp.where(lane_o == j, sj, s_o)
        ssum = ssum + sj
    idx_ref[...] = idx_o.astype(I32)
    gate_ref[...] = s_o / ssum * ROUTED_SCALE


def _outproj(mix, xn, wout_bf, g1, b1, wr_bf, br):
    T = mix.shape[0]
    tm = _row_tile(T)
    row = lambda i: (i, 0)
    const = lambda i: (0, 0)
    return pl.pallas_call(
        _outproj_kernel,
        out_shape=[jax.ShapeDtypeStruct((T, D_MODEL), F32), jax.ShapeDtypeStruct((T, D_MODEL // 2), I32),
                   jax.ShapeDtypeStruct((T, 128), I32), jax.ShapeDtypeStruct((T, 128), F32)],
        grid=(T // tm,),
        in_specs=[pl.BlockSpec((tm, D_MODEL), row), pl.BlockSpec((tm, D_MODEL), row),
                  pl.BlockSpec((D_MODEL, D_MODEL), const), pl.BlockSpec((1, D_MODEL), const),
                  pl.BlockSpec((1, D_MODEL), const), pl.BlockSpec((D_MODEL, N_EXPERTS), const),
                  pl.BlockSpec((1, N_EXPERTS), const)],
        out_specs=[pl.BlockSpec((tm, D_MODEL), row), pl.BlockSpec((tm, D_MODEL // 2), row),
                   pl.BlockSpec((tm, 128), row), pl.BlockSpec((tm, 128), row)],
        compiler_params=pltpu.CompilerParams(dimension_semantics=("arbitrary",), vmem_limit_bytes=VMEM_LIMIT),
        name="outproj",
    )(mix, xn, wout_bf, g1, b1, wr_bf, br)


def _moe_kernel(be_ref, bn_ref, tok_ref, dst_ref, xp_ref, wg_ref, wu_ref, wd_ref, yt_ref,
                tile_scr, ybuf, sem):
    del be_ref
    b = pl.program_id(0)
    nb = pl.num_programs(0)
    slot = b % 2
    n = bn_ref[b]
    R = MOE_ROWS

    def rows_sent(cnt):
        return pl.multiple_of(lax.shift_left(lax.shift_right_logical(cnt + 7, 3), 3), 8)

    def wait_rows(s, cnt):
        pltpu.make_async_copy(ybuf.at[s, pl.ds(0, cnt), :], yt_ref.at[pl.ds(0, cnt), :], sem.at[s]).wait()

    @pl.when(b == 0)
    def _():
        n_spare = 2 * TOP_K
        ybuf[0, 0:n_spare, :] = jnp.zeros((n_spare, D_MODEL), F32)
        init = pltpu.make_async_copy(ybuf.at[0, pl.ds(0, n_spare), :],
                                     yt_ref.at[pl.ds(yt_ref.shape[0] - n_spare, n_spare), :], sem.at[0])
        init.start()
        init.wait()

    @pl.when(b >= 2)
    def _():
        sent = rows_sent(bn_ref[jnp.maximum(b - 2, 0)])

        @pl.when(sent > 0)
        def _():
            wait_rows(slot, sent)

    @pl.when(n > 0)
    def _():
        for r in range(R):
            t4 = pl.multiple_of(tok_ref[0, 0, r], 4)
            tile_scr[pl.ds(r, 4, stride=TILE_STRIDE), :] = xp_ref[pl.ds(t4, 4), :]
        los, his = [], []
        for j in range(4):
            w = lax.bitcast_convert_type(tile_scr[pl.ds(j * TILE_STRIDE, R), :], jnp.uint32)
            los.append(lax.bitcast_convert_type(lax.shift_left(w, jnp.uint32(16)), F32).astype(BF16))
            his.append(lax.bitcast_convert_type(w & jnp.uint32(0xFFFF0000), F32).astype(BF16))
        xg = jnp.concatenate(los + his, axis=1)
        hg = _bdot(xg, wg_ref[0].astype(BF16))
        hu = _bdot(xg, wu_ref[0].astype(BF16))
        hb = (hg * _sigmoid(hg)) * hu
        ybuf[slot] = _bdot(hb.astype(BF16), wd_ref[0].astype(BF16))

        def send_group(g, carry):
            for u in range(8):
                r = g * 8 + u
                d = dst_ref[0, 0, r]
                pltpu.make_async_copy(ybuf.at[slot, pl.ds(r, 1), :], yt_ref.at[pl.ds(d, 1), :], sem.at[slot]).start()
            return carry

        lax.fori_loop(0, lax.shift_right_logical(n + 7, 3), send_group, 0)

    @pl.when(b == nb - 1)
    def _():
        sent = rows_sent(n)

        @pl.when(sent > 0)
        def _():
            wait_rows(slot, sent)

        @pl.when(nb >= 2)
        def _():
            sent1 = rows_sent(bn_ref[jnp.maximum(b - 1, 0)])

            @pl.when(sent1 > 0)
            def _():
                wait_rows(1 - slot, sent1)


def _moe(blk_e, blk_n, slot_tok, slot_dst, xp, wg, wu, wd, n_tok):
    nb = blk_e.shape[0]
    R = MOE_ROWS
    wmap_in = lambda b, be, bn: (be[b], 0, 0)
    smap = lambda b, be, bn: (b, 0, 0)
    grid_spec = pltpu.PrefetchScalarGridSpec(
        num_scalar_prefetch=2,
        grid=(nb,),
        in_specs=[
            pl.BlockSpec((1, 1, R), smap, memory_space=pltpu.SMEM),
            pl.BlockSpec((1, 1, R), smap, memory_space=pltpu.SMEM),
            pl.BlockSpec(memory_space=pltpu.VMEM),
            pl.BlockSpec((1, D_MODEL, D_EXPERT), wmap_in),
            pl.BlockSpec((1, D_MODEL, D_EXPERT), wmap_in),
            pl.BlockSpec((1, D_EXPERT, D_MODEL), wmap_in),
        ],
        out_specs=pl.BlockSpec(memory_space=pl.ANY),
        scratch_shapes=[pltpu.VMEM((4 * TILE_STRIDE, 128), I32),
                        pltpu.VMEM((2, R, D_MODEL), F32),
                        pltpu.SemaphoreType.DMA((2,))],
    )
    return pl.pallas_call(
        _moe_kernel,
        out_shape=jax.ShapeDtypeStruct(((n_tok + 2) * TOP_K, D_MODEL), F32),
        grid_spec=grid_spec,
        compiler_params=pltpu.CompilerParams(dimension_semantics=("arbitrary",),
                                             vmem_limit_bytes=58 * 1024 * 1024),
        name="moe",
    )(blk_e, blk_n, slot_tok, slot_dst, xp, wg, wu, wd)


def _route_plan(idx, n_tok):
    R = MOE_ROWS
    n_assign = n_tok * TOP_K
    nb = (n_assign + N_EXPERTS * (R - 1)) // R
    e_flat = idx.reshape(-1)
    counts = jnp.zeros((N_EXPERTS,), I32).at[e_flat].add(1)
    nblk_e = (counts + R - 1) // R
    bend = jnp.cumsum(nblk_e)
    bstart = bend - nblk_e
    order = jnp.argsort(e_flat, stable=True).astype(I32)
    e_sorted = e_flat[order]
    cstart = jnp.cumsum(counts) - counts
    slot = bstart[e_sorted] * R + (jnp.arange(n_assign, dtype=I32) - cstart[e_sorted])
    blk = jnp.arange(nb, dtype=I32)
    blk_e = jnp.minimum(jnp.searchsorted(bend, blk, side="right"), N_EXPERTS - 1).astype(I32)
    blk_n = jnp.clip(counts[blk_e] - (blk - bstart[blk_e]) * R, 0, R).astype(I32)
    blk_n = jnp.where(blk < bend[-1], blk_n, 0)
    r = jnp.arange(nb * R, dtype=I32)
    trash = n_assign + ((r // R) % 2) * 8 + (r % 8)
    slot_tok = jnp.zeros((nb * R,), I32).at[slot].set((order // TOP_K) * 4)
    slot_dst = trash.at[slot].set(order)
    return blk_e, blk_n, slot_tok.reshape(nb, 1, R), slot_dst.reshape(nb, 1, R)


def _combine_kernel(x1_ref, yt_ref, gate_ref, wgs_ref, wus_ref, wds_ref, g2_ref, b2_ref, out_ref):
    x1 = x1_ref[...]
    xb = x1.astype(BF16)
    hg = _bdot(xb, wgs_ref[...])
    hs = (hg * _sigmoid(hg)) * _bdot(xb, wus_ref[...])
    moe = _bdot(hs.astype(BF16), wds_ref[...])
    gate = gate_ref[...]
    routed = yt_ref[:, 0, :] * gate[:, 0:1]
    for j in range(1, TOP_K):
        routed = routed + yt_ref[:, j, :] * gate[:, j:j + 1]
    out_ref[...] = _ln(ALPHA * x1 + (moe + routed), g2_ref[...], b2_ref[...])


def _combine(x1, yt3, gate, wgs_bf, wus_bf, wds_bf, g2, b2, row0, nrows):
    tm = 128
    assert nrows % tm == 0 and row0 % tm == 0
    off = row0 // tm
    row = lambda i: (i + off, 0)
    const = lambda i: (0, 0)
    return pl.pallas_call(
        _combine_kernel,
        out_shape=jax.ShapeDtypeStruct((nrows, D_MODEL), F32),
        grid=(nrows // tm,),
        in_specs=[pl.BlockSpec((tm, D_MODEL), row),
                  pl.BlockSpec((tm, TOP_K, D_MODEL), lambda i: (i + off, 0, 0)),
                  pl.BlockSpec((tm, 128), row),
                  pl.BlockSpec((D_MODEL, D_EXPERT), const), pl.BlockSpec((D_MODEL, D_EXPERT), const),
                  pl.BlockSpec((D_EXPERT, D_MODEL), const),
                  pl.BlockSpec((1, D_MODEL), const), pl.BlockSpec((1, D_MODEL), const)],
        out_specs=pl.BlockSpec((tm, D_MODEL), lambda i: (i, 0)),
        compiler_params=pltpu.CompilerParams(dimension_semantics=("arbitrary",), vmem_limit_bytes=VMEM_LIMIT),
        name="combine",
    )(x1, yt3, gate, wgs_bf, wus_bf, wds_bf, g2, b2)


def kernel(x_prompt, x_sample, state_hgrn, state_pool, meta_tokens, ln_emb_g, ln_emb_b, w_in, lb_logits, hgrn_norm_g, w_pool, pool_scale, w_out, ln1_g, ln1_b, w_router, b_router, w_gate_e, w_up_e, w_down_e, w_gate_s, w_up_s, w_down_s, ln2_g, ln2_b):
    nseq, seqlen, _ = x_prompt.shape
    ndec = x_sample.shape[0]
    l = 0
    row = lambda a: a.reshape(1, -1)
    w_in_bf = w_in[l].astype(BF16)
    wpool_bf = w_pool[l].astype(BF16)
    lng, lnb = row(ln_emb_g), row(ln_emb_b)
    ng, ps = row(hgrn_norm_g[l]), row(pool_scale[l])
    proj = functools.partial(_ln_proj, ln_g=lng, ln_b=lnb, w_in_bf=w_in_bf, lb_logits=lb_logits, norm_g=ng)

    m_xn, m_q, m_k, m_g, m_v, m_gs, m_p = proj(meta_tokens)
    pad = lambda a: jnp.pad(a, ((0, MIX_BLOCK - N_META), (0, 0)))
    zero_state = jnp.zeros((HGRN_HEADS, HEAD_DIM, HEAD_DIM), F32)
    _, s_meta = _mixer(pad(m_q), pad(m_k), pad(m_g), pad(m_v), pad(m_gs), pad(m_p), zero_state,
                       jnp.zeros((SUB, POOL_WIDTH), F32), wpool_bf, ps, 1, MIX_BLOCK)

    p_xn, p_q, p_k, p_g, p_v, p_gs, p_p = proj(x_prompt.reshape(nseq * seqlen, D_MODEL))
    p_mix, s_prompt = _mixer(p_q, p_k, p_g, p_v, p_gs, p_p, s_meta[0], m_p, wpool_bf, ps, nseq, seqlen)

    d_xn, d_q, d_k, d_g, d_v, d_gs, d_p = proj(x_sample.reshape(ndec, D_MODEL))
    cols = lambda a: a.reshape(ndec // STEP_SEQS, STEP_SEQS, HGRN_HEADS, HEAD_DIM).transpose(2, 0, 3, 1)
    s_dec, d_oa = _mixer_step(cols(d_q), cols(d_k), cols(d_g), d_v, d_gs, state_hgrn[l])
    d_ob = _pool_step(state_pool[l].transpose(1, 0, 2), d_p, wpool_bf, ps)
    d_mix = jnp.concatenate([d_oa, d_ob], axis=1).astype(BF16)

    mix = jnp.concatenate([p_mix, d_mix], axis=0)
    xn = jnp.concatenate([p_xn, d_xn], axis=0)
    n_tok = mix.shape[0]
    x1, x1p, idx, gate = _outproj(mix, xn, w_out[l].astype(BF16), row(ln1_g[l]), row(ln1_b[l]),
                                  w_router[l].astype(BF16), row(b_router[l]))
    blk_e, blk_n, slot_tok, slot_dst = _route_plan(idx[:, :TOP_K], n_tok)
    xp = x1p.reshape(n_tok * 4, 128)
    yt = _moe(blk_e, blk_n, slot_tok, slot_dst, xp, w_gate_e[l], w_up_e[l], w_down_e[l], n_tok)
    yt3 = yt.reshape(n_tok + 2, TOP_K, D_MODEL)
    comb = functools.partial(_combine, x1, yt3, gate, w_gate_s[l].astype(BF16), w_up_s[l].astype(BF16),
                             w_down_s[l].astype(BF16), row(ln2_g[l]), row(ln2_b[l]))
    y_prompt = comb(0, nseq * seqlen).reshape(nseq, seqlen, D_MODEL)
    y_sample = comb(nseq * seqlen, ndec).reshape(ndec, 1, D_MODEL)

    state_pool_prompt = p_p.reshape(nseq, seqlen, POOL_WIDTH)[:, seqlen - POOL_BUF:, :]
    state_pool_sample = jnp.concatenate([state_pool[l][:, 1:, :], d_p[:, None, :]], axis=1)
    return (y_prompt, y_sample, s_prompt[None], state_pool_prompt[None], s_dec[None], state_pool_sample[None])
```

```python
import functools

import jax
import jax.numpy as jnp
from jax import lax
from jax.experimental import pallas as pl
from jax.experimental.pallas import tpu as pltpu

F32 = jnp.float32
BF16 = jnp.bfloat16
I32 = jnp.int32

D_MODEL = 1024
N_META = 16
HGRN_WIDTH = 512
HGRN_HEADS = 4
HEAD_DIM = 128
POOL_WIDTH = 512
POOL_WINDOWS = (2, 4, 8, 16)
POOL_GROUP_DIM = 128
POOL_BUF = 15
IN_WIDTH = 4 * HGRN_WIDTH + POOL_WIDTH
N_EXPERTS = 256
TOP_K = 8
D_EXPERT = 256
ROUTED_SCALE = 2.5
DEPTH = 1
ALPHA = (2 * DEPTH) ** 0.25
LN_EPS = 1e-5
RMS_EPS = 1e-6

SUB = 16
MIX_BLOCK = 128
MOE_ROWS = 256
TILE_STRIDE = MOE_ROWS + 8
ORDER_ALIGN = 128
VMEM_LIMIT = 48 * 1024 * 1024


def _ln(x, g, b):
    mu = jnp.mean(x, axis=-1, keepdims=True)
    xc = x - mu
    var = jnp.mean(xc * xc, axis=-1, keepdims=True)
    return xc * lax.rsqrt(var + LN_EPS) * g + b


def _sigmoid(z):
    return 1.0 / (1.0 + jnp.exp(-z))


def _bdot(a, b):
    return jnp.dot(a, b, preferred_element_type=F32)


def _ln_proj_kernel(x_ref, g_ref, b_ref, w_ref, lbl_ref, ng_ref,
                    xn_ref, q_ref, k_ref, gl_ref, v_ref, gs_ref, p_ref):
    xn = _ln(x_ref[...], g_ref[...], b_ref[...])
    xn_ref[...] = xn
    proj = _bdot(xn.astype(BF16), w_ref[...])
    lbl = lbl_ref[...]
    e = jnp.exp(lbl - jnp.max(lbl, axis=0, keepdims=True))
    lb = e[0:1] / jnp.sum(e, axis=0, keepdims=True)
    W = HGRN_WIDTH
    q = proj[:, 0:W]
    f = proj[:, W:2 * W]
    q_ref[...] = q * _sigmoid(q)
    k_ref[...] = (1.0 - lb) * _sigmoid(-f)
    gl_ref[...] = jnp.log(lb + (1.0 - lb) * _sigmoid(f))
    v_ref[...] = proj[:, 2 * W:3 * W]
    g = proj[:, 3 * W:4 * W]
    gs_ref[...] = ng_ref[...] * (g * _sigmoid(g))
    p_ref[...] = proj[:, 4 * W:]


def _row_tile(n_rows):
    for tm in (256, 128, 64, 32, 16, 8):
        if n_rows % tm == 0:
            return tm
    raise ValueError(f"row count {n_rows} is not a multiple of 8")


def _ln_proj(x, ln_g, ln_b, w_in_bf, lb_logits, norm_g):
    T = x.shape[0]
    tm = _row_tile(T)
    row = lambda i: (i, 0)
    const = lambda i: (0, 0)
    outs = [jax.ShapeDtypeStruct((T, D_MODEL), F32)] + [jax.ShapeDtypeStruct((T, HGRN_WIDTH), F32)] * 6
    return pl.pallas_call(
        _ln_proj_kernel,
        out_shape=outs,
        grid=(T // tm,),
        in_specs=[
            pl.BlockSpec((tm, D_MODEL), row),
            pl.BlockSpec((1, D_MODEL), const),
            pl.BlockSpec((1, D_MODEL), const),
            pl.BlockSpec((D_MODEL, IN_WIDTH), const),
            pl.BlockSpec((DEPTH + 1, HGRN_WIDTH), const),
            pl.BlockSpec((1, HGRN_WIDTH), const),
        ],
        out_specs=[pl.BlockSpec((tm, D_MODEL), row)] + [pl.BlockSpec((tm, HGRN_WIDTH), row)] * 6,
        compiler_params=pltpu.CompilerParams(dimension_semantics=("arbitrary",), vmem_limit_bytes=VMEM_LIMIT),
        name="ln_proj",
    )(x, ln_g, ln_b, w_in_bf, lb_logits, norm_g)


def _pool_group(pe, p_cur, gi, w):
    sl = slice(gi * POOL_GROUP_DIM, (gi + 1) * POOL_GROUP_DIM)
    s = pe[:, sl]
    sh = 1
    while sh < w:
        s = s + pltpu.roll(s, sh, 0)
        sh *= 2
    return s[SUB:, :] * (1.0 / w) - p_cur[:, sl]


def _mixer_kernel(q_ref, k_ref, g_ref, v_ref, gs_ref, p_ref, s0_ref, pp0_ref, wpool_ref, pscale_ref,
                  mix_ref, sfin_ref, st_scr, pe_scr):
    i = pl.program_id(1)
    nblk = pl.num_programs(1)
    n = MIX_BLOCK

    @pl.when(i == 0)
    def _():
        for h in range(HGRN_HEADS):
            st_scr[h] = s0_ref[h].T
        pe_scr[0:SUB, :] = pp0_ref[...]

    rows = lax.broadcasted_iota(I32, (n, HEAD_DIM), 0)
    r16 = rows & (SUB - 1)
    t8 = lax.broadcasted_iota(I32, (8, HEAD_DIM), 0)
    zero_bf = jnp.zeros((SUB, HEAD_DIM), BF16)

    o_heads = []
    for h in range(HGRN_HEADS):
        hs = slice(h * HEAD_DIM, (h + 1) * HEAD_DIM)
        Q = q_ref[:, hs]
        K = k_ref[:, hs]
        G = g_ref[:, hs]
        V = v_ref[:, hs]
        bf = G
        br = G
        for sh in (1, 2, 4, 8):
            bf = bf + jnp.where(r16 >= sh, pltpu.roll(bf, sh, 0), 0.0)
            br = br + jnp.where(r16 < SUB - sh, pltpu.roll(br, n - sh, 0), 0.0)
        br = br - G
        qt = (Q * jnp.exp(bf)).astype(BF16)
        kt = (K * jnp.exp(br)).astype(BF16)
        vt = V.T.astype(BF16)
        st = st_scr[h]
        o_parts = []
        for c in range(n // SUB):
            r0 = c * SUB
            b_top, b_bot = bf[r0:r0 + 8], bf[r0 + 8:r0 + 16]
            q_top, q_bot = Q[r0:r0 + 8], Q[r0 + 8:r0 + 16]
            acc_top = jnp.zeros((8, HEAD_DIM), F32)
            acc_bot = jnp.zeros((8, HEAD_DIM), F32)
            for s in range(SUB):
                bs = bf[r0 + s:r0 + s + 1]
                ks = K[r0 + s:r0 + s + 1]
                vs = V[r0 + s:r0 + s + 1]
                if s < 8:
                    col = jnp.sum(q_top * jnp.exp(b_top - bs) * ks, axis=-1, keepdims=True)
                    col = jnp.where(t8[:, 0:1] >= s, col, 0.0)
                    acc_top = acc_top + col * vs
                    col = jnp.sum(q_bot * jnp.exp(b_bot - bs) * ks, axis=-1, keepdims=True)
                    acc_bot = acc_bot + col * vs
                else:
                    col = jnp.sum(q_bot * jnp.exp(b_bot - bs) * ks, axis=-1, keepdims=True)
                    col = jnp.where(t8[:, 0:1] + 8 >= s, col, 0.0)
                    acc_bot = acc_bot + col * vs
            o_diag = jnp.concatenate([acc_top, acc_bot], axis=0)
            o_inter = lax.dot_general(qt[r0:r0 + SUB], st.astype(BF16), (((1,), (1,)), ((), ())),
                                      preferred_element_type=F32)
            o_parts.append(o_inter + o_diag)
            kmask = jnp.concatenate([zero_bf] * c + [kt[r0:r0 + SUB]] + [zero_bf] * (n // SUB - 1 - c), axis=0)
            d_st = _bdot(vt, kmask)
            st = st * jnp.exp(bf[r0 + SUB - 1:r0 + SUB]) + d_st
        st_scr[h] = st
        o = jnp.concatenate(o_parts, axis=0)
        o = o * lax.rsqrt(jnp.mean(o * o, axis=-1, keepdims=True) + RMS_EPS)
        o_heads.append(o * gs_ref[:, hs])

    p_cur = p_ref[...]
    pe_scr[SUB:SUB + n, :] = p_cur
    pe = pe_scr[...]
    ob = []
    for gi, w in enumerate(POOL_WINDOWS):
        pooled = _pool_group(pe, p_cur, gi, w)
        sl = slice(gi * POOL_GROUP_DIM, (gi + 1) * POOL_GROUP_DIM)
        ob.append(_bdot(pooled.astype(BF16), wpool_ref[gi]) * pscale_ref[:, sl])
    pe_scr[0:SUB, :] = p_cur[n - SUB:, :]

    mix_ref[...] = jnp.concatenate(o_heads + ob, axis=1).astype(BF16)

    @pl.when(i == nblk - 1)
    def _():
        for h in range(HGRN_HEADS):
            sfin_ref[0, h] = st_scr[h].T


def _mixer(q, k, g, v, gs, p, s0, pp0, wpool_bf, pscale, nseq, seqlen):
    nblk = seqlen // MIX_BLOCK
    tok = lambda b, i: (b * nblk + i, 0)
    tspec = pl.BlockSpec((MIX_BLOCK, HGRN_WIDTH), tok)
    return pl.pallas_call(
        _mixer_kernel,
        out_shape=[jax.ShapeDtypeStruct((nseq * seqlen, D_MODEL), BF16),
                   jax.ShapeDtypeStruct((nseq, HGRN_HEADS, HEAD_DIM, HEAD_DIM), F32)],
        grid=(nseq, nblk),
        in_specs=[tspec] * 6 + [
            pl.BlockSpec((HGRN_HEADS, HEAD_DIM, HEAD_DIM), lambda b, i: (0, 0, 0)),
            pl.BlockSpec((SUB, POOL_WIDTH), lambda b, i: (0, 0)),
            pl.BlockSpec((len(POOL_WINDOWS), POOL_GROUP_DIM, POOL_GROUP_DIM), lambda b, i: (0, 0, 0)),
            pl.BlockSpec((1, POOL_WIDTH), lambda b, i: (0, 0)),
        ],
        out_specs=[pl.BlockSpec((MIX_BLOCK, D_MODEL), tok),
                   pl.BlockSpec((1, HGRN_HEADS, HEAD_DIM, HEAD_DIM), lambda b, i: (b, 0, 0, 0))],
        scratch_shapes=[pltpu.VMEM((HGRN_HEADS, HEAD_DIM, HEAD_DIM), F32),
                        pltpu.VMEM((SUB + MIX_BLOCK, POOL_WIDTH), F32)],
        compiler_params=pltpu.CompilerParams(dimension_semantics=("arbitrary", "arbitrary"),
                                             vmem_limit_bytes=VMEM_LIMIT),
        name="mixer",
    )(q, k, g, v, gs, p, s0, pp0, wpool_bf, pscale)


STEP_SEQS = 32


def _mixer_step_kernel(qt_ref, kt_ref, gt_ref, v_ref, gs_ref, s_ref, snew_ref, oa_ref):
    qt = qt_ref[0, 0]
    kt = kt_ref[0, 0]
    dt = jnp.exp(gt_ref[0, 0])
    rows = []
    for bb in range(STEP_SEQS):
        sn = s_ref[bb, 0] * dt[:, bb:bb + 1] + kt[:, bb:bb + 1] * v_ref[bb:bb + 1, :]
        snew_ref[bb, 0] = sn
        rows.append(jnp.sum(sn * qt[:, bb:bb + 1], axis=0, keepdims=True))
    o = jnp.concatenate(rows, axis=0)
    o = o * lax.rsqrt(jnp.mean(o * o, axis=-1, keepdims=True) + RMS_EPS)
    oa_ref[...] = o * gs_ref[...]


def _mixer_step(qT, kT, gT, v, gs, state):
    nseq = v.shape[0]
    nbc = nseq // STEP_SEQS
    cspec = pl.BlockSpec((1, 1, HEAD_DIM, STEP_SEQS), lambda h, c: (h, c, 0, 0))
    rspec = pl.BlockSpec((STEP_SEQS, HEAD_DIM), lambda h, c: (c, h))
    sspec = pl.BlockSpec((STEP_SEQS, 1, HEAD_DIM, HEAD_DIM), lambda h, c: (c, h, 0, 0))
    return pl.pallas_call(
        _mixer_step_kernel,
        out_shape=[jax.ShapeDtypeStruct(state.shape, F32), jax.ShapeDtypeStruct((nseq, HGRN_WIDTH), F32)],
        grid=(HGRN_HEADS, nbc),
        in_specs=[cspec, cspec, cspec, rspec, rspec, sspec],
        out_specs=[sspec, rspec],
        compiler_params=pltpu.CompilerParams(dimension_semantics=("arbitrary", "arbitrary"),
                                             vmem_limit_bytes=VMEM_LIMIT),
        name="mixer_step",
    )(qT, kT, gT, v, gs, state)


def _pool_step_kernel(sp_ref, p_ref, wpool_ref, pscale_ref, ob_ref):
    p_cur = p_ref[...]
    outs = []
    for gi, w in enumerate(POOL_WINDOWS):
        sl = slice(gi * POOL_GROUP_DIM, (gi + 1) * POOL_GROUP_DIM)
        s = p_cur[:, sl]
        for r in range(POOL_BUF - (w - 1), POOL_BUF):
            s = s + sp_ref[r][:, sl]
        pooled = s * (1.0 / w) - p_cur[:, sl]
        outs.append(_bdot(pooled.astype(BF16), wpool_ref[gi]) * pscale_ref[:, sl])
    ob_ref[...] = jnp.concatenate(outs, axis=1)


def _pool_step(spT, p, wpool_bf, pscale):
    nseq = p.shape[0]
    return pl.pallas_call(
        _pool_step_kernel,
        out_shape=jax.ShapeDtypeStruct((nseq, POOL_WIDTH), F32),
        name="pool_step",
    )(spT, p, wpool_bf, pscale)


def _outproj_kernel(mix_ref, xn_ref, wout_ref, g1_ref, b1_ref, wr_ref, br_ref,
                    x1_ref, x1p_ref, idx_ref, gate_ref, cnt_ref):
    tm = mix_ref.shape[0]
    mix = _bdot(mix_ref[...], wout_ref[...])
    x1 = _ln(ALPHA * xn_ref[...] + mix, g1_ref[...], b1_ref[...])
    x1_ref[...] = x1
    xb = x1.astype(BF16)
    xr = xb.astype(F32)
    half = D_MODEL // 2
    lo = lax.shift_right_logical(lax.bitcast_convert_type(xr[:, :half], jnp.uint32), jnp.uint32(16))
    hi = lax.bitcast_convert_type(xr[:, half:], jnp.uint32) & jnp.uint32(0xFFFF0000)
    x1p_ref[...] = lax.bitcast_convert_type(hi | lo, I32)

    scores = _sigmoid(_bdot(xb, wr_ref[...]))
    sel = scores + br_ref[...]
    lane = lax.broadcasted_iota(I32, (tm, N_EXPERTS), 1).astype(F32)
    lane_o = lax.broadcasted_iota(I32, (tm, 128), 1)
    idx_o = jnp.zeros((tm, 128), F32)
    s_o = jnp.zeros((tm, 128), F32)
    ssum = jnp.zeros((tm, 1), F32)
    chosen = jnp.zeros((tm, N_EXPERTS), F32)
    for j in range(TOP_K):
        m = jnp.max(sel, axis=-1, keepdims=True)
        am = jnp.min(jnp.where(sel == m, lane, float(N_EXPERTS)), axis=-1, keepdims=True)
        hit = lane == am
        sj = jnp.sum(jnp.where(hit, scores, 0.0), axis=-1, keepdims=True)
        sel = jnp.where(hit, -jnp.inf, sel)
        chosen = jnp.where(hit, 1.0, chosen)
        idx_o = jnp.where(lane_o == j, am, idx_o)
        s_o = jnp.where(lane_o == j, sj, s_o)
        ssum = ssum + sj
    idx_ref[...] = idx_o.astype(I32)
    gate_ref[...] = s_o / ssum * ROUTED_SCALE

    @pl.when(pl.program_id(0) == 0)
    def _():
        cnt_ref[...] = jnp.zeros_like(cnt_ref)

    cnt_ref[...] += jnp.sum(chosen, axis=0, keepdims=True)


def _outproj(mix, xn, wout_bf, g1, b1, wr_bf, br):
    T = mix.shape[0]
    tm = _row_tile(T)
    row = lambda i: (i, 0)
    const = lambda i: (0, 0)
    return pl.pallas_call(
        _outproj_kernel,
        out_shape=[jax.ShapeDtypeStruct((T, D_MODEL), F32), jax.ShapeDtypeStruct((T, D_MODEL // 2), I32),
                   jax.ShapeDtypeStruct((T, 128), I32), jax.ShapeDtypeStruct((T, 128), F32),
                   jax.ShapeDtypeStruct((1, N_EXPERTS), F32)],
        grid=(T // tm,),
        in_specs=[pl.BlockSpec((tm, D_MODEL), row), pl.BlockSpec((tm, D_MODEL), row),
                  pl.BlockSpec((D_MODEL, D_MODEL), const), pl.BlockSpec((1, D_MODEL), const),
                  pl.BlockSpec((1, D_MODEL), const), pl.BlockSpec((D_MODEL, N_EXPERTS), const),
                  pl.BlockSpec((1, N_EXPERTS), const)],
        out_specs=[pl.BlockSpec((tm, D_MODEL), row), pl.BlockSpec((tm, D_MODEL // 2), row),
                   pl.BlockSpec((tm, 128), row), pl.BlockSpec((tm, 128), row),
                   pl.BlockSpec((1, N_EXPERTS), const)],
        compiler_params=pltpu.CompilerParams(dimension_semantics=("arbitrary",), vmem_limit_bytes=VMEM_LIMIT),
        name="outproj",
    )(mix, xn, wout_bf, g1, b1, wr_bf, br)


def _moe_kernel(be_ref, bn_ref, bsrc_ref, bdel_ref, order_ref, xp_ref, wg_ref, wu_ref, wd_ref, yt_ref,
                tile_scr, ybuf, ord_smem, sem, sem_o):
    del be_ref
    b = pl.program_id(0)
    nb = pl.num_programs(0)
    slot = b % 2
    n = bn_ref[b]
    delta = bdel_ref[b]
    R = MOE_ROWS
    n_assign = yt_ref.shape[0] - 2 * TOP_K

    def fetch(blk, s):
        src = pl.multiple_of(bsrc_ref[blk], ORDER_ALIGN)
        return pltpu.make_async_copy(order_ref.at[pl.ds(src, R + ORDER_ALIGN)], ord_smem.at[s], sem_o.at[s])

    @pl.when(b == 0)
    def _():
        fetch(0, 0).start()

    @pl.when(b + 1 < nb)
    def _():
        fetch(jnp.minimum(b + 1, nb - 1), 1 - slot).start()

    fetch(b, slot).wait()

    def rows_sent(cnt):
        return pl.multiple_of(lax.shift_left(lax.shift_right_logical(cnt + 7, 3), 3), 8)

    def wait_rows(s, cnt):
        pltpu.make_async_copy(ybuf.at[s, pl.ds(0, cnt), :], yt_ref.at[pl.ds(0, cnt), :], sem.at[s]).wait()

    @pl.when(b == 0)
    def _():
        n_spare = 2 * TOP_K
        ybuf[0, 0:n_spare, :] = jnp.zeros((n_spare, D_MODEL), F32)
        init = pltpu.make_async_copy(ybuf.at[0, pl.ds(0, n_spare), :],
                                     yt_ref.at[pl.ds(yt_ref.shape[0] - n_spare, n_spare), :], sem.at[0])
        init.start()
        init.wait()

    @pl.when(b >= 2)
    def _():
        sent = rows_sent(bn_ref[jnp.maximum(b - 2, 0)])

        @pl.when(sent > 0)
        def _():
            wait_rows(slot, sent)

    @pl.when(n > 0)
    def _():
        for r in range(R):
            tok = lax.shift_right_logical(ord_smem[slot, delta + r], 3)
            t4 = pl.multiple_of(lax.shift_left(tok, 2), 4)
            tile_scr[pl.ds(r, 4, stride=TILE_STRIDE), :] = xp_ref[pl.ds(t4, 4), :]
        los, his = [], []
        for j in range(4):
            w = lax.bitcast_convert_type(tile_scr[pl.ds(j * TILE_STRIDE, R), :], jnp.uint32)
            los.append(lax.bitcast_convert_type(lax.shift_left(w, jnp.uint32(16)), F32).astype(BF16))
            his.append(lax.bitcast_convert_type(w & jnp.uint32(0xFFFF0000), F32).astype(BF16))
        xg = jnp.concatenate(los + his, axis=1)
        hg = _bdot(xg, wg_ref[0].astype(BF16))
        hu = _bdot(xg, wu_ref[0].astype(BF16))
        hb = (hg * _sigmoid(hg)) * hu
        ybuf[slot] = _bdot(hb.astype(BF16), wd_ref[0].astype(BF16))

        def send_group(g, carry):
            for u in range(8):
                r = g * 8 + u
                d = jnp.where(r < n, ord_smem[slot, delta + r], n_assign + slot * 8 + u)
                pltpu.make_async_copy(ybuf.at[slot, pl.ds(r, 1), :], yt_ref.at[pl.ds(d, 1), :], sem.at[slot]).start()
            return carry

        lax.fori_loop(0, lax.shift_right_logical(n + 7, 3), send_group, 0)

    @pl.when(b == nb - 1)
    def _():
        sent = rows_sent(n)

        @pl.when(sent > 0)
        def _():
            wait_rows(slot, sent)

        @pl.when(nb >= 2)
        def _():
            sent1 = rows_sent(bn_ref[jnp.maximum(b - 1, 0)])

            @pl.when(sent1 > 0)
            def _():
                wait_rows(1 - slot, sent1)


def _moe(blk_e, blk_n, blk_src, blk_delta, order, xp, wg, wu, wd, n_tok):
    nb = blk_e.shape[0]
    R = MOE_ROWS
    wmap_in = lambda b, be, bn, bs, bd: (be[b], 0, 0)
    grid_spec = pltpu.PrefetchScalarGridSpec(
        num_scalar_prefetch=4,
        grid=(nb,),
        in_specs=[
            pl.BlockSpec(memory_space=pl.ANY),
            pl.BlockSpec(memory_space=pltpu.VMEM),
            pl.BlockSpec((1, D_MODEL, D_EXPERT), wmap_in),
            pl.BlockSpec((1, D_MODEL, D_EXPERT), wmap_in),
            pl.BlockSpec((1, D_EXPERT, D_MODEL), wmap_in),
        ],
        out_specs=pl.BlockSpec(memory_space=pl.ANY),
        scratch_shapes=[pltpu.VMEM((4 * TILE_STRIDE, 128), I32),
                        pltpu.VMEM((2, R, D_MODEL), F32),
                        pltpu.SMEM((2, R + ORDER_ALIGN), I32),
                        pltpu.SemaphoreType.DMA((2,)),
                        pltpu.SemaphoreType.DMA((2,))],
    )
    return pl.pallas_call(
        _moe_kernel,
        out_shape=jax.ShapeDtypeStruct(((n_tok + 2) * TOP_K, D_MODEL), F32),
        grid_spec=grid_spec,
        compiler_params=pltpu.CompilerParams(dimension_semantics=("arbitrary",),
                                             vmem_limit_bytes=58 * 1024 * 1024),
        name="moe",
    )(blk_e, blk_n, blk_src, blk_delta, order, xp, wg, wu, wd)


def _route_plan(idx, counts, n_tok):
    R = MOE_ROWS
    n_assign = n_tok * TOP_K
    nb = (n_assign + N_EXPERTS * (R - 1)) // R
    e_flat = idx.reshape(-1)
    _, order = lax.sort((e_flat, jnp.arange(n_assign, dtype=I32)), num_keys=1)
    order = jnp.concatenate([order, jnp.zeros((R + ORDER_ALIGN,), I32)])
    nblk_e = (counts + R - 1) // R
    bend = jnp.cumsum(nblk_e)
    bstart = bend - nblk_e
    cstart = jnp.cumsum(counts) - counts
    blk = jnp.arange(nb, dtype=I32)
    blk_e = jnp.minimum(jnp.sum((bend[None, :] <= blk[:, None]).astype(I32), axis=1), N_EXPERTS - 1)
    k = blk - bstart[blk_e]
    active = blk < bend[-1]
    blk_n = jnp.where(active, jnp.clip(counts[blk_e] - k * R, 0, R), 0).astype(I32)
    src = jnp.where(active, cstart[blk_e] + k * R, 0).astype(I32)
    blk_src = (src // ORDER_ALIGN) * ORDER_ALIGN
    return blk_e.astype(I32), blk_n, blk_src, src - blk_src, order


def _combine_kernel(x1_ref, yt_ref, gate_ref, wgs_ref, wus_ref, wds_ref, g2_ref, b2_ref, out_ref):
    x1 = x1_ref[...]
    xb = x1.astype(BF16)
    hg = _bdot(xb, wgs_ref[...])
    hs = (hg * _sigmoid(hg)) * _bdot(xb, wus_ref[...])
    moe = _bdot(hs.astype(BF16), wds_ref[...])
    gate = gate_ref[...]
    routed = yt_ref[:, 0, :] * gate[:, 0:1]
    for j in range(1, TOP_K):
        routed = routed + yt_ref[:, j, :] * gate[:, j:j + 1]
    out_ref[...] = _ln(ALPHA * x1 + (moe + routed), g2_ref[...], b2_ref[...])


def _combine(x1, yt3, gate, wgs_bf, wus_bf, wds_bf, g2, b2, row0, nrows):
    tm = 128
    assert nrows % tm == 0 and row0 % tm == 0
    off = row0 // tm
    row = lambda i: (i + off, 0)
    const = lambda i: (0, 0)
    return pl.pallas_call(
        _combine_kernel,
        out_shape=jax.ShapeDtypeStruct((nrows, D_MODEL), F32),
        grid=(nrows // tm,),
        in_specs=[pl.BlockSpec((tm, D_MODEL), row),
                  pl.BlockSpec((tm, TOP_K, D_MODEL), lambda i: (i + off, 0, 0)),
                  pl.BlockSpec((tm, 128), row),
                  pl.BlockSpec((D_MODEL, D_EXPERT), const), pl.BlockSpec((D_MODEL, D_EXPERT), const),
                  pl.BlockSpec((D_EXPERT, D_MODEL), const),
                  pl.BlockSpec((1, D_MODEL), const), pl.BlockSpec((1, D_MODEL), const)],
        out_specs=pl.BlockSpec((tm, D_MODEL), lambda i: (i, 0)),
        compiler_params=pltpu.CompilerParams(dimension_semantics=("arbitrary",), vmem_limit_bytes=VMEM_LIMIT),
        name="combine",
    )(x1, yt3, gate, wgs_bf, wus_bf, wds_bf, g2, b2)


def kernel(x_prompt, x_sample, state_hgrn, state_pool, meta_tokens, ln_emb_g, ln_emb_b, w_in, lb_logits, hgrn_norm_g, w_pool, pool_scale, w_out, ln1_g, ln1_b, w_router, b_router, w_gate_e, w_up_e, w_down_e, w_gate_s, w_up_s, w_down_s, ln2_g, ln2_b):
    nseq, seqlen, _ = x_prompt.shape
    ndec = x_sample.shape[0]
    l = 0
    row = lambda a: a.reshape(1, -1)
    w_in_bf = w_in[l].astype(BF16)
    wpool_bf = w_pool[l].astype(BF16)
    lng, lnb = row(ln_emb_g), row(ln_emb_b)
    ng, ps = row(hgrn_norm_g[l]), row(pool_scale[l])
    proj = functools.partial(_ln_proj, ln_g=lng, ln_b=lnb, w_in_bf=w_in_bf, lb_logits=lb_logits, norm_g=ng)

    m_xn, m_q, m_k, m_g, m_v, m_gs, m_p = proj(meta_tokens)
    pad = lambda a: jnp.pad(a, ((0, MIX_BLOCK - N_META), (0, 0)))
    zero_state = jnp.zeros((HGRN_HEADS, HEAD_DIM, HEAD_DIM), F32)
    _, s_meta = _mixer(pad(m_q), pad(m_k), pad(m_g), pad(m_v), pad(m_gs), pad(m_p), zero_state,
                       jnp.zeros((SUB, POOL_WIDTH), F32), wpool_bf, ps, 1, MIX_BLOCK)

    p_xn, p_q, p_k, p_g, p_v, p_gs, p_p = proj(x_prompt.reshape(nseq * seqlen, D_MODEL))
    p_mix, s_prompt = _mixer(p_q, p_k, p_g, p_v, p_gs, p_p, s_meta[0], m_p, wpool_bf, ps, nseq, seqlen)

    d_xn, d_q, d_k, d_g, d_v, d_gs, d_p = proj(x_sample.reshape(ndec, D_MODEL))
    cols = lambda a: a.reshape(ndec // STEP_SEQS, STEP_SEQS, HGRN_HEADS, HEAD_DIM).transpose(2, 0, 3, 1)
    s_dec, d_oa = _mixer_step(cols(d_q), cols(d_k), cols(d_g), d_v, d_gs, state_hgrn[l])
    d_ob = _pool_step(state_pool[l].transpose(1, 0, 2), d_p, wpool_bf, ps)
    d_mix = jnp.concatenate([d_oa, d_ob], axis=1).astype(BF16)

    mix = jnp.concatenate([p_mix, d_mix], axis=0)
    xn = jnp.concatenate([p_xn, d_xn], axis=0)
    n_tok = mix.shape[0]
    x1, x1p, idx, gate, cnt = _outproj(mix, xn, w_out[l].astype(BF16), row(ln1_g[l]), row(ln1_b[l]),
                                       w_router[l].astype(BF16), row(b_router[l]))
    plan = _route_plan(idx[:, :TOP_K], cnt[0].astype(I32), n_tok)
    xp = x1p.reshape(n_tok * 4, 128)
    yt = _moe(*plan, xp, w_gate_e[l], w_up_e[l], w_down_e[l], n_tok)
    yt3 = yt.reshape(n_tok + 2, TOP_K, D_MODEL)
    comb = functools.partial(_combine, x1, yt3, gate, w_gate_s[l].astype(BF16), w_up_s[l].astype(BF16),
                             w_down_s[l].astype(BF16), row(ln2_g[l]), row(ln2_b[l]))
    y_prompt = comb(0, nseq * seqlen).reshape(nseq, seqlen, D_MODEL)
    y_sample = comb(nseq * seqlen, ndec).reshape(ndec, 1, D_MODEL)

    state_pool_prompt = p_p.reshape(nseq, seqlen, POOL_WIDTH)[:, seqlen - POOL_BUF:, :]
    state_pool_sample = jnp.concatenate([state_pool[l][:, 1:, :], d_p[:, None, :]], axis=1)
    return (y_prompt, y_sample, s_prompt[None], state_pool_prompt[None], s_dec[None], state_pool_sample[None])
```

```python
import functools

import jax
import jax.numpy as jnp
from jax import lax
from jax.experimental import pallas as pl
from jax.experimental.pallas import tpu as pltpu

F32 = jnp.float32
BF16 = jnp.bfloat16
I32 = jnp.int32

D_MODEL = 1024
N_META = 16
HGRN_WIDTH = 512
HGRN_HEADS = 4
HEAD_DIM = 128
POOL_WIDTH = 512
POOL_WINDOWS = (2, 4, 8, 16)
POOL_GROUP_DIM = 128
POOL_BUF = 15
IN_WIDTH = 4 * HGRN_WIDTH + POOL_WIDTH
N_EXPERTS = 256
TOP_K = 8
D_EXPERT = 256
ROUTED_SCALE = 2.5
DEPTH = 1
ALPHA = (2 * DEPTH) ** 0.25
LN_EPS = 1e-5
RMS_EPS = 1e-6

SUB = 16
MIX_BLOCK = 128
MOE_ROWS = 256
TILE_STRIDE = MOE_ROWS + 8
ORDER_ALIGN = 128
ROW_CHUNKS = D_MODEL // 128
VMEM_LIMIT = 48 * 1024 * 1024


def _ln(x, g, b):
    mu = jnp.mean(x, axis=-1, keepdims=True)
    xc = x - mu
    var = jnp.mean(xc * xc, axis=-1, keepdims=True)
    return xc * lax.rsqrt(var + LN_EPS) * g + b


def _sigmoid(z):
    return 1.0 / (1.0 + jnp.exp(-z))


def _bdot(a, b):
    return jnp.dot(a, b, preferred_element_type=F32)


def _ln_proj_kernel(x_ref, g_ref, b_ref, w_ref, lbl_ref, ng_ref,
                    xn_ref, q_ref, k_ref, gl_ref, v_ref, gs_ref, p_ref):
    xn = _ln(x_ref[...], g_ref[...], b_ref[...])
    xn_ref[...] = xn
    proj = _bdot(xn.astype(BF16), w_ref[...])
    lbl = lbl_ref[...]
    e = jnp.exp(lbl - jnp.max(lbl, axis=0, keepdims=True))
    lb = e[0:1] / jnp.sum(e, axis=0, keepdims=True)
    W = HGRN_WIDTH
    q = proj[:, 0:W]
    f = proj[:, W:2 * W]
    q_ref[...] = q * _sigmoid(q)
    k_ref[...] = (1.0 - lb) * _sigmoid(-f)
    gl_ref[...] = jnp.log(lb + (1.0 - lb) * _sigmoid(f))
    v_ref[...] = proj[:, 2 * W:3 * W]
    g = proj[:, 3 * W:4 * W]
    gs_ref[...] = ng_ref[...] * (g * _sigmoid(g))
    p_ref[...] = proj[:, 4 * W:]


def _row_tile(n_rows):
    for tm in (256, 128, 64, 32, 16, 8):
        if n_rows % tm == 0:
            return tm
    raise ValueError(f"row count {n_rows} is not a multiple of 8")


def _ln_proj(x, ln_g, ln_b, w_in_bf, lb_logits, norm_g):
    T = x.shape[0]
    tm = _row_tile(T)
    row = lambda i: (i, 0)
    const = lambda i: (0, 0)
    outs = [jax.ShapeDtypeStruct((T, D_MODEL), F32)] + [jax.ShapeDtypeStruct((T, HGRN_WIDTH), F32)] * 6
    return pl.pallas_call(
        _ln_proj_kernel,
        out_shape=outs,
        grid=(T // tm,),
        in_specs=[
            pl.BlockSpec((tm, D_MODEL), row),
            pl.BlockSpec((1, D_MODEL), const),
            pl.BlockSpec((1, D_MODEL), const),
            pl.BlockSpec((D_MODEL, IN_WIDTH), const),
            pl.BlockSpec((DEPTH + 1, HGRN_WIDTH), const),
            pl.BlockSpec((1, HGRN_WIDTH), const),
        ],
        out_specs=[pl.BlockSpec((tm, D_MODEL), row)] + [pl.BlockSpec((tm, HGRN_WIDTH), row)] * 6,
        compiler_params=pltpu.CompilerParams(dimension_semantics=("arbitrary",), vmem_limit_bytes=VMEM_LIMIT),
        name="ln_proj",
    )(x, ln_g, ln_b, w_in_bf, lb_logits, norm_g)


def _pool_group(pe, p_cur, gi, w):
    sl = slice(gi * POOL_GROUP_DIM, (gi + 1) * POOL_GROUP_DIM)
    s = pe[:, sl]
    sh = 1
    while sh < w:
        s = s + pltpu.roll(s, sh, 0)
        sh *= 2
    return s[SUB:, :] * (1.0 / w) - p_cur[:, sl]


def _mixer_kernel(q_ref, k_ref, g_ref, v_ref, gs_ref, p_ref, s0_ref, pp0_ref, wpool_ref, pscale_ref,
                  mix_ref, sfin_ref, st_scr, pe_scr):
    i = pl.program_id(1)
    nblk = pl.num_programs(1)
    n = MIX_BLOCK

    @pl.when(i == 0)
    def _():
        for h in range(HGRN_HEADS):
            st_scr[h] = s0_ref[h].T
        pe_scr[0:SUB, :] = pp0_ref[...]

    rows = lax.broadcasted_iota(I32, (n, HEAD_DIM), 0)
    r16 = rows & (SUB - 1)
    t8 = lax.broadcasted_iota(I32, (8, HEAD_DIM), 0)
    zero_bf = jnp.zeros((SUB, HEAD_DIM), BF16)

    o_heads = []
    for h in range(HGRN_HEADS):
        hs = slice(h * HEAD_DIM, (h + 1) * HEAD_DIM)
        Q = q_ref[:, hs]
        K = k_ref[:, hs]
        G = g_ref[:, hs]
        V = v_ref[:, hs]
        bf = G
        br = G
        for sh in (1, 2, 4, 8):
            bf = bf + jnp.where(r16 >= sh, pltpu.roll(bf, sh, 0), 0.0)
            br = br + jnp.where(r16 < SUB - sh, pltpu.roll(br, n - sh, 0), 0.0)
        br = br - G
        qt = (Q * jnp.exp(bf)).astype(BF16)
        kt = (K * jnp.exp(br)).astype(BF16)
        vt = V.T.astype(BF16)
        st = st_scr[h]
        o_parts = []
        for c in range(n // SUB):
            r0 = c * SUB
            b_top, b_bot = bf[r0:r0 + 8], bf[r0 + 8:r0 + 16]
            q_top, q_bot = Q[r0:r0 + 8], Q[r0 + 8:r0 + 16]
            acc_top = jnp.zeros((8, HEAD_DIM), F32)
            acc_bot = jnp.zeros((8, HEAD_DIM), F32)
            for s in range(SUB):
                bs = bf[r0 + s:r0 + s + 1]
                ks = K[r0 + s:r0 + s + 1]
                vs = V[r0 + s:r0 + s + 1]
                if s < 8:
                    col = jnp.sum(q_top * jnp.exp(b_top - bs) * ks, axis=-1, keepdims=True)
                    col = jnp.where(t8[:, 0:1] >= s, col, 0.0)
                    acc_top = acc_top + col * vs
                    col = jnp.sum(q_bot * jnp.exp(b_bot - bs) * ks, axis=-1, keepdims=True)
                    acc_bot = acc_bot + col * vs
                else:
                    col = jnp.sum(q_bot * jnp.exp(b_bot - bs) * ks, axis=-1, keepdims=True)
                    col = jnp.where(t8[:, 0:1] + 8 >= s, col, 0.0)
                    acc_bot = acc_bot + col * vs
            o_diag = jnp.concatenate([acc_top, acc_bot], axis=0)
            o_inter = lax.dot_general(qt[r0:r0 + SUB], st.astype(BF16), (((1,), (1,)), ((), ())),
                                      preferred_element_type=F32)
            o_parts.append(o_inter + o_diag)
            kmask = jnp.concatenate([zero_bf] * c + [kt[r0:r0 + SUB]] + [zero_bf] * (n // SUB - 1 - c), axis=0)
            d_st = _bdot(vt, kmask)
            st = st * jnp.exp(bf[r0 + SUB - 1:r0 + SUB]) + d_st
        st_scr[h] = st
        o = jnp.concatenate(o_parts, axis=0)
        o = o * lax.rsqrt(jnp.mean(o * o, axis=-1, keepdims=True) + RMS_EPS)
        o_heads.append(o * gs_ref[:, hs])

    p_cur = p_ref[...]
    pe_scr[SUB:SUB + n, :] = p_cur
    pe = pe_scr[...]
    ob = []
    for gi, w in enumerate(POOL_WINDOWS):
        pooled = _pool_group(pe, p_cur, gi, w)
        sl = slice(gi * POOL_GROUP_DIM, (gi + 1) * POOL_GROUP_DIM)
        ob.append(_bdot(pooled.astype(BF16), wpool_ref[gi]) * pscale_ref[:, sl])
    pe_scr[0:SUB, :] = p_cur[n - SUB:, :]

    mix_ref[...] = jnp.concatenate(o_heads + ob, axis=1).astype(BF16)

    @pl.when(i == nblk - 1)
    def _():
        for h in range(HGRN_HEADS):
            sfin_ref[0, h] = st_scr[h].T


def _mixer(q, k, g, v, gs, p, s0, pp0, wpool_bf, pscale, nseq, seqlen):
    nblk = seqlen // MIX_BLOCK
    tok = lambda b, i: (b * nblk + i, 0)
    tspec = pl.BlockSpec((MIX_BLOCK, HGRN_WIDTH), tok)
    return pl.pallas_call(
        _mixer_kernel,
        out_shape=[jax.ShapeDtypeStruct((nseq * seqlen, D_MODEL), BF16),
                   jax.ShapeDtypeStruct((nseq, HGRN_HEADS, HEAD_DIM, HEAD_DIM), F32)],
        grid=(nseq, nblk),
        in_specs=[tspec] * 6 + [
            pl.BlockSpec((HGRN_HEADS, HEAD_DIM, HEAD_DIM), lambda b, i: (0, 0, 0)),
            pl.BlockSpec((SUB, POOL_WIDTH), lambda b, i: (0, 0)),
            pl.BlockSpec((len(POOL_WINDOWS), POOL_GROUP_DIM, POOL_GROUP_DIM), lambda b, i: (0, 0, 0)),
            pl.BlockSpec((1, POOL_WIDTH), lambda b, i: (0, 0)),
        ],
        out_specs=[pl.BlockSpec((MIX_BLOCK, D_MODEL), tok),
                   pl.BlockSpec((1, HGRN_HEADS, HEAD_DIM, HEAD_DIM), lambda b, i: (b, 0, 0, 0))],
        scratch_shapes=[pltpu.VMEM((HGRN_HEADS, HEAD_DIM, HEAD_DIM), F32),
                        pltpu.VMEM((SUB + MIX_BLOCK, POOL_WIDTH), F32)],
        compiler_params=pltpu.CompilerParams(dimension_semantics=("arbitrary", "arbitrary"),
                                             vmem_limit_bytes=VMEM_LIMIT),
        name="mixer",
    )(q, k, g, v, gs, p, s0, pp0, wpool_bf, pscale)


STEP_SEQS = 32


def _mixer_step_kernel(qt_ref, kt_ref, gt_ref, v_ref, gs_ref, s_ref, snew_ref, oa_ref):
    qt = qt_ref[0, 0]
    kt = kt_ref[0, 0]
    dt = jnp.exp(gt_ref[0, 0])
    rows = []
    for bb in range(STEP_SEQS):
        sn = s_ref[bb, 0] * dt[:, bb:bb + 1] + kt[:, bb:bb + 1] * v_ref[bb:bb + 1, :]
        snew_ref[bb, 0] = sn
        rows.append(jnp.sum(sn * qt[:, bb:bb + 1], axis=0, keepdims=True))
    o = jnp.concatenate(rows, axis=0)
    o = o * lax.rsqrt(jnp.mean(o * o, axis=-1, keepdims=True) + RMS_EPS)
    oa_ref[...] = o * gs_ref[...]


def _mixer_step(qT, kT, gT, v, gs, state):
    nseq = v.shape[0]
    nbc = nseq // STEP_SEQS
    cspec = pl.BlockSpec((1, 1, HEAD_DIM, STEP_SEQS), lambda h, c: (h, c, 0, 0))
    rspec = pl.BlockSpec((STEP_SEQS, HEAD_DIM), lambda h, c: (c, h))
    sspec = pl.BlockSpec((STEP_SEQS, 1, HEAD_DIM, HEAD_DIM), lambda h, c: (c, h, 0, 0))
    return pl.pallas_call(
        _mixer_step_kernel,
        out_shape=[jax.ShapeDtypeStruct(state.shape, F32), jax.ShapeDtypeStruct((nseq, HGRN_WIDTH), F32)],
        grid=(HGRN_HEADS, nbc),
        in_specs=[cspec, cspec, cspec, rspec, rspec, sspec],
        out_specs=[sspec, rspec],
        compiler_params=pltpu.CompilerParams(dimension_semantics=("arbitrary", "arbitrary"),
                                             vmem_limit_bytes=VMEM_LIMIT),
        name="mixer_step",
    )(qT, kT, gT, v, gs, state)


def _pool_step_kernel(sp_ref, p_ref, wpool_ref, pscale_ref, ob_ref):
    p_cur = p_ref[...]
    outs = []
    for gi, w in enumerate(POOL_WINDOWS):
        sl = slice(gi * POOL_GROUP_DIM, (gi + 1) * POOL_GROUP_DIM)
        s = p_cur[:, sl]
        for r in range(POOL_BUF - (w - 1), POOL_BUF):
            s = s + sp_ref[r][:, sl]
        pooled = s * (1.0 / w) - p_cur[:, sl]
        outs.append(_bdot(pooled.astype(BF16), wpool_ref[gi]) * pscale_ref[:, sl])
    ob_ref[...] = jnp.concatenate(outs, axis=1)


def _pool_step(spT, p, wpool_bf, pscale):
    nseq = p.shape[0]
    return pl.pallas_call(
        _pool_step_kernel,
        out_shape=jax.ShapeDtypeStruct((nseq, POOL_WIDTH), F32),
        name="pool_step",
    )(spT, p, wpool_bf, pscale)


def _outproj_kernel(mix_ref, xn_ref, wout_ref, g1_ref, b1_ref, wr_ref, br_ref,
                    x1_ref, x1p_ref, idx_ref, gate_ref, cnt_ref):
    tm = mix_ref.shape[0]
    mix = _bdot(mix_ref[...], wout_ref[...])
    x1 = _ln(ALPHA * xn_ref[...] + mix, g1_ref[...], b1_ref[...])
    x1_ref[...] = x1
    xb = x1.astype(BF16)
    xr = xb.astype(F32)
    half = D_MODEL // 2
    lo = lax.shift_right_logical(lax.bitcast_convert_type(xr[:, :half], jnp.uint32), jnp.uint32(16))
    hi = lax.bitcast_convert_type(xr[:, half:], jnp.uint32) & jnp.uint32(0xFFFF0000)
    x1p_ref[...] = lax.bitcast_convert_type(hi | lo, I32)

    scores = _sigmoid(_bdot(xb, wr_ref[...]))
    sel = scores + br_ref[...]
    lane = lax.broadcasted_iota(I32, (tm, N_EXPERTS), 1).astype(F32)
    lane_o = lax.broadcasted_iota(I32, (tm, 128), 1)
    idx_o = jnp.zeros((tm, 128), F32)
    ssum = jnp.zeros((tm, 1), F32)
    chosen = jnp.zeros((tm, N_EXPERTS), F32)
    s_sel = []
    for j in range(TOP_K):
        m = jnp.max(sel, axis=-1, keepdims=True)
        am = jnp.min(jnp.where(sel == m, lane, float(N_EXPERTS)), axis=-1, keepdims=True)
        hit = lane == am
        sj = jnp.sum(jnp.where(hit, scores, 0.0), axis=-1, keepdims=True)
        sel = jnp.where(hit, -jnp.inf, sel)
        chosen = jnp.where(hit, 1.0, chosen)
        idx_o = jnp.where(lane_o == j, am, idx_o)
        s_sel.append(sj)
        ssum = ssum + sj
    idx_ref[...] = idx_o.astype(I32)
    for j in range(TOP_K):
        gate_ref[pl.ds(j, tm, stride=TOP_K), :] = jnp.broadcast_to(s_sel[j] / ssum * ROUTED_SCALE, (tm, 128))

    @pl.when(pl.program_id(0) == 0)
    def _():
        cnt_ref[...] = jnp.zeros_like(cnt_ref)

    cnt_ref[...] += jnp.sum(chosen, axis=0, keepdims=True)


def _outproj(mix, xn, wout_bf, g1, b1, wr_bf, br):
    T = mix.shape[0]
    tm = _row_tile(T)
    row = lambda i: (i, 0)
    const = lambda i: (0, 0)
    return pl.pallas_call(
        _outproj_kernel,
        out_shape=[jax.ShapeDtypeStruct((T, D_MODEL), F32), jax.ShapeDtypeStruct((T, D_MODEL // 2), I32),
                   jax.ShapeDtypeStruct((T, 128), I32), jax.ShapeDtypeStruct((T * TOP_K, 128), F32),
                   jax.ShapeDtypeStruct((1, N_EXPERTS), F32)],
        grid=(T // tm,),
        in_specs=[pl.BlockSpec((tm, D_MODEL), row), pl.BlockSpec((tm, D_MODEL), row),
                  pl.BlockSpec((D_MODEL, D_MODEL), const), pl.BlockSpec((1, D_MODEL), const),
                  pl.BlockSpec((1, D_MODEL), const), pl.BlockSpec((D_MODEL, N_EXPERTS), const),
                  pl.BlockSpec((1, N_EXPERTS), const)],
        out_specs=[pl.BlockSpec((tm, D_MODEL), row), pl.BlockSpec((tm, D_MODEL // 2), row),
                   pl.BlockSpec((tm, 128), row), pl.BlockSpec((tm * TOP_K, 128), row),
                   pl.BlockSpec((1, N_EXPERTS), const)],
        compiler_params=pltpu.CompilerParams(dimension_semantics=("arbitrary",), vmem_limit_bytes=VMEM_LIMIT),
        name="outproj",
    )(mix, xn, wout_bf, g1, b1, wr_bf, br)


def _moe_kernel(be_ref, bn_ref, bsrc_ref, bdel_ref, order_ref, xp_ref, wg_ref, wu_ref, wd_ref, yt_ref,
                tile_scr, ybuf, ord_smem, sem, sem_o):
    del be_ref
    b = pl.program_id(0)
    nb = pl.num_programs(0)
    slot = b % 2
    n = bn_ref[b]
    delta = bdel_ref[b]
    R = MOE_ROWS
    C = ROW_CHUNKS
    n_assign = yt_ref.shape[0] // C - 2 * TOP_K

    def fetch(blk, s):
        src = pl.multiple_of(bsrc_ref[blk], ORDER_ALIGN)
        return pltpu.make_async_copy(order_ref.at[pl.ds(src, R + ORDER_ALIGN)], ord_smem.at[s], sem_o.at[s])

    @pl.when(b == 0)
    def _():
        fetch(0, 0).start()

    @pl.when(b + 1 < nb)
    def _():
        fetch(jnp.minimum(b + 1, nb - 1), 1 - slot).start()

    fetch(b, slot).wait()

    def rows_sent(cnt):
        return pl.multiple_of(lax.shift_left(lax.shift_right_logical(cnt + 7, 3), 3), 8)

    def wait_rows(s, cnt):
        pltpu.make_async_copy(ybuf.at[s, pl.ds(0, cnt * C), :], yt_ref.at[pl.ds(0, cnt * C), :], sem.at[s]).wait()

    @pl.when(b == 0)
    def _():
        n_spare = 2 * TOP_K * C
        ybuf[0, 0:n_spare, :] = jnp.zeros((n_spare, 128), F32)
        init = pltpu.make_async_copy(ybuf.at[0, pl.ds(0, n_spare), :],
                                     yt_ref.at[pl.ds(yt_ref.shape[0] - n_spare, n_spare), :], sem.at[0])
        init.start()
        init.wait()

    @pl.when(b >= 2)
    def _():
        sent = rows_sent(bn_ref[jnp.maximum(b - 2, 0)])

        @pl.when(sent > 0)
        def _():
            wait_rows(slot, sent)

    @pl.when(n > 0)
    def _():
        for r in range(R):
            tok = lax.shift_right_logical(ord_smem[slot, delta + r], 3)
            t4 = pl.multiple_of(lax.shift_left(tok, 2), 4)
            tile_scr[pl.ds(r, 4, stride=TILE_STRIDE), :] = xp_ref[pl.ds(t4, 4), :]
        los, his = [], []
        for j in range(4):
            w = lax.bitcast_convert_type(tile_scr[pl.ds(j * TILE_STRIDE, R), :], jnp.uint32)
            los.append(lax.bitcast_convert_type(lax.shift_left(w, jnp.uint32(16)), F32).astype(BF16))
            his.append(lax.bitcast_convert_type(w & jnp.uint32(0xFFFF0000), F32).astype(BF16))
        xg = jnp.concatenate(los + his, axis=1)
        hg = _bdot(xg, wg_ref[0].astype(BF16))
        hu = _bdot(xg, wu_ref[0].astype(BF16))
        hb = (hg * _sigmoid(hg)) * hu
        y = _bdot(hb.astype(BF16), wd_ref[0].astype(BF16))
        yb = ybuf.at[slot]
        for c in range(C):
            yb[pl.ds(c, R, stride=C), :] = y[:, c * 128:(c + 1) * 128]

        def send_group(g, carry):
            for u in range(8):
                r = g * 8 + u
                d = jnp.where(r < n, ord_smem[slot, delta + r], n_assign + slot * 8 + u)
                pltpu.make_async_copy(yb.at[pl.ds(pl.multiple_of(r * C, C), C), :],
                                      yt_ref.at[pl.ds(pl.multiple_of(d * C, C), C), :], sem.at[slot]).start()
            return carry

        lax.fori_loop(0, lax.shift_right_logical(n + 7, 3), send_group, 0)

    @pl.when(b == nb - 1)
    def _():
        sent = rows_sent(n)

        @pl.when(sent > 0)
        def _():
            wait_rows(slot, sent)

        @pl.when(nb >= 2)
        def _():
            sent1 = rows_sent(bn_ref[jnp.maximum(b - 1, 0)])

            @pl.when(sent1 > 0)
            def _():
                wait_rows(1 - slot, sent1)


def _moe(blk_e, blk_n, blk_src, blk_delta, order, xp, wg, wu, wd, n_tok):
    nb = blk_e.shape[0]
    R = MOE_ROWS
    wmap_in = lambda b, be, bn, bs, bd: (be[b], 0, 0)
    grid_spec = pltpu.PrefetchScalarGridSpec(
        num_scalar_prefetch=4,
        grid=(nb,),
        in_specs=[
            pl.BlockSpec(memory_space=pl.ANY),
            pl.BlockSpec(memory_space=pltpu.VMEM),
            pl.BlockSpec((1, D_MODEL, D_EXPERT), wmap_in),
            pl.BlockSpec((1, D_MODEL, D_EXPERT), wmap_in),
            pl.BlockSpec((1, D_EXPERT, D_MODEL), wmap_in),
        ],
        out_specs=pl.BlockSpec(memory_space=pl.ANY),
        scratch_shapes=[pltpu.VMEM((4 * TILE_STRIDE, 128), I32),
                        pltpu.VMEM((2, R * ROW_CHUNKS, 128), F32),
                        pltpu.SMEM((2, R + ORDER_ALIGN), I32),
                        pltpu.SemaphoreType.DMA((2,)),
                        pltpu.SemaphoreType.DMA((2,))],
    )
    return pl.pallas_call(
        _moe_kernel,
        out_shape=jax.ShapeDtypeStruct(((n_tok + 2) * TOP_K * ROW_CHUNKS, 128), F32),
        grid_spec=grid_spec,
        compiler_params=pltpu.CompilerParams(dimension_semantics=("arbitrary",),
                                             vmem_limit_bytes=58 * 1024 * 1024),
        name="moe",
    )(blk_e, blk_n, blk_src, blk_delta, order, xp, wg, wu, wd)


def _route_plan(idx, counts, n_tok):
    R = MOE_ROWS
    n_assign = n_tok * TOP_K
    nb = (n_assign + N_EXPERTS * (R - 1)) // R
    e_flat = idx.reshape(-1)
    _, order = lax.sort((e_flat, jnp.arange(n_assign, dtype=I32)), num_keys=1)
    order = jnp.concatenate([order, jnp.zeros((R + ORDER_ALIGN,), I32)])
    nblk_e = (counts + R - 1) // R
    bend = jnp.cumsum(nblk_e)
    bstart = bend - nblk_e
    cstart = jnp.cumsum(counts) - counts
    blk = jnp.arange(nb, dtype=I32)
    blk_e = jnp.minimum(jnp.sum((bend[None, :] <= blk[:, None]).astype(I32), axis=1), N_EXPERTS - 1)
    k = blk - bstart[blk_e]
    active = blk < bend[-1]
    blk_n = jnp.where(active, jnp.clip(counts[blk_e] - k * R, 0, R), 0).astype(I32)
    src = jnp.where(active, cstart[blk_e] + k * R, 0).astype(I32)
    blk_src = (src // ORDER_ALIGN) * ORDER_ALIGN
    return blk_e.astype(I32), blk_n, blk_src, src - blk_src, order


def _combine_kernel(x1_ref, yt_ref, gate_ref, wgs_ref, wus_ref, wds_ref, g2_ref, b2_ref, out_ref, r_scr):
    tm = x1_ref.shape[0]
    C = ROW_CHUNKS
    x1 = x1_ref[...]
    xb = x1.astype(BF16)
    hg = _bdot(xb, wgs_ref[...])
    hs = (hg * _sigmoid(hg)) * _bdot(xb, wus_ref[...])
    moe = _bdot(hs.astype(BF16), wds_ref[...])
    routed = yt_ref[:, 0] * gate_ref[:, 0:1, :]
    for j in range(1, TOP_K):
        routed = routed + yt_ref[:, j] * gate_ref[:, j:j + 1, :]
    r_scr[...] = routed.reshape(tm * C, 128)
    routed = jnp.concatenate([r_scr[pl.ds(c, tm, stride=C), :] for c in range(C)], axis=1)
    out_ref[...] = _ln(ALPHA * x1 + (moe + routed), g2_ref[...], b2_ref[...])


def _combine(x1, yt4, gate3, wgs_bf, wus_bf, wds_bf, g2, b2, row0, nrows):
    tm = 128
    assert nrows % tm == 0 and row0 % tm == 0
    off = row0 // tm
    row = lambda i: (i + off, 0)
    const = lambda i: (0, 0)
    return pl.pallas_call(
        _combine_kernel,
        out_shape=jax.ShapeDtypeStruct((nrows, D_MODEL), F32),
        grid=(nrows // tm,),
        in_specs=[pl.BlockSpec((tm, D_MODEL), row),
                  pl.BlockSpec((tm, TOP_K, ROW_CHUNKS, 128), lambda i: (i + off, 0, 0, 0)),
                  pl.BlockSpec((tm, TOP_K, 128), lambda i: (i + off, 0, 0)),
                  pl.BlockSpec((D_MODEL, D_EXPERT), const), pl.BlockSpec((D_MODEL, D_EXPERT), const),
                  pl.BlockSpec((D_EXPERT, D_MODEL), const),
                  pl.BlockSpec((1, D_MODEL), const), pl.BlockSpec((1, D_MODEL), const)],
        out_specs=pl.BlockSpec((tm, D_MODEL), lambda i: (i, 0)),
        scratch_shapes=[pltpu.VMEM((tm * ROW_CHUNKS, 128), F32)],
        compiler_params=pltpu.CompilerParams(dimension_semantics=("arbitrary",), vmem_limit_bytes=VMEM_LIMIT),
        name="combine",
    )(x1, yt4, gate3, wgs_bf, wus_bf, wds_bf, g2, b2)


def kernel(x_prompt, x_sample, state_hgrn, state_pool, meta_tokens, ln_emb_g, ln_emb_b, w_in, lb_logits, hgrn_norm_g, w_pool, pool_scale, w_out, ln1_g, ln1_b, w_router, b_router, w_gate_e, w_up_e, w_down_e, w_gate_s, w_up_s, w_down_s, ln2_g, ln2_b):
    nseq, seqlen, _ = x_prompt.shape
    ndec = x_sample.shape[0]
    l = 0
    row = lambda a: a.reshape(1, -1)
    w_in_bf = w_in[l].astype(BF16)
    wpool_bf = w_pool[l].astype(BF16)
    lng, lnb = row(ln_emb_g), row(ln_emb_b)
    ng, ps = row(hgrn_norm_g[l]), row(pool_scale[l])
    proj = functools.partial(_ln_proj, ln_g=lng, ln_b=lnb, w_in_bf=w_in_bf, lb_logits=lb_logits, norm_g=ng)

    m_xn, m_q, m_k, m_g, m_v, m_gs, m_p = proj(meta_tokens)
    pad = lambda a: jnp.pad(a, ((0, MIX_BLOCK - N_META), (0, 0)))
    zero_state = jnp.zeros((HGRN_HEADS, HEAD_DIM, HEAD_DIM), F32)
    _, s_meta = _mixer(pad(m_q), pad(m_k), pad(m_g), pad(m_v), pad(m_gs), pad(m_p), zero_state,
                       jnp.zeros((SUB, POOL_WIDTH), F32), wpool_bf, ps, 1, MIX_BLOCK)

    p_xn, p_q, p_k, p_g, p_v, p_gs, p_p = proj(x_prompt.reshape(nseq * seqlen, D_MODEL))
    p_mix, s_prompt = _mixer(p_q, p_k, p_g, p_v, p_gs, p_p, s_meta[0], m_p, wpool_bf, ps, nseq, seqlen)

    d_xn, d_q, d_k, d_g, d_v, d_gs, d_p = proj(x_sample.reshape(ndec, D_MODEL))
    cols = lambda a: a.reshape(ndec // STEP_SEQS, STEP_SEQS, HGRN_HEADS, HEAD_DIM).transpose(2, 0, 3, 1)
    s_dec, d_oa = _mixer_step(cols(d_q), cols(d_k), cols(d_g), d_v, d_gs, state_hgrn[l])
    d_ob = _pool_step(state_pool[l].transpose(1, 0, 2), d_p, wpool_bf, ps)
    d_mix = jnp.concatenate([d_oa, d_ob], axis=1).astype(BF16)

    mix = jnp.concatenate([p_mix, d_mix], axis=0)
    xn = jnp.concatenate([p_xn, d_xn], axis=0)
    n_tok = mix.shape[0]
    x1, x1p, idx, gate, cnt = _outproj(mix, xn, w_out[l].astype(BF16), row(ln1_g[l]), row(ln1_b[l]),
                                       w_router[l].astype(BF16), row(b_router[l]))
    plan = _route_plan(idx[:, :TOP_K], cnt[0].astype(I32), n_tok)
    xp = x1p.reshape(n_tok * 4, 128)
    yt = _moe(*plan, xp, w_gate_e[l], w_up_e[l], w_down_e[l], n_tok)
    yt4 = yt.reshape(n_tok + 2, TOP_K, ROW_CHUNKS, 128)
    gate3 = gate.reshape(n_tok, TOP_K, 128)
    comb = functools.partial(_combine, x1, yt4, gate3, w_gate_s[l].astype(BF16), w_up_s[l].astype(BF16),
                             w_down_s[l].astype(BF16), row(ln2_g[l]), row(ln2_b[l]))
    y_prompt = comb(0, nseq * seqlen).reshape(nseq, seqlen, D_MODEL)
    y_sample = comb(nseq * seqlen, ndec).reshape(ndec, 1, D_MODEL)

    state_pool_prompt = p_p.reshape(nseq, seqlen, POOL_WIDTH)[:, seqlen - POOL_BUF:, :]
    state_pool_sample = jnp.concatenate([state_pool[l][:, 1:, :], d_p[:, None, :]], axis=1)
    return (y_prompt, y_sample, s_prompt[None], state_pool_prompt[None], s_dec[None], state_pool_sample[None])
```

```python
import functools

import jax
import jax.numpy as jnp
from jax import lax
from jax.experimental import pallas as pl
from jax.experimental.pallas import tpu as pltpu

F32 = jnp.float32
BF16 = jnp.bfloat16
I32 = jnp.int32

D_MODEL = 1024
N_META = 16
HGRN_WIDTH = 512
HGRN_HEADS = 4
HEAD_DIM = 128
POOL_WIDTH = 512
POOL_WINDOWS = (2, 4, 8, 16)
POOL_GROUP_DIM = 128
POOL_BUF = 15
IN_WIDTH = 4 * HGRN_WIDTH + POOL_WIDTH
N_EXPERTS = 256
TOP_K = 8
D_EXPERT = 256
ROUTED_SCALE = 2.5
DEPTH = 1
ALPHA = (2 * DEPTH) ** 0.25
LN_EPS = 1e-5
RMS_EPS = 1e-6

SUB = 16
MIX_BLOCK = 128
MOE_ROWS = 256
TILE_STRIDE = MOE_ROWS + 8
ORDER_ALIGN = 128
ROW_CHUNKS = D_MODEL // 128
VMEM_LIMIT = 48 * 1024 * 1024


def _ln(x, g, b):
    mu = jnp.mean(x, axis=-1, keepdims=True)
    xc = x - mu
    var = jnp.mean(xc * xc, axis=-1, keepdims=True)
    return xc * lax.rsqrt(var + LN_EPS) * g + b


def _sigmoid(z):
    return 1.0 / (1.0 + jnp.exp(-z))


def _bdot(a, b):
    return jnp.dot(a, b, preferred_element_type=F32)


def _ln_proj_kernel(x_ref, g_ref, b_ref, w_ref, lbl_ref, ng_ref,
                    xn_ref, q_ref, k_ref, gl_ref, v_ref, gs_ref, p_ref):
    xn = _ln(x_ref[...], g_ref[...], b_ref[...])
    xn_ref[...] = xn
    proj = _bdot(xn.astype(BF16), w_ref[...])
    lbl = lbl_ref[...]
    e = jnp.exp(lbl - jnp.max(lbl, axis=0, keepdims=True))
    lb = e[0:1] / jnp.sum(e, axis=0, keepdims=True)
    W = HGRN_WIDTH
    q = proj[:, 0:W]
    f = proj[:, W:2 * W]
    q_ref[...] = q * _sigmoid(q)
    k_ref[...] = (1.0 - lb) * _sigmoid(-f)
    gl_ref[...] = jnp.log(lb + (1.0 - lb) * _sigmoid(f))
    v_ref[...] = proj[:, 2 * W:3 * W]
    g = proj[:, 3 * W:4 * W]
    gs_ref[...] = ng_ref[...] * (g * _sigmoid(g))
    p_ref[...] = proj[:, 4 * W:]


def _row_tile(n_rows):
    for tm in (256, 128, 64, 32, 16, 8):
        if n_rows % tm == 0:
            return tm
    raise ValueError(f"row count {n_rows} is not a multiple of 8")


def _ln_proj(x, ln_g, ln_b, w_in_bf, lb_logits, norm_g):
    T = x.shape[0]
    tm = _row_tile(T)
    row = lambda i: (i, 0)
    const = lambda i: (0, 0)
    outs = [jax.ShapeDtypeStruct((T, D_MODEL), F32)] + [jax.ShapeDtypeStruct((T, HGRN_WIDTH), F32)] * 6
    return pl.pallas_call(
        _ln_proj_kernel,
        out_shape=outs,
        grid=(T // tm,),
        in_specs=[
            pl.BlockSpec((tm, D_MODEL), row),
            pl.BlockSpec((1, D_MODEL), const),
            pl.BlockSpec((1, D_MODEL), const),
            pl.BlockSpec((D_MODEL, IN_WIDTH), const),
            pl.BlockSpec((DEPTH + 1, HGRN_WIDTH), const),
            pl.BlockSpec((1, HGRN_WIDTH), const),
        ],
        out_specs=[pl.BlockSpec((tm, D_MODEL), row)] + [pl.BlockSpec((tm, HGRN_WIDTH), row)] * 6,
        compiler_params=pltpu.CompilerParams(dimension_semantics=("arbitrary",), vmem_limit_bytes=VMEM_LIMIT),
        name="ln_proj",
    )(x, ln_g, ln_b, w_in_bf, lb_logits, norm_g)


def _pool_group(pe, p_cur, gi, w):
    sl = slice(gi * POOL_GROUP_DIM, (gi + 1) * POOL_GROUP_DIM)
    s = pe[:, sl]
    sh = 1
    while sh < w:
        s = s + pltpu.roll(s, sh, 0)
        sh *= 2
    return s[SUB:, :] * (1.0 / w) - p_cur[:, sl]


def _mixer_kernel(q_ref, k_ref, g_ref, v_ref, gs_ref, p_ref, s0_ref, pp0_ref, wpool_ref, pscale_ref,
                  mix_ref, sfin_ref, st_scr, pe_scr):
    i = pl.program_id(1)
    nblk = pl.num_programs(1)
    n = MIX_BLOCK

    @pl.when(i == 0)
    def _():
        for h in range(HGRN_HEADS):
            st_scr[h] = s0_ref[h].T
        pe_scr[0:SUB, :] = pp0_ref[...]

    rows = lax.broadcasted_iota(I32, (n, HEAD_DIM), 0)
    r16 = rows & (SUB - 1)
    t8 = lax.broadcasted_iota(I32, (8, HEAD_DIM), 0)
    zero_bf = jnp.zeros((SUB, HEAD_DIM), BF16)

    o_heads = []
    for h in range(HGRN_HEADS):
        hs = slice(h * HEAD_DIM, (h + 1) * HEAD_DIM)
        Q = q_ref[:, hs]
        K = k_ref[:, hs]
        G = g_ref[:, hs]
        V = v_ref[:, hs]
        bf = G
        br = G
        for sh in (1, 2, 4, 8):
            bf = bf + jnp.where(r16 >= sh, pltpu.roll(bf, sh, 0), 0.0)
            br = br + jnp.where(r16 < SUB - sh, pltpu.roll(br, n - sh, 0), 0.0)
        br = br - G
        qt = (Q * jnp.exp(bf)).astype(BF16)
        kt = (K * jnp.exp(br)).astype(BF16)
        vt = V.T.astype(BF16)
        st = st_scr[h]
        o_parts = []
        for c in range(n // SUB):
            r0 = c * SUB
            b_top, b_bot = bf[r0:r0 + 8], bf[r0 + 8:r0 + 16]
            q_top, q_bot = Q[r0:r0 + 8], Q[r0 + 8:r0 + 16]
            acc_top = jnp.zeros((8, HEAD_DIM), F32)
            acc_bot = jnp.zeros((8, HEAD_DIM), F32)
            for s in range(SUB):
                bs = bf[r0 + s:r0 + s + 1]
                ks = K[r0 + s:r0 + s + 1]
                vs = V[r0 + s:r0 + s + 1]
                if s < 8:
                    col = jnp.sum(q_top * jnp.exp(b_top - bs) * ks, axis=-1, keepdims=True)
                    col = jnp.where(t8[:, 0:1] >= s, col, 0.0)
                    acc_top = acc_top + col * vs
                    col = jnp.sum(q_bot * jnp.exp(b_bot - bs) * ks, axis=-1, keepdims=True)
                    acc_bot = acc_bot + col * vs
                else:
                    col = jnp.sum(q_bot * jnp.exp(b_bot - bs) * ks, axis=-1, keepdims=True)
                    col = jnp.where(t8[:, 0:1] + 8 >= s, col, 0.0)
                    acc_bot = acc_bot + col * vs
            o_diag = jnp.concatenate([acc_top, acc_bot], axis=0)
            o_inter = lax.dot_general(qt[r0:r0 + SUB], st.astype(BF16), (((1,), (1,)), ((), ())),
                                      preferred_element_type=F32)
            o_parts.append(o_inter + o_diag)
            kmask = jnp.concatenate([zero_bf] * c + [kt[r0:r0 + SUB]] + [zero_bf] * (n // SUB - 1 - c), axis=0)
            d_st = _bdot(vt, kmask)
            st = st * jnp.exp(bf[r0 + SUB - 1:r0 + SUB]) + d_st
        st_scr[h] = st
        o = jnp.concatenate(o_parts, axis=0)
        o = o * lax.rsqrt(jnp.mean(o * o, axis=-1, keepdims=True) + RMS_EPS)
        o_heads.append(o * gs_ref[:, hs])

    p_cur = p_ref[...]
    pe_scr[SUB:SUB + n, :] = p_cur
    pe = pe_scr[...]
    ob = []
    for gi, w in enumerate(POOL_WINDOWS):
        pooled = _pool_group(pe, p_cur, gi, w)
        sl = slice(gi * POOL_GROUP_DIM, (gi + 1) * POOL_GROUP_DIM)
        ob.append(_bdot(pooled.astype(BF16), wpool_ref[gi]) * pscale_ref[:, sl])
    pe_scr[0:SUB, :] = p_cur[n - SUB:, :]

    mix_ref[...] = jnp.concatenate(o_heads + ob, axis=1).astype(BF16)

    @pl.when(i == nblk - 1)
    def _():
        for h in range(HGRN_HEADS):
            sfin_ref[0, h] = st_scr[h].T


def _mixer(q, k, g, v, gs, p, s0, pp0, wpool_bf, pscale, nseq, seqlen):
    nblk = seqlen // MIX_BLOCK
    tok = lambda b, i: (b * nblk + i, 0)
    tspec = pl.BlockSpec((MIX_BLOCK, HGRN_WIDTH), tok)
    return pl.pallas_call(
        _mixer_kernel,
        out_shape=[jax.ShapeDtypeStruct((nseq * seqlen, D_MODEL), BF16),
                   jax.ShapeDtypeStruct((nseq, HGRN_HEADS, HEAD_DIM, HEAD_DIM), F32)],
        grid=(nseq, nblk),
        in_specs=[tspec] * 6 + [
            pl.BlockSpec((HGRN_HEADS, HEAD_DIM, HEAD_DIM), lambda b, i: (0, 0, 0)),
            pl.BlockSpec((SUB, POOL_WIDTH), lambda b, i: (0, 0)),
            pl.BlockSpec((len(POOL_WINDOWS), POOL_GROUP_DIM, POOL_GROUP_DIM), lambda b, i: (0, 0, 0)),
            pl.BlockSpec((1, POOL_WIDTH), lambda b, i: (0, 0)),
        ],
        out_specs=[pl.BlockSpec((MIX_BLOCK, D_MODEL), tok),
                   pl.BlockSpec((1, HGRN_HEADS, HEAD_DIM, HEAD_DIM), lambda b, i: (b, 0, 0, 0))],
        scratch_shapes=[pltpu.VMEM((HGRN_HEADS, HEAD_DIM, HEAD_DIM), F32),
                        pltpu.VMEM((SUB + MIX_BLOCK, POOL_WIDTH), F32)],
        compiler_params=pltpu.CompilerParams(dimension_semantics=("arbitrary", "arbitrary"),
                                             vmem_limit_bytes=VMEM_LIMIT),
        name="mixer",
    )(q, k, g, v, gs, p, s0, pp0, wpool_bf, pscale)


STEP_SEQS = 32


def _mixer_step_kernel(qt_ref, kt_ref, gt_ref, v_ref, gs_ref, s_ref, snew_ref, oa_ref):
    qt = qt_ref[0, 0]
    kt = kt_ref[0, 0]
    dt = jnp.exp(gt_ref[0, 0])
    rows = []
    for bb in range(STEP_SEQS):
        sn = s_ref[bb, 0] * dt[:, bb:bb + 1] + kt[:, bb:bb + 1] * v_ref[bb:bb + 1, :]
        snew_ref[bb, 0] = sn
        rows.append(jnp.sum(sn * qt[:, bb:bb + 1], axis=0, keepdims=True))
    o = jnp.concatenate(rows, axis=0)
    o = o * lax.rsqrt(jnp.mean(o * o, axis=-1, keepdims=True) + RMS_EPS)
    oa_ref[...] = o * gs_ref[...]


def _mixer_step(qT, kT, gT, v, gs, state):
    nseq = v.shape[0]
    nbc = nseq // STEP_SEQS
    cspec = pl.BlockSpec((1, 1, HEAD_DIM, STEP_SEQS), lambda h, c: (h, c, 0, 0))
    rspec = pl.BlockSpec((STEP_SEQS, HEAD_DIM), lambda h, c: (c, h))
    sspec = pl.BlockSpec((STEP_SEQS, 1, HEAD_DIM, HEAD_DIM), lambda h, c: (c, h, 0, 0))
    return pl.pallas_call(
        _mixer_step_kernel,
        out_shape=[jax.ShapeDtypeStruct(state.shape, F32), jax.ShapeDtypeStruct((nseq, HGRN_WIDTH), F32)],
        grid=(HGRN_HEADS, nbc),
        in_specs=[cspec, cspec, cspec, rspec, rspec, sspec],
        out_specs=[sspec, rspec],
        compiler_params=pltpu.CompilerParams(dimension_semantics=("arbitrary", "arbitrary"),
                                             vmem_limit_bytes=VMEM_LIMIT),
        name="mixer_step",
    )(qT, kT, gT, v, gs, state)


def _pool_step_kernel(sp_ref, p_ref, wpool_ref, pscale_ref, ob_ref):
    p_cur = p_ref[...]
    outs = []
    for gi, w in enumerate(POOL_WINDOWS):
        sl = slice(gi * POOL_GROUP_DIM, (gi + 1) * POOL_GROUP_DIM)
        s = p_cur[:, sl]
        for r in range(POOL_BUF - (w - 1), POOL_BUF):
            s = s + sp_ref[r][:, sl]
        pooled = s * (1.0 / w) - p_cur[:, sl]
        outs.append(_bdot(pooled.astype(BF16), wpool_ref[gi]) * pscale_ref[:, sl])
    ob_ref[...] = jnp.concatenate(outs, axis=1)


def _pool_step(spT, p, wpool_bf, pscale):
    nseq = p.shape[0]
    return pl.pallas_call(
        _pool_step_kernel,
        out_shape=jax.ShapeDtypeStruct((nseq, POOL_WIDTH), F32),
        name="pool_step",
    )(spT, p, wpool_bf, pscale)


def _outproj_kernel(mix_ref, xn_ref, wout_ref, g1_ref, b1_ref, wr_ref, br_ref,
                    x1_ref, x1p_ref, idx_ref, gate_ref, cnt_ref):
    tm = mix_ref.shape[0]
    mix = _bdot(mix_ref[...], wout_ref[...])
    x1 = _ln(ALPHA * xn_ref[...] + mix, g1_ref[...], b1_ref[...])
    x1_ref[...] = x1
    xb = x1.astype(BF16)
    xr = xb.astype(F32)
    half = D_MODEL // 2
    lo = lax.shift_right_logical(lax.bitcast_convert_type(xr[:, :half], jnp.uint32), jnp.uint32(16))
    hi = lax.bitcast_convert_type(xr[:, half:], jnp.uint32) & jnp.uint32(0xFFFF0000)
    x1p_ref[...] = lax.bitcast_convert_type(hi | lo, I32)

    scores = _sigmoid(_bdot(xb, wr_ref[...]))
    sel = scores + br_ref[...]
    lane = lax.broadcasted_iota(I32, (tm, N_EXPERTS), 1).astype(F32)
    lane_o = lax.broadcasted_iota(I32, (tm, 128), 1)
    idx_o = jnp.zeros((tm, 128), F32)
    ssum = jnp.zeros((tm, 1), F32)
    chosen = jnp.zeros((tm, N_EXPERTS), F32)
    s_sel = []
    for j in range(TOP_K):
        m = jnp.max(sel, axis=-1, keepdims=True)
        am = jnp.min(jnp.where(sel == m, lane, float(N_EXPERTS)), axis=-1, keepdims=True)
        hit = lane == am
        sj = jnp.sum(jnp.where(hit, scores, 0.0), axis=-1, keepdims=True)
        sel = jnp.where(hit, -jnp.inf, sel)
        chosen = jnp.where(hit, 1.0, chosen)
        idx_o = jnp.where(lane_o == j, am, idx_o)
        s_sel.append(sj)
        ssum = ssum + sj
    idx_ref[...] = idx_o.astype(I32)
    for j in range(TOP_K):
        gate_ref[pl.ds(j, tm, stride=TOP_K), :] = jnp.broadcast_to(s_sel[j] / ssum * ROUTED_SCALE, (tm, 128))

    @pl.when(pl.program_id(0) == 0)
    def _():
        cnt_ref[...] = jnp.zeros_like(cnt_ref)

    cnt_ref[...] += jnp.sum(chosen, axis=0, keepdims=True)


def _outproj(mix, xn, wout_bf, g1, b1, wr_bf, br):
    T = mix.shape[0]
    tm = _row_tile(T)
    row = lambda i: (i, 0)
    const = lambda i: (0, 0)
    return pl.pallas_call(
        _outproj_kernel,
        out_shape=[jax.ShapeDtypeStruct((T, D_MODEL), F32), jax.ShapeDtypeStruct((T, D_MODEL // 2), I32),
                   jax.ShapeDtypeStruct((T, 128), I32), jax.ShapeDtypeStruct((T * TOP_K, 128), F32),
                   jax.ShapeDtypeStruct((1, N_EXPERTS), F32)],
        grid=(T // tm,),
        in_specs=[pl.BlockSpec((tm, D_MODEL), row), pl.BlockSpec((tm, D_MODEL), row),
                  pl.BlockSpec((D_MODEL, D_MODEL), const), pl.BlockSpec((1, D_MODEL), const),
                  pl.BlockSpec((1, D_MODEL), const), pl.BlockSpec((D_MODEL, N_EXPERTS), const),
                  pl.BlockSpec((1, N_EXPERTS), const)],
        out_specs=[pl.BlockSpec((tm, D_MODEL), row), pl.BlockSpec((tm, D_MODEL // 2), row),
                   pl.BlockSpec((tm, 128), row), pl.BlockSpec((tm * TOP_K, 128), row),
                   pl.BlockSpec((1, N_EXPERTS), const)],
        compiler_params=pltpu.CompilerParams(dimension_semantics=("arbitrary",), vmem_limit_bytes=VMEM_LIMIT),
        name="outproj",
    )(mix, xn, wout_bf, g1, b1, wr_bf, br)


def _moe_kernel(be_ref, bn_ref, bsrc_ref, bdel_ref, order_ref, xp_ref, wg_ref, wu_ref, wd_ref, yt_ref,
                tile_scr, ybuf, wgb, wub, wdb, ord_smem, sem, sem_o):
    b = pl.program_id(0)
    nb = pl.num_programs(0)
    slot = b % 2
    n = bn_ref[b]
    delta = bdel_ref[b]
    R = MOE_ROWS
    C = ROW_CHUNKS
    n_assign = yt_ref.shape[0] // C - 2 * TOP_K

    L = R + ORDER_ALIGN

    def fetch(blk, s):
        src = pl.multiple_of(bsrc_ref[blk], ORDER_ALIGN)
        dst = pl.multiple_of(s * L, ORDER_ALIGN)
        return pltpu.make_async_copy(order_ref.at[pl.ds(src, L)], ord_smem.at[pl.ds(dst, L)], sem_o.at[s])

    @pl.when(b == 0)
    def _():
        fetch(0, 0).start()

    @pl.when(b + 1 < nb)
    def _():
        fetch(jnp.minimum(b + 1, nb - 1), 1 - slot).start()

    fetch(b, slot).wait()
    obase = slot * L + delta

    def rows_sent(cnt):
        return pl.multiple_of(lax.shift_left(lax.shift_right_logical(cnt + 7, 3), 3), 8)

    def wait_rows(s, cnt):
        pltpu.make_async_copy(ybuf.at[s, pl.ds(0, cnt * C), :], yt_ref.at[pl.ds(0, cnt * C), :], sem.at[s]).wait()

    @pl.when(b == 0)
    def _():
        n_spare = 2 * TOP_K * C
        ybuf[0, 0:n_spare, :] = jnp.zeros((n_spare, 128), F32)
        init = pltpu.make_async_copy(ybuf.at[0, pl.ds(0, n_spare), :],
                                     yt_ref.at[pl.ds(yt_ref.shape[0] - n_spare, n_spare), :], sem.at[0])
        init.start()
        init.wait()

    @pl.when(b >= 2)
    def _():
        sent = rows_sent(bn_ref[jnp.maximum(b - 2, 0)])

        @pl.when(sent > 0)
        def _():
            wait_rows(slot, sent)

    @pl.when(jnp.logical_and(n > 0, jnp.logical_or(b == 0, be_ref[b] != be_ref[jnp.maximum(b - 1, 0)])))
    def _():
        wgb[...] = wg_ref[0].astype(BF16)
        wub[...] = wu_ref[0].astype(BF16)
        wdb[...] = wd_ref[0].astype(BF16)

    @pl.when(n > 0)
    def _():
        for r in range(R):
            tok = lax.shift_right_logical(ord_smem[obase + r], 3)
            t4 = pl.multiple_of(lax.shift_left(tok, 2), 4)
            tile_scr[pl.ds(r, 4, stride=TILE_STRIDE), :] = xp_ref[pl.ds(t4, 4), :]
        los, his = [], []
        for j in range(4):
            w = lax.bitcast_convert_type(tile_scr[pl.ds(j * TILE_STRIDE, R), :], jnp.uint32)
            los.append(lax.bitcast_convert_type(lax.shift_left(w, jnp.uint32(16)), F32).astype(BF16))
            his.append(lax.bitcast_convert_type(w & jnp.uint32(0xFFFF0000), F32).astype(BF16))
        xg = jnp.concatenate(los + his, axis=1)
        hg = _bdot(xg, wgb[...])
        hu = _bdot(xg, wub[...])
        hb = (hg * _sigmoid(hg)) * hu
        y = _bdot(hb.astype(BF16), wdb[...])
        yb = ybuf.at[slot]
        for c in range(C):
            yb[pl.ds(c, R, stride=C), :] = y[:, c * 128:(c + 1) * 128]

        def send(r, d):
            pltpu.make_async_copy(yb.at[pl.ds(pl.multiple_of(r * C, C), C), :],
                                  yt_ref.at[pl.ds(pl.multiple_of(d * C, C), C), :], sem.at[slot]).start()

        def send_group(g, carry):
            for u in range(8):
                send(g * 8 + u, ord_smem[obase + g * 8 + u])
            return carry

        n_full = lax.shift_right_logical(n, 3)
        lax.fori_loop(0, n_full, send_group, 0)

        @pl.when(n_full * 8 < n)
        def _():
            for u in range(8):
                r = n_full * 8 + u
                send(r, jnp.where(r < n, ord_smem[obase + r], n_assign + slot * 8 + u))

    @pl.when(b == nb - 1)
    def _():
        sent = rows_sent(n)

        @pl.when(sent > 0)
        def _():
            wait_rows(slot, sent)

        @pl.when(nb >= 2)
        def _():
            sent1 = rows_sent(bn_ref[jnp.maximum(b - 1, 0)])

            @pl.when(sent1 > 0)
            def _():
                wait_rows(1 - slot, sent1)


def _moe(blk_e, blk_n, blk_src, blk_delta, order, xp, wg, wu, wd, n_tok):
    nb = blk_e.shape[0]
    R = MOE_ROWS
    wmap_in = lambda b, be, bn, bs, bd: (be[b], 0, 0)
    grid_spec = pltpu.PrefetchScalarGridSpec(
        num_scalar_prefetch=4,
        grid=(nb,),
        in_specs=[
            pl.BlockSpec(memory_space=pl.ANY),
            pl.BlockSpec(memory_space=pltpu.VMEM),
            pl.BlockSpec((1, D_MODEL, D_EXPERT), wmap_in),
            pl.BlockSpec((1, D_MODEL, D_EXPERT), wmap_in),
            pl.BlockSpec((1, D_EXPERT, D_MODEL), wmap_in),
        ],
        out_specs=pl.BlockSpec(memory_space=pl.ANY),
        scratch_shapes=[pltpu.VMEM((4 * TILE_STRIDE, 128), I32),
                        pltpu.VMEM((2, R * ROW_CHUNKS, 128), F32),
                        pltpu.VMEM((D_MODEL, D_EXPERT), BF16),
                        pltpu.VMEM((D_MODEL, D_EXPERT), BF16),
                        pltpu.VMEM((D_EXPERT, D_MODEL), BF16),
                        pltpu.SMEM((2 * (R + ORDER_ALIGN),), I32),
                        pltpu.SemaphoreType.DMA((2,)),
                        pltpu.SemaphoreType.DMA((2,))],
    )
    return pl.pallas_call(
        _moe_kernel,
        out_shape=jax.ShapeDtypeStruct(((n_tok + 2) * TOP_K * ROW_CHUNKS, 128), F32),
        grid_spec=grid_spec,
        compiler_params=pltpu.CompilerParams(dimension_semantics=("arbitrary",),
                                             vmem_limit_bytes=58 * 1024 * 1024),
        name="moe",
    )(blk_e, blk_n, blk_src, blk_delta, order, xp, wg, wu, wd)


def _route_plan(idx, counts, n_tok):
    R = MOE_ROWS
    n_assign = n_tok * TOP_K
    nb = (n_assign + N_EXPERTS * (R - 1)) // R
    e_flat = idx.reshape(-1)
    _, order = lax.sort((e_flat, jnp.arange(n_assign, dtype=I32)), num_keys=1)
    order = jnp.concatenate([order, jnp.zeros((R + ORDER_ALIGN,), I32)])
    nblk_e = (counts + R - 1) // R
    bend = jnp.cumsum(nblk_e)
    bstart = bend - nblk_e
    cstart = jnp.cumsum(counts) - counts
    blk = jnp.arange(nb, dtype=I32)
    blk_e = jnp.minimum(jnp.sum((bend[None, :] <= blk[:, None]).astype(I32), axis=1), N_EXPERTS - 1)
    k = blk - bstart[blk_e]
    active = blk < bend[-1]
    blk_n = jnp.where(active, jnp.clip(counts[blk_e] - k * R, 0, R), 0).astype(I32)
    src = jnp.where(active, cstart[blk_e] + k * R, 0).astype(I32)
    blk_src = (src // ORDER_ALIGN) * ORDER_ALIGN
    return blk_e.astype(I32), blk_n, blk_src, src - blk_src, order


def _combine_kernel(x1_ref, yt_ref, gate_ref, wgs_ref, wus_ref, wds_ref, g2_ref, b2_ref, out_ref, r_scr):
    tm = x1_ref.shape[0]
    C = ROW_CHUNKS
    x1 = x1_ref[...]
    xb = x1.astype(BF16)
    hg = _bdot(xb, wgs_ref[...])
    hs = (hg * _sigmoid(hg)) * _bdot(xb, wus_ref[...])
    moe = _bdot(hs.astype(BF16), wds_ref[...])
    routed = yt_ref[:, 0] * gate_ref[:, 0:1, :]
    for j in range(1, TOP_K):
        routed = routed + yt_ref[:, j] * gate_ref[:, j:j + 1, :]
    r_scr[...] = routed.reshape(tm * C, 128)
    routed = jnp.concatenate([r_scr[pl.ds(c, tm, stride=C), :] for c in range(C)], axis=1)
    out_ref[...] = _ln(ALPHA * x1 + (moe + routed), g2_ref[...], b2_ref[...])


def _combine(x1, yt4, gate3, wgs_bf, wus_bf, wds_bf, g2, b2, row0, nrows):
    tm = 128
    assert nrows % tm == 0 and row0 % tm == 0
    off = row0 // tm
    row = lambda i: (i + off, 0)
    const = lambda i: (0, 0)
    return pl.pallas_call(
        _combine_kernel,
        out_shape=jax.ShapeDtypeStruct((nrows, D_MODEL), F32),
        grid=(nrows // tm,),
        in_specs=[pl.BlockSpec((tm, D_MODEL), row),
                  pl.BlockSpec((tm, TOP_K, ROW_CHUNKS, 128), lambda i: (i + off, 0, 0, 0)),
                  pl.BlockSpec((tm, TOP_K, 128), lambda i: (i + off, 0, 0)),
                  pl.BlockSpec((D_MODEL, D_EXPERT), const), pl.BlockSpec((D_MODEL, D_EXPERT), const),
                  pl.BlockSpec((D_EXPERT, D_MODEL), const),
                  pl.BlockSpec((1, D_MODEL), const), pl.BlockSpec((1, D_MODEL), const)],
        out_specs=pl.BlockSpec((tm, D_MODEL), lambda i: (i, 0)),
        scratch_shapes=[pltpu.VMEM((tm * ROW_CHUNKS, 128), F32)],
        compiler_params=pltpu.CompilerParams(dimension_semantics=("arbitrary",), vmem_limit_bytes=VMEM_LIMIT),
        name="combine",
    )(x1, yt4, gate3, wgs_bf, wus_bf, wds_bf, g2, b2)


def kernel(x_prompt, x_sample, state_hgrn, state_pool, meta_tokens, ln_emb_g, ln_emb_b, w_in, lb_logits, hgrn_norm_g, w_pool, pool_scale, w_out, ln1_g, ln1_b, w_router, b_router, w_gate_e, w_up_e, w_down_e, w_gate_s, w_up_s, w_down_s, ln2_g, ln2_b):
    nseq, seqlen, _ = x_prompt.shape
    ndec = x_sample.shape[0]
    l = 0
    row = lambda a: a.reshape(1, -1)
    w_in_bf = w_in[l].astype(BF16)
    wpool_bf = w_pool[l].astype(BF16)
    lng, lnb = row(ln_emb_g), row(ln_emb_b)
    ng, ps = row(hgrn_norm_g[l]), row(pool_scale[l])
    proj = functools.partial(_ln_proj, ln_g=lng, ln_b=lnb, w_in_bf=w_in_bf, lb_logits=lb_logits, norm_g=ng)

    m_xn, m_q, m_k, m_g, m_v, m_gs, m_p = proj(meta_tokens)
    pad = lambda a: jnp.pad(a, ((0, MIX_BLOCK - N_META), (0, 0)))
    zero_state = jnp.zeros((HGRN_HEADS, HEAD_DIM, HEAD_DIM), F32)
    _, s_meta = _mixer(pad(m_q), pad(m_k), pad(m_g), pad(m_v), pad(m_gs), pad(m_p), zero_state,
                       jnp.zeros((SUB, POOL_WIDTH), F32), wpool_bf, ps, 1, MIX_BLOCK)

    p_xn, p_q, p_k, p_g, p_v, p_gs, p_p = proj(x_prompt.reshape(nseq * seqlen, D_MODEL))
    p_mix, s_prompt = _mixer(p_q, p_k, p_g, p_v, p_gs, p_p, s_meta[0], m_p, wpool_bf, ps, nseq, seqlen)

    d_xn, d_q, d_k, d_g, d_v, d_gs, d_p = proj(x_sample.reshape(ndec, D_MODEL))
    cols = lambda a: a.reshape(ndec // STEP_SEQS, STEP_SEQS, HGRN_HEADS, HEAD_DIM).transpose(2, 0, 3, 1)
    s_dec, d_oa = _mixer_step(cols(d_q), cols(d_k), cols(d_g), d_v, d_gs, state_hgrn[l])
    d_ob = _pool_step(state_pool[l].transpose(1, 0, 2), d_p, wpool_bf, ps)
    d_mix = jnp.concatenate([d_oa, d_ob], axis=1).astype(BF16)

    mix = jnp.concatenate([p_mix, d_mix], axis=0)
    xn = jnp.concatenate([p_xn, d_xn], axis=0)
    n_tok = mix.shape[0]
    x1, x1p, idx, gate, cnt = _outproj(mix, xn, w_out[l].astype(BF16), row(ln1_g[l]), row(ln1_b[l]),
                                       w_router[l].astype(BF16), row(b_router[l]))
    plan = _route_plan(idx[:, :TOP_K], cnt[0].astype(I32), n_tok)
    xp = x1p.reshape(n_tok * 4, 128)
    yt = _moe(*plan, xp, w_gate_e[l], w_up_e[l], w_down_e[l], n_tok)
    yt4 = yt.reshape(n_tok + 2, TOP_K, ROW_CHUNKS, 128)
    gate3 = gate.reshape(n_tok, TOP_K, 128)
    comb = functools.partial(_combine, x1, yt4, gate3, w_gate_s[l].astype(BF16), w_up_s[l].astype(BF16),
                             w_down_s[l].astype(BF16), row(ln2_g[l]), row(ln2_b[l]))
    y_prompt = comb(0, nseq * seqlen).reshape(nseq, seqlen, D_MODEL)
    y_sample = comb(nseq * seqlen, ndec).reshape(ndec, 1, D_MODEL)

    state_pool_prompt = p_p.reshape(nseq, seqlen, POOL_WIDTH)[:, seqlen - POOL_BUF:, :]
    state_pool_sample = jnp.concatenate([state_pool[l][:, 1:, :], d_p[:, None, :]], axis=1)
    return (y_prompt, y_sample, s_prompt[None], state_pool_prompt[None], s_dec[None], state_pool_sample[None])
```

```python
import functools

import jax
import jax.numpy as jnp
from jax import lax
from jax.experimental import pallas as pl
from jax.experimental.pallas import tpu as pltpu

F32 = jnp.float32
BF16 = jnp.bfloat16
I32 = jnp.int32

D_MODEL = 1024
N_META = 16
HGRN_WIDTH = 512
HGRN_HEADS = 4
HEAD_DIM = 128
POOL_WIDTH = 512
POOL_WINDOWS = (2, 4, 8, 16)
POOL_GROUP_DIM = 128
POOL_BUF = 15
IN_WIDTH = 4 * HGRN_WIDTH + POOL_WIDTH
N_EXPERTS = 256
TOP_K = 8
D_EXPERT = 256
ROUTED_SCALE = 2.5
DEPTH = 1
ALPHA = (2 * DEPTH) ** 0.25
LN_EPS = 1e-5
RMS_EPS = 1e-6

SUB = 16
MIX_BLOCK = 128
MOE_ROWS = 256
TILE_STRIDE = MOE_ROWS + 8
ORDER_ALIGN = 128
ROW_CHUNKS = D_MODEL // 128
PACK_ROWS = D_MODEL // 2 // 128
OUT_TILE = 512
VMEM_LIMIT = 48 * 1024 * 1024


def _ln(x, g, b):
    mu = jnp.mean(x, axis=-1, keepdims=True)
    xc = x - mu
    var = jnp.mean(xc * xc, axis=-1, keepdims=True)
    return xc * lax.rsqrt(var + LN_EPS) * g + b


def _sigmoid(z):
    return 1.0 / (1.0 + jnp.exp(-z))


def _bdot(a, b):
    return jnp.dot(a, b, preferred_element_type=F32)


def _ln_proj_kernel(x_ref, g_ref, b_ref, w_ref, lbl_ref, ng_ref,
                    xn_ref, q_ref, k_ref, gl_ref, v_ref, gs_ref, p_ref):
    xn = _ln(x_ref[...], g_ref[...], b_ref[...])
    xn_ref[...] = xn
    proj = _bdot(xn.astype(BF16), w_ref[...])
    lbl = lbl_ref[...]
    e = jnp.exp(lbl - jnp.max(lbl, axis=0, keepdims=True))
    lb = e[0:1] / jnp.sum(e, axis=0, keepdims=True)
    W = HGRN_WIDTH
    q = proj[:, 0:W]
    f = proj[:, W:2 * W]
    q_ref[...] = q * _sigmoid(q)
    k_ref[...] = (1.0 - lb) * _sigmoid(-f)
    gl_ref[...] = jnp.log(lb + (1.0 - lb) * _sigmoid(f))
    v_ref[...] = proj[:, 2 * W:3 * W]
    g = proj[:, 3 * W:4 * W]
    gs_ref[...] = ng_ref[...] * (g * _sigmoid(g))
    p_ref[...] = proj[:, 4 * W:]


def _row_tile(n_rows):
    for tm in (256, 128, 64, 32, 16, 8):
        if n_rows % tm == 0:
            return tm
    raise ValueError(f"row count {n_rows} is not a multiple of 8")


def _ln_proj(x, ln_g, ln_b, w_in_bf, lb_logits, norm_g):
    T = x.shape[0]
    tm = _row_tile(T)
    row = lambda i: (i, 0)
    const = lambda i: (0, 0)
    outs = [jax.ShapeDtypeStruct((T, D_MODEL), F32)] + [jax.ShapeDtypeStruct((T, HGRN_WIDTH), F32)] * 6
    return pl.pallas_call(
        _ln_proj_kernel,
        out_shape=outs,
        grid=(T // tm,),
        in_specs=[
            pl.BlockSpec((tm, D_MODEL), row),
            pl.BlockSpec((1, D_MODEL), const),
            pl.BlockSpec((1, D_MODEL), const),
            pl.BlockSpec((D_MODEL, IN_WIDTH), const),
            pl.BlockSpec((DEPTH + 1, HGRN_WIDTH), const),
            pl.BlockSpec((1, HGRN_WIDTH), const),
        ],
        out_specs=[pl.BlockSpec((tm, D_MODEL), row)] + [pl.BlockSpec((tm, HGRN_WIDTH), row)] * 6,
        compiler_params=pltpu.CompilerParams(dimension_semantics=("arbitrary",), vmem_limit_bytes=VMEM_LIMIT),
        name="ln_proj",
    )(x, ln_g, ln_b, w_in_bf, lb_logits, norm_g)


def _pool_group(pe, p_cur, gi, w):
    sl = slice(gi * POOL_GROUP_DIM, (gi + 1) * POOL_GROUP_DIM)
    s = pe[:, sl]
    sh = 1
    while sh < w:
        s = s + pltpu.roll(s, sh, 0)
        sh *= 2
    return s[SUB:, :] * (1.0 / w) - p_cur[:, sl]


def _mixer_kernel(q_ref, k_ref, g_ref, v_ref, gs_ref, p_ref, s0_ref, pp0_ref, wpool_ref, pscale_ref,
                  mix_ref, sfin_ref, st_scr, pe_scr):
    i = pl.program_id(1)
    nblk = pl.num_programs(1)
    n = MIX_BLOCK

    @pl.when(i == 0)
    def _():
        for h in range(HGRN_HEADS):
            st_scr[h] = s0_ref[h].T
        pe_scr[0:SUB, :] = pp0_ref[...]

    rows = lax.broadcasted_iota(I32, (n, HEAD_DIM), 0)
    r16 = rows & (SUB - 1)
    t8 = lax.broadcasted_iota(I32, (8, HEAD_DIM), 0)
    zero_bf = jnp.zeros((SUB, HEAD_DIM), BF16)

    o_heads = []
    for h in range(HGRN_HEADS):
        hs = slice(h * HEAD_DIM, (h + 1) * HEAD_DIM)
        Q = q_ref[:, hs]
        K = k_ref[:, hs]
        G = g_ref[:, hs]
        V = v_ref[:, hs]
        bf = G
        br = G
        for sh in (1, 2, 4, 8):
            bf = bf + jnp.where(r16 >= sh, pltpu.roll(bf, sh, 0), 0.0)
            br = br + jnp.where(r16 < SUB - sh, pltpu.roll(br, n - sh, 0), 0.0)
        br = br - G
        qt = (Q * jnp.exp(bf)).astype(BF16)
        kt = (K * jnp.exp(br)).astype(BF16)
        vt = V.T.astype(BF16)
        st = st_scr[h]
        o_parts = []
        for c in range(n // SUB):
            r0 = c * SUB
            b_top, b_bot = bf[r0:r0 + 8], bf[r0 + 8:r0 + 16]
            q_top, q_bot = Q[r0:r0 + 8], Q[r0 + 8:r0 + 16]
            acc_top = jnp.zeros((8, HEAD_DIM), F32)
            acc_bot = jnp.zeros((8, HEAD_DIM), F32)
            for s in range(SUB):
                bs = bf[r0 + s:r0 + s + 1]
                ks = K[r0 + s:r0 + s + 1]
                vs = V[r0 + s:r0 + s + 1]
                if s < 8:
                    col = jnp.sum(q_top * jnp.exp(b_top - bs) * ks, axis=-1, keepdims=True)
                    col = jnp.where(t8[:, 0:1] >= s, col, 0.0)
                    acc_top = acc_top + col * vs
                    col = jnp.sum(q_bot * jnp.exp(b_bot - bs) * ks, axis=-1, keepdims=True)
                    acc_bot = acc_bot + col * vs
                else:
                    col = jnp.sum(q_bot * jnp.exp(b_bot - bs) * ks, axis=-1, keepdims=True)
                    col = jnp.where(t8[:, 0:1] + 8 >= s, col, 0.0)
                    acc_bot = acc_bot + col * vs
            o_diag = jnp.concatenate([acc_top, acc_bot], axis=0)
            o_inter = lax.dot_general(qt[r0:r0 + SUB], st.astype(BF16), (((1,), (1,)), ((), ())),
                                      preferred_element_type=F32)
            o_parts.append(o_inter + o_diag)
            kmask = jnp.concatenate([zero_bf] * c + [kt[r0:r0 + SUB]] + [zero_bf] * (n // SUB - 1 - c), axis=0)
            d_st = _bdot(vt, kmask)
            st = st * jnp.exp(bf[r0 + SUB - 1:r0 + SUB]) + d_st
        st_scr[h] = st
        o = jnp.concatenate(o_parts, axis=0)
        o = o * lax.rsqrt(jnp.mean(o * o, axis=-1, keepdims=True) + RMS_EPS)
        o_heads.append(o * gs_ref[:, hs])

    p_cur = p_ref[...]
    pe_scr[SUB:SUB + n, :] = p_cur
    pe = pe_scr[...]
    ob = []
    for gi, w in enumerate(POOL_WINDOWS):
        pooled = _pool_group(pe, p_cur, gi, w)
        sl = slice(gi * POOL_GROUP_DIM, (gi + 1) * POOL_GROUP_DIM)
        ob.append(_bdot(pooled.astype(BF16), wpool_ref[gi]) * pscale_ref[:, sl])
    pe_scr[0:SUB, :] = p_cur[n - SUB:, :]

    mix_ref[...] = jnp.concatenate(o_heads + ob, axis=1).astype(BF16)

    @pl.when(i == nblk - 1)
    def _():
        for h in range(HGRN_HEADS):
            sfin_ref[0, h] = st_scr[h].T


def _mixer(q, k, g, v, gs, p, s0, pp0, wpool_bf, pscale, nseq, seqlen):
    nblk = seqlen // MIX_BLOCK
    tok = lambda b, i: (b * nblk + i, 0)
    tspec = pl.BlockSpec((MIX_BLOCK, HGRN_WIDTH), tok)
    return pl.pallas_call(
        _mixer_kernel,
        out_shape=[jax.ShapeDtypeStruct((nseq * seqlen, D_MODEL), BF16),
                   jax.ShapeDtypeStruct((nseq, HGRN_HEADS, HEAD_DIM, HEAD_DIM), F32)],
        grid=(nseq, nblk),
        in_specs=[tspec] * 6 + [
            pl.BlockSpec((HGRN_HEADS, HEAD_DIM, HEAD_DIM), lambda b, i: (0, 0, 0)),
            pl.BlockSpec((SUB, POOL_WIDTH), lambda b, i: (0, 0)),
            pl.BlockSpec((len(POOL_WINDOWS), POOL_GROUP_DIM, POOL_GROUP_DIM), lambda b, i: (0, 0, 0)),
            pl.BlockSpec((1, POOL_WIDTH), lambda b, i: (0, 0)),
        ],
        out_specs=[pl.BlockSpec((MIX_BLOCK, D_MODEL), tok),
                   pl.BlockSpec((1, HGRN_HEADS, HEAD_DIM, HEAD_DIM), lambda b, i: (b, 0, 0, 0))],
        scratch_shapes=[pltpu.VMEM((HGRN_HEADS, HEAD_DIM, HEAD_DIM), F32),
                        pltpu.VMEM((SUB + MIX_BLOCK, POOL_WIDTH), F32)],
        compiler_params=pltpu.CompilerParams(dimension_semantics=("arbitrary", "arbitrary"),
                                             vmem_limit_bytes=VMEM_LIMIT),
        name="mixer",
    )(q, k, g, v, gs, p, s0, pp0, wpool_bf, pscale)


STEP_SEQS = 32


def _mixer_step_kernel(qt_ref, kt_ref, gt_ref, v_ref, gs_ref, s_ref, snew_ref, oa_ref):
    qt = qt_ref[0, 0]
    kt = kt_ref[0, 0]
    dt = jnp.exp(gt_ref[0, 0])
    rows = []
    for bb in range(STEP_SEQS):
        sn = s_ref[bb, 0] * dt[:, bb:bb + 1] + kt[:, bb:bb + 1] * v_ref[bb:bb + 1, :]
        snew_ref[bb, 0] = sn
        rows.append(jnp.sum(sn * qt[:, bb:bb + 1], axis=0, keepdims=True))
    o = jnp.concatenate(rows, axis=0)
    o = o * lax.rsqrt(jnp.mean(o * o, axis=-1, keepdims=True) + RMS_EPS)
    oa_ref[...] = o * gs_ref[...]


def _mixer_step(qT, kT, gT, v, gs, state):
    nseq = v.shape[0]
    nbc = nseq // STEP_SEQS
    cspec = pl.BlockSpec((1, 1, HEAD_DIM, STEP_SEQS), lambda h, c: (h, c, 0, 0))
    rspec = pl.BlockSpec((STEP_SEQS, HEAD_DIM), lambda h, c: (c, h))
    sspec = pl.BlockSpec((STEP_SEQS, 1, HEAD_DIM, HEAD_DIM), lambda h, c: (c, h, 0, 0))
    return pl.pallas_call(
        _mixer_step_kernel,
        out_shape=[jax.ShapeDtypeStruct(state.shape, F32), jax.ShapeDtypeStruct((nseq, HGRN_WIDTH), F32)],
        grid=(HGRN_HEADS, nbc),
        in_specs=[cspec, cspec, cspec, rspec, rspec, sspec],
        out_specs=[sspec, rspec],
        compiler_params=pltpu.CompilerParams(dimension_semantics=("arbitrary", "arbitrary"),
                                             vmem_limit_bytes=VMEM_LIMIT),
        name="mixer_step",
    )(qT, kT, gT, v, gs, state)


def _pool_step_kernel(sp_ref, p_ref, wpool_ref, pscale_ref, ob_ref):
    p_cur = p_ref[...]
    outs = []
    for gi, w in enumerate(POOL_WINDOWS):
        sl = slice(gi * POOL_GROUP_DIM, (gi + 1) * POOL_GROUP_DIM)
        s = p_cur[:, sl]
        for r in range(POOL_BUF - (w - 1), POOL_BUF):
            s = s + sp_ref[r][:, sl]
        pooled = s * (1.0 / w) - p_cur[:, sl]
        outs.append(_bdot(pooled.astype(BF16), wpool_ref[gi]) * pscale_ref[:, sl])
    ob_ref[...] = jnp.concatenate(outs, axis=1)


def _pool_step(spT, p, wpool_bf, pscale):
    nseq = p.shape[0]
    return pl.pallas_call(
        _pool_step_kernel,
        out_shape=jax.ShapeDtypeStruct((nseq, POOL_WIDTH), F32),
        name="pool_step",
    )(spT, p, wpool_bf, pscale)


def _outproj_kernel(mixp_ref, xnp_ref, mixd_ref, xnd_ref, wout_ref, g1_ref, b1_ref, wr_ref, br_ref,
                    x1_ref, x1p_ref, idx_ref, gate_ref, cnt_ref, *, n_prompt_blocks, n_valid_last):
    tm = mixp_ref.shape[0]
    i = pl.program_id(0)
    is_prompt = i < n_prompt_blocks
    mix_in = jnp.where(is_prompt, mixp_ref[...], mixd_ref[...])
    xn = jnp.where(is_prompt, xnp_ref[...], xnd_ref[...])
    mix = _bdot(mix_in, wout_ref[...])
    x1 = _ln(ALPHA * xn + mix, g1_ref[...], b1_ref[...])
    x1_ref[...] = x1
    xb = x1.astype(BF16)
    xr = xb.astype(F32)
    half = D_MODEL // 2
    lo = lax.shift_right_logical(lax.bitcast_convert_type(xr[:, :half], jnp.uint32), jnp.uint32(16))
    hi = lax.bitcast_convert_type(xr[:, half:], jnp.uint32) & jnp.uint32(0xFFFF0000)
    words = lax.bitcast_convert_type(hi | lo, I32)
    for c in range(PACK_ROWS):
        x1p_ref[pl.ds(c, tm, stride=PACK_ROWS), :] = words[:, c * 128:(c + 1) * 128]

    scores = _sigmoid(_bdot(xb, wr_ref[...]))
    sel = scores + br_ref[...]
    lane = lax.broadcasted_iota(I32, (tm, N_EXPERTS), 1).astype(F32)
    lane_o = lax.broadcasted_iota(I32, (tm, 128), 1)
    idx_o = jnp.zeros((tm, 128), F32)
    ssum = jnp.zeros((tm, 1), F32)
    chosen = jnp.zeros((tm, N_EXPERTS), F32)
    s_sel = []
    for j in range(TOP_K):
        m = jnp.max(sel, axis=-1, keepdims=True)
        am = jnp.min(jnp.where(sel == m, lane, float(N_EXPERTS)), axis=-1, keepdims=True)
        hit = lane == am
        sj = jnp.sum(jnp.where(hit, scores, 0.0), axis=-1, keepdims=True)
        sel = jnp.where(hit, -jnp.inf, sel)
        chosen = jnp.where(hit, 1.0, chosen)
        idx_o = jnp.where(lane_o == j, am, idx_o)
        s_sel.append(sj)
        ssum = ssum + sj
    idx_ref[...] = idx_o.astype(I32)
    for j in range(TOP_K):
        gate_ref[pl.ds(j, tm, stride=TOP_K), :] = jnp.broadcast_to(s_sel[j] / ssum * ROUTED_SCALE, (tm, 128))

    @pl.when(i == 0)
    def _():
        cnt_ref[...] = jnp.zeros_like(cnt_ref)

    row_id = lax.broadcasted_iota(I32, (tm, N_EXPERTS), 0)
    valid = jnp.logical_or(is_prompt, row_id < n_valid_last)
    cnt_ref[...] += jnp.sum(jnp.where(valid, chosen, 0.0), axis=0, keepdims=True)


def _outproj(mix_p, xn_p, mix_d, xn_d, wout_bf, g1, b1, wr_bf, br):
    tm = OUT_TILE
    n_prompt, n_dec = mix_p.shape[0], mix_d.shape[0]
    assert n_prompt % tm == 0 and n_dec <= tm and n_dec % 8 == 0
    nbp = n_prompt // tm
    T = (nbp + 1) * tm
    padrows = lambda a: jnp.pad(a, ((0, tm - n_dec), (0, 0)))
    row = lambda i: (i, 0)
    prow = lambda i: (jnp.minimum(i, nbp - 1), 0)
    const = lambda i: (0, 0)
    return pl.pallas_call(
        functools.partial(_outproj_kernel, n_prompt_blocks=nbp, n_valid_last=n_dec),
        out_shape=[jax.ShapeDtypeStruct((T, D_MODEL), F32), jax.ShapeDtypeStruct((T * PACK_ROWS, 128), I32),
                   jax.ShapeDtypeStruct((T, 128), I32), jax.ShapeDtypeStruct((T * TOP_K, 128), F32),
                   jax.ShapeDtypeStruct((1, N_EXPERTS), F32)],
        grid=(nbp + 1,),
        in_specs=[pl.BlockSpec((tm, D_MODEL), prow), pl.BlockSpec((tm, D_MODEL), prow),
                  pl.BlockSpec((tm, D_MODEL), const), pl.BlockSpec((tm, D_MODEL), const),
                  pl.BlockSpec((D_MODEL, D_MODEL), const), pl.BlockSpec((1, D_MODEL), const),
                  pl.BlockSpec((1, D_MODEL), const), pl.BlockSpec((D_MODEL, N_EXPERTS), const),
                  pl.BlockSpec((1, N_EXPERTS), const)],
        out_specs=[pl.BlockSpec((tm, D_MODEL), row), pl.BlockSpec((tm * PACK_ROWS, 128), row),
                   pl.BlockSpec((tm, 128), row), pl.BlockSpec((tm * TOP_K, 128), row),
                   pl.BlockSpec((1, N_EXPERTS), const)],
        compiler_params=pltpu.CompilerParams(dimension_semantics=("arbitrary",), vmem_limit_bytes=VMEM_LIMIT),
        name="outproj",
    )(mix_p, xn_p, padrows(mix_d), padrows(xn_d), wout_bf, g1, b1, wr_bf, br)


def _moe_kernel(be_ref, bn_ref, bsrc_ref, bdel_ref, order_ref, xp_ref, wg_ref, wu_ref, wd_ref, yt_ref,
                tile_scr, ybuf, wgb, wub, wdb, ord_smem, sem, sem_o):
    b = pl.program_id(0)
    nb = pl.num_programs(0)
    slot = b % 2
    n = bn_ref[b]
    delta = bdel_ref[b]
    R = MOE_ROWS
    C = ROW_CHUNKS
    n_assign = yt_ref.shape[0] // C - 2 * TOP_K

    L = R + ORDER_ALIGN

    def fetch(blk, s):
        src = pl.multiple_of(bsrc_ref[blk], ORDER_ALIGN)
        dst = pl.multiple_of(s * L, ORDER_ALIGN)
        return pltpu.make_async_copy(order_ref.at[pl.ds(src, L)], ord_smem.at[pl.ds(dst, L)], sem_o.at[s])

    @pl.when(b == 0)
    def _():
        fetch(0, 0).start()

    @pl.when(b + 1 < nb)
    def _():
        fetch(jnp.minimum(b + 1, nb - 1), 1 - slot).start()

    fetch(b, slot).wait()
    obase = slot * L + delta

    def rows_sent(cnt):
        return pl.multiple_of(lax.shift_left(lax.shift_right_logical(cnt + 7, 3), 3), 8)

    def wait_rows(s, cnt):
        pltpu.make_async_copy(ybuf.at[s, pl.ds(0, cnt * C), :], yt_ref.at[pl.ds(0, cnt * C), :], sem.at[s]).wait()

    @pl.when(b == 0)
    def _():
        n_spare = 2 * TOP_K * C
        ybuf[0, 0:n_spare, :] = jnp.zeros((n_spare, 128), F32)
        init = pltpu.make_async_copy(ybuf.at[0, pl.ds(0, n_spare), :],
                                     yt_ref.at[pl.ds(yt_ref.shape[0] - n_spare, n_spare), :], sem.at[0])
        init.start()
        init.wait()

    @pl.when(b >= 2)
    def _():
        sent = rows_sent(bn_ref[jnp.maximum(b - 2, 0)])

        @pl.when(sent > 0)
        def _():
            wait_rows(slot, sent)

    @pl.when(jnp.logical_and(n > 0, jnp.logical_or(b == 0, be_ref[b] != be_ref[jnp.maximum(b - 1, 0)])))
    def _():
        wgb[...] = wg_ref[0].astype(BF16)
        wub[...] = wu_ref[0].astype(BF16)
        wdb[...] = wd_ref[0].astype(BF16)

    @pl.when(n > 0)
    def _():
        for r in range(R):
            tok = lax.shift_right_logical(ord_smem[obase + r], 3)
            t4 = pl.multiple_of(tok * PACK_ROWS, PACK_ROWS)
            tile_scr[pl.ds(r, PACK_ROWS, stride=TILE_STRIDE), :] = xp_ref[pl.ds(t4, PACK_ROWS), :]
        los, his = [], []
        for j in range(PACK_ROWS):
            w = lax.bitcast_convert_type(tile_scr[pl.ds(j * TILE_STRIDE, R), :], jnp.uint32)
            los.append(lax.bitcast_convert_type(lax.shift_left(w, jnp.uint32(16)), F32).astype(BF16))
            his.append(lax.bitcast_convert_type(w & jnp.uint32(0xFFFF0000), F32).astype(BF16))
        xg = jnp.concatenate(los + his, axis=1)
        hg = _bdot(xg, wgb[...])
        hu = _bdot(xg, wub[...])
        hb = (hg * _sigmoid(hg)) * hu
        y = _bdot(hb.astype(BF16), wdb[...])
        yb = ybuf.at[slot]
        for c in range(C):
            yb[pl.ds(c, R, stride=C), :] = y[:, c * 128:(c + 1) * 128]

        def send(r, d):
            pltpu.make_async_copy(yb.at[pl.ds(pl.multiple_of(r * C, C), C), :],
                                  yt_ref.at[pl.ds(pl.multiple_of(d * C, C), C), :], sem.at[slot]).start()

        def send_group(g, carry):
            for u in range(8):
                send(g * 8 + u, ord_smem[obase + g * 8 + u])
            return carry

        n_full = lax.shift_right_logical(n, 3)
        lax.fori_loop(0, n_full, send_group, 0)

        @pl.when(n_full * 8 < n)
        def _():
            for u in range(8):
                r = n_full * 8 + u
                send(r, jnp.where(r < n, ord_smem[obase + r], n_assign + slot * 8 + u))

    @pl.when(b == nb - 1)
    def _():
        sent = rows_sent(n)

        @pl.when(sent > 0)
        def _():
            wait_rows(slot, sent)

        @pl.when(nb >= 2)
        def _():
            sent1 = rows_sent(bn_ref[jnp.maximum(b - 1, 0)])

            @pl.when(sent1 > 0)
            def _():
                wait_rows(1 - slot, sent1)


def _moe(blk_e, blk_n, blk_src, blk_delta, order, xp, wg, wu, wd, n_tok):
    nb = blk_e.shape[0]
    R = MOE_ROWS
    wmap_in = lambda b, be, bn, bs, bd: (be[b], 0, 0)
    grid_spec = pltpu.PrefetchScalarGridSpec(
        num_scalar_prefetch=4,
        grid=(nb,),
        in_specs=[
            pl.BlockSpec(memory_space=pl.ANY),
            pl.BlockSpec(memory_space=pltpu.VMEM),
            pl.BlockSpec((1, D_MODEL, D_EXPERT), wmap_in),
            pl.BlockSpec((1, D_MODEL, D_EXPERT), wmap_in),
            pl.BlockSpec((1, D_EXPERT, D_MODEL), wmap_in),
        ],
        out_specs=pl.BlockSpec(memory_space=pl.ANY),
        scratch_shapes=[pltpu.VMEM((PACK_ROWS * TILE_STRIDE, 128), I32),
                        pltpu.VMEM((2, R * ROW_CHUNKS, 128), F32),
                        pltpu.VMEM((D_MODEL, D_EXPERT), BF16),
                        pltpu.VMEM((D_MODEL, D_EXPERT), BF16),
                        pltpu.VMEM((D_EXPERT, D_MODEL), BF16),
                        pltpu.SMEM((2 * (R + ORDER_ALIGN),), I32),
                        pltpu.SemaphoreType.DMA((2,)),
                        pltpu.SemaphoreType.DMA((2,))],
    )
    return pl.pallas_call(
        _moe_kernel,
        out_shape=jax.ShapeDtypeStruct(((n_tok + 2) * TOP_K * ROW_CHUNKS, 128), F32),
        grid_spec=grid_spec,
        compiler_params=pltpu.CompilerParams(dimension_semantics=("arbitrary",),
                                             vmem_limit_bytes=58 * 1024 * 1024),
        name="moe",
    )(blk_e, blk_n, blk_src, blk_delta, order, xp, wg, wu, wd)


def _route_plan(idx, counts, n_tok):
    R = MOE_ROWS
    n_assign = n_tok * TOP_K
    nb = (n_assign + N_EXPERTS * (R - 1)) // R
    id_bits = (n_assign - 1).bit_length()
    assert id_bits + (N_EXPERTS - 1).bit_length() < 32
    key = lax.shift_left(idx.reshape(-1), id_bits) | jnp.arange(n_assign, dtype=I32)
    order = lax.sort(key) & ((1 << id_bits) - 1)
    order = jnp.concatenate([order, jnp.zeros((R + ORDER_ALIGN,), I32)])
    nblk_e = (counts + R - 1) // R
    bend = jnp.cumsum(nblk_e)
    bstart = bend - nblk_e
    cstart = jnp.cumsum(counts) - counts
    blk = jnp.arange(nb, dtype=I32)
    blk_e = jnp.minimum(jnp.sum((bend[None, :] <= blk[:, None]).astype(I32), axis=1), N_EXPERTS - 1)
    k = blk - bstart[blk_e]
    active = blk < bend[-1]
    blk_n = jnp.where(active, jnp.clip(counts[blk_e] - k * R, 0, R), 0).astype(I32)
    src = jnp.where(active, cstart[blk_e] + k * R, 0).astype(I32)
    blk_src = (src // ORDER_ALIGN) * ORDER_ALIGN
    return blk_e.astype(I32), blk_n, blk_src, src - blk_src, order


def _combine_kernel(x1_ref, yt_ref, gate_ref, wgs_ref, wus_ref, wds_ref, g2_ref, b2_ref, out_ref, r_scr):
    tm = x1_ref.shape[0]
    C = ROW_CHUNKS
    x1 = x1_ref[...]
    xb = x1.astype(BF16)
    hg = _bdot(xb, wgs_ref[...])
    hs = (hg * _sigmoid(hg)) * _bdot(xb, wus_ref[...])
    moe = _bdot(hs.astype(BF16), wds_ref[...])
    routed = yt_ref[:, 0] * gate_ref[:, 0:1, :]
    for j in range(1, TOP_K):
        routed = routed + yt_ref[:, j] * gate_ref[:, j:j + 1, :]
    r_scr[...] = routed.reshape(tm * C, 128)
    routed = jnp.concatenate([r_scr[pl.ds(c, tm, stride=C), :] for c in range(C)], axis=1)
    out_ref[...] = _ln(ALPHA * x1 + (moe + routed), g2_ref[...], b2_ref[...])


def _combine(x1, yt4, gate3, wgs_bf, wus_bf, wds_bf, g2, b2, row0, nrows):
    tm = 128
    assert nrows % tm == 0 and row0 % tm == 0
    off = row0 // tm
    row = lambda i: (i + off, 0)
    const = lambda i: (0, 0)
    return pl.pallas_call(
        _combine_kernel,
        out_shape=jax.ShapeDtypeStruct((nrows, D_MODEL), F32),
        grid=(nrows // tm,),
        in_specs=[pl.BlockSpec((tm, D_MODEL), row),
                  pl.BlockSpec((tm, TOP_K, ROW_CHUNKS, 128), lambda i: (i + off, 0, 0, 0)),
                  pl.BlockSpec((tm, TOP_K, 128), lambda i: (i + off, 0, 0)),
                  pl.BlockSpec((D_MODEL, D_EXPERT), const), pl.BlockSpec((D_MODEL, D_EXPERT), const),
                  pl.BlockSpec((D_EXPERT, D_MODEL), const),
                  pl.BlockSpec((1, D_MODEL), const), pl.BlockSpec((1, D_MODEL), const)],
        out_specs=pl.BlockSpec((tm, D_MODEL), lambda i: (i, 0)),
        scratch_shapes=[pltpu.VMEM((tm * ROW_CHUNKS, 128), F32)],
        compiler_params=pltpu.CompilerParams(dimension_semantics=("arbitrary",), vmem_limit_bytes=VMEM_LIMIT),
        name="combine",
    )(x1, yt4, gate3, wgs_bf, wus_bf, wds_bf, g2, b2)


def kernel(x_prompt, x_sample, state_hgrn, state_pool, meta_tokens, ln_emb_g, ln_emb_b, w_in, lb_logits, hgrn_norm_g, w_pool, pool_scale, w_out, ln1_g, ln1_b, w_router, b_router, w_gate_e, w_up_e, w_down_e, w_gate_s, w_up_s, w_down_s, ln2_g, ln2_b):
    nseq, seqlen, _ = x_prompt.shape
    ndec = x_sample.shape[0]
    l = 0
    row = lambda a: a.reshape(1, -1)
    w_in_bf = w_in[l].astype(BF16)
    wpool_bf = w_pool[l].astype(BF16)
    lng, lnb = row(ln_emb_g), row(ln_emb_b)
    ng, ps = row(hgrn_norm_g[l]), row(pool_scale[l])
    proj = functools.partial(_ln_proj, ln_g=lng, ln_b=lnb, w_in_bf=w_in_bf, lb_logits=lb_logits, norm_g=ng)

    m_xn, m_q, m_k, m_g, m_v, m_gs, m_p = proj(meta_tokens)
    pad = lambda a: jnp.pad(a, ((0, MIX_BLOCK - N_META), (0, 0)))
    zero_state = jnp.zeros((HGRN_HEADS, HEAD_DIM, HEAD_DIM), F32)
    _, s_meta = _mixer(pad(m_q), pad(m_k), pad(m_g), pad(m_v), pad(m_gs), pad(m_p), zero_state,
                       jnp.zeros((SUB, POOL_WIDTH), F32), wpool_bf, ps, 1, MIX_BLOCK)

    p_xn, p_q, p_k, p_g, p_v, p_gs, p_p = proj(x_prompt.reshape(nseq * seqlen, D_MODEL))
    p_mix, s_prompt = _mixer(p_q, p_k, p_g, p_v, p_gs, p_p, s_meta[0], m_p, wpool_bf, ps, nseq, seqlen)

    d_xn, d_q, d_k, d_g, d_v, d_gs, d_p = proj(x_sample.reshape(ndec, D_MODEL))
    cols = lambda a: a.reshape(ndec // STEP_SEQS, STEP_SEQS, HGRN_HEADS, HEAD_DIM).transpose(2, 0, 3, 1)
    s_dec, d_oa = _mixer_step(cols(d_q), cols(d_k), cols(d_g), d_v, d_gs, state_hgrn[l])
    d_ob = _pool_step(state_pool[l].transpose(1, 0, 2), d_p, wpool_bf, ps)
    d_mix = jnp.concatenate([d_oa, d_ob], axis=1).astype(BF16)

    n_tok = nseq * seqlen + ndec
    x1, xp, idx, gate, cnt = _outproj(p_mix, p_xn, d_mix, d_xn, w_out[l].astype(BF16), row(ln1_g[l]), row(ln1_b[l]),
                                      w_router[l].astype(BF16), row(b_router[l]))
    plan = _route_plan(idx[:n_tok, :TOP_K], cnt[0].astype(I32), n_tok)
    yt = _moe(*plan, xp, w_gate_e[l], w_up_e[l], w_down_e[l], n_tok)
    yt4 = yt.reshape(n_tok + 2, TOP_K, ROW_CHUNKS, 128)
    gate3 = gate.reshape(-1, TOP_K, 128)
    comb = functools.partial(_combine, x1, yt4, gate3, w_gate_s[l].astype(BF16), w_up_s[l].astype(BF16),
                             w_down_s[l].astype(BF16), row(ln2_g[l]), row(ln2_b[l]))
    y_prompt = comb(0, nseq * seqlen).reshape(nseq, seqlen, D_MODEL)
    y_sample = comb(nseq * seqlen, ndec).reshape(ndec, 1, D_MODEL)

    state_pool_prompt = p_p.reshape(nseq, seqlen, POOL_WIDTH)[:, seqlen - POOL_BUF:, :]
    state_pool_sample = jnp.concatenate([state_pool[l][:, 1:, :], d_p[:, None, :]], axis=1)
    return (y_prompt, y_sample, s_prompt[None], state_pool_prompt[None], s_dec[None], state_pool_sample[None])
```

```python
import functools

import jax
import jax.numpy as jnp
from jax import lax
from jax.experimental import pallas as pl
from jax.experimental.pallas import tpu as pltpu

F32 = jnp.float32
BF16 = jnp.bfloat16
I32 = jnp.int32

D_MODEL = 1024
N_META = 16
HGRN_WIDTH = 512
HGRN_HEADS = 4
HEAD_DIM = 128
POOL_WIDTH = 512
POOL_WINDOWS = (2, 4, 8, 16)
POOL_GROUP_DIM = 128
POOL_BUF = 15
IN_WIDTH = 4 * HGRN_WIDTH + POOL_WIDTH
N_EXPERTS = 256
TOP_K = 8
D_EXPERT = 256
ROUTED_SCALE = 2.5
DEPTH = 1
ALPHA = (2 * DEPTH) ** 0.25
LN_EPS = 1e-5
RMS_EPS = 1e-6

SUB = 16
MIX_BLOCK = 128
MOE_ROWS = 256
TILE_STRIDE = MOE_ROWS + 8
ORDER_ALIGN = 128
ROW_CHUNKS = D_MODEL // 128
PACK_ROWS = D_MODEL // 2 // 128
OUT_TILE = 512
VMEM_LIMIT = 48 * 1024 * 1024


def _ln(x, g, b):
    mu = jnp.mean(x, axis=-1, keepdims=True)
    xc = x - mu
    var = jnp.mean(xc * xc, axis=-1, keepdims=True)
    return xc * lax.rsqrt(var + LN_EPS) * g + b


def _sigmoid(z):
    return 1.0 / (1.0 + jnp.exp(-z))


def _bdot(a, b):
    return jnp.dot(a, b, preferred_element_type=F32)


def _ln_proj_kernel(x_ref, g_ref, b_ref, w_ref, lbl_ref, ng_ref,
                    xn_ref, q_ref, k_ref, gl_ref, v_ref, gs_ref, p_ref):
    xn = _ln(x_ref[...], g_ref[...], b_ref[...])
    xn_ref[...] = xn
    proj = _bdot(xn.astype(BF16), w_ref[...])
    lbl = lbl_ref[...]
    e = jnp.exp(lbl - jnp.max(lbl, axis=0, keepdims=True))
    lb = e[0:1] / jnp.sum(e, axis=0, keepdims=True)
    W = HGRN_WIDTH
    q = proj[:, 0:W]
    f = proj[:, W:2 * W]
    q_ref[...] = q * _sigmoid(q)
    k_ref[...] = (1.0 - lb) * _sigmoid(-f)
    gl_ref[...] = jnp.log(lb + (1.0 - lb) * _sigmoid(f))
    v_ref[...] = proj[:, 2 * W:3 * W]
    g = proj[:, 3 * W:4 * W]
    gs_ref[...] = ng_ref[...] * (g * _sigmoid(g))
    p_ref[...] = proj[:, 4 * W:]


def _row_tile(n_rows):
    for tm in (256, 128, 64, 32, 16, 8):
        if n_rows % tm == 0:
            return tm
    raise ValueError(f"row count {n_rows} is not a multiple of 8")


def _ln_proj(x, ln_g, ln_b, w_in_bf, lb_logits, norm_g):
    T = x.shape[0]
    tm = _row_tile(T)
    row = lambda i: (i, 0)
    const = lambda i: (0, 0)
    outs = [jax.ShapeDtypeStruct((T, D_MODEL), F32)] + [jax.ShapeDtypeStruct((T, HGRN_WIDTH), F32)] * 6
    return pl.pallas_call(
        _ln_proj_kernel,
        out_shape=outs,
        grid=(T // tm,),
        in_specs=[
            pl.BlockSpec((tm, D_MODEL), row),
            pl.BlockSpec((1, D_MODEL), const),
            pl.BlockSpec((1, D_MODEL), const),
            pl.BlockSpec((D_MODEL, IN_WIDTH), const),
            pl.BlockSpec((DEPTH + 1, HGRN_WIDTH), const),
            pl.BlockSpec((1, HGRN_WIDTH), const),
        ],
        out_specs=[pl.BlockSpec((tm, D_MODEL), row)] + [pl.BlockSpec((tm, HGRN_WIDTH), row)] * 6,
        compiler_params=pltpu.CompilerParams(dimension_semantics=("arbitrary",), vmem_limit_bytes=VMEM_LIMIT),
        name="ln_proj",
    )(x, ln_g, ln_b, w_in_bf, lb_logits, norm_g)


def _pool_group(pe, p_cur, gi, w):
    sl = slice(gi * POOL_GROUP_DIM, (gi + 1) * POOL_GROUP_DIM)
    s = pe[:, sl]
    sh = 1
    while sh < w:
        s = s + pltpu.roll(s, sh, 0)
        sh *= 2
    return s[SUB:, :] * (1.0 / w) - p_cur[:, sl]


def _mixer_kernel(q_ref, k_ref, g_ref, v_ref, gs_ref, p_ref, s0_ref, pp0_ref, wpool_ref, pscale_ref,
                  mix_ref, sfin_ref, st_scr, pe_scr):
    i = pl.program_id(1)
    nblk = pl.num_programs(1)
    n = MIX_BLOCK

    @pl.when(i == 0)
    def _():
        for h in range(HGRN_HEADS):
            st_scr[h] = s0_ref[h].T
        pe_scr[0:SUB, :] = pp0_ref[...]

    rows = lax.broadcasted_iota(I32, (n, HEAD_DIM), 0)
    r16 = rows & (SUB - 1)
    t8 = lax.broadcasted_iota(I32, (8, HEAD_DIM), 0)
    zero_bf = jnp.zeros((SUB, HEAD_DIM), BF16)

    o_heads = []
    for h in range(HGRN_HEADS):
        hs = slice(h * HEAD_DIM, (h + 1) * HEAD_DIM)
        Q = q_ref[:, hs]
        K = k_ref[:, hs]
        G = g_ref[:, hs]
        V = v_ref[:, hs]
        bf = G
        br = G
        for sh in (1, 2, 4, 8):
            bf = bf + jnp.where(r16 >= sh, pltpu.roll(bf, sh, 0), 0.0)
            br = br + jnp.where(r16 < SUB - sh, pltpu.roll(br, n - sh, 0), 0.0)
        br = br - G
        qt = (Q * jnp.exp(bf)).astype(BF16)
        kt = (K * jnp.exp(br)).astype(BF16)
        vt = V.T.astype(BF16)
        st = st_scr[h]
        o_parts = []
        for c in range(n // SUB):
            r0 = c * SUB
            b_top, b_bot = bf[r0:r0 + 8], bf[r0 + 8:r0 + 16]
            q_top, q_bot = Q[r0:r0 + 8], Q[r0 + 8:r0 + 16]
            acc_top = jnp.zeros((8, HEAD_DIM), F32)
            acc_bot = jnp.zeros((8, HEAD_DIM), F32)
            for s in range(SUB):
                bs = bf[r0 + s:r0 + s + 1]
                ks = K[r0 + s:r0 + s + 1]
                vs = V[r0 + s:r0 + s + 1]
                if s < 8:
                    col = jnp.sum(q_top * jnp.exp(b_top - bs) * ks, axis=-1, keepdims=True)
                    col = jnp.where(t8[:, 0:1] >= s, col, 0.0)
                    acc_top = acc_top + col * vs
                    col = jnp.sum(q_bot * jnp.exp(b_bot - bs) * ks, axis=-1, keepdims=True)
                    acc_bot = acc_bot + col * vs
                else:
                    col = jnp.sum(q_bot * jnp.exp(b_bot - bs) * ks, axis=-1, keepdims=True)
                    col = jnp.where(t8[:, 0:1] + 8 >= s, col, 0.0)
                    acc_bot = acc_bot + col * vs
            o_diag = jnp.concatenate([acc_top, acc_bot], axis=0)
            o_inter = lax.dot_general(qt[r0:r0 + SUB], st.astype(BF16), (((1,), (1,)), ((), ())),
                                      preferred_element_type=F32)
            o_parts.append(o_inter + o_diag)
            kmask = jnp.concatenate([zero_bf] * c + [kt[r0:r0 + SUB]] + [zero_bf] * (n // SUB - 1 - c), axis=0)
            d_st = _bdot(vt, kmask)
            st = st * jnp.exp(bf[r0 + SUB - 1:r0 + SUB]) + d_st
        st_scr[h] = st
        o = jnp.concatenate(o_parts, axis=0)
        o = o * lax.rsqrt(jnp.mean(o * o, axis=-1, keepdims=True) + RMS_EPS)
        o_heads.append(o * gs_ref[:, hs])

    p_cur = p_ref[...]
    pe_scr[SUB:SUB + n, :] = p_cur
    pe = pe_scr[...]
    ob = []
    for gi, w in enumerate(POOL_WINDOWS):
        pooled = _pool_group(pe, p_cur, gi, w)
        sl = slice(gi * POOL_GROUP_DIM, (gi + 1) * POOL_GROUP_DIM)
        ob.append(_bdot(pooled.astype(BF16), wpool_ref[gi]) * pscale_ref[:, sl])
    pe_scr[0:SUB, :] = p_cur[n - SUB:, :]

    mix_ref[...] = jnp.concatenate(o_heads + ob, axis=1).astype(BF16)

    @pl.when(i == nblk - 1)
    def _():
        for h in range(HGRN_HEADS):
            sfin_ref[0, h] = st_scr[h].T


def _mixer(q, k, g, v, gs, p, s0, pp0, wpool_bf, pscale, nseq, seqlen):
    nblk = seqlen // MIX_BLOCK
    tok = lambda b, i: (b * nblk + i, 0)
    tspec = pl.BlockSpec((MIX_BLOCK, HGRN_WIDTH), tok)
    return pl.pallas_call(
        _mixer_kernel,
        out_shape=[jax.ShapeDtypeStruct((nseq * seqlen, D_MODEL), BF16),
                   jax.ShapeDtypeStruct((nseq, HGRN_HEADS, HEAD_DIM, HEAD_DIM), F32)],
        grid=(nseq, nblk),
        in_specs=[tspec] * 6 + [
            pl.BlockSpec((HGRN_HEADS, HEAD_DIM, HEAD_DIM), lambda b, i: (0, 0, 0)),
            pl.BlockSpec((SUB, POOL_WIDTH), lambda b, i: (0, 0)),
            pl.BlockSpec((len(POOL_WINDOWS), POOL_GROUP_DIM, POOL_GROUP_DIM), lambda b, i: (0, 0, 0)),
            pl.BlockSpec((1, POOL_WIDTH), lambda b, i: (0, 0)),
        ],
        out_specs=[pl.BlockSpec((MIX_BLOCK, D_MODEL), tok),
                   pl.BlockSpec((1, HGRN_HEADS, HEAD_DIM, HEAD_DIM), lambda b, i: (b, 0, 0, 0))],
        scratch_shapes=[pltpu.VMEM((HGRN_HEADS, HEAD_DIM, HEAD_DIM), F32),
                        pltpu.VMEM((SUB + MIX_BLOCK, POOL_WIDTH), F32)],
        compiler_params=pltpu.CompilerParams(dimension_semantics=("arbitrary", "arbitrary"),
                                             vmem_limit_bytes=VMEM_LIMIT),
        name="mixer",
    )(q, k, g, v, gs, p, s0, pp0, wpool_bf, pscale)


STEP_SEQS = 32


def _mixer_step_kernel(qt_ref, kt_ref, gt_ref, v_ref, gs_ref, s_ref, snew_ref, oa_ref):
    qt = qt_ref[0, 0]
    kt = kt_ref[0, 0]
    dt = jnp.exp(gt_ref[0, 0])
    rows = []
    for bb in range(STEP_SEQS):
        sn = s_ref[bb, 0] * dt[:, bb:bb + 1] + kt[:, bb:bb + 1] * v_ref[bb:bb + 1, :]
        snew_ref[bb, 0] = sn
        rows.append(jnp.sum(sn * qt[:, bb:bb + 1], axis=0, keepdims=True))
    o = jnp.concatenate(rows, axis=0)
    o = o * lax.rsqrt(jnp.mean(o * o, axis=-1, keepdims=True) + RMS_EPS)
    oa_ref[...] = o * gs_ref[...]


def _mixer_step(qT, kT, gT, v, gs, state):
    nseq = v.shape[0]
    nbc = nseq // STEP_SEQS
    cspec = pl.BlockSpec((1, 1, HEAD_DIM, STEP_SEQS), lambda h, c: (h, c, 0, 0))
    rspec = pl.BlockSpec((STEP_SEQS, HEAD_DIM), lambda h, c: (c, h))
    sspec = pl.BlockSpec((STEP_SEQS, 1, HEAD_DIM, HEAD_DIM), lambda h, c: (c, h, 0, 0))
    return pl.pallas_call(
        _mixer_step_kernel,
        out_shape=[jax.ShapeDtypeStruct(state.shape, F32), jax.ShapeDtypeStruct((nseq, HGRN_WIDTH), F32)],
        grid=(HGRN_HEADS, nbc),
        in_specs=[cspec, cspec, cspec, rspec, rspec, sspec],
        out_specs=[sspec, rspec],
        compiler_params=pltpu.CompilerParams(dimension_semantics=("arbitrary", "arbitrary"),
                                             vmem_limit_bytes=VMEM_LIMIT),
        name="mixer_step",
    )(qT, kT, gT, v, gs, state)


def _pool_step_kernel(sp_ref, p_ref, wpool_ref, pscale_ref, ob_ref):
    p_cur = p_ref[...]
    outs = []
    for gi, w in enumerate(POOL_WINDOWS):
        sl = slice(gi * POOL_GROUP_DIM, (gi + 1) * POOL_GROUP_DIM)
        s = p_cur[:, sl]
        for r in range(POOL_BUF - (w - 1), POOL_BUF):
            s = s + sp_ref[r][:, sl]
        pooled = s * (1.0 / w) - p_cur[:, sl]
        outs.append(_bdot(pooled.astype(BF16), wpool_ref[gi]) * pscale_ref[:, sl])
    ob_ref[...] = jnp.concatenate(outs, axis=1)


def _pool_step(spT, p, wpool_bf, pscale):
    nseq = p.shape[0]
    return pl.pallas_call(
        _pool_step_kernel,
        out_shape=jax.ShapeDtypeStruct((nseq, POOL_WIDTH), F32),
        name="pool_step",
    )(spT, p, wpool_bf, pscale)


def _outproj_kernel(mixp_ref, xnp_ref, mixd_ref, xnd_ref, wout_ref, g1_ref, b1_ref, wr_ref, br_ref,
                    x1_ref, x1p_ref, idx_ref, gate_ref, cnt_ref, *, n_prompt_blocks, n_valid_last):
    tm = mixp_ref.shape[0]
    i = pl.program_id(0)
    is_prompt = i < n_prompt_blocks
    mix_in = jnp.where(is_prompt, mixp_ref[...], mixd_ref[...])
    xn = jnp.where(is_prompt, xnp_ref[...], xnd_ref[...])
    mix = _bdot(mix_in, wout_ref[...])
    x1 = _ln(ALPHA * xn + mix, g1_ref[...], b1_ref[...])
    x1_ref[...] = x1
    xb = x1.astype(BF16)
    xr = xb.astype(F32)
    half = D_MODEL // 2
    lo = lax.shift_right_logical(lax.bitcast_convert_type(xr[:, :half], jnp.uint32), jnp.uint32(16))
    hi = lax.bitcast_convert_type(xr[:, half:], jnp.uint32) & jnp.uint32(0xFFFF0000)
    words = lax.bitcast_convert_type(hi | lo, I32)
    for c in range(PACK_ROWS):
        x1p_ref[pl.ds(c, tm, stride=PACK_ROWS), :] = words[:, c * 128:(c + 1) * 128]

    scores = _sigmoid(_bdot(xb, wr_ref[...]))
    sel = scores + br_ref[...]
    lane = lax.broadcasted_iota(I32, (tm, N_EXPERTS), 1).astype(F32)
    lane_o = lax.broadcasted_iota(I32, (tm, 128), 1)
    idx_o = jnp.zeros((tm, 128), F32)
    ssum = jnp.zeros((tm, 1), F32)
    chosen = jnp.zeros((tm, N_EXPERTS), F32)
    s_sel = []
    for j in range(TOP_K):
        m = jnp.max(sel, axis=-1, keepdims=True)
        am = jnp.min(jnp.where(sel == m, lane, float(N_EXPERTS)), axis=-1, keepdims=True)
        hit = lane == am
        sj = jnp.sum(jnp.where(hit, scores, 0.0), axis=-1, keepdims=True)
        sel = jnp.where(hit, -jnp.inf, sel)
        chosen = jnp.where(hit, 1.0, chosen)
        idx_o = jnp.where(lane_o == j, am, idx_o)
        s_sel.append(sj)
        ssum = ssum + sj
    idx_ref[...] = idx_o.astype(I32)
    for j in range(TOP_K):
        gate_ref[pl.ds(j, tm, stride=TOP_K), :] = jnp.broadcast_to(s_sel[j] / ssum * ROUTED_SCALE, (tm, 128))

    @pl.when(i == 0)
    def _():
        cnt_ref[...] = jnp.zeros_like(cnt_ref)

    row_id = lax.broadcasted_iota(I32, (tm, N_EXPERTS), 0)
    valid = jnp.logical_or(is_prompt, row_id < n_valid_last)
    cnt_ref[...] += jnp.sum(jnp.where(valid, chosen, 0.0), axis=0, keepdims=True)


def _outproj(mix_p, xn_p, mix_d, xn_d, wout_bf, g1, b1, wr_bf, br):
    tm = OUT_TILE
    n_prompt, n_dec = mix_p.shape[0], mix_d.shape[0]
    assert n_prompt % tm == 0 and n_dec <= tm and n_dec % 8 == 0
    nbp = n_prompt // tm
    T = (nbp + 1) * tm
    padrows = lambda a: jnp.pad(a, ((0, tm - n_dec), (0, 0)))
    row = lambda i: (i, 0)
    prow = lambda i: (jnp.minimum(i, nbp - 1), 0)
    const = lambda i: (0, 0)
    return pl.pallas_call(
        functools.partial(_outproj_kernel, n_prompt_blocks=nbp, n_valid_last=n_dec),
        out_shape=[jax.ShapeDtypeStruct((T, D_MODEL), F32), jax.ShapeDtypeStruct((T * PACK_ROWS, 128), I32),
                   jax.ShapeDtypeStruct((T, 128), I32), jax.ShapeDtypeStruct((T * TOP_K, 128), F32),
                   jax.ShapeDtypeStruct((1, N_EXPERTS), F32)],
        grid=(nbp + 1,),
        in_specs=[pl.BlockSpec((tm, D_MODEL), prow), pl.BlockSpec((tm, D_MODEL), prow),
                  pl.BlockSpec((tm, D_MODEL), const), pl.BlockSpec((tm, D_MODEL), const),
                  pl.BlockSpec((D_MODEL, D_MODEL), const), pl.BlockSpec((1, D_MODEL), const),
                  pl.BlockSpec((1, D_MODEL), const), pl.BlockSpec((D_MODEL, N_EXPERTS), const),
                  pl.BlockSpec((1, N_EXPERTS), const)],
        out_specs=[pl.BlockSpec((tm, D_MODEL), row), pl.BlockSpec((tm * PACK_ROWS, 128), row),
                   pl.BlockSpec((tm, 128), row), pl.BlockSpec((tm * TOP_K, 128), row),
                   pl.BlockSpec((1, N_EXPERTS), const)],
        compiler_params=pltpu.CompilerParams(dimension_semantics=("arbitrary",), vmem_limit_bytes=VMEM_LIMIT),
        name="outproj",
    )(mix_p, xn_p, padrows(mix_d), padrows(xn_d), wout_bf, g1, b1, wr_bf, br)


def _moe_kernel(be_ref, bn_ref, bsrc_ref, bdel_ref, order_ref, xp_ref, wg_ref, wu_ref, wd_ref, yt_ref,
                tile_scr, ybuf, wgb, wub, wdb, ord_smem, sem, sem_o):
    b = pl.program_id(0)
    nb = pl.num_programs(0)
    slot = b % 2
    n = bn_ref[b]
    delta = bdel_ref[b]
    R = MOE_ROWS
    C = ROW_CHUNKS
    n_assign = yt_ref.shape[0] // C - 2 * TOP_K

    L = R + ORDER_ALIGN

    def fetch(blk, s):
        src = pl.multiple_of(bsrc_ref[blk], ORDER_ALIGN)
        dst = pl.multiple_of(s * L, ORDER_ALIGN)
        return pltpu.make_async_copy(order_ref.at[pl.ds(src, L)], ord_smem.at[pl.ds(dst, L)], sem_o.at[s])

    @pl.when(b == 0)
    def _():
        fetch(0, 0).start()

    @pl.when(b + 1 < nb)
    def _():
        fetch(jnp.minimum(b + 1, nb - 1), 1 - slot).start()

    fetch(b, slot).wait()
    obase = slot * L + delta

    def rows_sent(cnt):
        return pl.multiple_of(lax.shift_left(lax.shift_right_logical(cnt + 7, 3), 3), 8)

    def wait_rows(s, cnt):
        pltpu.make_async_copy(ybuf.at[s, pl.ds(0, cnt * C), :], yt_ref.at[pl.ds(0, cnt * C), :], sem.at[s]).wait()

    @pl.when(b == 0)
    def _():
        n_spare = 2 * TOP_K * C
        ybuf[0, 0:n_spare, :] = jnp.zeros((n_spare, 128), F32)
        init = pltpu.make_async_copy(ybuf.at[0, pl.ds(0, n_spare), :],
                                     yt_ref.at[pl.ds(yt_ref.shape[0] - n_spare, n_spare), :], sem.at[0])
        init.start()
        init.wait()

    @pl.when(b >= 2)
    def _():
        sent = rows_sent(bn_ref[jnp.maximum(b - 2, 0)])

        @pl.when(sent > 0)
        def _():
            wait_rows(slot, sent)

    @pl.when(jnp.logical_and(n > 0, jnp.logical_or(b == 0, be_ref[b] != be_ref[jnp.maximum(b - 1, 0)])))
    def _():
        wgb[...] = wg_ref[0].astype(BF16)
        wub[...] = wu_ref[0].astype(BF16)
        wdb[...] = wd_ref[0].astype(BF16)

    @pl.when(n > 0)
    def _():
        for r in range(R):
            tok = lax.shift_right_logical(ord_smem[obase + r], 3)
            t4 = pl.multiple_of(tok * PACK_ROWS, PACK_ROWS)
            tile_scr[pl.ds(r, PACK_ROWS, stride=TILE_STRIDE), :] = xp_ref[pl.ds(t4, PACK_ROWS), :]
        los, his = [], []
        for j in range(PACK_ROWS):
            w = lax.bitcast_convert_type(tile_scr[pl.ds(j * TILE_STRIDE, R), :], jnp.uint32)
            los.append(lax.bitcast_convert_type(lax.shift_left(w, jnp.uint32(16)), F32).astype(BF16))
            his.append(lax.bitcast_convert_type(w & jnp.uint32(0xFFFF0000), F32).astype(BF16))
        xg = jnp.concatenate(los + his, axis=1)
        hg = _bdot(xg, wgb[...])
        hu = _bdot(xg, wub[...])
        hb = (hg * _sigmoid(hg)) * hu
        y = _bdot(hb.astype(BF16), wdb[...])
        yb = ybuf.at[slot]
        for c in range(C):
            yb[pl.ds(c, R, stride=C), :] = y[:, c * 128:(c + 1) * 128]

        def send(r, d, u):
            pltpu.make_async_copy(yb.at[pl.ds(pl.multiple_of(r * C, C), C), :],
                                  yt_ref.at[pl.ds(pl.multiple_of(d * C, C), C), :], sem.at[slot]).start(priority=u % 2)

        def send_group(g, carry):
            for u in range(8):
                send(g * 8 + u, ord_smem[obase + g * 8 + u], u)
            return carry

        n_full = lax.shift_right_logical(n, 3)
        lax.fori_loop(0, n_full, send_group, 0)

        @pl.when(n_full * 8 < n)
        def _():
            for u in range(8):
                r = n_full * 8 + u
                send(r, jnp.where(r < n, ord_smem[obase + r], n_assign + slot * 8 + u), u)

    @pl.when(b == nb - 1)
    def _():
        sent = rows_sent(n)

        @pl.when(sent > 0)
        def _():
            wait_rows(slot, sent)

        @pl.when(nb >= 2)
        def _():
            sent1 = rows_sent(bn_ref[jnp.maximum(b - 1, 0)])

            @pl.when(sent1 > 0)
            def _():
                wait_rows(1 - slot, sent1)


def _moe(blk_e, blk_n, blk_src, blk_delta, order, xp, wg, wu, wd, n_tok):
    nb = blk_e.shape[0]
    R = MOE_ROWS
    wmap_in = lambda b, be, bn, bs, bd: (be[b], 0, 0)
    grid_spec = pltpu.PrefetchScalarGridSpec(
        num_scalar_prefetch=4,
        grid=(nb,),
        in_specs=[
            pl.BlockSpec(memory_space=pl.ANY),
            pl.BlockSpec(memory_space=pltpu.VMEM),
            pl.BlockSpec((1, D_MODEL, D_EXPERT), wmap_in),
            pl.BlockSpec((1, D_MODEL, D_EXPERT), wmap_in),
            pl.BlockSpec((1, D_EXPERT, D_MODEL), wmap_in),
        ],
        out_specs=pl.BlockSpec(memory_space=pl.ANY),
        scratch_shapes=[pltpu.VMEM((PACK_ROWS * TILE_STRIDE, 128), I32),
                        pltpu.VMEM((2, R * ROW_CHUNKS, 128), F32),
                        pltpu.VMEM((D_MODEL, D_EXPERT), BF16),
                        pltpu.VMEM((D_MODEL, D_EXPERT), BF16),
                        pltpu.VMEM((D_EXPERT, D_MODEL), BF16),
                        pltpu.SMEM((2 * (R + ORDER_ALIGN),), I32),
                        pltpu.SemaphoreType.DMA((2,)),
                        pltpu.SemaphoreType.DMA((2,))],
    )
    return pl.pallas_call(
        _moe_kernel,
        out_shape=jax.ShapeDtypeStruct(((n_tok + 2) * TOP_K * ROW_CHUNKS, 128), F32),
        grid_spec=grid_spec,
        compiler_params=pltpu.CompilerParams(dimension_semantics=("arbitrary",),
                                             vmem_limit_bytes=58 * 1024 * 1024),
        name="moe",
    )(blk_e, blk_n, blk_src, blk_delta, order, xp, wg, wu, wd)


def _route_plan(idx, counts, n_tok):
    R = MOE_ROWS
    n_assign = n_tok * TOP_K
    nb = (n_assign + N_EXPERTS * (R - 1)) // R
    id_bits = (n_assign - 1).bit_length()
    assert id_bits + (N_EXPERTS - 1).bit_length() < 32
    key = lax.shift_left(idx.reshape(-1), id_bits) | jnp.arange(n_assign, dtype=I32)
    order = lax.sort(key) & ((1 << id_bits) - 1)
    order = jnp.concatenate([order, jnp.zeros((R + ORDER_ALIGN,), I32)])
    nblk_e = (counts + R - 1) // R
    bend = jnp.cumsum(nblk_e)
    bstart = bend - nblk_e
    cstart = jnp.cumsum(counts) - counts
    blk = jnp.arange(nb, dtype=I32)
    blk_e = jnp.minimum(jnp.sum((bend[None, :] <= blk[:, None]).astype(I32), axis=1), N_EXPERTS - 1)
    k = blk - bstart[blk_e]
    active = blk < bend[-1]
    blk_n = jnp.where(active, jnp.clip(counts[blk_e] - k * R, 0, R), 0).astype(I32)
    src = jnp.where(active, cstart[blk_e] + k * R, 0).astype(I32)
    blk_src = (src // ORDER_ALIGN) * ORDER_ALIGN
    return blk_e.astype(I32), blk_n, blk_src, src - blk_src, order


def _combine_kernel(x1_ref, yt_ref, gate_ref, wgs_ref, wus_ref, wds_ref, g2_ref, b2_ref, out_ref, r_scr):
    tm = x1_ref.shape[0]
    C = ROW_CHUNKS
    x1 = x1_ref[...]
    xb = x1.astype(BF16)
    hg = _bdot(xb, wgs_ref[...])
    hs = (hg * _sigmoid(hg)) * _bdot(xb, wus_ref[...])
    moe = _bdot(hs.astype(BF16), wds_ref[...])
    routed = yt_ref[:, 0] * gate_ref[:, 0:1, :]
    for j in range(1, TOP_K):
        routed = routed + yt_ref[:, j] * gate_ref[:, j:j + 1, :]
    r_scr[...] = routed.reshape(tm * C, 128)
    routed = jnp.concatenate([r_scr[pl.ds(c, tm, stride=C), :] for c in range(C)], axis=1)
    out_ref[...] = _ln(ALPHA * x1 + (moe + routed), g2_ref[...], b2_ref[...])


def _combine(x1, yt4, gate3, wgs_bf, wus_bf, wds_bf, g2, b2, row0, nrows):
    tm = 128
    assert nrows % tm == 0 and row0 % tm == 0
    off = row0 // tm
    row = lambda i: (i + off, 0)
    const = lambda i: (0, 0)
    return pl.pallas_call(
        _combine_kernel,
        out_shape=jax.ShapeDtypeStruct((nrows, D_MODEL), F32),
        grid=(nrows // tm,),
        in_specs=[pl.BlockSpec((tm, D_MODEL), row),
                  pl.BlockSpec((tm, TOP_K, ROW_CHUNKS, 128), lambda i: (i + off, 0, 0, 0)),
                  pl.BlockSpec((tm, TOP_K, 128), lambda i: (i + off, 0, 0)),
                  pl.BlockSpec((D_MODEL, D_EXPERT), const), pl.BlockSpec((D_MODEL, D_EXPERT), const),
                  pl.BlockSpec((D_EXPERT, D_MODEL), const),
                  pl.BlockSpec((1, D_MODEL), const), pl.BlockSpec((1, D_MODEL), const)],
        out_specs=pl.BlockSpec((tm, D_MODEL), lambda i: (i, 0)),
        scratch_shapes=[pltpu.VMEM((tm * ROW_CHUNKS, 128), F32)],
        compiler_params=pltpu.CompilerParams(dimension_semantics=("arbitrary",), vmem_limit_bytes=VMEM_LIMIT),
        name="combine",
    )(x1, yt4, gate3, wgs_bf, wus_bf, wds_bf, g2, b2)


def kernel(x_prompt, x_sample, state_hgrn, state_pool, meta_tokens, ln_emb_g, ln_emb_b, w_in, lb_logits, hgrn_norm_g, w_pool, pool_scale, w_out, ln1_g, ln1_b, w_router, b_router, w_gate_e, w_up_e, w_down_e, w_gate_s, w_up_s, w_down_s, ln2_g, ln2_b):
    nseq, seqlen, _ = x_prompt.shape
    ndec = x_sample.shape[0]
    l = 0
    row = lambda a: a.reshape(1, -1)
    w_in_bf = w_in[l].astype(BF16)
    wpool_bf = w_pool[l].astype(BF16)
    lng, lnb = row(ln_emb_g), row(ln_emb_b)
    ng, ps = row(hgrn_norm_g[l]), row(pool_scale[l])
    proj = functools.partial(_ln_proj, ln_g=lng, ln_b=lnb, w_in_bf=w_in_bf, lb_logits=lb_logits, norm_g=ng)

    m_xn, m_q, m_k, m_g, m_v, m_gs, m_p = proj(meta_tokens)
    pad = lambda a: jnp.pad(a, ((0, MIX_BLOCK - N_META), (0, 0)))
    zero_state = jnp.zeros((HGRN_HEADS, HEAD_DIM, HEAD_DIM), F32)
    _, s_meta = _mixer(pad(m_q), pad(m_k), pad(m_g), pad(m_v), pad(m_gs), pad(m_p), zero_state,
                       jnp.zeros((SUB, POOL_WIDTH), F32), wpool_bf, ps, 1, MIX_BLOCK)

    p_xn, p_q, p_k, p_g, p_v, p_gs, p_p = proj(x_prompt.reshape(nseq * seqlen, D_MODEL))
    p_mix, s_prompt = _mixer(p_q, p_k, p_g, p_v, p_gs, p_p, s_meta[0], m_p, wpool_bf, ps, nseq, seqlen)

    d_xn, d_q, d_k, d_g, d_v, d_gs, d_p = proj(x_sample.reshape(ndec, D_MODEL))
    cols = lambda a: a.reshape(ndec // STEP_SEQS, STEP_SEQS, HGRN_HEADS, HEAD_DIM).transpose(2, 0, 3, 1)
    s_dec, d_oa = _mixer_step(cols(d_q), cols(d_k), cols(d_g), d_v, d_gs, state_hgrn[l])
    d_ob = _pool_step(state_pool[l].transpose(1, 0, 2), d_p, wpool_bf, ps)
    d_mix = jnp.concatenate([d_oa, d_ob], axis=1).astype(BF16)

    n_tok = nseq * seqlen + ndec
    x1, xp, idx, gate, cnt = _outproj(p_mix, p_xn, d_mix, d_xn, w_out[l].astype(BF16), row(ln1_g[l]), row(ln1_b[l]),
                                      w_router[l].astype(BF16), row(b_router[l]))
    plan = _route_plan(idx[:n_tok, :TOP_K], cnt[0].astype(I32), n_tok)
    yt = _moe(*plan, xp, w_gate_e[l], w_up_e[l], w_down_e[l], n_tok)
    yt4 = yt.reshape(n_tok + 2, TOP_K, ROW_CHUNKS, 128)
    gate3 = gate.reshape(-1, TOP_K, 128)
    comb = functools.partial(_combine, x1, yt4, gate3, w_gate_s[l].astype(BF16), w_up_s[l].astype(BF16),
                             w_down_s[l].astype(BF16), row(ln2_g[l]), row(ln2_b[l]))
    y_prompt = comb(0, nseq * seqlen).reshape(nseq, seqlen, D_MODEL)
    y_sample = comb(nseq * seqlen, ndec).reshape(ndec, 1, D_MODEL)

    state_pool_prompt = p_p.reshape(nseq, seqlen, POOL_WIDTH)[:, seqlen - POOL_BUF:, :]
    state_pool_sample = jnp.concatenate([state_pool[l][:, 1:, :], d_p[:, None, :]], axis=1)
    return (y_prompt, y_sample, s_prompt[None], state_pool_prompt[None], s_dec[None], state_pool_sample[None])
```

```python
import functools

import jax
import jax.numpy as jnp
from jax import lax
from jax.experimental import pallas as pl
from jax.experimental.pallas import tpu as pltpu

F32 = jnp.float32
BF16 = jnp.bfloat16
I32 = jnp.int32

D_MODEL = 1024
N_META = 16
HGRN_WIDTH = 512
HGRN_HEADS = 4
HEAD_DIM = 128
POOL_WIDTH = 512
POOL_WINDOWS = (2, 4, 8, 16)
POOL_GROUP_DIM = 128
POOL_BUF = 15
IN_WIDTH = 4 * HGRN_WIDTH + POOL_WIDTH
N_EXPERTS = 256
TOP_K = 8
D_EXPERT = 256
ROUTED_SCALE = 2.5
DEPTH = 1
ALPHA = (2 * DEPTH) ** 0.25
LN_EPS = 1e-5
RMS_EPS = 1e-6

SUB = 16
MIX_BLOCK = 128
MOE_ROWS = 256
TILE_STRIDE = MOE_ROWS + 8
ORDER_ALIGN = 128
ROW_CHUNKS = D_MODEL // 128
PACK_ROWS = D_MODEL // 2 // 128
OUT_TILE = 512
VMEM_LIMIT = 48 * 1024 * 1024


def _ln(x, g, b):
    mu = jnp.mean(x, axis=-1, keepdims=True)
    xc = x - mu
    var = jnp.mean(xc * xc, axis=-1, keepdims=True)
    return xc * lax.rsqrt(var + LN_EPS) * g + b


def _sigmoid(z):
    return 1.0 / (1.0 + jnp.exp(-z))


def _bdot(a, b):
    return jnp.dot(a, b, preferred_element_type=F32)


def _ln_proj_kernel(x_ref, g_ref, b_ref, w_ref, lbl_ref, ng_ref,
                    xn_ref, q_ref, k_ref, gl_ref, v_ref, gs_ref, p_ref):
    xn = _ln(x_ref[...], g_ref[...], b_ref[...])
    xn_ref[...] = xn
    proj = _bdot(xn.astype(BF16), w_ref[...])
    lbl = lbl_ref[...]
    e = jnp.exp(lbl - jnp.max(lbl, axis=0, keepdims=True))
    lb = e[0:1] / jnp.sum(e, axis=0, keepdims=True)
    W = HGRN_WIDTH
    q = proj[:, 0:W]
    f = proj[:, W:2 * W]
    q_ref[...] = q * _sigmoid(q)
    k_ref[...] = (1.0 - lb) * _sigmoid(-f)
    gl_ref[...] = jnp.log(lb + (1.0 - lb) * _sigmoid(f))
    v_ref[...] = proj[:, 2 * W:3 * W]
    g = proj[:, 3 * W:4 * W]
    gs_ref[...] = ng_ref[...] * (g * _sigmoid(g))
    p_ref[...] = proj[:, 4 * W:]


def _row_tile(n_rows):
    for tm in (256, 128, 64, 32, 16, 8):
        if n_rows % tm == 0:
            return tm
    raise ValueError(f"row count {n_rows} is not a multiple of 8")


def _ln_proj(x, ln_g, ln_b, w_in_bf, lb_logits, norm_g):
    T = x.shape[0]
    tm = _row_tile(T)
    row = lambda i: (i, 0)
    const = lambda i: (0, 0)
    outs = [jax.ShapeDtypeStruct((T, D_MODEL), F32)] + [jax.ShapeDtypeStruct((T, HGRN_WIDTH), F32)] * 6
    return pl.pallas_call(
        _ln_proj_kernel,
        out_shape=outs,
        grid=(T // tm,),
        in_specs=[
            pl.BlockSpec((tm, D_MODEL), row),
            pl.BlockSpec((1, D_MODEL), const),
            pl.BlockSpec((1, D_MODEL), const),
            pl.BlockSpec((D_MODEL, IN_WIDTH), const),
            pl.BlockSpec((DEPTH + 1, HGRN_WIDTH), const),
            pl.BlockSpec((1, HGRN_WIDTH), const),
        ],
        out_specs=[pl.BlockSpec((tm, D_MODEL), row)] + [pl.BlockSpec((tm, HGRN_WIDTH), row)] * 6,
        compiler_params=pltpu.CompilerParams(dimension_semantics=("arbitrary",), vmem_limit_bytes=VMEM_LIMIT),
        name="ln_proj",
    )(x, ln_g, ln_b, w_in_bf, lb_logits, norm_g)


def _pool_group(pe, p_cur, gi, w):
    sl = slice(gi * POOL_GROUP_DIM, (gi + 1) * POOL_GROUP_DIM)
    s = pe[:, sl]
    sh = 1
    while sh < w:
        s = s + pltpu.roll(s, sh, 0)
        sh *= 2
    return s[SUB:, :] * (1.0 / w) - p_cur[:, sl]


def _mixer_kernel(q_ref, k_ref, g_ref, v_ref, gs_ref, p_ref, s0_ref, pp0_ref, wpool_ref, pscale_ref,
                  mix_ref, sfin_ref, st_scr, pe_scr):
    i = pl.program_id(1)
    nblk = pl.num_programs(1)
    n = MIX_BLOCK

    @pl.when(i == 0)
    def _():
        for h in range(HGRN_HEADS):
            st_scr[h] = s0_ref[h].T
        pe_scr[0:SUB, :] = pp0_ref[...]

    rows = lax.broadcasted_iota(I32, (n, HEAD_DIM), 0)
    r16 = rows & (SUB - 1)
    t8 = lax.broadcasted_iota(I32, (8, HEAD_DIM), 0)
    zero_bf = jnp.zeros((SUB, HEAD_DIM), BF16)

    o_heads = []
    for h in range(HGRN_HEADS):
        hs = slice(h * HEAD_DIM, (h + 1) * HEAD_DIM)
        Q = q_ref[:, hs]
        K = k_ref[:, hs]
        G = g_ref[:, hs]
        V = v_ref[:, hs]
        bf = G
        br = G
        for sh in (1, 2, 4, 8):
            bf = bf + jnp.where(r16 >= sh, pltpu.roll(bf, sh, 0), 0.0)
            br = br + jnp.where(r16 < SUB - sh, pltpu.roll(br, n - sh, 0), 0.0)
        br = br - G
        qt = (Q * jnp.exp(bf)).astype(BF16)
        kt = (K * jnp.exp(br)).astype(BF16)
        vt = V.T.astype(BF16)
        st = st_scr[h]
        o_parts = []
        for c in range(n // SUB):
            r0 = c * SUB
            b_top, b_bot = bf[r0:r0 + 8], bf[r0 + 8:r0 + 16]
            q_top, q_bot = Q[r0:r0 + 8], Q[r0 + 8:r0 + 16]
            acc_top = jnp.zeros((8, HEAD_DIM), F32)
            acc_bot = jnp.zeros((8, HEAD_DIM), F32)
            for s in range(SUB):
                bs = bf[r0 + s:r0 + s + 1]
                ks = K[r0 + s:r0 + s + 1]
                vs = V[r0 + s:r0 + s + 1]
                if s < 8:
                    col = jnp.sum(q_top * jnp.exp(b_top - bs) * ks, axis=-1, keepdims=True)
                    col = jnp.where(t8[:, 0:1] >= s, col, 0.0)
                    acc_top = acc_top + col * vs
                    col = jnp.sum(q_bot * jnp.exp(b_bot - bs) * ks, axis=-1, keepdims=True)
                    acc_bot = acc_bot + col * vs
                else:
                    col = jnp.sum(q_bot * jnp.exp(b_bot - bs) * ks, axis=-1, keepdims=True)
                    col = jnp.where(t8[:, 0:1] + 8 >= s, col, 0.0)
                    acc_bot = acc_bot + col * vs
            o_diag = jnp.concatenate([acc_top, acc_bot], axis=0)
            o_inter = lax.dot_general(qt[r0:r0 + SUB], st.astype(BF16), (((1,), (1,)), ((), ())),
                                      preferred_element_type=F32)
            o_parts.append(o_inter + o_diag)
            kmask = jnp.concatenate([zero_bf] * c + [kt[r0:r0 + SUB]] + [zero_bf] * (n // SUB - 1 - c), axis=0)
            d_st = _bdot(vt, kmask)
            st = st * jnp.exp(bf[r0 + SUB - 1:r0 + SUB]) + d_st
        st_scr[h] = st
        o = jnp.concatenate(o_parts, axis=0)
        o = o * lax.rsqrt(jnp.mean(o * o, axis=-1, keepdims=True) + RMS_EPS)
        o_heads.append(o * gs_ref[:, hs])

    p_cur = p_ref[...]
    pe_scr[SUB:SUB + n, :] = p_cur
    pe = pe_scr[...]
    ob = []
    for gi, w in enumerate(POOL_WINDOWS):
        pooled = _pool_group(pe, p_cur, gi, w)
        sl = slice(gi * POOL_GROUP_DIM, (gi + 1) * POOL_GROUP_DIM)
        ob.append(_bdot(pooled.astype(BF16), wpool_ref[gi]) * pscale_ref[:, sl])
    pe_scr[0:SUB, :] = p_cur[n - SUB:, :]

    mix_ref[...] = jnp.concatenate(o_heads + ob, axis=1).astype(BF16)

    @pl.when(i == nblk - 1)
    def _():
        for h in range(HGRN_HEADS):
            sfin_ref[0, h] = st_scr[h].T


def _mixer(q, k, g, v, gs, p, s0, pp0, wpool_bf, pscale, nseq, seqlen):
    nblk = seqlen // MIX_BLOCK
    tok = lambda b, i: (b * nblk + i, 0)
    tspec = pl.BlockSpec((MIX_BLOCK, HGRN_WIDTH), tok)
    return pl.pallas_call(
        _mixer_kernel,
        out_shape=[jax.ShapeDtypeStruct((nseq * seqlen, D_MODEL), BF16),
                   jax.ShapeDtypeStruct((nseq, HGRN_HEADS, HEAD_DIM, HEAD_DIM), F32)],
        grid=(nseq, nblk),
        in_specs=[tspec] * 6 + [
            pl.BlockSpec((HGRN_HEADS, HEAD_DIM, HEAD_DIM), lambda b, i: (0, 0, 0)),
            pl.BlockSpec((SUB, POOL_WIDTH), lambda b, i: (0, 0)),
            pl.BlockSpec((len(POOL_WINDOWS), POOL_GROUP_DIM, POOL_GROUP_DIM), lambda b, i: (0, 0, 0)),
            pl.BlockSpec((1, POOL_WIDTH), lambda b, i: (0, 0)),
        ],
        out_specs=[pl.BlockSpec((MIX_BLOCK, D_MODEL), tok),
                   pl.BlockSpec((1, HGRN_HEADS, HEAD_DIM, HEAD_DIM), lambda b, i: (b, 0, 0, 0))],
        scratch_shapes=[pltpu.VMEM((HGRN_HEADS, HEAD_DIM, HEAD_DIM), F32),
                        pltpu.VMEM((SUB + MIX_BLOCK, POOL_WIDTH), F32)],
        compiler_params=pltpu.CompilerParams(dimension_semantics=("arbitrary", "arbitrary"),
                                             vmem_limit_bytes=VMEM_LIMIT),
        name="mixer",
    )(q, k, g, v, gs, p, s0, pp0, wpool_bf, pscale)


STEP_SEQS = 32


def _mixer_step_kernel(qt_ref, kt_ref, gt_ref, v_ref, gs_ref, s_ref, snew_ref, oa_ref):
    qt = qt_ref[0, 0]
    kt = kt_ref[0, 0]
    dt = jnp.exp(gt_ref[0, 0])
    rows = []
    for bb in range(STEP_SEQS):
        sn = s_ref[bb, 0] * dt[:, bb:bb + 1] + kt[:, bb:bb + 1] * v_ref[bb:bb + 1, :]
        snew_ref[bb, 0] = sn
        rows.append(jnp.sum(sn * qt[:, bb:bb + 1], axis=0, keepdims=True))
    o = jnp.concatenate(rows, axis=0)
    o = o * lax.rsqrt(jnp.mean(o * o, axis=-1, keepdims=True) + RMS_EPS)
    oa_ref[...] = o * gs_ref[...]


def _mixer_step(qT, kT, gT, v, gs, state):
    nseq = v.shape[0]
    nbc = nseq // STEP_SEQS
    cspec = pl.BlockSpec((1, 1, HEAD_DIM, STEP_SEQS), lambda h, c: (h, c, 0, 0))
    rspec = pl.BlockSpec((STEP_SEQS, HEAD_DIM), lambda h, c: (c, h))
    sspec = pl.BlockSpec((STEP_SEQS, 1, HEAD_DIM, HEAD_DIM), lambda h, c: (c, h, 0, 0))
    return pl.pallas_call(
        _mixer_step_kernel,
        out_shape=[jax.ShapeDtypeStruct(state.shape, F32), jax.ShapeDtypeStruct((nseq, HGRN_WIDTH), F32)],
        grid=(HGRN_HEADS, nbc),
        in_specs=[cspec, cspec, cspec, rspec, rspec, sspec],
        out_specs=[sspec, rspec],
        compiler_params=pltpu.CompilerParams(dimension_semantics=("arbitrary", "arbitrary"),
                                             vmem_limit_bytes=VMEM_LIMIT),
        name="mixer_step",
    )(qT, kT, gT, v, gs, state)


def _pool_step_kernel(sp_ref, p_ref, wpool_ref, pscale_ref, ob_ref):
    p_cur = p_ref[...]
    outs = []
    for gi, w in enumerate(POOL_WINDOWS):
        sl = slice(gi * POOL_GROUP_DIM, (gi + 1) * POOL_GROUP_DIM)
        s = p_cur[:, sl]
        for r in range(POOL_BUF - (w - 1), POOL_BUF):
            s = s + sp_ref[r][:, sl]
        pooled = s * (1.0 / w) - p_cur[:, sl]
        outs.append(_bdot(pooled.astype(BF16), wpool_ref[gi]) * pscale_ref[:, sl])
    ob_ref[...] = jnp.concatenate(outs, axis=1)


def _pool_step(spT, p, wpool_bf, pscale):
    nseq = p.shape[0]
    return pl.pallas_call(
        _pool_step_kernel,
        out_shape=jax.ShapeDtypeStruct((nseq, POOL_WIDTH), F32),
        name="pool_step",
    )(spT, p, wpool_bf, pscale)


def _outproj_kernel(mixp_ref, xnp_ref, mixd_ref, xnd_ref, wout_ref, g1_ref, b1_ref, wr_ref, br_ref,
                    x1_ref, x1p_ref, idx_ref, gate_ref, cnt_ref, *, n_prompt_blocks, n_valid_last):
    tm = mixp_ref.shape[0]
    i = pl.program_id(0)
    is_prompt = i < n_prompt_blocks
    mix_in = jnp.where(is_prompt, mixp_ref[...], mixd_ref[...])
    xn = jnp.where(is_prompt, xnp_ref[...], xnd_ref[...])
    mix = _bdot(mix_in, wout_ref[...])
    x1 = _ln(ALPHA * xn + mix, g1_ref[...], b1_ref[...])
    x1_ref[...] = x1
    xb = x1.astype(BF16)
    xr = xb.astype(F32)
    half = D_MODEL // 2
    lo = lax.shift_right_logical(lax.bitcast_convert_type(xr[:, :half], jnp.uint32), jnp.uint32(16))
    hi = lax.bitcast_convert_type(xr[:, half:], jnp.uint32) & jnp.uint32(0xFFFF0000)
    words = lax.bitcast_convert_type(hi | lo, I32)
    for c in range(PACK_ROWS):
        x1p_ref[pl.ds(c, tm, stride=PACK_ROWS), :] = words[:, c * 128:(c + 1) * 128]

    scores = _sigmoid(_bdot(xb, wr_ref[...]))
    sel = scores + br_ref[...]
    lane = lax.broadcasted_iota(I32, (tm, N_EXPERTS), 1).astype(F32)
    lane_o = lax.broadcasted_iota(I32, (tm, 128), 1)
    idx_o = jnp.zeros((tm, 128), F32)
    ssum = jnp.zeros((tm, 1), F32)
    chosen = jnp.zeros((tm, N_EXPERTS), F32)
    s_sel = []
    for j in range(TOP_K):
        m = jnp.max(sel, axis=-1, keepdims=True)
        am = jnp.min(jnp.where(sel == m, lane, float(N_EXPERTS)), axis=-1, keepdims=True)
        hit = lane == am
        sj = jnp.sum(jnp.where(hit, scores, 0.0), axis=-1, keepdims=True)
        sel = jnp.where(hit, -jnp.inf, sel)
        chosen = jnp.where(hit, 1.0, chosen)
        idx_o = jnp.where(lane_o == j, am, idx_o)
        s_sel.append(sj)
        ssum = ssum + sj
    idx_ref[...] = idx_o.astype(I32)
    for j in range(TOP_K):
        gate_ref[pl.ds(j, tm, stride=TOP_K), :] = jnp.broadcast_to(s_sel[j] / ssum * ROUTED_SCALE, (tm, 128))

    @pl.when(i == 0)
    def _():
        cnt_ref[...] = jnp.zeros_like(cnt_ref)

    row_id = lax.broadcasted_iota(I32, (tm, N_EXPERTS), 0)
    valid = jnp.logical_or(is_prompt, row_id < n_valid_last)
    cnt_ref[...] += jnp.sum(jnp.where(valid, chosen, 0.0), axis=0, keepdims=True)


def _outproj(mix_p, xn_p, mix_d, xn_d, wout_bf, g1, b1, wr_bf, br):
    tm = OUT_TILE
    n_prompt, n_dec = mix_p.shape[0], mix_d.shape[0]
    assert n_prompt % tm == 0 and n_dec <= tm and n_dec % 8 == 0
    nbp = n_prompt // tm
    T = (nbp + 1) * tm
    padrows = lambda a: jnp.pad(a, ((0, tm - n_dec), (0, 0)))
    row = lambda i: (i, 0)
    prow = lambda i: (jnp.minimum(i, nbp - 1), 0)
    const = lambda i: (0, 0)
    return pl.pallas_call(
        functools.partial(_outproj_kernel, n_prompt_blocks=nbp, n_valid_last=n_dec),
        out_shape=[jax.ShapeDtypeStruct((T, D_MODEL), F32), jax.ShapeDtypeStruct((T * PACK_ROWS, 128), I32),
                   jax.ShapeDtypeStruct((T, 128), I32), jax.ShapeDtypeStruct((T * TOP_K, 128), F32),
                   jax.ShapeDtypeStruct((1, N_EXPERTS), F32)],
        grid=(nbp + 1,),
        in_specs=[pl.BlockSpec((tm, D_MODEL), prow), pl.BlockSpec((tm, D_MODEL), prow),
                  pl.BlockSpec((tm, D_MODEL), const), pl.BlockSpec((tm, D_MODEL), const),
                  pl.BlockSpec((D_MODEL, D_MODEL), const), pl.BlockSpec((1, D_MODEL), const),
                  pl.BlockSpec((1, D_MODEL), const), pl.BlockSpec((D_MODEL, N_EXPERTS), const),
                  pl.BlockSpec((1, N_EXPERTS), const)],
        out_specs=[pl.BlockSpec((tm, D_MODEL), row), pl.BlockSpec((tm * PACK_ROWS, 128), row),
                   pl.BlockSpec((tm, 128), row), pl.BlockSpec((tm * TOP_K, 128), row),
                   pl.BlockSpec((1, N_EXPERTS), const)],
        compiler_params=pltpu.CompilerParams(dimension_semantics=("arbitrary",), vmem_limit_bytes=VMEM_LIMIT),
        name="outproj",
    )(mix_p, xn_p, padrows(mix_d), padrows(xn_d), wout_bf, g1, b1, wr_bf, br)


def _moe_kernel(eb0_ref, enb_ref, wsel_ref, tot_ref, bn_ref, bsrc_ref, bdel_ref,
                order_ref, xp_ref, wg_ref, wu_ref, wd_ref, yt_ref,
                tile_scr, ybuf, wgb, wub, wdb, ord_smem, sem, sem_o):
    del wsel_ref
    e = pl.program_id(0)
    g0 = eb0_ref[e]
    nblk = enb_ref[e]
    total = tot_ref[0]
    nb_max = bn_ref.shape[0]
    R = MOE_ROWS
    C = ROW_CHUNKS
    L = R + ORDER_ALIGN
    n_assign = yt_ref.shape[0] // C - 2 * TOP_K

    def fetch(g):
        gc = jnp.minimum(g, nb_max - 1)
        s = g % 4
        src = pl.multiple_of(bsrc_ref[gc], ORDER_ALIGN)
        dst = pl.multiple_of(s * L, ORDER_ALIGN)
        return pltpu.make_async_copy(order_ref.at[pl.ds(src, L)], ord_smem.at[pl.ds(dst, L)], sem_o.at[s])

    def list_base(g):
        return (g % 4) * L + bdel_ref[jnp.minimum(g, nb_max - 1)]

    def gather(g):
        base = list_base(g)
        trow = (g % 2) * (PACK_ROWS * TILE_STRIDE)
        for r in range(R):
            tok = lax.shift_right_logical(ord_smem[base + r], 3)
            t4 = pl.multiple_of(tok * PACK_ROWS, PACK_ROWS)
            tile_scr[pl.ds(trow + r, PACK_ROWS, stride=TILE_STRIDE), :] = xp_ref[pl.ds(t4, PACK_ROWS), :]

    def rows_sent(cnt):
        return pl.multiple_of(lax.shift_left(lax.shift_right_logical(cnt + 7, 3), 3), 8)

    def wait_rows(s, cnt):
        pltpu.make_async_copy(ybuf.at[s, pl.ds(0, cnt * C), :], yt_ref.at[pl.ds(0, cnt * C), :], sem.at[s]).wait()

    @pl.when(e == 0)
    def _():
        n_spare = 2 * TOP_K * C
        ybuf[0, 0:n_spare, :] = jnp.zeros((n_spare, 128), F32)
        init = pltpu.make_async_copy(ybuf.at[0, pl.ds(0, n_spare), :],
                                     yt_ref.at[pl.ds(yt_ref.shape[0] - n_spare, n_spare), :], sem.at[0])
        init.start()
        init.wait()
        fetch(0).start()
        fetch(1).start()
        fetch(0).wait()
        gather(0)

    @pl.when(nblk > 0)
    def _():
        wgb[...] = wg_ref[0].astype(BF16)
        wub[...] = wu_ref[0].astype(BF16)
        wdb[...] = wd_ref[0].astype(BF16)

    def block(k, carry):
        g = g0 + k
        slot = g % 2
        n = bn_ref[g]
        fetch(g + 1).wait()
        fetch(g + 2).start()

        @pl.when(g >= 2)
        def _():
            wait_rows(slot, rows_sent(bn_ref[jnp.maximum(g - 2, 0)]))

        trow = pl.multiple_of(slot * (PACK_ROWS * TILE_STRIDE), 8)
        los, his = [], []
        for j in range(PACK_ROWS):
            w = lax.bitcast_convert_type(tile_scr[pl.ds(trow + j * TILE_STRIDE, R), :], jnp.uint32)
            los.append(lax.bitcast_convert_type(lax.shift_left(w, jnp.uint32(16)), F32).astype(BF16))
            his.append(lax.bitcast_convert_type(w & jnp.uint32(0xFFFF0000), F32).astype(BF16))
        xg = jnp.concatenate(los + his, axis=1)
        gather(g + 1)
        hg = _bdot(xg, wgb[...])
        hu = _bdot(xg, wub[...])
        hb = (hg * _sigmoid(hg)) * hu
        y = _bdot(hb.astype(BF16), wdb[...])
        yb = ybuf.at[slot]
        for c in range(C):
            yb[pl.ds(c, R, stride=C), :] = y[:, c * 128:(c + 1) * 128]

        base = list_base(g)

        def send(r, d, u):
            pltpu.make_async_copy(yb.at[pl.ds(pl.multiple_of(r * C, C), C), :],
                                  yt_ref.at[pl.ds(pl.multiple_of(d * C, C), C), :], sem.at[slot]).start(priority=u % 2)

        def send_group(i, c2):
            for u in range(8):
                send(i * 8 + u, ord_smem[base + i * 8 + u], u)
            return c2

        n_full = lax.shift_right_logical(n, 3)
        lax.fori_loop(0, n_full, send_group, 0)

        @pl.when(n_full * 8 < n)
        def _():
            for u in range(8):
                r = n_full * 8 + u
                send(r, jnp.where(r < n, ord_smem[base + r], n_assign + slot * 8 + u), u)

        return carry

    lax.fori_loop(0, nblk, block, 0)

    @pl.when(e == pl.num_programs(0) - 1)
    def _():
        fetch(total + 1).wait()
        wait_rows((total - 1) % 2, rows_sent(bn_ref[jnp.maximum(total - 1, 0)]))

        @pl.when(total >= 2)
        def _():
            wait_rows(total % 2, rows_sent(bn_ref[jnp.maximum(total - 2, 0)]))


def _moe(ex_b0, ex_nb, ex_w, total, blk_n, blk_src, blk_delta, order, xp, wg, wu, wd, n_tok):
    R = MOE_ROWS
    wmap_in = lambda e, eb0, enb, wsel, tot, bn, bs, bd: (wsel[e], 0, 0)
    grid_spec = pltpu.PrefetchScalarGridSpec(
        num_scalar_prefetch=7,
        grid=(N_EXPERTS,),
        in_specs=[
            pl.BlockSpec(memory_space=pl.ANY),
            pl.BlockSpec(memory_space=pltpu.VMEM),
            pl.BlockSpec((1, D_MODEL, D_EXPERT), wmap_in),
            pl.BlockSpec((1, D_MODEL, D_EXPERT), wmap_in),
            pl.BlockSpec((1, D_EXPERT, D_MODEL), wmap_in),
        ],
        out_specs=pl.BlockSpec(memory_space=pl.ANY),
        scratch_shapes=[pltpu.VMEM((2 * PACK_ROWS * TILE_STRIDE, 128), I32),
                        pltpu.VMEM((2, R * ROW_CHUNKS, 128), F32),
                        pltpu.VMEM((D_MODEL, D_EXPERT), BF16),
                        pltpu.VMEM((D_MODEL, D_EXPERT), BF16),
                        pltpu.VMEM((D_EXPERT, D_MODEL), BF16),
                        pltpu.SMEM((4 * (R + ORDER_ALIGN),), I32),
                        pltpu.SemaphoreType.DMA((2,)),
                        pltpu.SemaphoreType.DMA((4,))],
    )
    return pl.pallas_call(
        _moe_kernel,
        out_shape=jax.ShapeDtypeStruct(((n_tok + 2) * TOP_K * ROW_CHUNKS, 128), F32),
        grid_spec=grid_spec,
        compiler_params=pltpu.CompilerParams(dimension_semantics=("arbitrary",),
                                             vmem_limit_bytes=58 * 1024 * 1024),
        name="moe",
    )(ex_b0, ex_nb, ex_w, total, blk_n, blk_src, blk_delta, order, xp, wg, wu, wd)


def _route_plan(idx, counts, n_tok):
    R = MOE_ROWS
    n_assign = n_tok * TOP_K
    nb = (n_assign + N_EXPERTS * (R - 1)) // R
    id_bits = (n_assign - 1).bit_length()
    assert id_bits + (N_EXPERTS - 1).bit_length() < 32
    key = lax.shift_left(idx.reshape(-1), id_bits) | jnp.arange(n_assign, dtype=I32)
    order = lax.sort(key) & ((1 << id_bits) - 1)
    order = jnp.concatenate([order, jnp.zeros((R + ORDER_ALIGN,), I32)])
    nblk_e = (counts + R - 1) // R
    bend = jnp.cumsum(nblk_e)
    bstart = bend - nblk_e
    cstart = jnp.cumsum(counts) - counts
    blk = jnp.arange(nb, dtype=I32)
    blk_e = jnp.minimum(jnp.sum((bend[None, :] <= blk[:, None]).astype(I32), axis=1), N_EXPERTS - 1)
    k = blk - bstart[blk_e]
    active = blk < bend[-1]
    blk_n = jnp.where(active, jnp.clip(counts[blk_e] - k * R, 0, R), 0).astype(I32)
    src = jnp.where(active, cstart[blk_e] + k * R, 0).astype(I32)
    blk_src = (src // ORDER_ALIGN) * ORDER_ALIGN
    ex = jnp.arange(N_EXPERTS, dtype=I32)
    ex_w = jnp.maximum(lax.cummax(jnp.where(nblk_e > 0, ex, -1)), 0).astype(I32)
    return (bstart.astype(I32), nblk_e.astype(I32), ex_w, bend[-1:].astype(I32),
            blk_n, blk_src, src - blk_src, order)


def _combine_kernel(x1_ref, yt_ref, gate_ref, wgs_ref, wus_ref, wds_ref, g2_ref, b2_ref, out_ref, r_scr):
    tm = x1_ref.shape[0]
    C = ROW_CHUNKS
    x1 = x1_ref[...]
    xb = x1.astype(BF16)
    hg = _bdot(xb, wgs_ref[...])
    hs = (hg * _sigmoid(hg)) * _bdot(xb, wus_ref[...])
    moe = _bdot(hs.astype(BF16), wds_ref[...])
    routed = yt_ref[:, 0] * gate_ref[:, 0:1, :]
    for j in range(1, TOP_K):
        routed = routed + yt_ref[:, j] * gate_ref[:, j:j + 1, :]
    r_scr[...] = routed.reshape(tm * C, 128)
    routed = jnp.concatenate([r_scr[pl.ds(c, tm, stride=C), :] for c in range(C)], axis=1)
    out_ref[...] = _ln(ALPHA * x1 + (moe + routed), g2_ref[...], b2_ref[...])


def _combine(x1, yt4, gate3, wgs_bf, wus_bf, wds_bf, g2, b2, row0, nrows):
    tm = 128
    assert nrows % tm == 0 and row0 % tm == 0
    off = row0 // tm
    row = lambda i: (i + off, 0)
    const = lambda i: (0, 0)
    return pl.pallas_call(
        _combine_kernel,
        out_shape=jax.ShapeDtypeStruct((nrows, D_MODEL), F32),
        grid=(nrows // tm,),
        in_specs=[pl.BlockSpec((tm, D_MODEL), row),
                  pl.BlockSpec((tm, TOP_K, ROW_CHUNKS, 128), lambda i: (i + off, 0, 0, 0)),
                  pl.BlockSpec((tm, TOP_K, 128), lambda i: (i + off, 0, 0)),
                  pl.BlockSpec((D_MODEL, D_EXPERT), const), pl.BlockSpec((D_MODEL, D_EXPERT), const),
                  pl.BlockSpec((D_EXPERT, D_MODEL), const),
                  pl.BlockSpec((1, D_MODEL), const), pl.BlockSpec((1, D_MODEL), const)],
        out_specs=pl.BlockSpec((tm, D_MODEL), lambda i: (i, 0)),
        scratch_shapes=[pltpu.VMEM((tm * ROW_CHUNKS, 128), F32)],
        compiler_params=pltpu.CompilerParams(dimension_semantics=("arbitrary",), vmem_limit_bytes=VMEM_LIMIT),
        name="combine",
    )(x1, yt4, gate3, wgs_bf, wus_bf, wds_bf, g2, b2)


def kernel(x_prompt, x_sample, state_hgrn, state_pool, meta_tokens, ln_emb_g, ln_emb_b, w_in, lb_logits, hgrn_norm_g, w_pool, pool_scale, w_out, ln1_g, ln1_b, w_router, b_router, w_gate_e, w_up_e, w_down_e, w_gate_s, w_up_s, w_down_s, ln2_g, ln2_b):
    nseq, seqlen, _ = x_prompt.shape
    ndec = x_sample.shape[0]
    l = 0
    row = lambda a: a.reshape(1, -1)
    w_in_bf = w_in[l].astype(BF16)
    wpool_bf = w_pool[l].astype(BF16)
    lng, lnb = row(ln_emb_g), row(ln_emb_b)
    ng, ps = row(hgrn_norm_g[l]), row(pool_scale[l])
    proj = functools.partial(_ln_proj, ln_g=lng, ln_b=lnb, w_in_bf=w_in_bf, lb_logits=lb_logits, norm_g=ng)

    m_xn, m_q, m_k, m_g, m_v, m_gs, m_p = proj(meta_tokens)
    pad = lambda a: jnp.pad(a, ((0, MIX_BLOCK - N_META), (0, 0)))
    zero_state = jnp.zeros((HGRN_HEADS, HEAD_DIM, HEAD_DIM), F32)
    _, s_meta = _mixer(pad(m_q), pad(m_k), pad(m_g), pad(m_v), pad(m_gs), pad(m_p), zero_state,
                       jnp.zeros((SUB, POOL_WIDTH), F32), wpool_bf, ps, 1, MIX_BLOCK)

    p_xn, p_q, p_k, p_g, p_v, p_gs, p_p = proj(x_prompt.reshape(nseq * seqlen, D_MODEL))
    p_mix, s_prompt = _mixer(p_q, p_k, p_g, p_v, p_gs, p_p, s_meta[0], m_p, wpool_bf, ps, nseq, seqlen)

    d_xn, d_q, d_k, d_g, d_v, d_gs, d_p = proj(x_sample.reshape(ndec, D_MODEL))
    cols = lambda a: a.reshape(ndec // STEP_SEQS, STEP_SEQS, HGRN_HEADS, HEAD_DIM).transpose(2, 0, 3, 1)
    s_dec, d_oa = _mixer_step(cols(d_q), cols(d_k), cols(d_g), d_v, d_gs, state_hgrn[l])
    d_ob = _pool_step(state_pool[l].transpose(1, 0, 2), d_p, wpool_bf, ps)
    d_mix = jnp.concatenate([d_oa, d_ob], axis=1).astype(BF16)

    n_tok = nseq * seqlen + ndec
    x1, xp, idx, gate, cnt = _outproj(p_mix, p_xn, d_mix, d_xn, w_out[l].astype(BF16), row(ln1_g[l]), row(ln1_b[l]),
                                      w_router[l].astype(BF16), row(b_router[l]))
    plan = _route_plan(idx[:n_tok, :TOP_K], cnt[0].astype(I32), n_tok)
    yt = _moe(*plan, xp, w_gate_e[l], w_up_e[l], w_down_e[l], n_tok)
    yt4 = yt.reshape(n_tok + 2, TOP_K, ROW_CHUNKS, 128)
    gate3 = gate.reshape(-1, TOP_K, 128)
    comb = functools.partial(_combine, x1, yt4, gate3, w_gate_s[l].astype(BF16), w_up_s[l].astype(BF16),
                             w_down_s[l].astype(BF16), row(ln2_g[l]), row(ln2_b[l]))
    y_prompt = comb(0, nseq * seqlen).reshape(nseq, seqlen, D_MODEL)
    y_sample = comb(nseq * seqlen, ndec).reshape(ndec, 1, D_MODEL)

    state_pool_prompt = p_p.reshape(nseq, seqlen, POOL_WIDTH)[:, seqlen - POOL_BUF:, :]
    state_pool_sample = jnp.concatenate([state_pool[l][:, 1:, :], d_p[:, None, :]], axis=1)
    return (y_prompt, y_sample, s_prompt[None], state_pool_prompt[None], s_dec[None], state_pool_sample[None])
```

```python
import functools

import jax
import jax.numpy as jnp
from jax import lax
from jax.experimental import pallas as pl
from jax.experimental.pallas import tpu as pltpu

F32 = jnp.float32
BF16 = jnp.bfloat16
I32 = jnp.int32

D_MODEL = 1024
N_META = 16
HGRN_WIDTH = 512
HGRN_HEADS = 4
HEAD_DIM = 128
POOL_WIDTH = 512
POOL_WINDOWS = (2, 4, 8, 16)
POOL_GROUP_DIM = 128
POOL_BUF = 15
IN_WIDTH = 4 * HGRN_WIDTH + POOL_WIDTH
N_EXPERTS = 256
TOP_K = 8
D_EXPERT = 256
ROUTED_SCALE = 2.5
DEPTH = 1
ALPHA = (2 * DEPTH) ** 0.25
LN_EPS = 1e-5
RMS_EPS = 1e-6

SUB = 16
MIX_BLOCK = 128
MOE_ROWS = 256
TILE_STRIDE = MOE_ROWS + 8
ORDER_ALIGN = 128
ROW_CHUNKS = D_MODEL // 128
PACK_ROWS = D_MODEL // 2 // 128
OUT_TILE = 512
VMEM_LIMIT = 48 * 1024 * 1024


def _ln(x, g, b):
    mu = jnp.mean(x, axis=-1, keepdims=True)
    xc = x - mu
    var = jnp.mean(xc * xc, axis=-1, keepdims=True)
    return xc * lax.rsqrt(var + LN_EPS) * g + b


def _sigmoid(z):
    return 1.0 / (1.0 + jnp.exp(-z))


def _bdot(a, b):
    return jnp.dot(a, b, preferred_element_type=F32)


def _ln_proj_kernel(x_ref, g_ref, b_ref, w_ref, lbl_ref, ng_ref,
                    xn_ref, q_ref, k_ref, gl_ref, v_ref, gs_ref, p_ref):
    xn = _ln(x_ref[...], g_ref[...], b_ref[...])
    xn_ref[...] = xn
    proj = _bdot(xn.astype(BF16), w_ref[...])
    lbl = lbl_ref[...]
    e = jnp.exp(lbl - jnp.max(lbl, axis=0, keepdims=True))
    lb = e[0:1] / jnp.sum(e, axis=0, keepdims=True)
    W = HGRN_WIDTH
    q = proj[:, 0:W]
    f = proj[:, W:2 * W]
    q_ref[...] = q * _sigmoid(q)
    k_ref[...] = (1.0 - lb) * _sigmoid(-f)
    gl_ref[...] = jnp.log(lb + (1.0 - lb) * _sigmoid(f))
    v_ref[...] = proj[:, 2 * W:3 * W]
    g = proj[:, 3 * W:4 * W]
    gs_ref[...] = ng_ref[...] * (g * _sigmoid(g))
    p_ref[...] = proj[:, 4 * W:]


def _row_tile(n_rows):
    for tm in (256, 128, 64, 32, 16, 8):
        if n_rows % tm == 0:
            return tm
    raise ValueError(f"row count {n_rows} is not a multiple of 8")


def _ln_proj(x, ln_g, ln_b, w_in_bf, lb_logits, norm_g):
    T = x.shape[0]
    tm = _row_tile(T)
    row = lambda i: (i, 0)
    const = lambda i: (0, 0)
    outs = [jax.ShapeDtypeStruct((T, D_MODEL), F32)] + [jax.ShapeDtypeStruct((T, HGRN_WIDTH), F32)] * 6
    return pl.pallas_call(
        _ln_proj_kernel,
        out_shape=outs,
        grid=(T // tm,),
        in_specs=[
            pl.BlockSpec((tm, D_MODEL), row),
            pl.BlockSpec((1, D_MODEL), const),
            pl.BlockSpec((1, D_MODEL), const),
            pl.BlockSpec((D_MODEL, IN_WIDTH), const),
            pl.BlockSpec((DEPTH + 1, HGRN_WIDTH), const),
            pl.BlockSpec((1, HGRN_WIDTH), const),
        ],
        out_specs=[pl.BlockSpec((tm, D_MODEL), row)] + [pl.BlockSpec((tm, HGRN_WIDTH), row)] * 6,
        compiler_params=pltpu.CompilerParams(dimension_semantics=("arbitrary",), vmem_limit_bytes=VMEM_LIMIT),
        name="ln_proj",
    )(x, ln_g, ln_b, w_in_bf, lb_logits, norm_g)


def _pool_group(pe, p_cur, gi, w):
    sl = slice(gi * POOL_GROUP_DIM, (gi + 1) * POOL_GROUP_DIM)
    s = pe[:, sl]
    sh = 1
    while sh < w:
        s = s + pltpu.roll(s, sh, 0)
        sh *= 2
    return s[SUB:, :] * (1.0 / w) - p_cur[:, sl]


def _mixer_kernel(q_ref, k_ref, g_ref, v_ref, gs_ref, p_ref, s0_ref, pp0_ref, wpool_ref, pscale_ref,
                  mix_ref, sfin_ref, st_scr, pe_scr):
    i = pl.program_id(1)
    nblk = pl.num_programs(1)
    n = MIX_BLOCK

    @pl.when(i == 0)
    def _():
        for h in range(HGRN_HEADS):
            st_scr[h] = s0_ref[h].T
        pe_scr[0:SUB, :] = pp0_ref[...]

    rows = lax.broadcasted_iota(I32, (n, HEAD_DIM), 0)
    r16 = rows & (SUB - 1)
    t8 = lax.broadcasted_iota(I32, (8, HEAD_DIM), 0)
    zero_bf = jnp.zeros((SUB, HEAD_DIM), BF16)

    o_heads = []
    for h in range(HGRN_HEADS):
        hs = slice(h * HEAD_DIM, (h + 1) * HEAD_DIM)
        Q = q_ref[:, hs]
        K = k_ref[:, hs]
        G = g_ref[:, hs]
        V = v_ref[:, hs]
        bf = G
        br = G
        for sh in (1, 2, 4, 8):
            bf = bf + jnp.where(r16 >= sh, pltpu.roll(bf, sh, 0), 0.0)
            br = br + jnp.where(r16 < SUB - sh, pltpu.roll(br, n - sh, 0), 0.0)
        br = br - G
        qt = (Q * jnp.exp(bf)).astype(BF16)
        kt = (K * jnp.exp(br)).astype(BF16)
        vt = V.T.astype(BF16)
        st = st_scr[h]
        o_parts = []
        for c in range(n // SUB):
            r0 = c * SUB
            b_top, b_bot = bf[r0:r0 + 8], bf[r0 + 8:r0 + 16]
            q_top, q_bot = Q[r0:r0 + 8], Q[r0 + 8:r0 + 16]
            acc_top = jnp.zeros((8, HEAD_DIM), F32)
            acc_bot = jnp.zeros((8, HEAD_DIM), F32)
            for s in range(SUB):
                bs = bf[r0 + s:r0 + s + 1]
                ks = K[r0 + s:r0 + s + 1]
                vs = V[r0 + s:r0 + s + 1]
                if s < 8:
                    col = jnp.sum(q_top * jnp.exp(b_top - bs) * ks, axis=-1, keepdims=True)
                    col = jnp.where(t8[:, 0:1] >= s, col, 0.0)
                    acc_top = acc_top + col * vs
                    col = jnp.sum(q_bot * jnp.exp(b_bot - bs) * ks, axis=-1, keepdims=True)
                    acc_bot = acc_bot + col * vs
                else:
                    col = jnp.sum(q_bot * jnp.exp(b_bot - bs) * ks, axis=-1, keepdims=True)
                    col = jnp.where(t8[:, 0:1] + 8 >= s, col, 0.0)
                    acc_bot = acc_bot + col * vs
            o_diag = jnp.concatenate([acc_top, acc_bot], axis=0)
            o_inter = lax.dot_general(qt[r0:r0 + SUB], st.astype(BF16), (((1,), (1,)), ((), ())),
                                      preferred_element_type=F32)
            o_parts.append(o_inter + o_diag)
            kmask = jnp.concatenate([zero_bf] * c + [kt[r0:r0 + SUB]] + [zero_bf] * (n // SUB - 1 - c), axis=0)
            d_st = _bdot(vt, kmask)
            st = st * jnp.exp(bf[r0 + SUB - 1:r0 + SUB]) + d_st
        st_scr[h] = st
        o = jnp.concatenate(o_parts, axis=0)
        o = o * lax.rsqrt(jnp.mean(o * o, axis=-1, keepdims=True) + RMS_EPS)
        o_heads.append(o * gs_ref[:, hs])

    p_cur = p_ref[...]
    pe_scr[SUB:SUB + n, :] = p_cur
    pe = pe_scr[...]
    ob = []
    for gi, w in enumerate(POOL_WINDOWS):
        pooled = _pool_group(pe, p_cur, gi, w)
        sl = slice(gi * POOL_GROUP_DIM, (gi + 1) * POOL_GROUP_DIM)
        ob.append(_bdot(pooled.astype(BF16), wpool_ref[gi]) * pscale_ref[:, sl])
    pe_scr[0:SUB, :] = p_cur[n - SUB:, :]

    mix_ref[...] = jnp.concatenate(o_heads + ob, axis=1).astype(BF16)

    @pl.when(i == nblk - 1)
    def _():
        for h in range(HGRN_HEADS):
            sfin_ref[0, h] = st_scr[h].T


def _mixer(q, k, g, v, gs, p, s0, pp0, wpool_bf, pscale, nseq, seqlen):
    nblk = seqlen // MIX_BLOCK
    tok = lambda b, i: (b * nblk + i, 0)
    tspec = pl.BlockSpec((MIX_BLOCK, HGRN_WIDTH), tok)
    return pl.pallas_call(
        _mixer_kernel,
        out_shape=[jax.ShapeDtypeStruct((nseq * seqlen, D_MODEL), BF16),
                   jax.ShapeDtypeStruct((nseq, HGRN_HEADS, HEAD_DIM, HEAD_DIM), F32)],
        grid=(nseq, nblk),
        in_specs=[tspec] * 6 + [
            pl.BlockSpec((HGRN_HEADS, HEAD_DIM, HEAD_DIM), lambda b, i: (0, 0, 0)),
            pl.BlockSpec((SUB, POOL_WIDTH), lambda b, i: (0, 0)),
            pl.BlockSpec((len(POOL_WINDOWS), POOL_GROUP_DIM, POOL_GROUP_DIM), lambda b, i: (0, 0, 0)),
            pl.BlockSpec((1, POOL_WIDTH), lambda b, i: (0, 0)),
        ],
        out_specs=[pl.BlockSpec((MIX_BLOCK, D_MODEL), tok),
                   pl.BlockSpec((1, HGRN_HEADS, HEAD_DIM, HEAD_DIM), lambda b, i: (b, 0, 0, 0))],
        scratch_shapes=[pltpu.VMEM((HGRN_HEADS, HEAD_DIM, HEAD_DIM), F32),
                        pltpu.VMEM((SUB + MIX_BLOCK, POOL_WIDTH), F32)],
        compiler_params=pltpu.CompilerParams(dimension_semantics=("arbitrary", "arbitrary"),
                                             vmem_limit_bytes=VMEM_LIMIT),
        name="mixer",
    )(q, k, g, v, gs, p, s0, pp0, wpool_bf, pscale)


STEP_SEQS = 32


def _mixer_step_kernel(qt_ref, kt_ref, gt_ref, v_ref, gs_ref, s_ref, snew_ref, oa_ref):
    qt = qt_ref[0, 0]
    kt = kt_ref[0, 0]
    dt = jnp.exp(gt_ref[0, 0])
    rows = []
    for bb in range(STEP_SEQS):
        sn = s_ref[bb, 0] * dt[:, bb:bb + 1] + kt[:, bb:bb + 1] * v_ref[bb:bb + 1, :]
        snew_ref[bb, 0] = sn
        rows.append(jnp.sum(sn * qt[:, bb:bb + 1], axis=0, keepdims=True))
    o = jnp.concatenate(rows, axis=0)
    o = o * lax.rsqrt(jnp.mean(o * o, axis=-1, keepdims=True) + RMS_EPS)
    oa_ref[...] = o * gs_ref[...]


def _mixer_step(qT, kT, gT, v, gs, state):
    nseq = v.shape[0]
    nbc = nseq // STEP_SEQS
    cspec = pl.BlockSpec((1, 1, HEAD_DIM, STEP_SEQS), lambda h, c: (h, c, 0, 0))
    rspec = pl.BlockSpec((STEP_SEQS, HEAD_DIM), lambda h, c: (c, h))
    sspec = pl.BlockSpec((STEP_SEQS, 1, HEAD_DIM, HEAD_DIM), lambda h, c: (c, h, 0, 0))
    return pl.pallas_call(
        _mixer_step_kernel,
        out_shape=[jax.ShapeDtypeStruct(state.shape, F32), jax.ShapeDtypeStruct((nseq, HGRN_WIDTH), F32)],
        grid=(HGRN_HEADS, nbc),
        in_specs=[cspec, cspec, cspec, rspec, rspec, sspec],
        out_specs=[sspec, rspec],
        compiler_params=pltpu.CompilerParams(dimension_semantics=("arbitrary", "arbitrary"),
                                             vmem_limit_bytes=VMEM_LIMIT),
        name="mixer_step",
    )(qT, kT, gT, v, gs, state)


def _pool_step_kernel(sp_ref, p_ref, wpool_ref, pscale_ref, ob_ref):
    p_cur = p_ref[...]
    outs = []
    for gi, w in enumerate(POOL_WINDOWS):
        sl = slice(gi * POOL_GROUP_DIM, (gi + 1) * POOL_GROUP_DIM)
        s = p_cur[:, sl]
        for r in range(POOL_BUF - (w - 1), POOL_BUF):
            s = s + sp_ref[r][:, sl]
        pooled = s * (1.0 / w) - p_cur[:, sl]
        outs.append(_bdot(pooled.astype(BF16), wpool_ref[gi]) * pscale_ref[:, sl])
    ob_ref[...] = jnp.concatenate(outs, axis=1)


def _pool_step(spT, p, wpool_bf, pscale):
    nseq = p.shape[0]
    return pl.pallas_call(
        _pool_step_kernel,
        out_shape=jax.ShapeDtypeStruct((nseq, POOL_WIDTH), F32),
        name="pool_step",
    )(spT, p, wpool_bf, pscale)


def _outproj_kernel(mixp_ref, xnp_ref, mixd_ref, xnd_ref, wout_ref, g1_ref, b1_ref, wr_ref, br_ref,
                    x1_ref, x1p_ref, idx_ref, gate_ref, cnt_ref, *, n_prompt_blocks, n_valid_last):
    tm = mixp_ref.shape[0]
    i = pl.program_id(0)
    is_prompt = i < n_prompt_blocks
    mix_in = jnp.where(is_prompt, mixp_ref[...], mixd_ref[...])
    xn = jnp.where(is_prompt, xnp_ref[...], xnd_ref[...])
    mix = _bdot(mix_in, wout_ref[...])
    x1 = _ln(ALPHA * xn + mix, g1_ref[...], b1_ref[...])
    x1_ref[...] = x1
    xb = x1.astype(BF16)
    xr = xb.astype(F32)
    half = D_MODEL // 2
    lo = lax.shift_right_logical(lax.bitcast_convert_type(xr[:, :half], jnp.uint32), jnp.uint32(16))
    hi = lax.bitcast_convert_type(xr[:, half:], jnp.uint32) & jnp.uint32(0xFFFF0000)
    words = lax.bitcast_convert_type(hi | lo, I32)
    for c in range(PACK_ROWS):
        x1p_ref[pl.ds(c, tm, stride=PACK_ROWS), :] = words[:, c * 128:(c + 1) * 128]

    scores = _sigmoid(_bdot(xb, wr_ref[...]))
    sel = scores + br_ref[...]
    lane = lax.broadcasted_iota(I32, (tm, N_EXPERTS), 1).astype(F32)
    lane_o = lax.broadcasted_iota(I32, (tm, 128), 1)
    idx_o = jnp.zeros((tm, 128), F32)
    ssum = jnp.zeros((tm, 1), F32)
    chosen = jnp.zeros((tm, N_EXPERTS), F32)
    s_sel = []
    for j in range(TOP_K):
        m = jnp.max(sel, axis=-1, keepdims=True)
        am = jnp.min(jnp.where(sel == m, lane, float(N_EXPERTS)), axis=-1, keepdims=True)
        hit = lane == am
        sj = jnp.sum(jnp.where(hit, scores, 0.0), axis=-1, keepdims=True)
        sel = jnp.where(hit, -jnp.inf, sel)
        chosen = jnp.where(hit, 1.0, chosen)
        idx_o = jnp.where(lane_o == j, am, idx_o)
        s_sel.append(sj)
        ssum = ssum + sj
    idx_ref[...] = idx_o.astype(I32)
    for j in range(TOP_K):
        gate_ref[pl.ds(j, tm, stride=TOP_K), :] = jnp.broadcast_to(s_sel[j] / ssum * ROUTED_SCALE, (tm, 128))

    @pl.when(i == 0)
    def _():
        cnt_ref[...] = jnp.zeros_like(cnt_ref)

    row_id = lax.broadcasted_iota(I32, (tm, N_EXPERTS), 0)
    valid = jnp.logical_or(is_prompt, row_id < n_valid_last)
    cnt_ref[...] += jnp.sum(jnp.where(valid, chosen, 0.0), axis=0, keepdims=True)


def _outproj(mix_p, xn_p, mix_d, xn_d, wout_bf, g1, b1, wr_bf, br):
    tm = OUT_TILE
    n_prompt, n_dec = mix_p.shape[0], mix_d.shape[0]
    assert n_prompt % tm == 0 and n_dec <= tm and n_dec % 8 == 0
    nbp = n_prompt // tm
    T = (nbp + 1) * tm
    padrows = lambda a: jnp.pad(a, ((0, tm - n_dec), (0, 0)))
    row = lambda i: (i, 0)
    prow = lambda i: (jnp.minimum(i, nbp - 1), 0)
    const = lambda i: (0, 0)
    return pl.pallas_call(
        functools.partial(_outproj_kernel, n_prompt_blocks=nbp, n_valid_last=n_dec),
        out_shape=[jax.ShapeDtypeStruct((T, D_MODEL), F32), jax.ShapeDtypeStruct((T * PACK_ROWS, 128), I32),
                   jax.ShapeDtypeStruct((T, 128), I32), jax.ShapeDtypeStruct((T * TOP_K, 128), F32),
                   jax.ShapeDtypeStruct((1, N_EXPERTS), F32)],
        grid=(nbp + 1,),
        in_specs=[pl.BlockSpec((tm, D_MODEL), prow), pl.BlockSpec((tm, D_MODEL), prow),
                  pl.BlockSpec((tm, D_MODEL), const), pl.BlockSpec((tm, D_MODEL), const),
                  pl.BlockSpec((D_MODEL, D_MODEL), const), pl.BlockSpec((1, D_MODEL), const),
                  pl.BlockSpec((1, D_MODEL), const), pl.BlockSpec((D_MODEL, N_EXPERTS), const),
                  pl.BlockSpec((1, N_EXPERTS), const)],
        out_specs=[pl.BlockSpec((tm, D_MODEL), row), pl.BlockSpec((tm * PACK_ROWS, 128), row),
                   pl.BlockSpec((tm, 128), row), pl.BlockSpec((tm * TOP_K, 128), row),
                   pl.BlockSpec((1, N_EXPERTS), const)],
        compiler_params=pltpu.CompilerParams(dimension_semantics=("arbitrary",), vmem_limit_bytes=VMEM_LIMIT),
        name="outproj",
    )(mix_p, xn_p, padrows(mix_d), padrows(xn_d), wout_bf, g1, b1, wr_bf, br)


def _moe_kernel(eb0_ref, enb_ref, wsel_ref, tot_ref, bn_ref, bsrc_ref, bdel_ref,
                order_ref, xp_ref, wg_ref, wu_ref, wd_ref, yt_ref,
                tile_scr, ybuf, wgb, wub, wdb, ord_smem, sem, sem_o):
    del wsel_ref
    e = pl.program_id(0)
    g0 = eb0_ref[e]
    nblk = enb_ref[e]
    total = tot_ref[0]
    nb_max = bn_ref.shape[0]
    R = MOE_ROWS
    C = ROW_CHUNKS
    L = R + ORDER_ALIGN
    n_assign = yt_ref.shape[0] // C - 2 * TOP_K

    def fetch(g):
        gc = jnp.minimum(g, nb_max - 1)
        s = g % 4
        src = pl.multiple_of(bsrc_ref[gc], ORDER_ALIGN)
        dst = pl.multiple_of(s * L, ORDER_ALIGN)
        return pltpu.make_async_copy(order_ref.at[pl.ds(src, L)], ord_smem.at[pl.ds(dst, L)], sem_o.at[s])

    def list_base(g):
        return (g % 4) * L + bdel_ref[jnp.minimum(g, nb_max - 1)]

    def gather(g):
        base = list_base(g)
        trow = (g % 2) * (PACK_ROWS * TILE_STRIDE)
        for r in range(R):
            tok = lax.shift_right_logical(ord_smem[base + r], 3)
            t4 = pl.multiple_of(tok * PACK_ROWS, PACK_ROWS)
            tile_scr[pl.ds(trow + r, PACK_ROWS, stride=TILE_STRIDE), :] = xp_ref[pl.ds(t4, PACK_ROWS), :]

    def rows_sent(cnt):
        return pl.multiple_of(lax.shift_left(lax.shift_right_logical(cnt + 7, 3), 3), 8)

    def wait_rows(s, cnt):
        pltpu.make_async_copy(ybuf.at[s, pl.ds(0, cnt * C), :], yt_ref.at[pl.ds(0, cnt * C), :], sem.at[s]).wait()

    @pl.when(e == 0)
    def _():
        n_spare = 2 * TOP_K * C
        ybuf[0, 0:n_spare, :] = jnp.zeros((n_spare, 128), F32)
        init = pltpu.make_async_copy(ybuf.at[0, pl.ds(0, n_spare), :],
                                     yt_ref.at[pl.ds(yt_ref.shape[0] - n_spare, n_spare), :], sem.at[0])
        init.start()
        init.wait()
        fetch(0).start()
        fetch(1).start()
        fetch(0).wait()
        gather(0)

    @pl.when(nblk > 0)
    def _():
        wgb[...] = wg_ref[0].astype(BF16)
        wub[...] = wu_ref[0].astype(BF16)
        wdb[...] = wd_ref[0].astype(BF16)

    def block(k, carry):
        g = g0 + k
        slot = g % 2
        n = bn_ref[g]
        fetch(g + 1).wait()
        fetch(g + 2).start()

        @pl.when(g >= 2)
        def _():
            wait_rows(slot, rows_sent(bn_ref[jnp.maximum(g - 2, 0)]))

        trow = pl.multiple_of(slot * (PACK_ROWS * TILE_STRIDE), 8)

        def compute(send_prev):
            los, his = [], []
            for j in range(PACK_ROWS):
                w = lax.bitcast_convert_type(tile_scr[pl.ds(trow + j * TILE_STRIDE, R), :], jnp.uint32)
                los.append(lax.bitcast_convert_type(lax.shift_left(w, jnp.uint32(16)), F32).astype(BF16))
                his.append(lax.bitcast_convert_type(w & jnp.uint32(0xFFFF0000), F32).astype(BF16))
            xg = jnp.concatenate(los + his, axis=1)
            gather(g + 1)
            if send_prev:
                pbase = list_base(g - 1)
                for r in range(R):
                    send(1 - slot, r, ord_smem[pbase + r], r)
            hg = _bdot(xg, wgb[...])
            hu = _bdot(xg, wub[...])
            hb = (hg * _sigmoid(hg)) * hu
            y = _bdot(hb.astype(BF16), wdb[...])
            yb = ybuf.at[slot]
            for c in range(C):
                yb[pl.ds(c, R, stride=C), :] = y[:, c * 128:(c + 1) * 128]

        prev_full = jnp.logical_and(g >= 1, bn_ref[jnp.maximum(g - 1, 0)] == R)

        @pl.when(prev_full)
        def _():
            compute(True)

        @pl.when(jnp.logical_not(prev_full))
        def _():
            compute(False)

        @pl.when(n < R)
        def _():
            send_rows(g)

        return carry

    def send(s, r, d, u):
        pltpu.make_async_copy(ybuf.at[s, pl.ds(pl.multiple_of(r * C, C), C), :],
                              yt_ref.at[pl.ds(pl.multiple_of(d * C, C), C), :], sem.at[s]).start(priority=u % 2)

    def send_rows(g):
        slot = g % 2
        n = bn_ref[g]
        base = list_base(g)

        def send_group(i, c2):
            for u in range(8):
                send(slot, i * 8 + u, ord_smem[base + i * 8 + u], u)
            return c2

        n_full = lax.shift_right_logical(n, 3)
        lax.fori_loop(0, n_full, send_group, 0)

        @pl.when(n_full * 8 < n)
        def _():
            for u in range(8):
                r = n_full * 8 + u
                send(slot, r, jnp.where(r < n, ord_smem[base + r], n_assign + slot * 8 + u), u)

    lax.fori_loop(0, nblk, block, 0)

    @pl.when(e == pl.num_programs(0) - 1)
    def _():
        @pl.when(bn_ref[jnp.maximum(total - 1, 0)] == R)
        def _():
            send_rows(total - 1)

        fetch(total + 1).wait()
        wait_rows((total - 1) % 2, rows_sent(bn_ref[jnp.maximum(total - 1, 0)]))

        @pl.when(total >= 2)
        def _():
            wait_rows(total % 2, rows_sent(bn_ref[jnp.maximum(total - 2, 0)]))


def _moe(ex_b0, ex_nb, ex_w, total, blk_n, blk_src, blk_delta, order, xp, wg, wu, wd, n_tok):
    R = MOE_ROWS
    wmap_in = lambda e, eb0, enb, wsel, tot, bn, bs, bd: (wsel[e], 0, 0)
    grid_spec = pltpu.PrefetchScalarGridSpec(
        num_scalar_prefetch=7,
        grid=(N_EXPERTS,),
        in_specs=[
            pl.BlockSpec(memory_space=pl.ANY),
            pl.BlockSpec(memory_space=pltpu.VMEM),
            pl.BlockSpec((1, D_MODEL, D_EXPERT), wmap_in),
            pl.BlockSpec((1, D_MODEL, D_EXPERT), wmap_in),
            pl.BlockSpec((1, D_EXPERT, D_MODEL), wmap_in),
        ],
        out_specs=pl.BlockSpec(memory_space=pl.ANY),
        scratch_shapes=[pltpu.VMEM((2 * PACK_ROWS * TILE_STRIDE, 128), I32),
                        pltpu.VMEM((2, R * ROW_CHUNKS, 128), F32),
                        pltpu.VMEM((D_MODEL, D_EXPERT), BF16),
                        pltpu.VMEM((D_MODEL, D_EXPERT), BF16),
                        pltpu.VMEM((D_EXPERT, D_MODEL), BF16),
                        pltpu.SMEM((4 * (R + ORDER_ALIGN),), I32),
                        pltpu.SemaphoreType.DMA((2,)),
                        pltpu.SemaphoreType.DMA((4,))],
    )
    return pl.pallas_call(
        _moe_kernel,
        out_shape=jax.ShapeDtypeStruct(((n_tok + 2) * TOP_K * ROW_CHUNKS, 128), F32),
        grid_spec=grid_spec,
        compiler_params=pltpu.CompilerParams(dimension_semantics=("arbitrary",),
                                             vmem_limit_bytes=58 * 1024 * 1024),
        name="moe",
    )(ex_b0, ex_nb, ex_w, total, blk_n, blk_src, blk_delta, order, xp, wg, wu, wd)


def _route_plan(idx, counts, n_tok):
    R = MOE_ROWS
    n_assign = n_tok * TOP_K
    nb = (n_assign + N_EXPERTS * (R - 1)) // R
    id_bits = (n_assign - 1).bit_length()
    assert id_bits + (N_EXPERTS - 1).bit_length() < 32
    key = lax.shift_left(idx.reshape(-1), id_bits) | jnp.arange(n_assign, dtype=I32)
    order = lax.sort(key) & ((1 << id_bits) - 1)
    order = jnp.concatenate([order, jnp.zeros((R + ORDER_ALIGN,), I32)])
    nblk_e = (counts + R - 1) // R
    bend = jnp.cumsum(nblk_e)
    bstart = bend - nblk_e
    cstart = jnp.cumsum(counts) - counts
    blk = jnp.arange(nb, dtype=I32)
    blk_e = jnp.minimum(jnp.sum((bend[None, :] <= blk[:, None]).astype(I32), axis=1), N_EXPERTS - 1)
    k = blk - bstart[blk_e]
    active = blk < bend[-1]
    blk_n = jnp.where(active, jnp.clip(counts[blk_e] - k * R, 0, R), 0).astype(I32)
    src = jnp.where(active, cstart[blk_e] + k * R, 0).astype(I32)
    blk_src = (src // ORDER_ALIGN) * ORDER_ALIGN
    ex = jnp.arange(N_EXPERTS, dtype=I32)
    ex_w = jnp.maximum(lax.cummax(jnp.where(nblk_e > 0, ex, -1)), 0).astype(I32)
    return (bstart.astype(I32), nblk_e.astype(I32), ex_w, bend[-1:].astype(I32),
            blk_n, blk_src, src - blk_src, order)


def _combine_kernel(x1_ref, yt_ref, gate_ref, wgs_ref, wus_ref, wds_ref, g2_ref, b2_ref, out_ref, r_scr):
    tm = x1_ref.shape[0]
    C = ROW_CHUNKS
    x1 = x1_ref[...]
    xb = x1.astype(BF16)
    hg = _bdot(xb, wgs_ref[...])
    hs = (hg * _sigmoid(hg)) * _bdot(xb, wus_ref[...])
    moe = _bdot(hs.astype(BF16), wds_ref[...])
    routed = yt_ref[:, 0] * gate_ref[:, 0:1, :]
    for j in range(1, TOP_K):
        routed = routed + yt_ref[:, j] * gate_ref[:, j:j + 1, :]
    r_scr[...] = routed.reshape(tm * C, 128)
    routed = jnp.concatenate([r_scr[pl.ds(c, tm, stride=C), :] for c in range(C)], axis=1)
    out_ref[...] = _ln(ALPHA * x1 + (moe + routed), g2_ref[...], b2_ref[...])


def _combine(x1, yt4, gate3, wgs_bf, wus_bf, wds_bf, g2, b2, row0, nrows):
    tm = 128
    assert nrows % tm == 0 and row0 % tm == 0
    off = row0 // tm
    row = lambda i: (i + off, 0)
    const = lambda i: (0, 0)
    return pl.pallas_call(
        _combine_kernel,
        out_shape=jax.ShapeDtypeStruct((nrows, D_MODEL), F32),
        grid=(nrows // tm,),
        in_specs=[pl.BlockSpec((tm, D_MODEL), row),
                  pl.BlockSpec((tm, TOP_K, ROW_CHUNKS, 128), lambda i: (i + off, 0, 0, 0)),
                  pl.BlockSpec((tm, TOP_K, 128), lambda i: (i + off, 0, 0)),
                  pl.BlockSpec((D_MODEL, D_EXPERT), const), pl.BlockSpec((D_MODEL, D_EXPERT), const),
                  pl.BlockSpec((D_EXPERT, D_MODEL), const),
                  pl.BlockSpec((1, D_MODEL), const), pl.BlockSpec((1, D_MODEL), const)],
        out_specs=pl.BlockSpec((tm, D_MODEL), lambda i: (i, 0)),
        scratch_shapes=[pltpu.VMEM((tm * ROW_CHUNKS, 128), F32)],
        compiler_params=pltpu.CompilerParams(dimension_semantics=("arbitrary",), vmem_limit_bytes=VMEM_LIMIT),
        name="combine",
    )(x1, yt4, gate3, wgs_bf, wus_bf, wds_bf, g2, b2)


def kernel(x_prompt, x_sample, state_hgrn, state_pool, meta_tokens, ln_emb_g, ln_emb_b, w_in, lb_logits, hgrn_norm_g, w_pool, pool_scale, w_out, ln1_g, ln1_b, w_router, b_router, w_gate_e, w_up_e, w_down_e, w_gate_s, w_up_s, w_down_s, ln2_g, ln2_b):
    nseq, seqlen, _ = x_prompt.shape
    ndec = x_sample.shape[0]
    l = 0
    row = lambda a: a.reshape(1, -1)
    w_in_bf = w_in[l].astype(BF16)
    wpool_bf = w_pool[l].astype(BF16)
    lng, lnb = row(ln_emb_g), row(ln_emb_b)
    ng, ps = row(hgrn_norm_g[l]), row(pool_scale[l])
    proj = functools.partial(_ln_proj, ln_g=lng, ln_b=lnb, w_in_bf=w_in_bf, lb_logits=lb_logits, norm_g=ng)

    m_xn, m_q, m_k, m_g, m_v, m_gs, m_p = proj(meta_tokens)
    pad = lambda a: jnp.pad(a, ((0, MIX_BLOCK - N_META), (0, 0)))
    zero_state = jnp.zeros((HGRN_HEADS, HEAD_DIM, HEAD_DIM), F32)
    _, s_meta = _mixer(pad(m_q), pad(m_k), pad(m_g), pad(m_v), pad(m_gs), pad(m_p), zero_state,
                       jnp.zeros((SUB, POOL_WIDTH), F32), wpool_bf, ps, 1, MIX_BLOCK)

    p_xn, p_q, p_k, p_g, p_v, p_gs, p_p = proj(x_prompt.reshape(nseq * seqlen, D_MODEL))
    p_mix, s_prompt = _mixer(p_q, p_k, p_g, p_v, p_gs, p_p, s_meta[0], m_p, wpool_bf, ps, nseq, seqlen)

    d_xn, d_q, d_k, d_g, d_v, d_gs, d_p = proj(x_sample.reshape(ndec, D_MODEL))
    cols = lambda a: a.reshape(ndec // STEP_SEQS, STEP_SEQS, HGRN_HEADS, HEAD_DIM).transpose(2, 0, 3, 1)
    s_dec, d_oa = _mixer_step(cols(d_q), cols(d_k), cols(d_g), d_v, d_gs, state_hgrn[l])
    d_ob = _pool_step(state_pool[l].transpose(1, 0, 2), d_p, wpool_bf, ps)
    d_mix = jnp.concatenate([d_oa, d_ob], axis=1).astype(BF16)

    n_tok = nseq * seqlen + ndec
    x1, xp, idx, gate, cnt = _outproj(p_mix, p_xn, d_mix, d_xn, w_out[l].astype(BF16), row(ln1_g[l]), row(ln1_b[l]),
                                      w_router[l].astype(BF16), row(b_router[l]))
    plan = _route_plan(idx[:n_tok, :TOP_K], cnt[0].astype(I32), n_tok)
    yt = _moe(*plan, xp, w_gate_e[l], w_up_e[l], w_down_e[l], n_tok)
    yt4 = yt.reshape(n_tok + 2, TOP_K, ROW_CHUNKS, 128)
    gate3 = gate.reshape(-1, TOP_K, 128)
    comb = functools.partial(_combine, x1, yt4, gate3, w_gate_s[l].astype(BF16), w_up_s[l].astype(BF16),
                             w_down_s[l].astype(BF16), row(ln2_g[l]), row(ln2_b[l]))
    y_prompt = comb(0, nseq * seqlen).reshape(nseq, seqlen, D_MODEL)
    y_sample = comb(nseq * seqlen, ndec).reshape(ndec, 1, D_MODEL)

    state_pool_prompt = p_p.reshape(nseq, seqlen, POOL_WIDTH)[:, seqlen - POOL_BUF:, :]
    state_pool_sample = jnp.concatenate([state_pool[l][:, 1:, :], d_p[:, None, :]], axis=1)
    return (y_prompt, y_sample, s_prompt[None], state_pool_prompt[None], s_dec[None], state_pool_sample[None])
```

```python
import functools

import jax
import jax.numpy as jnp
from jax import lax
from jax.experimental import pallas as pl
from jax.experimental.pallas import tpu as pltpu

F32 = jnp.float32
BF16 = jnp.bfloat16
I32 = jnp.int32

D_MODEL = 1024
N_META = 16
HGRN_WIDTH = 512
HGRN_HEADS = 4
HEAD_DIM = 128
POOL_WIDTH = 512
POOL_WINDOWS = (2, 4, 8, 16)
POOL_GROUP_DIM = 128
POOL_BUF = 15
IN_WIDTH = 4 * HGRN_WIDTH + POOL_WIDTH
N_EXPERTS = 256
TOP_K = 8
D_EXPERT = 256
ROUTED_SCALE = 2.5
DEPTH = 1
ALPHA = (2 * DEPTH) ** 0.25
LN_EPS = 1e-5
RMS_EPS = 1e-6

SUB = 16
MIX_BLOCK = 128
MOE_ROWS = 256
TILE_STRIDE = MOE_ROWS + 8
ORDER_ALIGN = 128
ROW_CHUNKS = D_MODEL // 128
PACK_ROWS = D_MODEL // 2 // 128
OUT_TILE = 512
VMEM_LIMIT = 48 * 1024 * 1024


def _ln(x, g, b):
    mu = jnp.mean(x, axis=-1, keepdims=True)
    xc = x - mu
    var = jnp.mean(xc * xc, axis=-1, keepdims=True)
    return xc * lax.rsqrt(var + LN_EPS) * g + b


def _sigmoid(z):
    return 1.0 / (1.0 + jnp.exp(-z))


def _bdot(a, b):
    return jnp.dot(a, b, preferred_element_type=F32)


def _ln_proj_kernel(x_ref, g_ref, b_ref, w_ref, lbl_ref, ng_ref,
                    xn_ref, q_ref, k_ref, gl_ref, v_ref, gs_ref, p_ref):
    xn = _ln(x_ref[...], g_ref[...], b_ref[...])
    xn_ref[...] = xn
    proj = _bdot(xn.astype(BF16), w_ref[...])
    lbl = lbl_ref[...]
    e = jnp.exp(lbl - jnp.max(lbl, axis=0, keepdims=True))
    lb = e[0:1] / jnp.sum(e, axis=0, keepdims=True)
    W = HGRN_WIDTH
    q = proj[:, 0:W]
    f = proj[:, W:2 * W]
    q_ref[...] = q * _sigmoid(q)
    k_ref[...] = (1.0 - lb) * _sigmoid(-f)
    gl_ref[...] = jnp.log(lb + (1.0 - lb) * _sigmoid(f))
    v_ref[...] = proj[:, 2 * W:3 * W]
    g = proj[:, 3 * W:4 * W]
    gs_ref[...] = ng_ref[...] * (g * _sigmoid(g))
    p_ref[...] = proj[:, 4 * W:]


def _row_tile(n_rows):
    for tm in (256, 128, 64, 32, 16, 8):
        if n_rows % tm == 0:
            return tm
    raise ValueError(f"row count {n_rows} is not a multiple of 8")


def _ln_proj(x, ln_g, ln_b, w_in_bf, lb_logits, norm_g):
    T = x.shape[0]
    tm = _row_tile(T)
    row = lambda i: (i, 0)
    const = lambda i: (0, 0)
    outs = [jax.ShapeDtypeStruct((T, D_MODEL), F32)] + [jax.ShapeDtypeStruct((T, HGRN_WIDTH), F32)] * 6
    return pl.pallas_call(
        _ln_proj_kernel,
        out_shape=outs,
        grid=(T // tm,),
        in_specs=[
            pl.BlockSpec((tm, D_MODEL), row),
            pl.BlockSpec((1, D_MODEL), const),
            pl.BlockSpec((1, D_MODEL), const),
            pl.BlockSpec((D_MODEL, IN_WIDTH), const),
            pl.BlockSpec((DEPTH + 1, HGRN_WIDTH), const),
            pl.BlockSpec((1, HGRN_WIDTH), const),
        ],
        out_specs=[pl.BlockSpec((tm, D_MODEL), row)] + [pl.BlockSpec((tm, HGRN_WIDTH), row)] * 6,
        compiler_params=pltpu.CompilerParams(dimension_semantics=("arbitrary",), vmem_limit_bytes=VMEM_LIMIT),
        name="ln_proj",
    )(x, ln_g, ln_b, w_in_bf, lb_logits, norm_g)


def _pool_group(pe, p_cur, gi, w):
    sl = slice(gi * POOL_GROUP_DIM, (gi + 1) * POOL_GROUP_DIM)
    s = pe[:, sl]
    sh = 1
    while sh < w:
        s = s + pltpu.roll(s, sh, 0)
        sh *= 2
    return s[SUB:, :] * (1.0 / w) - p_cur[:, sl]


def _mixer_kernel(q_ref, k_ref, g_ref, v_ref, gs_ref, p_ref, s0_ref, pp0_ref, wpool_ref, pscale_ref,
                  mix_ref, sfin_ref, st_scr, pe_scr):
    i = pl.program_id(1)
    nblk = pl.num_programs(1)
    n = MIX_BLOCK

    @pl.when(i == 0)
    def _():
        for h in range(HGRN_HEADS):
            st_scr[h] = s0_ref[h].T
        pe_scr[0:SUB, :] = pp0_ref[...]

    rows = lax.broadcasted_iota(I32, (n, HEAD_DIM), 0)
    r16 = rows & (SUB - 1)
    t8 = lax.broadcasted_iota(I32, (8, HEAD_DIM), 0)
    zero_bf = jnp.zeros((SUB, HEAD_DIM), BF16)

    o_heads = []
    for h in range(HGRN_HEADS):
        hs = slice(h * HEAD_DIM, (h + 1) * HEAD_DIM)
        Q = q_ref[:, hs]
        K = k_ref[:, hs]
        G = g_ref[:, hs]
        V = v_ref[:, hs]
        bf = G
        br = G
        for sh in (1, 2, 4, 8):
            bf = bf + jnp.where(r16 >= sh, pltpu.roll(bf, sh, 0), 0.0)
            br = br + jnp.where(r16 < SUB - sh, pltpu.roll(br, n - sh, 0), 0.0)
        br = br - G
        qt = (Q * jnp.exp(bf)).astype(BF16)
        kt = (K * jnp.exp(br)).astype(BF16)
        vt = V.T.astype(BF16)
        st = st_scr[h]
        o_parts = []
        for c in range(n // SUB):
            r0 = c * SUB
            b_top, b_bot = bf[r0:r0 + 8], bf[r0 + 8:r0 + 16]
            q_top, q_bot = Q[r0:r0 + 8], Q[r0 + 8:r0 + 16]
            acc_top = jnp.zeros((8, HEAD_DIM), F32)
            acc_bot = jnp.zeros((8, HEAD_DIM), F32)
            for s in range(SUB):
                bs = bf[r0 + s:r0 + s + 1]
                ks = K[r0 + s:r0 + s + 1]
                vs = V[r0 + s:r0 + s + 1]
                if s < 8:
                    col = jnp.sum(q_top * jnp.exp(b_top - bs) * ks, axis=-1, keepdims=True)
                    col = jnp.where(t8[:, 0:1] >= s, col, 0.0)
                    acc_top = acc_top + col * vs
                    col = jnp.sum(q_bot * jnp.exp(b_bot - bs) * ks, axis=-1, keepdims=True)
                    acc_bot = acc_bot + col * vs
                else:
                    col = jnp.sum(q_bot * jnp.exp(b_bot - bs) * ks, axis=-1, keepdims=True)
                    col = jnp.where(t8[:, 0:1] + 8 >= s, col, 0.0)
                    acc_bot = acc_bot + col * vs
            o_diag = jnp.concatenate([acc_top, acc_bot], axis=0)
            o_inter = lax.dot_general(qt[r0:r0 + SUB], st.astype(BF16), (((1,), (1,)), ((), ())),
                                      preferred_element_type=F32)
            o_parts.append(o_inter + o_diag)
            kmask = jnp.concatenate([zero_bf] * c + [kt[r0:r0 + SUB]] + [zero_bf] * (n // SUB - 1 - c), axis=0)
            d_st = _bdot(vt, kmask)
            st = st * jnp.exp(bf[r0 + SUB - 1:r0 + SUB]) + d_st
        st_scr[h] = st
        o = jnp.concatenate(o_parts, axis=0)
        o = o * lax.rsqrt(jnp.mean(o * o, axis=-1, keepdims=True) + RMS_EPS)
        o_heads.append(o * gs_ref[:, hs])

    p_cur = p_ref[...]
    pe_scr[SUB:SUB + n, :] = p_cur
    pe = pe_scr[...]
    ob = []
    for gi, w in enumerate(POOL_WINDOWS):
        pooled = _pool_group(pe, p_cur, gi, w)
        sl = slice(gi * POOL_GROUP_DIM, (gi + 1) * POOL_GROUP_DIM)
        ob.append(_bdot(pooled.astype(BF16), wpool_ref[gi]) * pscale_ref[:, sl])
    pe_scr[0:SUB, :] = p_cur[n - SUB:, :]

    mix_ref[...] = jnp.concatenate(o_heads + ob, axis=1).astype(BF16)

    @pl.when(i == nblk - 1)
    def _():
        for h in range(HGRN_HEADS):
            sfin_ref[0, h] = st_scr[h].T


def _mixer(q, k, g, v, gs, p, s0, pp0, wpool_bf, pscale, nseq, seqlen):
    nblk = seqlen // MIX_BLOCK
    tok = lambda b, i: (b * nblk + i, 0)
    tspec = pl.BlockSpec((MIX_BLOCK, HGRN_WIDTH), tok)
    return pl.pallas_call(
        _mixer_kernel,
        out_shape=[jax.ShapeDtypeStruct((nseq * seqlen, D_MODEL), BF16),
                   jax.ShapeDtypeStruct((nseq, HGRN_HEADS, HEAD_DIM, HEAD_DIM), F32)],
        grid=(nseq, nblk),
        in_specs=[tspec] * 6 + [
            pl.BlockSpec((HGRN_HEADS, HEAD_DIM, HEAD_DIM), lambda b, i: (0, 0, 0)),
            pl.BlockSpec((SUB, POOL_WIDTH), lambda b, i: (0, 0)),
            pl.BlockSpec((len(POOL_WINDOWS), POOL_GROUP_DIM, POOL_GROUP_DIM), lambda b, i: (0, 0, 0)),
            pl.BlockSpec((1, POOL_WIDTH), lambda b, i: (0, 0)),
        ],
        out_specs=[pl.BlockSpec((MIX_BLOCK, D_MODEL), tok),
                   pl.BlockSpec((1, HGRN_HEADS, HEAD_DIM, HEAD_DIM), lambda b, i: (b, 0, 0, 0))],
        scratch_shapes=[pltpu.VMEM((HGRN_HEADS, HEAD_DIM, HEAD_DIM), F32),
                        pltpu.VMEM((SUB + MIX_BLOCK, POOL_WIDTH), F32)],
        compiler_params=pltpu.CompilerParams(dimension_semantics=("arbitrary", "arbitrary"),
                                             vmem_limit_bytes=VMEM_LIMIT),
        name="mixer",
    )(q, k, g, v, gs, p, s0, pp0, wpool_bf, pscale)


STEP_SEQS = 32


def _mixer_step_kernel(qt_ref, kt_ref, gt_ref, v_ref, gs_ref, s_ref, snew_ref, oa_ref):
    qt = qt_ref[0, 0]
    kt = kt_ref[0, 0]
    dt = jnp.exp(gt_ref[0, 0])
    rows = []
    for bb in range(STEP_SEQS):
        sn = s_ref[bb, 0] * dt[:, bb:bb + 1] + kt[:, bb:bb + 1] * v_ref[bb:bb + 1, :]
        snew_ref[bb, 0] = sn
        rows.append(jnp.sum(sn * qt[:, bb:bb + 1], axis=0, keepdims=True))
    o = jnp.concatenate(rows, axis=0)
    o = o * lax.rsqrt(jnp.mean(o * o, axis=-1, keepdims=True) + RMS_EPS)
    oa_ref[...] = o * gs_ref[...]


def _mixer_step(qT, kT, gT, v, gs, state):
    nseq = v.shape[0]
    nbc = nseq // STEP_SEQS
    cspec = pl.BlockSpec((1, 1, HEAD_DIM, STEP_SEQS), lambda h, c: (h, c, 0, 0))
    rspec = pl.BlockSpec((STEP_SEQS, HEAD_DIM), lambda h, c: (c, h))
    sspec = pl.BlockSpec((STEP_SEQS, 1, HEAD_DIM, HEAD_DIM), lambda h, c: (c, h, 0, 0))
    return pl.pallas_call(
        _mixer_step_kernel,
        out_shape=[jax.ShapeDtypeStruct(state.shape, F32), jax.ShapeDtypeStruct((nseq, HGRN_WIDTH), F32)],
        grid=(HGRN_HEADS, nbc),
        in_specs=[cspec, cspec, cspec, rspec, rspec, sspec],
        out_specs=[sspec, rspec],
        compiler_params=pltpu.CompilerParams(dimension_semantics=("arbitrary", "arbitrary"),
                                             vmem_limit_bytes=VMEM_LIMIT),
        name="mixer_step",
    )(qT, kT, gT, v, gs, state)


def _pool_step_kernel(sp_ref, p_ref, wpool_ref, pscale_ref, ob_ref):
    p_cur = p_ref[...]
    outs = []
    for gi, w in enumerate(POOL_WINDOWS):
        sl = slice(gi * POOL_GROUP_DIM, (gi + 1) * POOL_GROUP_DIM)
        s = p_cur[:, sl]
        for r in range(POOL_BUF - (w - 1), POOL_BUF):
            s = s + sp_ref[r][:, sl]
        pooled = s * (1.0 / w) - p_cur[:, sl]
        outs.append(_bdot(pooled.astype(BF16), wpool_ref[gi]) * pscale_ref[:, sl])
    ob_ref[...] = jnp.concatenate(outs, axis=1)


def _pool_step(spT, p, wpool_bf, pscale):
    nseq = p.shape[0]
    return pl.pallas_call(
        _pool_step_kernel,
        out_shape=jax.ShapeDtypeStruct((nseq, POOL_WIDTH), F32),
        name="pool_step",
    )(spT, p, wpool_bf, pscale)


def _outproj_kernel(mixp_ref, xnp_ref, mixd_ref, xnd_ref, wout_ref, g1_ref, b1_ref, wr_ref, br_ref,
                    x1_ref, x1p_ref, idx_ref, gate_ref, cnt_ref, *, n_prompt_blocks, n_valid_last):
    tm = mixp_ref.shape[0]
    i = pl.program_id(0)
    is_prompt = i < n_prompt_blocks
    mix_in = jnp.where(is_prompt, mixp_ref[...], mixd_ref[...])
    xn = jnp.where(is_prompt, xnp_ref[...], xnd_ref[...])
    mix = _bdot(mix_in, wout_ref[...])
    x1 = _ln(ALPHA * xn + mix, g1_ref[...], b1_ref[...])
    x1_ref[...] = x1
    xb = x1.astype(BF16)
    xr = xb.astype(F32)
    half = D_MODEL // 2
    lo = lax.shift_right_logical(lax.bitcast_convert_type(xr[:, :half], jnp.uint32), jnp.uint32(16))
    hi = lax.bitcast_convert_type(xr[:, half:], jnp.uint32) & jnp.uint32(0xFFFF0000)
    words = lax.bitcast_convert_type(hi | lo, I32)
    for c in range(PACK_ROWS):
        x1p_ref[pl.ds(c, tm, stride=PACK_ROWS), :] = words[:, c * 128:(c + 1) * 128]

    scores = _sigmoid(_bdot(xb, wr_ref[...]))
    sel = scores + br_ref[...]
    lane = lax.broadcasted_iota(I32, (tm, N_EXPERTS), 1).astype(F32)
    lane_o = lax.broadcasted_iota(I32, (tm, 128), 1)
    idx_o = jnp.zeros((tm, 128), F32)
    ssum = jnp.zeros((tm, 1), F32)
    chosen = jnp.zeros((tm, N_EXPERTS), F32)
    s_sel = []
    for j in range(TOP_K):
        m = jnp.max(sel, axis=-1, keepdims=True)
        am = jnp.min(jnp.where(sel == m, lane, float(N_EXPERTS)), axis=-1, keepdims=True)
        hit = lane == am
        sj = jnp.sum(jnp.where(hit, scores, 0.0), axis=-1, keepdims=True)
        sel = jnp.where(hit, -jnp.inf, sel)
        chosen = jnp.where(hit, 1.0, chosen)
        idx_o = jnp.where(lane_o == j, am, idx_o)
        s_sel.append(sj)
        ssum = ssum + sj
    idx_ref[...] = idx_o.astype(I32)
    gates = jnp.zeros((tm, 128), F32)
    for j in range(TOP_K):
        gates = jnp.where(lane_o == j, s_sel[j] / ssum * ROUTED_SCALE, gates)
    gate_ref[...] = gates

    @pl.when(i == 0)
    def _():
        cnt_ref[...] = jnp.zeros_like(cnt_ref)

    row_id = lax.broadcasted_iota(I32, (tm, N_EXPERTS), 0)
    valid = jnp.logical_or(is_prompt, row_id < n_valid_last)
    cnt_ref[...] += jnp.sum(jnp.where(valid, chosen, 0.0), axis=0, keepdims=True)


def _outproj(mix_p, xn_p, mix_d, xn_d, wout_bf, g1, b1, wr_bf, br):
    tm = OUT_TILE
    n_prompt, n_dec = mix_p.shape[0], mix_d.shape[0]
    assert n_prompt % tm == 0 and n_dec <= tm and n_dec % 8 == 0
    nbp = n_prompt // tm
    T = (nbp + 1) * tm
    padrows = lambda a: jnp.pad(a, ((0, tm - n_dec), (0, 0)))
    row = lambda i: (i, 0)
    prow = lambda i: (jnp.minimum(i, nbp - 1), 0)
    const = lambda i: (0, 0)
    return pl.pallas_call(
        functools.partial(_outproj_kernel, n_prompt_blocks=nbp, n_valid_last=n_dec),
        out_shape=[jax.ShapeDtypeStruct((T, D_MODEL), F32), jax.ShapeDtypeStruct((T * PACK_ROWS, 128), I32),
                   jax.ShapeDtypeStruct((T, 128), I32), jax.ShapeDtypeStruct((T, 128), F32),
                   jax.ShapeDtypeStruct((1, N_EXPERTS), F32)],
        grid=(nbp + 1,),
        in_specs=[pl.BlockSpec((tm, D_MODEL), prow), pl.BlockSpec((tm, D_MODEL), prow),
                  pl.BlockSpec((tm, D_MODEL), const), pl.BlockSpec((tm, D_MODEL), const),
                  pl.BlockSpec((D_MODEL, D_MODEL), const), pl.BlockSpec((1, D_MODEL), const),
                  pl.BlockSpec((1, D_MODEL), const), pl.BlockSpec((D_MODEL, N_EXPERTS), const),
                  pl.BlockSpec((1, N_EXPERTS), const)],
        out_specs=[pl.BlockSpec((tm, D_MODEL), row), pl.BlockSpec((tm * PACK_ROWS, 128), row),
                   pl.BlockSpec((tm, 128), row), pl.BlockSpec((tm, 128), row),
                   pl.BlockSpec((1, N_EXPERTS), const)],
        compiler_params=pltpu.CompilerParams(dimension_semantics=("arbitrary",), vmem_limit_bytes=VMEM_LIMIT),
        name="outproj",
    )(mix_p, xn_p, padrows(mix_d), padrows(xn_d), wout_bf, g1, b1, wr_bf, br)


def _moe_kernel(eb0_ref, enb_ref, wsel_ref, tot_ref, bn_ref, bsrc_ref, bdel_ref,
                order_ref, tokrow_ref, xp_ref, wg_ref, wu_ref, wd_ref, yt_ref,
                tile_scr, ybuf, wgb, wub, wdb, ord_smem, tok_smem, sem, sem_o, sem_t):
    del wsel_ref
    e = pl.program_id(0)
    g0 = eb0_ref[e]
    nblk = enb_ref[e]
    total = tot_ref[0]
    nb_max = bn_ref.shape[0]
    R = MOE_ROWS
    C = ROW_CHUNKS
    L = R + ORDER_ALIGN
    n_assign = yt_ref.shape[0] // C - 2 * TOP_K

    class _Fetch:
        def __init__(self, g):
            gc = jnp.minimum(g, nb_max - 1)
            s = g % 4
            src = pl.multiple_of(bsrc_ref[gc], ORDER_ALIGN)
            dst = pl.multiple_of(s * L, ORDER_ALIGN)
            self.copies = (
                pltpu.make_async_copy(order_ref.at[pl.ds(src, L)], ord_smem.at[pl.ds(dst, L)], sem_o.at[s]),
                pltpu.make_async_copy(tokrow_ref.at[pl.ds(src, L)], tok_smem.at[pl.ds(dst, L)], sem_t.at[s]))

        def start(self):
            for c in self.copies:
                c.start()

        def wait(self):
            for c in self.copies:
                c.wait()

    fetch = _Fetch

    def list_base(g):
        return (g % 4) * L + bdel_ref[jnp.minimum(g, nb_max - 1)]

    def gather(g):
        base = list_base(g)
        trow = (g % 2) * (PACK_ROWS * TILE_STRIDE)
        for r in range(R):
            t4 = pl.multiple_of(tok_smem[base + r], PACK_ROWS)
            tile_scr[pl.ds(trow + r, PACK_ROWS, stride=TILE_STRIDE), :] = xp_ref[pl.ds(t4, PACK_ROWS), :]

    def rows_sent(cnt):
        return pl.multiple_of(lax.shift_left(lax.shift_right_logical(cnt + 7, 3), 3), 8)

    def wait_rows(s, cnt):
        pltpu.make_async_copy(ybuf.at[s, pl.ds(0, cnt * C), :], yt_ref.at[pl.ds(0, cnt * C), :], sem.at[s]).wait()

    @pl.when(e == 0)
    def _():
        n_spare = 2 * TOP_K * C
        ybuf[0, 0:n_spare, :] = jnp.zeros((n_spare, 128), F32)
        init = pltpu.make_async_copy(ybuf.at[0, pl.ds(0, n_spare), :],
                                     yt_ref.at[pl.ds(yt_ref.shape[0] - n_spare, n_spare), :], sem.at[0])
        init.start()
        init.wait()
        fetch(0).start()
        fetch(1).start()
        fetch(0).wait()
        gather(0)

    @pl.when(nblk > 0)
    def _():
        wgb[...] = wg_ref[0].astype(BF16)
        wub[...] = wu_ref[0].astype(BF16)
        wdb[...] = wd_ref[0].astype(BF16)

    def block(k, carry):
        g = g0 + k
        slot = g % 2
        n = bn_ref[g]
        fetch(g + 1).wait()
        fetch(g + 2).start()

        @pl.when(g >= 2)
        def _():
            wait_rows(slot, rows_sent(bn_ref[jnp.maximum(g - 2, 0)]))

        trow = pl.multiple_of(slot * (PACK_ROWS * TILE_STRIDE), 8)

        def compute(send_prev):
            los, his = [], []
            for j in range(PACK_ROWS):
                w = lax.bitcast_convert_type(tile_scr[pl.ds(trow + j * TILE_STRIDE, R), :], jnp.uint32)
                los.append(lax.bitcast_convert_type(lax.shift_left(w, jnp.uint32(16)), F32).astype(BF16))
                his.append(lax.bitcast_convert_type(w & jnp.uint32(0xFFFF0000), F32).astype(BF16))
            xg = jnp.concatenate(los + his, axis=1)
            gather(g + 1)
            if send_prev:
                pbase = list_base(g - 1)
                for r in range(R):
                    send(1 - slot, r, ord_smem[pbase + r], r)
            hg = _bdot(xg, wgb[...])
            hu = _bdot(xg, wub[...])
            hb = (hg * _sigmoid(hg)) * hu
            y = _bdot(hb.astype(BF16), wdb[...])
            yb = ybuf.at[slot]
            for c in range(C):
                yb[pl.ds(c, R, stride=C), :] = y[:, c * 128:(c + 1) * 128]

        prev_full = jnp.logical_and(g >= 1, bn_ref[jnp.maximum(g - 1, 0)] == R)

        @pl.when(prev_full)
        def _():
            compute(True)

        @pl.when(jnp.logical_not(prev_full))
        def _():
            compute(False)

        @pl.when(n < R)
        def _():
            send_rows(g)

        return carry

    def send(s, r, d, u):
        pltpu.make_async_copy(ybuf.at[s, pl.ds(pl.multiple_of(r * C, C), C), :],
                              yt_ref.at[pl.ds(pl.multiple_of(d * C, C), C), :], sem.at[s]).start(priority=u % 2)

    def send_rows(g):
        slot = g % 2
        n = bn_ref[g]
        base = list_base(g)

        def send_group(i, c2):
            for u in range(8):
                send(slot, i * 8 + u, ord_smem[base + i * 8 + u], u)
            return c2

        n_full = lax.shift_right_logical(n, 3)
        lax.fori_loop(0, n_full, send_group, 0)

        @pl.when(n_full * 8 < n)
        def _():
            for u in range(8):
                r = n_full * 8 + u
                send(slot, r, jnp.where(r < n, ord_smem[base + r], n_assign + slot * 8 + u), u)

    lax.fori_loop(0, nblk, block, 0)

    @pl.when(e == pl.num_programs(0) - 1)
    def _():
        @pl.when(bn_ref[jnp.maximum(total - 1, 0)] == R)
        def _():
            send_rows(total - 1)

        fetch(total + 1).wait()
        wait_rows((total - 1) % 2, rows_sent(bn_ref[jnp.maximum(total - 1, 0)]))

        @pl.when(total >= 2)
        def _():
            wait_rows(total % 2, rows_sent(bn_ref[jnp.maximum(total - 2, 0)]))


def _moe(ex_b0, ex_nb, ex_w, total, blk_n, blk_src, blk_delta, order, tokrow, xp, wg, wu, wd, n_tok):
    R = MOE_ROWS
    wmap_in = lambda e, eb0, enb, wsel, tot, bn, bs, bd: (wsel[e], 0, 0)
    grid_spec = pltpu.PrefetchScalarGridSpec(
        num_scalar_prefetch=7,
        grid=(N_EXPERTS,),
        in_specs=[
            pl.BlockSpec(memory_space=pl.ANY),
            pl.BlockSpec(memory_space=pl.ANY),
            pl.BlockSpec(memory_space=pltpu.VMEM),
            pl.BlockSpec((1, D_MODEL, D_EXPERT), wmap_in),
            pl.BlockSpec((1, D_MODEL, D_EXPERT), wmap_in),
            pl.BlockSpec((1, D_EXPERT, D_MODEL), wmap_in),
        ],
        out_specs=pl.BlockSpec(memory_space=pl.ANY),
        scratch_shapes=[pltpu.VMEM((2 * PACK_ROWS * TILE_STRIDE, 128), I32),
                        pltpu.VMEM((2, R * ROW_CHUNKS, 128), F32),
                        pltpu.VMEM((D_MODEL, D_EXPERT), BF16),
                        pltpu.VMEM((D_MODEL, D_EXPERT), BF16),
                        pltpu.VMEM((D_EXPERT, D_MODEL), BF16),
                        pltpu.SMEM((4 * (R + ORDER_ALIGN),), I32),
                        pltpu.SMEM((4 * (R + ORDER_ALIGN),), I32),
                        pltpu.SemaphoreType.DMA((2,)),
                        pltpu.SemaphoreType.DMA((4,)),
                        pltpu.SemaphoreType.DMA((4,))],
    )
    return pl.pallas_call(
        _moe_kernel,
        out_shape=jax.ShapeDtypeStruct(((n_tok + 2) * TOP_K * ROW_CHUNKS, 128), F32),
        grid_spec=grid_spec,
        compiler_params=pltpu.CompilerParams(dimension_semantics=("arbitrary",),
                                             vmem_limit_bytes=58 * 1024 * 1024),
        name="moe",
    )(ex_b0, ex_nb, ex_w, total, blk_n, blk_src, blk_delta, order, tokrow, xp, wg, wu, wd)


def _route_plan(idx, counts, n_tok):
    R = MOE_ROWS
    n_assign = n_tok * TOP_K
    nb = (n_assign + N_EXPERTS * (R - 1)) // R
    id_bits = (n_assign - 1).bit_length()
    assert id_bits + (N_EXPERTS - 1).bit_length() < 32
    key = lax.shift_left(idx.reshape(-1), id_bits) | jnp.arange(n_assign, dtype=I32)
    order = lax.sort(key) & ((1 << id_bits) - 1)
    order = jnp.concatenate([order, jnp.zeros((R + ORDER_ALIGN,), I32)])
    nblk_e = (counts + R - 1) // R
    bend = jnp.cumsum(nblk_e)
    bstart = bend - nblk_e
    cstart = jnp.cumsum(counts) - counts
    blk = jnp.arange(nb, dtype=I32)
    blk_e = jnp.minimum(jnp.sum((bend[None, :] <= blk[:, None]).astype(I32), axis=1), N_EXPERTS - 1)
    k = blk - bstart[blk_e]
    active = blk < bend[-1]
    blk_n = jnp.where(active, jnp.clip(counts[blk_e] - k * R, 0, R), 0).astype(I32)
    src = jnp.where(active, cstart[blk_e] + k * R, 0).astype(I32)
    blk_src = (src // ORDER_ALIGN) * ORDER_ALIGN
    ex = jnp.arange(N_EXPERTS, dtype=I32)
    ex_w = jnp.maximum(lax.cummax(jnp.where(nblk_e > 0, ex, -1)), 0).astype(I32)
    tokrow = (order // TOP_K) * PACK_ROWS
    return (bstart.astype(I32), nblk_e.astype(I32), ex_w, bend[-1:].astype(I32),
            blk_n, blk_src, src - blk_src, order, tokrow)


def _combine_kernel(x1_ref, yt_ref, gate_ref, wgs_ref, wus_ref, wds_ref, g2_ref, b2_ref, out_ref, r_scr, g_scr):
    tm = x1_ref.shape[0]
    C = ROW_CHUNKS
    x1 = x1_ref[...]
    xb = x1.astype(BF16)
    hg = _bdot(xb, wgs_ref[...])
    hs = (hg * _sigmoid(hg)) * _bdot(xb, wus_ref[...])
    moe = _bdot(hs.astype(BF16), wds_ref[...])
    gate = gate_ref[...]
    for j in range(TOP_K):
        g_scr[pl.ds(j, tm, stride=TOP_K), :] = jnp.broadcast_to(gate[:, j:j + 1], (tm, 128))
    g3 = g_scr[...].reshape(tm, TOP_K, 128)
    routed = yt_ref[:, 0] * g3[:, 0:1, :]
    for j in range(1, TOP_K):
        routed = routed + yt_ref[:, j] * g3[:, j:j + 1, :]
    r_scr[...] = routed.reshape(tm * C, 128)
    routed = jnp.concatenate([r_scr[pl.ds(c, tm, stride=C), :] for c in range(C)], axis=1)
    out_ref[...] = _ln(ALPHA * x1 + (moe + routed), g2_ref[...], b2_ref[...])


def _combine(x1, yt4, gate, wgs_bf, wus_bf, wds_bf, g2, b2, row0, nrows):
    tm = 128
    assert nrows % tm == 0 and row0 % tm == 0
    off = row0 // tm
    row = lambda i: (i + off, 0)
    const = lambda i: (0, 0)
    return pl.pallas_call(
        _combine_kernel,
        out_shape=jax.ShapeDtypeStruct((nrows, D_MODEL), F32),
        grid=(nrows // tm,),
        in_specs=[pl.BlockSpec((tm, D_MODEL), row),
                  pl.BlockSpec((tm, TOP_K, ROW_CHUNKS, 128), lambda i: (i + off, 0, 0, 0)),
                  pl.BlockSpec((tm, 128), row),
                  pl.BlockSpec((D_MODEL, D_EXPERT), const), pl.BlockSpec((D_MODEL, D_EXPERT), const),
                  pl.BlockSpec((D_EXPERT, D_MODEL), const),
                  pl.BlockSpec((1, D_MODEL), const), pl.BlockSpec((1, D_MODEL), const)],
        out_specs=pl.BlockSpec((tm, D_MODEL), lambda i: (i, 0)),
        scratch_shapes=[pltpu.VMEM((tm * ROW_CHUNKS, 128), F32), pltpu.VMEM((tm * TOP_K, 128), F32)],
        compiler_params=pltpu.CompilerParams(dimension_semantics=("arbitrary",), vmem_limit_bytes=VMEM_LIMIT),
        name="combine",
    )(x1, yt4, gate, wgs_bf, wus_bf, wds_bf, g2, b2)


def kernel(x_prompt, x_sample, state_hgrn, state_pool, meta_tokens, ln_emb_g, ln_emb_b, w_in, lb_logits, hgrn_norm_g, w_pool, pool_scale, w_out, ln1_g, ln1_b, w_router, b_router, w_gate_e, w_up_e, w_down_e, w_gate_s, w_up_s, w_down_s, ln2_g, ln2_b):
    nseq, seqlen, _ = x_prompt.shape
    ndec = x_sample.shape[0]
    l = 0
    row = lambda a: a.reshape(1, -1)
    w_in_bf = w_in[l].astype(BF16)
    wpool_bf = w_pool[l].astype(BF16)
    lng, lnb = row(ln_emb_g), row(ln_emb_b)
    ng, ps = row(hgrn_norm_g[l]), row(pool_scale[l])
    proj = functools.partial(_ln_proj, ln_g=lng, ln_b=lnb, w_in_bf=w_in_bf, lb_logits=lb_logits, norm_g=ng)

    m_xn, m_q, m_k, m_g, m_v, m_gs, m_p = proj(meta_tokens)
    pad = lambda a: jnp.pad(a, ((0, MIX_BLOCK - N_META), (0, 0)))
    zero_state = jnp.zeros((HGRN_HEADS, HEAD_DIM, HEAD_DIM), F32)
    _, s_meta = _mixer(pad(m_q), pad(m_k), pad(m_g), pad(m_v), pad(m_gs), pad(m_p), zero_state,
                       jnp.zeros((SUB, POOL_WIDTH), F32), wpool_bf, ps, 1, MIX_BLOCK)

    p_xn, p_q, p_k, p_g, p_v, p_gs, p_p = proj(x_prompt.reshape(nseq * seqlen, D_MODEL))
    p_mix, s_prompt = _mixer(p_q, p_k, p_g, p_v, p_gs, p_p, s_meta[0], m_p, wpool_bf, ps, nseq, seqlen)

    d_xn, d_q, d_k, d_g, d_v, d_gs, d_p = proj(x_sample.reshape(ndec, D_MODEL))
    cols = lambda a: a.reshape(ndec // STEP_SEQS, STEP_SEQS, HGRN_HEADS, HEAD_DIM).transpose(2, 0, 3, 1)
    s_dec, d_oa = _mixer_step(cols(d_q), cols(d_k), cols(d_g), d_v, d_gs, state_hgrn[l])
    d_ob = _pool_step(state_pool[l].transpose(1, 0, 2), d_p, wpool_bf, ps)
    d_mix = jnp.concatenate([d_oa, d_ob], axis=1).astype(BF16)

    n_tok = nseq * seqlen + ndec
    x1, xp, idx, gate, cnt = _outproj(p_mix, p_xn, d_mix, d_xn, w_out[l].astype(BF16), row(ln1_g[l]), row(ln1_b[l]),
                                      w_router[l].astype(BF16), row(b_router[l]))
    plan = _route_plan(idx[:n_tok, :TOP_K], cnt[0].astype(I32), n_tok)
    yt = _moe(*plan, xp, w_gate_e[l], w_up_e[l], w_down_e[l], n_tok)
    yt4 = yt.reshape(n_tok + 2, TOP_K, ROW_CHUNKS, 128)
    comb = functools.partial(_combine, x1, yt4, gate, w_gate_s[l].astype(BF16), w_up_s[l].astype(BF16),
                             w_down_s[l].astype(BF16), row(ln2_g[l]), row(ln2_b[l]))
    y_prompt = comb(0, nseq * seqlen).reshape(nseq, seqlen, D_MODEL)
    y_sample = comb(nseq * seqlen, ndec).reshape(ndec, 1, D_MODEL)

    state_pool_prompt = p_p.reshape(nseq, seqlen, POOL_WIDTH)[:, seqlen - POOL_BUF:, :]
    state_pool_sample = jnp.concatenate([state_pool[l][:, 1:, :], d_p[:, None, :]], axis=1)
    return (y_prompt, y_sample, s_prompt[None], state_pool_prompt[None], s_dec[None], state_pool_sample[None])
```

```python
import functools

import jax
import jax.numpy as jnp
from jax import lax
from jax.experimental import pallas as pl
from jax.experimental.pallas import tpu as pltpu

F32 = jnp.float32
BF16 = jnp.bfloat16
I32 = jnp.int32

D_MODEL = 1024
N_META = 16
HGRN_WIDTH = 512
HGRN_HEADS = 4
HEAD_DIM = 128
POOL_WIDTH = 512
POOL_WINDOWS = (2, 4, 8, 16)
POOL_GROUP_DIM = 128
POOL_BUF = 15
IN_WIDTH = 4 * HGRN_WIDTH + POOL_WIDTH
N_EXPERTS = 256
TOP_K = 8
D_EXPERT = 256
ROUTED_SCALE = 2.5
DEPTH = 1
ALPHA = (2 * DEPTH) ** 0.25
LN_EPS = 1e-5
RMS_EPS = 1e-6

SUB = 16
MIX_BLOCK = 128
MOE_ROWS = 256
TILE_STRIDE = MOE_ROWS + 8
ORDER_ALIGN = 128
PACK_ROWS = D_MODEL // 2 // 128
OUT_TILE = 512
VMEM_LIMIT = 48 * 1024 * 1024


def _ln(x, g, b):
    mu = jnp.mean(x, axis=-1, keepdims=True)
    xc = x - mu
    var = jnp.mean(xc * xc, axis=-1, keepdims=True)
    return xc * lax.rsqrt(var + LN_EPS) * g + b


def _sigmoid(z):
    return 1.0 / (1.0 + jnp.exp(-z))


def _bdot(a, b):
    return jnp.dot(a, b, preferred_element_type=F32)


def _pack_bf16_pairs(x):
    half = x.shape[1] // 2
    xr = x.astype(BF16).astype(F32)
    lo = lax.shift_right_logical(lax.bitcast_convert_type(xr[:, :half], jnp.uint32), jnp.uint32(16))
    hi = lax.bitcast_convert_type(xr[:, half:], jnp.uint32) & jnp.uint32(0xFFFF0000)
    return lax.bitcast_convert_type(hi | lo, I32)


def _unpack_bf16_pairs(words):
    w = lax.bitcast_convert_type(words, jnp.uint32)
    lo = lax.bitcast_convert_type(lax.shift_left(w, jnp.uint32(16)), F32)
    hi = lax.bitcast_convert_type(w & jnp.uint32(0xFFFF0000), F32)
    return lo, hi


def _ln_proj_kernel(x_ref, g_ref, b_ref, w_ref, lbl_ref, ng_ref,
                    xn_ref, q_ref, k_ref, gl_ref, v_ref, gs_ref, p_ref):
    xn = _ln(x_ref[...], g_ref[...], b_ref[...])
    xn_ref[...] = xn
    proj = _bdot(xn.astype(BF16), w_ref[...])
    lbl = lbl_ref[...]
    e = jnp.exp(lbl - jnp.max(lbl, axis=0, keepdims=True))
    lb = e[0:1] / jnp.sum(e, axis=0, keepdims=True)
    W = HGRN_WIDTH
    q = proj[:, 0:W]
    f = proj[:, W:2 * W]
    q_ref[...] = q * _sigmoid(q)
    k_ref[...] = (1.0 - lb) * _sigmoid(-f)
    gl_ref[...] = jnp.log(lb + (1.0 - lb) * _sigmoid(f))
    v_ref[...] = proj[:, 2 * W:3 * W]
    g = proj[:, 3 * W:4 * W]
    gs_ref[...] = ng_ref[...] * (g * _sigmoid(g))
    p_ref[...] = proj[:, 4 * W:]


def _row_tile(n_rows):
    for tm in (256, 128, 64, 32, 16, 8):
        if n_rows % tm == 0:
            return tm
    raise ValueError(f"row count {n_rows} is not a multiple of 8")


def _ln_proj(x, ln_g, ln_b, w_in_bf, lb_logits, norm_g):
    T = x.shape[0]
    tm = _row_tile(T)
    row = lambda i: (i, 0)
    const = lambda i: (0, 0)
    outs = [jax.ShapeDtypeStruct((T, D_MODEL), F32)] + [jax.ShapeDtypeStruct((T, HGRN_WIDTH), F32)] * 6
    return pl.pallas_call(
        _ln_proj_kernel,
        out_shape=outs,
        grid=(T // tm,),
        in_specs=[
            pl.BlockSpec((tm, D_MODEL), row),
            pl.BlockSpec((1, D_MODEL), const),
            pl.BlockSpec((1, D_MODEL), const),
            pl.BlockSpec((D_MODEL, IN_WIDTH), const),
            pl.BlockSpec((DEPTH + 1, HGRN_WIDTH), const),
            pl.BlockSpec((1, HGRN_WIDTH), const),
        ],
        out_specs=[pl.BlockSpec((tm, D_MODEL), row)] + [pl.BlockSpec((tm, HGRN_WIDTH), row)] * 6,
        compiler_params=pltpu.CompilerParams(dimension_semantics=("arbitrary",), vmem_limit_bytes=VMEM_LIMIT),
        name="ln_proj",
    )(x, ln_g, ln_b, w_in_bf, lb_logits, norm_g)


def _pool_group(pe, p_cur, gi, w):
    sl = slice(gi * POOL_GROUP_DIM, (gi + 1) * POOL_GROUP_DIM)
    s = pe[:, sl]
    sh = 1
    while sh < w:
        s = s + pltpu.roll(s, sh, 0)
        sh *= 2
    return s[SUB:, :] * (1.0 / w) - p_cur[:, sl]


def _mixer_kernel(q_ref, k_ref, g_ref, v_ref, gs_ref, p_ref, s0_ref, pp0_ref, wpool_ref, pscale_ref,
                  mix_ref, sfin_ref, st_scr, pe_scr):
    i = pl.program_id(1)
    nblk = pl.num_programs(1)
    n = MIX_BLOCK

    @pl.when(i == 0)
    def _():
        for h in range(HGRN_HEADS):
            st_scr[h] = s0_ref[h].T
        pe_scr[0:SUB, :] = pp0_ref[...]

    rows = lax.broadcasted_iota(I32, (n, HEAD_DIM), 0)
    r16 = rows & (SUB - 1)
    t8 = lax.broadcasted_iota(I32, (8, HEAD_DIM), 0)
    zero_bf = jnp.zeros((SUB, HEAD_DIM), BF16)

    o_heads = []
    for h in range(HGRN_HEADS):
        hs = slice(h * HEAD_DIM, (h + 1) * HEAD_DIM)
        Q = q_ref[:, hs]
        K = k_ref[:, hs]
        G = g_ref[:, hs]
        V = v_ref[:, hs]
        bf = G
        br = G
        for sh in (1, 2, 4, 8):
            bf = bf + jnp.where(r16 >= sh, pltpu.roll(bf, sh, 0), 0.0)
            br = br + jnp.where(r16 < SUB - sh, pltpu.roll(br, n - sh, 0), 0.0)
        br = br - G
        qt = (Q * jnp.exp(bf)).astype(BF16)
        kt = (K * jnp.exp(br)).astype(BF16)
        vt = V.T.astype(BF16)
        st = st_scr[h]
        o_parts = []
        for c in range(n // SUB):
            r0 = c * SUB
            b_top, b_bot = bf[r0:r0 + 8], bf[r0 + 8:r0 + 16]
            q_top, q_bot = Q[r0:r0 + 8], Q[r0 + 8:r0 + 16]
            acc_top = jnp.zeros((8, HEAD_DIM), F32)
            acc_bot = jnp.zeros((8, HEAD_DIM), F32)
            for s in range(SUB):
                bs = bf[r0 + s:r0 + s + 1]
                ks = K[r0 + s:r0 + s + 1]
                vs = V[r0 + s:r0 + s + 1]
                if s < 8:
                    col = jnp.sum(q_top * jnp.exp(b_top - bs) * ks, axis=-1, keepdims=True)
                    col = jnp.where(t8[:, 0:1] >= s, col, 0.0)
                    acc_top = acc_top + col * vs
                    col = jnp.sum(q_bot * jnp.exp(b_bot - bs) * ks, axis=-1, keepdims=True)
                    acc_bot = acc_bot + col * vs
                else:
                    col = jnp.sum(q_bot * jnp.exp(b_bot - bs) * ks, axis=-1, keepdims=True)
                    col = jnp.where(t8[:, 0:1] + 8 >= s, col, 0.0)
                    acc_bot = acc_bot + col * vs
            o_diag = jnp.concatenate([acc_top, acc_bot], axis=0)
            o_inter = lax.dot_general(qt[r0:r0 + SUB], st.astype(BF16), (((1,), (1,)), ((), ())),
                                      preferred_element_type=F32)
            o_parts.append(o_inter + o_diag)
            kmask = jnp.concatenate([zero_bf] * c + [kt[r0:r0 + SUB]] + [zero_bf] * (n // SUB - 1 - c), axis=0)
            d_st = _bdot(vt, kmask)
            st = st * jnp.exp(bf[r0 + SUB - 1:r0 + SUB]) + d_st
        st_scr[h] = st
        o = jnp.concatenate(o_parts, axis=0)
        o = o * lax.rsqrt(jnp.mean(o * o, axis=-1, keepdims=True) + RMS_EPS)
        o_heads.append(o * gs_ref[:, hs])

    p_cur = p_ref[...]
    pe_scr[SUB:SUB + n, :] = p_cur
    pe = pe_scr[...]
    ob = []
    for gi, w in enumerate(POOL_WINDOWS):
        pooled = _pool_group(pe, p_cur, gi, w)
        sl = slice(gi * POOL_GROUP_DIM, (gi + 1) * POOL_GROUP_DIM)
        ob.append(_bdot(pooled.astype(BF16), wpool_ref[gi]) * pscale_ref[:, sl])
    pe_scr[0:SUB, :] = p_cur[n - SUB:, :]

    mix_ref[...] = jnp.concatenate(o_heads + ob, axis=1).astype(BF16)

    @pl.when(i == nblk - 1)
    def _():
        for h in range(HGRN_HEADS):
            sfin_ref[0, h] = st_scr[h].T


def _mixer(q, k, g, v, gs, p, s0, pp0, wpool_bf, pscale, nseq, seqlen):
    nblk = seqlen // MIX_BLOCK
    tok = lambda b, i: (b * nblk + i, 0)
    tspec = pl.BlockSpec((MIX_BLOCK, HGRN_WIDTH), tok)
    return pl.pallas_call(
        _mixer_kernel,
        out_shape=[jax.ShapeDtypeStruct((nseq * seqlen, D_MODEL), BF16),
                   jax.ShapeDtypeStruct((nseq, HGRN_HEADS, HEAD_DIM, HEAD_DIM), F32)],
        grid=(nseq, nblk),
        in_specs=[tspec] * 6 + [
            pl.BlockSpec((HGRN_HEADS, HEAD_DIM, HEAD_DIM), lambda b, i: (0, 0, 0)),
            pl.BlockSpec((SUB, POOL_WIDTH), lambda b, i: (0, 0)),
            pl.BlockSpec((len(POOL_WINDOWS), POOL_GROUP_DIM, POOL_GROUP_DIM), lambda b, i: (0, 0, 0)),
            pl.BlockSpec((1, POOL_WIDTH), lambda b, i: (0, 0)),
        ],
        out_specs=[pl.BlockSpec((MIX_BLOCK, D_MODEL), tok),
                   pl.BlockSpec((1, HGRN_HEADS, HEAD_DIM, HEAD_DIM), lambda b, i: (b, 0, 0, 0))],
        scratch_shapes=[pltpu.VMEM((HGRN_HEADS, HEAD_DIM, HEAD_DIM), F32),
                        pltpu.VMEM((SUB + MIX_BLOCK, POOL_WIDTH), F32)],
        compiler_params=pltpu.CompilerParams(dimension_semantics=("arbitrary", "arbitrary"),
                                             vmem_limit_bytes=VMEM_LIMIT),
        name="mixer",
    )(q, k, g, v, gs, p, s0, pp0, wpool_bf, pscale)


STEP_SEQS = 32


def _mixer_step_kernel(qt_ref, kt_ref, gt_ref, v_ref, gs_ref, s_ref, snew_ref, oa_ref):
    qt = qt_ref[0, 0]
    kt = kt_ref[0, 0]
    dt = jnp.exp(gt_ref[0, 0])
    rows = []
    for bb in range(STEP_SEQS):
        sn = s_ref[bb, 0] * dt[:, bb:bb + 1] + kt[:, bb:bb + 1] * v_ref[bb:bb + 1, :]
        snew_ref[bb, 0] = sn
        rows.append(jnp.sum(sn * qt[:, bb:bb + 1], axis=0, keepdims=True))
    o = jnp.concatenate(rows, axis=0)
    o = o * lax.rsqrt(jnp.mean(o * o, axis=-1, keepdims=True) + RMS_EPS)
    oa_ref[...] = o * gs_ref[...]


def _mixer_step(qT, kT, gT, v, gs, state):
    nseq = v.shape[0]
    nbc = nseq // STEP_SEQS
    cspec = pl.BlockSpec((1, 1, HEAD_DIM, STEP_SEQS), lambda h, c: (h, c, 0, 0))
    rspec = pl.BlockSpec((STEP_SEQS, HEAD_DIM), lambda h, c: (c, h))
    sspec = pl.BlockSpec((STEP_SEQS, 1, HEAD_DIM, HEAD_DIM), lambda h, c: (c, h, 0, 0))
    return pl.pallas_call(
        _mixer_step_kernel,
        out_shape=[jax.ShapeDtypeStruct(state.shape, F32), jax.ShapeDtypeStruct((nseq, HGRN_WIDTH), F32)],
        grid=(HGRN_HEADS, nbc),
        in_specs=[cspec, cspec, cspec, rspec, rspec, sspec],
        out_specs=[sspec, rspec],
        compiler_params=pltpu.CompilerParams(dimension_semantics=("arbitrary", "arbitrary"),
                                             vmem_limit_bytes=VMEM_LIMIT),
        name="mixer_step",
    )(qT, kT, gT, v, gs, state)


def _pool_step_kernel(sp_ref, p_ref, wpool_ref, pscale_ref, ob_ref):
    p_cur = p_ref[...]
    outs = []
    for gi, w in enumerate(POOL_WINDOWS):
        sl = slice(gi * POOL_GROUP_DIM, (gi + 1) * POOL_GROUP_DIM)
        s = p_cur[:, sl]
        for r in range(POOL_BUF - (w - 1), POOL_BUF):
            s = s + sp_ref[r][:, sl]
        pooled = s * (1.0 / w) - p_cur[:, sl]
        outs.append(_bdot(pooled.astype(BF16), wpool_ref[gi]) * pscale_ref[:, sl])
    ob_ref[...] = jnp.concatenate(outs, axis=1)


def _pool_step(spT, p, wpool_bf, pscale):
    nseq = p.shape[0]
    return pl.pallas_call(
        _pool_step_kernel,
        out_shape=jax.ShapeDtypeStruct((nseq, POOL_WIDTH), F32),
        name="pool_step",
    )(spT, p, wpool_bf, pscale)


def _outproj_kernel(mixp_ref, xnp_ref, mixd_ref, xnd_ref, wout_ref, g1_ref, b1_ref, wr_ref, br_ref,
                    x1_ref, x1p_ref, idx_ref, gate_ref, cnt_ref, *, n_prompt_blocks, n_valid_last):
    tm = mixp_ref.shape[0]
    i = pl.program_id(0)
    is_prompt = i < n_prompt_blocks
    mix_in = jnp.where(is_prompt, mixp_ref[...], mixd_ref[...])
    xn = jnp.where(is_prompt, xnp_ref[...], xnd_ref[...])
    mix = _bdot(mix_in, wout_ref[...])
    x1 = _ln(ALPHA * xn + mix, g1_ref[...], b1_ref[...])
    x1_ref[...] = x1
    xb = x1.astype(BF16)
    words = _pack_bf16_pairs(x1)
    for c in range(PACK_ROWS):
        x1p_ref[pl.ds(c, tm, stride=PACK_ROWS), :] = words[:, c * 128:(c + 1) * 128]

    scores = _sigmoid(_bdot(xb, wr_ref[...]))
    sel = scores + br_ref[...]
    lane = lax.broadcasted_iota(I32, (tm, N_EXPERTS), 1).astype(F32)
    lane_o = lax.broadcasted_iota(I32, (tm, 128), 1)
    idx_o = jnp.zeros((tm, 128), F32)
    ssum = jnp.zeros((tm, 1), F32)
    chosen = jnp.zeros((tm, N_EXPERTS), F32)
    s_sel = []
    for j in range(TOP_K):
        m = jnp.max(sel, axis=-1, keepdims=True)
        am = jnp.min(jnp.where(sel == m, lane, float(N_EXPERTS)), axis=-1, keepdims=True)
        hit = lane == am
        sj = jnp.sum(jnp.where(hit, scores, 0.0), axis=-1, keepdims=True)
        sel = jnp.where(hit, -jnp.inf, sel)
        chosen = jnp.where(hit, 1.0, chosen)
        idx_o = jnp.where(lane_o == j, am, idx_o)
        s_sel.append(sj)
        ssum = ssum + sj
    idx_ref[...] = idx_o.astype(I32)
    gates = jnp.zeros((tm, 128), F32)
    for j in range(TOP_K):
        gates = jnp.where(lane_o == j, s_sel[j] / ssum * ROUTED_SCALE, gates)
    gate_ref[...] = gates

    @pl.when(i == 0)
    def _():
        cnt_ref[...] = jnp.zeros_like(cnt_ref)

    row_id = lax.broadcasted_iota(I32, (tm, N_EXPERTS), 0)
    valid = jnp.logical_or(is_prompt, row_id < n_valid_last)
    cnt_ref[...] += jnp.sum(jnp.where(valid, chosen, 0.0), axis=0, keepdims=True)


def _outproj(mix_p, xn_p, mix_d, xn_d, wout_bf, g1, b1, wr_bf, br):
    tm = OUT_TILE
    n_prompt, n_dec = mix_p.shape[0], mix_d.shape[0]
    assert n_prompt % tm == 0 and n_dec <= tm and n_dec % 8 == 0
    nbp = n_prompt // tm
    T = (nbp + 1) * tm
    padrows = lambda a: jnp.pad(a, ((0, tm - n_dec), (0, 0)))
    row = lambda i: (i, 0)
    prow = lambda i: (jnp.minimum(i, nbp - 1), 0)
    const = lambda i: (0, 0)
    return pl.pallas_call(
        functools.partial(_outproj_kernel, n_prompt_blocks=nbp, n_valid_last=n_dec),
        out_shape=[jax.ShapeDtypeStruct((T, D_MODEL), F32), jax.ShapeDtypeStruct((T * PACK_ROWS, 128), I32),
                   jax.ShapeDtypeStruct((T, 128), I32), jax.ShapeDtypeStruct((T, 128), F32),
                   jax.ShapeDtypeStruct((1, N_EXPERTS), F32)],
        grid=(nbp + 1,),
        in_specs=[pl.BlockSpec((tm, D_MODEL), prow), pl.BlockSpec((tm, D_MODEL), prow),
                  pl.BlockSpec((tm, D_MODEL), const), pl.BlockSpec((tm, D_MODEL), const),
                  pl.BlockSpec((D_MODEL, D_MODEL), const), pl.BlockSpec((1, D_MODEL), const),
                  pl.BlockSpec((1, D_MODEL), const), pl.BlockSpec((D_MODEL, N_EXPERTS), const),
                  pl.BlockSpec((1, N_EXPERTS), const)],
        out_specs=[pl.BlockSpec((tm, D_MODEL), row), pl.BlockSpec((tm * PACK_ROWS, 128), row),
                   pl.BlockSpec((tm, 128), row), pl.BlockSpec((tm, 128), row),
                   pl.BlockSpec((1, N_EXPERTS), const)],
        compiler_params=pltpu.CompilerParams(dimension_semantics=("arbitrary",), vmem_limit_bytes=VMEM_LIMIT),
        name="outproj",
    )(mix_p, xn_p, padrows(mix_d), padrows(xn_d), wout_bf, g1, b1, wr_bf, br)


def _moe_kernel(eb0_ref, enb_ref, wsel_ref, tot_ref, bn_ref, bsrc_ref, bdel_ref,
                order_ref, tokrow_ref, xp_ref, wg_ref, wu_ref, wd_ref, yt_ref,
                tile_scr, ybuf, wgb, wub, wdb, ord_smem, tok_smem, sem, sem_o, sem_t):
    del wsel_ref
    e = pl.program_id(0)
    g0 = eb0_ref[e]
    nblk = enb_ref[e]
    total = tot_ref[0]
    nb_max = bn_ref.shape[0]
    R = MOE_ROWS
    C = PACK_ROWS
    L = R + ORDER_ALIGN
    n_assign = yt_ref.shape[0] // C - 2 * TOP_K

    class _Fetch:
        def __init__(self, g):
            gc = jnp.minimum(g, nb_max - 1)
            s = g % 4
            src = pl.multiple_of(bsrc_ref[gc], ORDER_ALIGN)
            dst = pl.multiple_of(s * L, ORDER_ALIGN)
            self.copies = (
                pltpu.make_async_copy(order_ref.at[pl.ds(src, L)], ord_smem.at[pl.ds(dst, L)], sem_o.at[s]),
                pltpu.make_async_copy(tokrow_ref.at[pl.ds(src, L)], tok_smem.at[pl.ds(dst, L)], sem_t.at[s]))

        def start(self):
            for c in self.copies:
                c.start()

        def wait(self):
            for c in self.copies:
                c.wait()

    fetch = _Fetch

    def list_base(g):
        return (g % 4) * L + bdel_ref[jnp.minimum(g, nb_max - 1)]

    def gather(g):
        base = list_base(g)
        trow = (g % 2) * (PACK_ROWS * TILE_STRIDE)
        for r in range(R):
            t4 = pl.multiple_of(tok_smem[base + r], PACK_ROWS)
            tile_scr[pl.ds(trow + r, PACK_ROWS, stride=TILE_STRIDE), :] = xp_ref[pl.ds(t4, PACK_ROWS), :]

    def rows_sent(cnt):
        return pl.multiple_of(lax.shift_left(lax.shift_right_logical(cnt + 7, 3), 3), 8)

    def wait_rows(s, cnt):
        pltpu.make_async_copy(ybuf.at[s, pl.ds(0, cnt * C), :], yt_ref.at[pl.ds(0, cnt * C), :], sem.at[s]).wait()

    @pl.when(e == 0)
    def _():
        n_spare = 2 * TOP_K * C
        ybuf[0, 0:n_spare, :] = jnp.zeros((n_spare, 128), I32)
        init = pltpu.make_async_copy(ybuf.at[0, pl.ds(0, n_spare), :],
                                     yt_ref.at[pl.ds(yt_ref.shape[0] - n_spare, n_spare), :], sem.at[0])
        init.start()
        init.wait()
        fetch(0).start()
        fetch(1).start()
        fetch(0).wait()
        gather(0)

    @pl.when(nblk > 0)
    def _():
        wgb[...] = wg_ref[0].astype(BF16)
        wub[...] = wu_ref[0].astype(BF16)
        wdb[...] = wd_ref[0].astype(BF16)

    def block(k, carry):
        g = g0 + k
        slot = g % 2
        n = bn_ref[g]
        fetch(g + 1).wait()
        fetch(g + 2).start()

        @pl.when(g >= 2)
        def _():
            wait_rows(slot, rows_sent(bn_ref[jnp.maximum(g - 2, 0)]))

        trow = pl.multiple_of(slot * (PACK_ROWS * TILE_STRIDE), 8)

        def compute(send_prev):
            los, his = [], []
            for j in range(PACK_ROWS):
                lo, hi = _unpack_bf16_pairs(tile_scr[pl.ds(trow + j * TILE_STRIDE, R), :])
                los.append(lo.astype(BF16))
                his.append(hi.astype(BF16))
            xg = jnp.concatenate(los + his, axis=1)
            gather(g + 1)
            if send_prev:
                pbase = list_base(g - 1)
                for r in range(R):
                    send(1 - slot, r, ord_smem[pbase + r], r)
            hg = _bdot(xg, wgb[...])
            hu = _bdot(xg, wub[...])
            hb = (hg * _sigmoid(hg)) * hu
            y = _bdot(hb.astype(BF16), wdb[...])
            words = _pack_bf16_pairs(y)
            yb = ybuf.at[slot]
            for c in range(C):
                yb[pl.ds(c, R, stride=C), :] = words[:, c * 128:(c + 1) * 128]

        prev_full = jnp.logical_and(g >= 1, bn_ref[jnp.maximum(g - 1, 0)] == R)

        @pl.when(prev_full)
        def _():
            compute(True)

        @pl.when(jnp.logical_not(prev_full))
        def _():
            compute(False)

        @pl.when(n < R)
        def _():
            send_rows(g)

        return carry

    def send(s, r, d, u):
        pltpu.make_async_copy(ybuf.at[s, pl.ds(pl.multiple_of(r * C, C), C), :],
                              yt_ref.at[pl.ds(pl.multiple_of(d * C, C), C), :], sem.at[s]).start(priority=u % 2)

    def send_rows(g):
        slot = g % 2
        n = bn_ref[g]
        base = list_base(g)

        def send_group(i, c2):
            for u in range(8):
                send(slot, i * 8 + u, ord_smem[base + i * 8 + u], u)
            return c2

        n_full = lax.shift_right_logical(n, 3)
        lax.fori_loop(0, n_full, send_group, 0)

        @pl.when(n_full * 8 < n)
        def _():
            for u in range(8):
                r = n_full * 8 + u
                send(slot, r, jnp.where(r < n, ord_smem[base + r], n_assign + slot * 8 + u), u)

    lax.fori_loop(0, nblk, block, 0)

    @pl.when(e == pl.num_programs(0) - 1)
    def _():
        @pl.when(bn_ref[jnp.maximum(total - 1, 0)] == R)
        def _():
            send_rows(total - 1)

        fetch(total + 1).wait()
        wait_rows((total - 1) % 2, rows_sent(bn_ref[jnp.maximum(total - 1, 0)]))

        @pl.when(total >= 2)
        def _():
            wait_rows(total % 2, rows_sent(bn_ref[jnp.maximum(total - 2, 0)]))


def _moe(ex_b0, ex_nb, ex_w, total, blk_n, blk_src, blk_delta, order, tokrow, xp, wg, wu, wd, n_tok):
    R = MOE_ROWS
    wmap_in = lambda e, eb0, enb, wsel, tot, bn, bs, bd: (wsel[e], 0, 0)
    grid_spec = pltpu.PrefetchScalarGridSpec(
        num_scalar_prefetch=7,
        grid=(N_EXPERTS,),
        in_specs=[
            pl.BlockSpec(memory_space=pl.ANY),
            pl.BlockSpec(memory_space=pl.ANY),
            pl.BlockSpec(memory_space=pltpu.VMEM),
            pl.BlockSpec((1, D_MODEL, D_EXPERT), wmap_in),
            pl.BlockSpec((1, D_MODEL, D_EXPERT), wmap_in),
            pl.BlockSpec((1, D_EXPERT, D_MODEL), wmap_in),
        ],
        out_specs=pl.BlockSpec(memory_space=pl.ANY),
        scratch_shapes=[pltpu.VMEM((2 * PACK_ROWS * TILE_STRIDE, 128), I32),
                        pltpu.VMEM((2, R * PACK_ROWS, 128), I32),
                        pltpu.VMEM((D_MODEL, D_EXPERT), BF16),
                        pltpu.VMEM((D_MODEL, D_EXPERT), BF16),
                        pltpu.VMEM((D_EXPERT, D_MODEL), BF16),
                        pltpu.SMEM((4 * (R + ORDER_ALIGN),), I32),
                        pltpu.SMEM((4 * (R + ORDER_ALIGN),), I32),
                        pltpu.SemaphoreType.DMA((2,)),
                        pltpu.SemaphoreType.DMA((4,)),
                        pltpu.SemaphoreType.DMA((4,))],
    )
    return pl.pallas_call(
        _moe_kernel,
        out_shape=jax.ShapeDtypeStruct(((n_tok + 2) * TOP_K * PACK_ROWS, 128), I32),
        grid_spec=grid_spec,
        compiler_params=pltpu.CompilerParams(dimension_semantics=("arbitrary",),
                                             vmem_limit_bytes=58 * 1024 * 1024),
        name="moe",
    )(ex_b0, ex_nb, ex_w, total, blk_n, blk_src, blk_delta, order, tokrow, xp, wg, wu, wd)


def _route_plan(idx, counts, n_tok):
    R = MOE_ROWS
    n_assign = n_tok * TOP_K
    nb = (n_assign + N_EXPERTS * (R - 1)) // R
    id_bits = (n_assign - 1).bit_length()
    assert id_bits + (N_EXPERTS - 1).bit_length() < 32
    key = lax.shift_left(idx.reshape(-1), id_bits) | jnp.arange(n_assign, dtype=I32)
    order = lax.sort(key) & ((1 << id_bits) - 1)
    order = jnp.concatenate([order, jnp.zeros((R + ORDER_ALIGN,), I32)])
    nblk_e = (counts + R - 1) // R
    bend = jnp.cumsum(nblk_e)
    bstart = bend - nblk_e
    cstart = jnp.cumsum(counts) - counts
    blk = jnp.arange(nb, dtype=I32)
    blk_e = jnp.minimum(jnp.sum((bend[None, :] <= blk[:, None]).astype(I32), axis=1), N_EXPERTS - 1)
    k = blk - bstart[blk_e]
    active = blk < bend[-1]
    blk_n = jnp.where(active, jnp.clip(counts[blk_e] - k * R, 0, R), 0).astype(I32)
    src = jnp.where(active, cstart[blk_e] + k * R, 0).astype(I32)
    blk_src = (src // ORDER_ALIGN) * ORDER_ALIGN
    ex = jnp.arange(N_EXPERTS, dtype=I32)
    ex_w = jnp.maximum(lax.cummax(jnp.where(nblk_e > 0, ex, -1)), 0).astype(I32)
    tokrow = (order // TOP_K) * PACK_ROWS
    return (bstart.astype(I32), nblk_e.astype(I32), ex_w, bend[-1:].astype(I32),
            blk_n, blk_src, src - blk_src, order, tokrow)


def _combine_kernel(x1_ref, yt_ref, gate_ref, wgs_ref, wus_ref, wds_ref, g2_ref, b2_ref, out_ref):
    tm = x1_ref.shape[0]
    x1 = x1_ref[...]
    xb = x1.astype(BF16)
    hg = _bdot(xb, wgs_ref[...])
    hs = (hg * _sigmoid(hg)) * _bdot(xb, wus_ref[...])
    moe = _bdot(hs.astype(BF16), wds_ref[...])
    gate = gate_ref[...]
    per_tok = TOP_K * PACK_ROWS
    lo_acc = [None] * PACK_ROWS
    hi_acc = [None] * PACK_ROWS
    for j in range(TOP_K):
        gj = gate[:, j:j + 1]
        for c in range(PACK_ROWS):
            lo, hi = _unpack_bf16_pairs(yt_ref[pl.ds(j * PACK_ROWS + c, tm, stride=per_tok), :])
            lo_acc[c] = lo * gj if j == 0 else lo_acc[c] + lo * gj
            hi_acc[c] = hi * gj if j == 0 else hi_acc[c] + hi * gj
    routed = jnp.concatenate(lo_acc + hi_acc, axis=1)
    out_ref[...] = _ln(ALPHA * x1 + (moe + routed), g2_ref[...], b2_ref[...])


def _combine(x1, yt, gate, wgs_bf, wus_bf, wds_bf, g2, b2, row0, nrows):
    tm = 128
    assert nrows % tm == 0 and row0 % tm == 0
    off = row0 // tm
    row = lambda i: (i + off, 0)
    const = lambda i: (0, 0)
    return pl.pallas_call(
        _combine_kernel,
        out_shape=jax.ShapeDtypeStruct((nrows, D_MODEL), F32),
        grid=(nrows // tm,),
        in_specs=[pl.BlockSpec((tm, D_MODEL), row),
                  pl.BlockSpec((tm * TOP_K * PACK_ROWS, 128), row),
                  pl.BlockSpec((tm, 128), row),
                  pl.BlockSpec((D_MODEL, D_EXPERT), const), pl.BlockSpec((D_MODEL, D_EXPERT), const),
                  pl.BlockSpec((D_EXPERT, D_MODEL), const),
                  pl.BlockSpec((1, D_MODEL), const), pl.BlockSpec((1, D_MODEL), const)],
        out_specs=pl.BlockSpec((tm, D_MODEL), lambda i: (i, 0)),
        compiler_params=pltpu.CompilerParams(dimension_semantics=("arbitrary",), vmem_limit_bytes=VMEM_LIMIT),
        name="combine",
    )(x1, yt, gate, wgs_bf, wus_bf, wds_bf, g2, b2)


def kernel(x_prompt, x_sample, state_hgrn, state_pool, meta_tokens, ln_emb_g, ln_emb_b, w_in, lb_logits, hgrn_norm_g, w_pool, pool_scale, w_out, ln1_g, ln1_b, w_router, b_router, w_gate_e, w_up_e, w_down_e, w_gate_s, w_up_s, w_down_s, ln2_g, ln2_b):
    nseq, seqlen, _ = x_prompt.shape
    ndec = x_sample.shape[0]
    l = 0
    row = lambda a: a.reshape(1, -1)
    w_in_bf = w_in[l].astype(BF16)
    wpool_bf = w_pool[l].astype(BF16)
    lng, lnb = row(ln_emb_g), row(ln_emb_b)
    ng, ps = row(hgrn_norm_g[l]), row(pool_scale[l])
    proj = functools.partial(_ln_proj, ln_g=lng, ln_b=lnb, w_in_bf=w_in_bf, lb_logits=lb_logits, norm_g=ng)

    m_xn, m_q, m_k, m_g, m_v, m_gs, m_p = proj(meta_tokens)
    pad = lambda a: jnp.pad(a, ((0, MIX_BLOCK - N_META), (0, 0)))
    zero_state = jnp.zeros((HGRN_HEADS, HEAD_DIM, HEAD_DIM), F32)
    _, s_meta = _mixer(pad(m_q), pad(m_k), pad(m_g), pad(m_v), pad(m_gs), pad(m_p), zero_state,
                       jnp.zeros((SUB, POOL_WIDTH), F32), wpool_bf, ps, 1, MIX_BLOCK)

    p_xn, p_q, p_k, p_g, p_v, p_gs, p_p = proj(x_prompt.reshape(nseq * seqlen, D_MODEL))
    p_mix, s_prompt = _mixer(p_q, p_k, p_g, p_v, p_gs, p_p, s_meta[0], m_p, wpool_bf, ps, nseq, seqlen)

    d_xn, d_q, d_k, d_g, d_v, d_gs, d_p = proj(x_sample.reshape(ndec, D_MODEL))
    cols = lambda a: a.reshape(ndec // STEP_SEQS, STEP_SEQS, HGRN_HEADS, HEAD_DIM).transpose(2, 0, 3, 1)
    s_dec, d_oa = _mixer_step(cols(d_q), cols(d_k), cols(d_g), d_v, d_gs, state_hgrn[l])
    d_ob = _pool_step(state_pool[l].transpose(1, 0, 2), d_p, wpool_bf, ps)
    d_mix = jnp.concatenate([d_oa, d_ob], axis=1).astype(BF16)

    n_tok = nseq * seqlen + ndec
    x1, xp, idx, gate, cnt = _outproj(p_mix, p_xn, d_mix, d_xn, w_out[l].astype(BF16), row(ln1_g[l]), row(ln1_b[l]),
                                      w_router[l].astype(BF16), row(b_router[l]))
    plan = _route_plan(idx[:n_tok, :TOP_K], cnt[0].astype(I32), n_tok)
    yt = _moe(*plan, xp, w_gate_e[l], w_up_e[l], w_down_e[l], n_tok)
    comb = functools.partial(_combine, x1, yt, gate, w_gate_s[l].astype(BF16), w_up_s[l].astype(BF16),
                             w_down_s[l].astype(BF16), row(ln2_g[l]), row(ln2_b[l]))
    y_prompt = comb(0, nseq * seqlen).reshape(nseq, seqlen, D_MODEL)
    y_sample = comb(nseq * seqlen, ndec).reshape(ndec, 1, D_MODEL)

    state_pool_prompt = p_p.reshape(nseq, seqlen, POOL_WIDTH)[:, seqlen - POOL_BUF:, :]
    state_pool_sample = jnp.concatenate([state_pool[l][:, 1:, :], d_p[:, None, :]], axis=1)
    return (y_prompt, y_sample, s_prompt[None], state_pool_prompt[None], s_dec[None], state_pool_sample[None])
```

```python
import functools

import jax
import jax.numpy as jnp
from jax import lax
from jax.experimental import pallas as pl
from jax.experimental.pallas import tpu as pltpu

F32 = jnp.float32
BF16 = jnp.bfloat16
I32 = jnp.int32

D_MODEL = 1024
N_META = 16
HGRN_WIDTH = 512
HGRN_HEADS = 4
HEAD_DIM = 128
POOL_WIDTH = 512
POOL_WINDOWS = (2, 4, 8, 16)
POOL_GROUP_DIM = 128
POOL_BUF = 15
IN_WIDTH = 4 * HGRN_WIDTH + POOL_WIDTH
N_EXPERTS = 256
TOP_K = 8
D_EXPERT = 256
ROUTED_SCALE = 2.5
DEPTH = 1
ALPHA = (2 * DEPTH) ** 0.25
LN_EPS = 1e-5
RMS_EPS = 1e-6

SUB = 16
MIX_BLOCK = 128
MOE_ROWS = 256
TILE_STRIDE = MOE_ROWS + 8
ORDER_ALIGN = 128
PACK_ROWS = D_MODEL // 2 // 128
OUT_TILE = 512
VMEM_LIMIT = 48 * 1024 * 1024


def _ln(x, g, b):
    mu = jnp.mean(x, axis=-1, keepdims=True)
    xc = x - mu
    var = jnp.mean(xc * xc, axis=-1, keepdims=True)
    return xc * lax.rsqrt(var + LN_EPS) * g + b


def _sigmoid(z):
    return 1.0 / (1.0 + jnp.exp(-z))


def _bdot(a, b):
    return jnp.dot(a, b, preferred_element_type=F32)


def _pack_bf16_pairs(x):
    half = x.shape[1] // 2
    xr = x.astype(BF16).astype(F32)
    lo = lax.shift_right_logical(lax.bitcast_convert_type(xr[:, :half], jnp.uint32), jnp.uint32(16))
    hi = lax.bitcast_convert_type(xr[:, half:], jnp.uint32) & jnp.uint32(0xFFFF0000)
    return lax.bitcast_convert_type(hi | lo, I32)


def _unpack_bf16_pairs(words):
    w = lax.bitcast_convert_type(words, jnp.uint32)
    lo = lax.bitcast_convert_type(lax.shift_left(w, jnp.uint32(16)), F32)
    hi = lax.bitcast_convert_type(w & jnp.uint32(0xFFFF0000), F32)
    return lo, hi


def _ln_proj_kernel(x_ref, g_ref, b_ref, w_ref, lbl_ref, ng_ref,
                    xn_ref, q_ref, k_ref, gl_ref, v_ref, gs_ref, p_ref):
    xn = _ln(x_ref[...], g_ref[...], b_ref[...])
    xn_ref[...] = xn
    proj = _bdot(xn.astype(BF16), w_ref[...])
    lbl = lbl_ref[...]
    e = jnp.exp(lbl - jnp.max(lbl, axis=0, keepdims=True))
    lb = e[0:1] / jnp.sum(e, axis=0, keepdims=True)
    W = HGRN_WIDTH
    q = proj[:, 0:W]
    f = proj[:, W:2 * W]
    q_ref[...] = q * _sigmoid(q)
    k_ref[...] = (1.0 - lb) * _sigmoid(-f)
    gl_ref[...] = jnp.log(lb + (1.0 - lb) * _sigmoid(f))
    v_ref[...] = proj[:, 2 * W:3 * W]
    g = proj[:, 3 * W:4 * W]
    gs_ref[...] = ng_ref[...] * (g * _sigmoid(g))
    p_ref[...] = proj[:, 4 * W:]


def _row_tile(n_rows):
    for tm in (256, 128, 64, 32, 16, 8):
        if n_rows % tm == 0:
            return tm
    raise ValueError(f"row count {n_rows} is not a multiple of 8")


def _ln_proj(x, ln_g, ln_b, w_in_bf, lb_logits, norm_g):
    T = x.shape[0]
    tm = _row_tile(T)
    row = lambda i: (i, 0)
    const = lambda i: (0, 0)
    outs = [jax.ShapeDtypeStruct((T, D_MODEL), F32)] + [jax.ShapeDtypeStruct((T, HGRN_WIDTH), F32)] * 6
    return pl.pallas_call(
        _ln_proj_kernel,
        out_shape=outs,
        grid=(T // tm,),
        in_specs=[
            pl.BlockSpec((tm, D_MODEL), row),
            pl.BlockSpec((1, D_MODEL), const),
            pl.BlockSpec((1, D_MODEL), const),
            pl.BlockSpec((D_MODEL, IN_WIDTH), const),
            pl.BlockSpec((DEPTH + 1, HGRN_WIDTH), const),
            pl.BlockSpec((1, HGRN_WIDTH), const),
        ],
        out_specs=[pl.BlockSpec((tm, D_MODEL), row)] + [pl.BlockSpec((tm, HGRN_WIDTH), row)] * 6,
        compiler_params=pltpu.CompilerParams(dimension_semantics=("arbitrary",), vmem_limit_bytes=VMEM_LIMIT),
        name="ln_proj",
    )(x, ln_g, ln_b, w_in_bf, lb_logits, norm_g)


def _pool_group(pe, p_cur, gi, w):
    sl = slice(gi * POOL_GROUP_DIM, (gi + 1) * POOL_GROUP_DIM)
    s = pe[:, sl]
    sh = 1
    while sh < w:
        s = s + pltpu.roll(s, sh, 0)
        sh *= 2
    return s[SUB:, :] * (1.0 / w) - p_cur[:, sl]


def _mixer_kernel(q_ref, k_ref, g_ref, v_ref, gs_ref, p_ref, s0_ref, pp0_ref, wpool_ref, pscale_ref,
                  mix_ref, sfin_ref, st_scr, pe_scr):
    i = pl.program_id(1)
    nblk = pl.num_programs(1)
    n = MIX_BLOCK

    @pl.when(i == 0)
    def _():
        for h in range(HGRN_HEADS):
            st_scr[h] = s0_ref[h].T
        pe_scr[0:SUB, :] = pp0_ref[...]

    rows = lax.broadcasted_iota(I32, (n, HEAD_DIM), 0)
    r16 = rows & (SUB - 1)
    t8 = lax.broadcasted_iota(I32, (8, HEAD_DIM), 0)
    zero_bf = jnp.zeros((SUB, HEAD_DIM), BF16)

    o_heads = []
    for h in range(HGRN_HEADS):
        hs = slice(h * HEAD_DIM, (h + 1) * HEAD_DIM)
        Q = q_ref[:, hs]
        K = k_ref[:, hs]
        G = g_ref[:, hs]
        V = v_ref[:, hs]
        bf = G
        br = G
        for sh in (1, 2, 4, 8):
            bf = bf + jnp.where(r16 >= sh, pltpu.roll(bf, sh, 0), 0.0)
            br = br + jnp.where(r16 < SUB - sh, pltpu.roll(br, n - sh, 0), 0.0)
        br = br - G
        qt = (Q * jnp.exp(bf)).astype(BF16)
        kt = (K * jnp.exp(br)).astype(BF16)
        vt = V.T.astype(BF16)
        st = st_scr[h]
        o_parts = []
        for c in range(n // SUB):
            r0 = c * SUB
            b_top, b_bot = bf[r0:r0 + 8], bf[r0 + 8:r0 + 16]
            q_top, q_bot = Q[r0:r0 + 8], Q[r0 + 8:r0 + 16]
            acc_top = jnp.zeros((8, HEAD_DIM), F32)
            acc_bot = jnp.zeros((8, HEAD_DIM), F32)
            for s in range(SUB):
                bs = bf[r0 + s:r0 + s + 1]
                ks = K[r0 + s:r0 + s + 1]
                vs = V[r0 + s:r0 + s + 1]
                if s < 8:
                    col = jnp.sum(q_top * jnp.exp(b_top - bs) * ks, axis=-1, keepdims=True)
                    col = jnp.where(t8[:, 0:1] >= s, col, 0.0)
                    acc_top = acc_top + col * vs
                    col = jnp.sum(q_bot * jnp.exp(b_bot - bs) * ks, axis=-1, keepdims=True)
                    acc_bot = acc_bot + col * vs
                else:
                    col = jnp.sum(q_bot * jnp.exp(b_bot - bs) * ks, axis=-1, keepdims=True)
                    col = jnp.where(t8[:, 0:1] + 8 >= s, col, 0.0)
                    acc_bot = acc_bot + col * vs
            o_diag = jnp.concatenate([acc_top, acc_bot], axis=0)
            o_inter = lax.dot_general(qt[r0:r0 + SUB], st.astype(BF16), (((1,), (1,)), ((), ())),
                                      preferred_element_type=F32)
            o_parts.append(o_inter + o_diag)
            kmask = jnp.concatenate([zero_bf] * c + [kt[r0:r0 + SUB]] + [zero_bf] * (n // SUB - 1 - c), axis=0)
            d_st = _bdot(vt, kmask)
            st = st * jnp.exp(bf[r0 + SUB - 1:r0 + SUB]) + d_st
        st_scr[h] = st
        o = jnp.concatenate(o_parts, axis=0)
        o = o * lax.rsqrt(jnp.mean(o * o, axis=-1, keepdims=True) + RMS_EPS)
        o_heads.append(o * gs_ref[:, hs])

    p_cur = p_ref[...]
    pe_scr[SUB:SUB + n, :] = p_cur
    pe = pe_scr[...]
    ob = []
    for gi, w in enumerate(POOL_WINDOWS):
        pooled = _pool_group(pe, p_cur, gi, w)
        sl = slice(gi * POOL_GROUP_DIM, (gi + 1) * POOL_GROUP_DIM)
        ob.append(_bdot(pooled.astype(BF16), wpool_ref[gi]) * pscale_ref[:, sl])
    pe_scr[0:SUB, :] = p_cur[n - SUB:, :]

    mix_ref[...] = jnp.concatenate(o_heads + ob, axis=1).astype(BF16)

    @pl.when(i == nblk - 1)
    def _():
        for h in range(HGRN_HEADS):
            sfin_ref[0, h] = st_scr[h].T


def _mixer(q, k, g, v, gs, p, s0, pp0, wpool_bf, pscale, nseq, seqlen):
    nblk = seqlen // MIX_BLOCK
    tok = lambda b, i: (b * nblk + i, 0)
    tspec = pl.BlockSpec((MIX_BLOCK, HGRN_WIDTH), tok)
    return pl.pallas_call(
        _mixer_kernel,
        out_shape=[jax.ShapeDtypeStruct((nseq * seqlen, D_MODEL), BF16),
                   jax.ShapeDtypeStruct((nseq, HGRN_HEADS, HEAD_DIM, HEAD_DIM), F32)],
        grid=(nseq, nblk),
        in_specs=[tspec] * 6 + [
            pl.BlockSpec((HGRN_HEADS, HEAD_DIM, HEAD_DIM), lambda b, i: (0, 0, 0)),
            pl.BlockSpec((SUB, POOL_WIDTH), lambda b, i: (0, 0)),
            pl.BlockSpec((len(POOL_WINDOWS), POOL_GROUP_DIM, POOL_GROUP_DIM), lambda b, i: (0, 0, 0)),
            pl.BlockSpec((1, POOL_WIDTH), lambda b, i: (0, 0)),
        ],
        out_specs=[pl.BlockSpec((MIX_BLOCK, D_MODEL), tok),
                   pl.BlockSpec((1, HGRN_HEADS, HEAD_DIM, HEAD_DIM), lambda b, i: (b, 0, 0, 0))],
        scratch_shapes=[pltpu.VMEM((HGRN_HEADS, HEAD_DIM, HEAD_DIM), F32),
                        pltpu.VMEM((SUB + MIX_BLOCK, POOL_WIDTH), F32)],
        compiler_params=pltpu.CompilerParams(dimension_semantics=("arbitrary", "arbitrary"),
                                             vmem_limit_bytes=VMEM_LIMIT),
        name="mixer",
    )(q, k, g, v, gs, p, s0, pp0, wpool_bf, pscale)


STEP_SEQS = 32


def _mixer_step_kernel(qt_ref, kt_ref, gt_ref, v_ref, gs_ref, s_ref, snew_ref, oa_ref):
    qt = qt_ref[0, 0]
    kt = kt_ref[0, 0]
    dt = jnp.exp(gt_ref[0, 0])
    rows = []
    for bb in range(STEP_SEQS):
        sn = s_ref[bb, 0] * dt[:, bb:bb + 1] + kt[:, bb:bb + 1] * v_ref[bb:bb + 1, :]
        snew_ref[bb, 0] = sn
        rows.append(jnp.sum(sn * qt[:, bb:bb + 1], axis=0, keepdims=True))
    o = jnp.concatenate(rows, axis=0)
    o = o * lax.rsqrt(jnp.mean(o * o, axis=-1, keepdims=True) + RMS_EPS)
    oa_ref[...] = o * gs_ref[...]


def _mixer_step(qT, kT, gT, v, gs, state):
    nseq = v.shape[0]
    nbc = nseq // STEP_SEQS
    cspec = pl.BlockSpec((1, 1, HEAD_DIM, STEP_SEQS), lambda h, c: (h, c, 0, 0))
    rspec = pl.BlockSpec((STEP_SEQS, HEAD_DIM), lambda h, c: (c, h))
    sspec = pl.BlockSpec((STEP_SEQS, 1, HEAD_DIM, HEAD_DIM), lambda h, c: (c, h, 0, 0))
    return pl.pallas_call(
        _mixer_step_kernel,
        out_shape=[jax.ShapeDtypeStruct(state.shape, F32), jax.ShapeDtypeStruct((nseq, HGRN_WIDTH), F32)],
        grid=(HGRN_HEADS, nbc),
        in_specs=[cspec, cspec, cspec, rspec, rspec, sspec],
        out_specs=[sspec, rspec],
        compiler_params=pltpu.CompilerParams(dimension_semantics=("arbitrary", "arbitrary"),
                                             vmem_limit_bytes=VMEM_LIMIT),
        name="mixer_step",
    )(qT, kT, gT, v, gs, state)


def _pool_step_kernel(sp_ref, p_ref, wpool_ref, pscale_ref, ob_ref):
    p_cur = p_ref[...]
    outs = []
    for gi, w in enumerate(POOL_WINDOWS):
        sl = slice(gi * POOL_GROUP_DIM, (gi + 1) * POOL_GROUP_DIM)
        s = p_cur[:, sl]
        for r in range(POOL_BUF - (w - 1), POOL_BUF):
            s = s + sp_ref[r][:, sl]
        pooled = s * (1.0 / w) - p_cur[:, sl]
        outs.append(_bdot(pooled.astype(BF16), wpool_ref[gi]) * pscale_ref[:, sl])
    ob_ref[...] = jnp.concatenate(outs, axis=1)


def _pool_step(spT, p, wpool_bf, pscale):
    nseq = p.shape[0]
    return pl.pallas_call(
        _pool_step_kernel,
        out_shape=jax.ShapeDtypeStruct((nseq, POOL_WIDTH), F32),
        name="pool_step",
    )(spT, p, wpool_bf, pscale)


def _outproj_kernel(mixp_ref, xnp_ref, mixd_ref, xnd_ref, wout_ref, g1_ref, b1_ref, wr_ref, br_ref,
                    x1_ref, x1p_ref, idx_ref, gate_ref, cnt_ref, *, n_prompt_blocks, n_valid_last):
    tm = mixp_ref.shape[0]
    i = pl.program_id(0)
    is_prompt = i < n_prompt_blocks
    mix_in = jnp.where(is_prompt, mixp_ref[...], mixd_ref[...])
    xn = jnp.where(is_prompt, xnp_ref[...], xnd_ref[...])
    mix = _bdot(mix_in, wout_ref[...])
    x1 = _ln(ALPHA * xn + mix, g1_ref[...], b1_ref[...])
    x1_ref[...] = x1
    xb = x1.astype(BF16)
    words = _pack_bf16_pairs(x1)
    for c in range(PACK_ROWS):
        x1p_ref[pl.ds(c, tm, stride=PACK_ROWS), :] = words[:, c * 128:(c + 1) * 128]

    scores = _sigmoid(_bdot(xb, wr_ref[...]))
    sel = scores + br_ref[...]
    lane = lax.broadcasted_iota(I32, (tm, N_EXPERTS), 1).astype(F32)
    lane_o = lax.broadcasted_iota(I32, (tm, 128), 1)
    idx_o = jnp.zeros((tm, 128), F32)
    ssum = jnp.zeros((tm, 1), F32)
    chosen = jnp.zeros((tm, N_EXPERTS), F32)
    s_sel = []
    for j in range(TOP_K):
        m = jnp.max(sel, axis=-1, keepdims=True)
        am = jnp.min(jnp.where(sel == m, lane, float(N_EXPERTS)), axis=-1, keepdims=True)
        hit = lane == am
        sj = jnp.sum(jnp.where(hit, scores, 0.0), axis=-1, keepdims=True)
        sel = jnp.where(hit, -jnp.inf, sel)
        chosen = jnp.where(hit, 1.0, chosen)
        idx_o = jnp.where(lane_o == j, am, idx_o)
        s_sel.append(sj)
        ssum = ssum + sj
    idx_ref[...] = idx_o.astype(I32)
    gates = jnp.zeros((tm, 128), F32)
    for j in range(TOP_K):
        gates = jnp.where(lane_o == j, s_sel[j] / ssum * ROUTED_SCALE, gates)
    gate_ref[...] = gates

    @pl.when(i == 0)
    def _():
        cnt_ref[...] = jnp.zeros_like(cnt_ref)

    row_id = lax.broadcasted_iota(I32, (tm, N_EXPERTS), 0)
    valid = jnp.logical_or(is_prompt, row_id < n_valid_last)
    cnt_ref[...] += jnp.sum(jnp.where(valid, chosen, 0.0), axis=0, keepdims=True)


def _outproj(mix_p, xn_p, mix_d, xn_d, wout_bf, g1, b1, wr_bf, br):
    tm = OUT_TILE
    n_prompt, n_dec = mix_p.shape[0], mix_d.shape[0]
    assert n_prompt % tm == 0 and n_dec <= tm and n_dec % 8 == 0
    nbp = n_prompt // tm
    T = (nbp + 1) * tm
    padrows = lambda a: jnp.pad(a, ((0, tm - n_dec), (0, 0)))
    row = lambda i: (i, 0)
    prow = lambda i: (jnp.minimum(i, nbp - 1), 0)
    const = lambda i: (0, 0)
    return pl.pallas_call(
        functools.partial(_outproj_kernel, n_prompt_blocks=nbp, n_valid_last=n_dec),
        out_shape=[jax.ShapeDtypeStruct((T, D_MODEL), F32), jax.ShapeDtypeStruct((T * PACK_ROWS, 128), I32),
                   jax.ShapeDtypeStruct((T, 128), I32), jax.ShapeDtypeStruct((T, 128), F32),
                   jax.ShapeDtypeStruct((1, N_EXPERTS), F32)],
        grid=(nbp + 1,),
        in_specs=[pl.BlockSpec((tm, D_MODEL), prow), pl.BlockSpec((tm, D_MODEL), prow),
                  pl.BlockSpec((tm, D_MODEL), const), pl.BlockSpec((tm, D_MODEL), const),
                  pl.BlockSpec((D_MODEL, D_MODEL), const), pl.BlockSpec((1, D_MODEL), const),
                  pl.BlockSpec((1, D_MODEL), const), pl.BlockSpec((D_MODEL, N_EXPERTS), const),
                  pl.BlockSpec((1, N_EXPERTS), const)],
        out_specs=[pl.BlockSpec((tm, D_MODEL), row), pl.BlockSpec((tm * PACK_ROWS, 128), row),
                   pl.BlockSpec((tm, 128), row), pl.BlockSpec((tm, 128), row),
                   pl.BlockSpec((1, N_EXPERTS), const)],
        compiler_params=pltpu.CompilerParams(dimension_semantics=("arbitrary",), vmem_limit_bytes=VMEM_LIMIT),
        name="outproj",
    )(mix_p, xn_p, padrows(mix_d), padrows(xn_d), wout_bf, g1, b1, wr_bf, br)


def _moe_kernel(eb0_ref, enb_ref, wsel_ref, tot_ref, bn_ref, bsrc_ref, bdel_ref,
                order_ref, tokrow_ref, gsort_ref, xp_ref, wg_ref, wu_ref, wd_ref, yt_ref,
                tile_scr, gcol_scr, ybuf, wgb, wub, wdb, ord_smem, tok_smem, gate_smem, sem, sem_o, sem_t, sem_g):
    del wsel_ref
    e = pl.program_id(0)
    g0 = eb0_ref[e]
    nblk = enb_ref[e]
    total = tot_ref[0]
    nb_max = bn_ref.shape[0]
    R = MOE_ROWS
    C = PACK_ROWS
    L = R + ORDER_ALIGN
    n_assign = yt_ref.shape[0] // C - 2 * TOP_K

    class _Fetch:
        def __init__(self, g):
            gc = jnp.minimum(g, nb_max - 1)
            s = g % 4
            src = pl.multiple_of(bsrc_ref[gc], ORDER_ALIGN)
            dst = pl.multiple_of(s * L, ORDER_ALIGN)
            self.copies = (
                pltpu.make_async_copy(order_ref.at[pl.ds(src, L)], ord_smem.at[pl.ds(dst, L)], sem_o.at[s]),
                pltpu.make_async_copy(tokrow_ref.at[pl.ds(src, L)], tok_smem.at[pl.ds(dst, L)], sem_t.at[s]),
                pltpu.make_async_copy(gsort_ref.at[pl.ds(src, L)], gate_smem.at[pl.ds(dst, L)], sem_g.at[s]))

        def start(self):
            for c in self.copies:
                c.start()

        def wait(self):
            for c in self.copies:
                c.wait()

    fetch = _Fetch

    def list_base(g):
        return (g % 4) * L + bdel_ref[jnp.minimum(g, nb_max - 1)]

    def gather(g):
        base = list_base(g)
        trow = (g % 2) * (PACK_ROWS * TILE_STRIDE)
        grow = (g % 2) * R
        for r in range(R):
            t4 = pl.multiple_of(tok_smem[base + r], PACK_ROWS)
            tile_scr[pl.ds(trow + r, PACK_ROWS, stride=TILE_STRIDE), :] = xp_ref[pl.ds(t4, PACK_ROWS), :]
            gcol_scr[pl.ds(grow + r, 1), :] = jnp.full((1, 128), gate_smem[base + r], F32)

    def rows_sent(cnt):
        return pl.multiple_of(lax.shift_left(lax.shift_right_logical(cnt + 7, 3), 3), 8)

    def wait_rows(s, cnt):
        pltpu.make_async_copy(ybuf.at[s, pl.ds(0, cnt * C), :], yt_ref.at[pl.ds(0, cnt * C), :], sem.at[s]).wait()

    @pl.when(e == 0)
    def _():
        n_spare = 2 * TOP_K * C
        ybuf[0, 0:n_spare, :] = jnp.zeros((n_spare, 128), I32)
        init = pltpu.make_async_copy(ybuf.at[0, pl.ds(0, n_spare), :],
                                     yt_ref.at[pl.ds(yt_ref.shape[0] - n_spare, n_spare), :], sem.at[0])
        init.start()
        init.wait()
        fetch(0).start()
        fetch(1).start()
        fetch(0).wait()
        gather(0)

    @pl.when(nblk > 0)
    def _():
        wgb[...] = wg_ref[0].astype(BF16)
        wub[...] = wu_ref[0].astype(BF16)
        wdb[...] = wd_ref[0].astype(BF16)

    def block(k, carry):
        g = g0 + k
        slot = g % 2
        n = bn_ref[g]
        fetch(g + 1).wait()
        fetch(g + 2).start()

        @pl.when(g >= 2)
        def _():
            wait_rows(slot, rows_sent(bn_ref[jnp.maximum(g - 2, 0)]))

        trow = pl.multiple_of(slot * (PACK_ROWS * TILE_STRIDE), 8)

        def compute(send_prev):
            los, his = [], []
            for j in range(PACK_ROWS):
                lo, hi = _unpack_bf16_pairs(tile_scr[pl.ds(trow + j * TILE_STRIDE, R), :])
                los.append(lo.astype(BF16))
                his.append(hi.astype(BF16))
            xg = jnp.concatenate(los + his, axis=1)
            gather(g + 1)
            if send_prev:
                pbase = list_base(g - 1)
                for r in range(R):
                    send(1 - slot, r, ord_smem[pbase + r], r)
            hg = _bdot(xg, wgb[...])
            hu = _bdot(xg, wub[...])
            hb = (hg * _sigmoid(hg)) * hu
            y = _bdot(hb.astype(BF16), wdb[...])
            gcol = gcol_scr[pl.ds(pl.multiple_of(slot * R, 8), R), :]
            y = jnp.concatenate([y[:, c * 128:(c + 1) * 128] * gcol for c in range(D_MODEL // 128)], axis=1)
            words = _pack_bf16_pairs(y)
            yb = ybuf.at[slot]
            for c in range(C):
                yb[pl.ds(c, R, stride=C), :] = words[:, c * 128:(c + 1) * 128]

        prev_full = jnp.logical_and(g >= 1, bn_ref[jnp.maximum(g - 1, 0)] == R)

        @pl.when(prev_full)
        def _():
            compute(True)

        @pl.when(jnp.logical_not(prev_full))
        def _():
            compute(False)

        @pl.when(n < R)
        def _():
            send_rows(g)

        return carry

    def send(s, r, d, u):
        pltpu.make_async_copy(ybuf.at[s, pl.ds(pl.multiple_of(r * C, C), C), :],
                              yt_ref.at[pl.ds(pl.multiple_of(d * C, C), C), :], sem.at[s]).start(priority=u % 2)

    def send_rows(g):
        slot = g % 2
        n = bn_ref[g]
        base = list_base(g)

        def send_group(i, c2):
            for u in range(8):
                send(slot, i * 8 + u, ord_smem[base + i * 8 + u], u)
            return c2

        n_full = lax.shift_right_logical(n, 3)
        lax.fori_loop(0, n_full, send_group, 0)

        @pl.when(n_full * 8 < n)
        def _():
            for u in range(8):
                r = n_full * 8 + u
                send(slot, r, jnp.where(r < n, ord_smem[base + r], n_assign + slot * 8 + u), u)

    lax.fori_loop(0, nblk, block, 0)

    @pl.when(e == pl.num_programs(0) - 1)
    def _():
        @pl.when(bn_ref[jnp.maximum(total - 1, 0)] == R)
        def _():
            send_rows(total - 1)

        fetch(total + 1).wait()
        wait_rows((total - 1) % 2, rows_sent(bn_ref[jnp.maximum(total - 1, 0)]))

        @pl.when(total >= 2)
        def _():
            wait_rows(total % 2, rows_sent(bn_ref[jnp.maximum(total - 2, 0)]))


def _moe(ex_b0, ex_nb, ex_w, total, blk_n, blk_src, blk_delta, order, tokrow, gsort, xp, wg, wu, wd, n_tok):
    R = MOE_ROWS
    wmap_in = lambda e, eb0, enb, wsel, tot, bn, bs, bd: (wsel[e], 0, 0)
    grid_spec = pltpu.PrefetchScalarGridSpec(
        num_scalar_prefetch=7,
        grid=(N_EXPERTS,),
        in_specs=[
            pl.BlockSpec(memory_space=pl.ANY),
            pl.BlockSpec(memory_space=pl.ANY),
            pl.BlockSpec(memory_space=pl.ANY),
            pl.BlockSpec(memory_space=pltpu.VMEM),
            pl.BlockSpec((1, D_MODEL, D_EXPERT), wmap_in),
            pl.BlockSpec((1, D_MODEL, D_EXPERT), wmap_in),
            pl.BlockSpec((1, D_EXPERT, D_MODEL), wmap_in),
        ],
        out_specs=pl.BlockSpec(memory_space=pl.ANY),
        scratch_shapes=[pltpu.VMEM((2 * PACK_ROWS * TILE_STRIDE, 128), I32),
                        pltpu.VMEM((2 * R, 128), F32),
                        pltpu.VMEM((2, R * PACK_ROWS, 128), I32),
                        pltpu.VMEM((D_MODEL, D_EXPERT), BF16),
                        pltpu.VMEM((D_MODEL, D_EXPERT), BF16),
                        pltpu.VMEM((D_EXPERT, D_MODEL), BF16),
                        pltpu.SMEM((4 * (R + ORDER_ALIGN),), I32),
                        pltpu.SMEM((4 * (R + ORDER_ALIGN),), I32),
                        pltpu.SMEM((4 * (R + ORDER_ALIGN),), F32),
                        pltpu.SemaphoreType.DMA((2,)),
                        pltpu.SemaphoreType.DMA((4,)),
                        pltpu.SemaphoreType.DMA((4,)),
                        pltpu.SemaphoreType.DMA((4,))],
    )
    return pl.pallas_call(
        _moe_kernel,
        out_shape=jax.ShapeDtypeStruct(((n_tok + 2) * TOP_K * PACK_ROWS, 128), I32),
        grid_spec=grid_spec,
        compiler_params=pltpu.CompilerParams(dimension_semantics=("arbitrary",),
                                             vmem_limit_bytes=58 * 1024 * 1024),
        name="moe",
    )(ex_b0, ex_nb, ex_w, total, blk_n, blk_src, blk_delta, order, tokrow, gsort, xp, wg, wu, wd)


def _route_plan(idx, gate, counts, n_tok):
    R = MOE_ROWS
    n_assign = n_tok * TOP_K
    nb = (n_assign + N_EXPERTS * (R - 1)) // R
    id_bits = (n_assign - 1).bit_length()
    assert id_bits + (N_EXPERTS - 1).bit_length() < 32
    key = lax.shift_left(idx.reshape(-1), id_bits) | jnp.arange(n_assign, dtype=I32)
    key, gsort = lax.sort((key, gate.reshape(-1)), num_keys=1)
    order = key & ((1 << id_bits) - 1)
    tail = R + ORDER_ALIGN
    order = jnp.concatenate([order, jnp.zeros((tail,), I32)])
    gsort = jnp.concatenate([gsort, jnp.zeros((tail,), F32)])
    nblk_e = (counts + R - 1) // R
    bend = jnp.cumsum(nblk_e)
    bstart = bend - nblk_e
    cstart = jnp.cumsum(counts) - counts
    blk = jnp.arange(nb, dtype=I32)
    blk_e = jnp.minimum(jnp.sum((bend[None, :] <= blk[:, None]).astype(I32), axis=1), N_EXPERTS - 1)
    k = blk - bstart[blk_e]
    active = blk < bend[-1]
    blk_n = jnp.where(active, jnp.clip(counts[blk_e] - k * R, 0, R), 0).astype(I32)
    src = jnp.where(active, cstart[blk_e] + k * R, 0).astype(I32)
    blk_src = (src // ORDER_ALIGN) * ORDER_ALIGN
    ex = jnp.arange(N_EXPERTS, dtype=I32)
    ex_w = jnp.maximum(lax.cummax(jnp.where(nblk_e > 0, ex, -1)), 0).astype(I32)
    tokrow = (order // TOP_K) * PACK_ROWS
    return (bstart.astype(I32), nblk_e.astype(I32), ex_w, bend[-1:].astype(I32),
            blk_n, blk_src, src - blk_src, order, tokrow, gsort)


def _combine_kernel(x1_ref, yt_ref, wgs_ref, wus_ref, wds_ref, g2_ref, b2_ref, out_ref, r_scr):
    tm = x1_ref.shape[0]
    x1 = x1_ref[...]
    xb = x1.astype(BF16)
    hg = _bdot(xb, wgs_ref[...])
    hs = (hg * _sigmoid(hg)) * _bdot(xb, wus_ref[...])
    moe = _bdot(hs.astype(BF16), wds_ref[...])
    pairs = TOP_K * PACK_ROWS // 8
    lo, hi = _unpack_bf16_pairs(yt_ref[...].reshape(tm, pairs, 8, 128))
    lo = jnp.sum(lo, axis=1)
    hi = jnp.sum(hi, axis=1)
    lo = lo + pltpu.roll(lo, PACK_ROWS, 1)
    hi = hi + pltpu.roll(hi, PACK_ROWS, 1)
    sub = lax.broadcasted_iota(I32, (tm, 8, 128), 1)
    r_scr[...] = jnp.where(sub < PACK_ROWS, lo, hi).reshape(tm * 8, 128)
    routed = jnp.concatenate([r_scr[pl.ds(c, tm, stride=8), :] for c in range(8)], axis=1)
    out_ref[...] = _ln(ALPHA * x1 + (moe + routed), g2_ref[...], b2_ref[...])


def _combine(x1, yt, wgs_bf, wus_bf, wds_bf, g2, b2, row0, nrows):
    tm = 128
    assert nrows % tm == 0 and row0 % tm == 0
    off = row0 // tm
    row = lambda i: (i + off, 0)
    const = lambda i: (0, 0)
    return pl.pallas_call(
        _combine_kernel,
        out_shape=jax.ShapeDtypeStruct((nrows, D_MODEL), F32),
        grid=(nrows // tm,),
        in_specs=[pl.BlockSpec((tm, D_MODEL), row),
                  pl.BlockSpec((tm * TOP_K * PACK_ROWS, 128), row),
                  pl.BlockSpec((D_MODEL, D_EXPERT), const), pl.BlockSpec((D_MODEL, D_EXPERT), const),
                  pl.BlockSpec((D_EXPERT, D_MODEL), const),
                  pl.BlockSpec((1, D_MODEL), const), pl.BlockSpec((1, D_MODEL), const)],
        out_specs=pl.BlockSpec((tm, D_MODEL), lambda i: (i, 0)),
        scratch_shapes=[pltpu.VMEM((tm * 8, 128), F32)],
        compiler_params=pltpu.CompilerParams(dimension_semantics=("arbitrary",), vmem_limit_bytes=VMEM_LIMIT),
        name="combine",
    )(x1, yt, wgs_bf, wus_bf, wds_bf, g2, b2)


def kernel(x_prompt, x_sample, state_hgrn, state_pool, meta_tokens, ln_emb_g, ln_emb_b, w_in, lb_logits, hgrn_norm_g, w_pool, pool_scale, w_out, ln1_g, ln1_b, w_router, b_router, w_gate_e, w_up_e, w_down_e, w_gate_s, w_up_s, w_down_s, ln2_g, ln2_b):
    nseq, seqlen, _ = x_prompt.shape
    ndec = x_sample.shape[0]
    l = 0
    row = lambda a: a.reshape(1, -1)
    w_in_bf = w_in[l].astype(BF16)
    wpool_bf = w_pool[l].astype(BF16)
    lng, lnb = row(ln_emb_g), row(ln_emb_b)
    ng, ps = row(hgrn_norm_g[l]), row(pool_scale[l])
    proj = functools.partial(_ln_proj, ln_g=lng, ln_b=lnb, w_in_bf=w_in_bf, lb_logits=lb_logits, norm_g=ng)

    m_xn, m_q, m_k, m_g, m_v, m_gs, m_p = proj(meta_tokens)
    pad = lambda a: jnp.pad(a, ((0, MIX_BLOCK - N_META), (0, 0)))
    zero_state = jnp.zeros((HGRN_HEADS, HEAD_DIM, HEAD_DIM), F32)
    _, s_meta = _mixer(pad(m_q), pad(m_k), pad(m_g), pad(m_v), pad(m_gs), pad(m_p), zero_state,
                       jnp.zeros((SUB, POOL_WIDTH), F32), wpool_bf, ps, 1, MIX_BLOCK)

    p_xn, p_q, p_k, p_g, p_v, p_gs, p_p = proj(x_prompt.reshape(nseq * seqlen, D_MODEL))
    p_mix, s_prompt = _mixer(p_q, p_k, p_g, p_v, p_gs, p_p, s_meta[0], m_p, wpool_bf, ps, nseq, seqlen)

    d_xn, d_q, d_k, d_g, d_v, d_gs, d_p = proj(x_sample.reshape(ndec, D_MODEL))
    cols = lambda a: a.reshape(ndec // STEP_SEQS, STEP_SEQS, HGRN_HEADS, HEAD_DIM).transpose(2, 0, 3, 1)
    s_dec, d_oa = _mixer_step(cols(d_q), cols(d_k), cols(d_g), d_v, d_gs, state_hgrn[l])
    d_ob = _pool_step(state_pool[l].transpose(1, 0, 2), d_p, wpool_bf, ps)
    d_mix = jnp.concatenate([d_oa, d_ob], axis=1).astype(BF16)

    n_tok = nseq * seqlen + ndec
    x1, xp, idx, gate, cnt = _outproj(p_mix, p_xn, d_mix, d_xn, w_out[l].astype(BF16), row(ln1_g[l]), row(ln1_b[l]),
                                      w_router[l].astype(BF16), row(b_router[l]))
    plan = _route_plan(idx[:n_tok, :TOP_K], gate[:n_tok, :TOP_K], cnt[0].astype(I32), n_tok)
    yt = _moe(*plan, xp, w_gate_e[l], w_up_e[l], w_down_e[l], n_tok)
    comb = functools.partial(_combine, x1, yt, w_gate_s[l].astype(BF16), w_up_s[l].astype(BF16),
                             w_down_s[l].astype(BF16), row(ln2_g[l]), row(ln2_b[l]))
    y_prompt = comb(0, nseq * seqlen).reshape(nseq, seqlen, D_MODEL)
    y_sample = comb(nseq * seqlen, ndec).reshape(ndec, 1, D_MODEL)

    state_pool_prompt = p_p.reshape(nseq, seqlen, POOL_WIDTH)[:, seqlen - POOL_BUF:, :]
    state_pool_sample = jnp.concatenate([state_pool[l][:, 1:, :], d_p[:, None, :]], axis=1)
    return (y_prompt, y_sample, s_prompt[None], state_pool_prompt[None], s_dec[None], state_pool_sample[None])
```

```python
import functools

import jax
import jax.numpy as jnp
from jax import lax
from jax.experimental import pallas as pl
from jax.experimental.pallas import tpu as pltpu

F32 = jnp.float32
BF16 = jnp.bfloat16
I32 = jnp.int32

D_MODEL = 1024
N_META = 16
HGRN_WIDTH = 512
HGRN_HEADS = 4
HEAD_DIM = 128
POOL_WIDTH = 512
POOL_WINDOWS = (2, 4, 8, 16)
POOL_GROUP_DIM = 128
POOL_BUF = 15
IN_WIDTH = 4 * HGRN_WIDTH + POOL_WIDTH
N_EXPERTS = 256
TOP_K = 8
D_EXPERT = 256
ROUTED_SCALE = 2.5
DEPTH = 1
ALPHA = (2 * DEPTH) ** 0.25
LN_EPS = 1e-5
RMS_EPS = 1e-6

SUB = 16
MIX_BLOCK = 128
MOE_ROWS = 256
TILE_STRIDE = MOE_ROWS + 8
ORDER_ALIGN = 128
PACK_ROWS = D_MODEL // 2 // 128
TOK_ROWS = (TOP_K + 1) * PACK_ROWS
OUT_TILE = 512
VMEM_LIMIT = 48 * 1024 * 1024


def _ln(x, g, b):
    mu = jnp.mean(x, axis=-1, keepdims=True)
    xc = x - mu
    var = jnp.mean(xc * xc, axis=-1, keepdims=True)
    return xc * lax.rsqrt(var + LN_EPS) * g + b


def _sigmoid(z):
    return 1.0 / (1.0 + jnp.exp(-z))


def _bdot(a, b):
    return jnp.dot(a, b, preferred_element_type=F32)


def _pack_bf16_pairs(x):
    half = x.shape[1] // 2
    xr = x.astype(BF16).astype(F32)
    lo = lax.shift_right_logical(lax.bitcast_convert_type(xr[:, :half], jnp.uint32), jnp.uint32(16))
    hi = lax.bitcast_convert_type(xr[:, half:], jnp.uint32) & jnp.uint32(0xFFFF0000)
    return lax.bitcast_convert_type(hi | lo, I32)


def _unpack_bf16_pairs(words):
    w = lax.bitcast_convert_type(words, jnp.uint32)
    lo = lax.bitcast_convert_type(lax.shift_left(w, jnp.uint32(16)), F32)
    hi = lax.bitcast_convert_type(w & jnp.uint32(0xFFFF0000), F32)
    return lo, hi


def _ln_proj_kernel(x_ref, g_ref, b_ref, w_ref, lbl_ref, ng_ref,
                    xn_ref, q_ref, k_ref, gl_ref, v_ref, gs_ref, p_ref):
    xn = _ln(x_ref[...], g_ref[...], b_ref[...])
    xn_ref[...] = xn
    proj = _bdot(xn.astype(BF16), w_ref[...])
    lbl = lbl_ref[...]
    e = jnp.exp(lbl - jnp.max(lbl, axis=0, keepdims=True))
    lb = e[0:1] / jnp.sum(e, axis=0, keepdims=True)
    W = HGRN_WIDTH
    q = proj[:, 0:W]
    f = proj[:, W:2 * W]
    q_ref[...] = q * _sigmoid(q)
    k_ref[...] = (1.0 - lb) * _sigmoid(-f)
    gl_ref[...] = jnp.log(lb + (1.0 - lb) * _sigmoid(f))
    v_ref[...] = proj[:, 2 * W:3 * W]
    g = proj[:, 3 * W:4 * W]
    gs_ref[...] = ng_ref[...] * (g * _sigmoid(g))
    p_ref[...] = proj[:, 4 * W:]


def _row_tile(n_rows):
    for tm in (256, 128, 64, 32, 16, 8):
        if n_rows % tm == 0:
            return tm
    raise ValueError(f"row count {n_rows} is not a multiple of 8")


def _ln_proj(x, ln_g, ln_b, w_in_bf, lb_logits, norm_g):
    T = x.shape[0]
    tm = _row_tile(T)
    row = lambda i: (i, 0)
    const = lambda i: (0, 0)
    outs = [jax.ShapeDtypeStruct((T, D_MODEL), F32)] + [jax.ShapeDtypeStruct((T, HGRN_WIDTH), F32)] * 6
    return pl.pallas_call(
        _ln_proj_kernel,
        out_shape=outs,
        grid=(T // tm,),
        in_specs=[
            pl.BlockSpec((tm, D_MODEL), row),
            pl.BlockSpec((1, D_MODEL), const),
            pl.BlockSpec((1, D_MODEL), const),
            pl.BlockSpec((D_MODEL, IN_WIDTH), const),
            pl.BlockSpec((DEPTH + 1, HGRN_WIDTH), const),
            pl.BlockSpec((1, HGRN_WIDTH), const),
        ],
        out_specs=[pl.BlockSpec((tm, D_MODEL), row)] + [pl.BlockSpec((tm, HGRN_WIDTH), row)] * 6,
        compiler_params=pltpu.CompilerParams(dimension_semantics=("arbitrary",), vmem_limit_bytes=VMEM_LIMIT),
        name="ln_proj",
    )(x, ln_g, ln_b, w_in_bf, lb_logits, norm_g)


def _pool_group(pe, p_cur, gi, w):
    sl = slice(gi * POOL_GROUP_DIM, (gi + 1) * POOL_GROUP_DIM)
    s = pe[:, sl]
    sh = 1
    while sh < w:
        s = s + pltpu.roll(s, sh, 0)
        sh *= 2
    return s[SUB:, :] * (1.0 / w) - p_cur[:, sl]


def _mixer_kernel(q_ref, k_ref, g_ref, v_ref, gs_ref, p_ref, s0_ref, pp0_ref, wpool_ref, pscale_ref,
                  mix_ref, sfin_ref, st_scr, pe_scr):
    i = pl.program_id(1)
    nblk = pl.num_programs(1)
    n = MIX_BLOCK

    @pl.when(i == 0)
    def _():
        for h in range(HGRN_HEADS):
            st_scr[h] = s0_ref[h].T
        pe_scr[0:SUB, :] = pp0_ref[...]

    rows = lax.broadcasted_iota(I32, (n, HEAD_DIM), 0)
    r16 = rows & (SUB - 1)
    t8 = lax.broadcasted_iota(I32, (8, HEAD_DIM), 0)
    zero_bf = jnp.zeros((SUB, HEAD_DIM), BF16)

    o_heads = []
    for h in range(HGRN_HEADS):
        hs = slice(h * HEAD_DIM, (h + 1) * HEAD_DIM)
        Q = q_ref[:, hs]
        K = k_ref[:, hs]
        G = g_ref[:, hs]
        V = v_ref[:, hs]
        bf = G
        br = G
        for sh in (1, 2, 4, 8):
            bf = bf + jnp.where(r16 >= sh, pltpu.roll(bf, sh, 0), 0.0)
            br = br + jnp.where(r16 < SUB - sh, pltpu.roll(br, n - sh, 0), 0.0)
        br = br - G
        qt = (Q * jnp.exp(bf)).astype(BF16)
        kt = (K * jnp.exp(br)).astype(BF16)
        vt = V.T.astype(BF16)
        st = st_scr[h]
        o_parts = []
        for c in range(n // SUB):
            r0 = c * SUB
            b_top, b_bot = bf[r0:r0 + 8], bf[r0 + 8:r0 + 16]
            q_top, q_bot = Q[r0:r0 + 8], Q[r0 + 8:r0 + 16]
            acc_top = jnp.zeros((8, HEAD_DIM), F32)
            acc_bot = jnp.zeros((8, HEAD_DIM), F32)
            for s in range(SUB):
                bs = bf[r0 + s:r0 + s + 1]
                ks = K[r0 + s:r0 + s + 1]
                vs = V[r0 + s:r0 + s + 1]
                if s < 8:
                    col = jnp.sum(q_top * jnp.exp(b_top - bs) * ks, axis=-1, keepdims=True)
                    col = jnp.where(t8[:, 0:1] >= s, col, 0.0)
                    acc_top = acc_top + col * vs
                    col = jnp.sum(q_bot * jnp.exp(b_bot - bs) * ks, axis=-1, keepdims=True)
                    acc_bot = acc_bot + col * vs
                else:
                    col = jnp.sum(q_bot * jnp.exp(b_bot - bs) * ks, axis=-1, keepdims=True)
                    col = jnp.where(t8[:, 0:1] + 8 >= s, col, 0.0)
                    acc_bot = acc_bot + col * vs
            o_diag = jnp.concatenate([acc_top, acc_bot], axis=0)
            o_inter = lax.dot_general(qt[r0:r0 + SUB], st.astype(BF16), (((1,), (1,)), ((), ())),
                                      preferred_element_type=F32)
            o_parts.append(o_inter + o_diag)
            kmask = jnp.concatenate([zero_bf] * c + [kt[r0:r0 + SUB]] + [zero_bf] * (n // SUB - 1 - c), axis=0)
            d_st = _bdot(vt, kmask)
            st = st * jnp.exp(bf[r0 + SUB - 1:r0 + SUB]) + d_st
        st_scr[h] = st
        o = jnp.concatenate(o_parts, axis=0)
        o = o * lax.rsqrt(jnp.mean(o * o, axis=-1, keepdims=True) + RMS_EPS)
        o_heads.append(o * gs_ref[:, hs])

    p_cur = p_ref[...]
    pe_scr[SUB:SUB + n, :] = p_cur
    pe = pe_scr[...]
    ob = []
    for gi, w in enumerate(POOL_WINDOWS):
        pooled = _pool_group(pe, p_cur, gi, w)
        sl = slice(gi * POOL_GROUP_DIM, (gi + 1) * POOL_GROUP_DIM)
        ob.append(_bdot(pooled.astype(BF16), wpool_ref[gi]) * pscale_ref[:, sl])
    pe_scr[0:SUB, :] = p_cur[n - SUB:, :]

    mix_ref[...] = jnp.concatenate(o_heads + ob, axis=1).astype(BF16)

    @pl.when(i == nblk - 1)
    def _():
        for h in range(HGRN_HEADS):
            sfin_ref[0, h] = st_scr[h].T


def _mixer(q, k, g, v, gs, p, s0, pp0, wpool_bf, pscale, nseq, seqlen):
    nblk = seqlen // MIX_BLOCK
    tok = lambda b, i: (b * nblk + i, 0)
    tspec = pl.BlockSpec((MIX_BLOCK, HGRN_WIDTH), tok)
    return pl.pallas_call(
        _mixer_kernel,
        out_shape=[jax.ShapeDtypeStruct((nseq * seqlen, D_MODEL), BF16),
                   jax.ShapeDtypeStruct((nseq, HGRN_HEADS, HEAD_DIM, HEAD_DIM), F32)],
        grid=(nseq, nblk),
        in_specs=[tspec] * 6 + [
            pl.BlockSpec((HGRN_HEADS, HEAD_DIM, HEAD_DIM), lambda b, i: (0, 0, 0)),
            pl.BlockSpec((SUB, POOL_WIDTH), lambda b, i: (0, 0)),
            pl.BlockSpec((len(POOL_WINDOWS), POOL_GROUP_DIM, POOL_GROUP_DIM), lambda b, i: (0, 0, 0)),
            pl.BlockSpec((1, POOL_WIDTH), lambda b, i: (0, 0)),
        ],
        out_specs=[pl.BlockSpec((MIX_BLOCK, D_MODEL), tok),
                   pl.BlockSpec((1, HGRN_HEADS, HEAD_DIM, HEAD_DIM), lambda b, i: (b, 0, 0, 0))],
        scratch_shapes=[pltpu.VMEM((HGRN_HEADS, HEAD_DIM, HEAD_DIM), F32),
                        pltpu.VMEM((SUB + MIX_BLOCK, POOL_WIDTH), F32)],
        compiler_params=pltpu.CompilerParams(dimension_semantics=("arbitrary", "arbitrary"),
                                             vmem_limit_bytes=VMEM_LIMIT),
        name="mixer",
    )(q, k, g, v, gs, p, s0, pp0, wpool_bf, pscale)


STEP_SEQS = 32


def _mixer_step_kernel(qt_ref, kt_ref, gt_ref, v_ref, gs_ref, s_ref, snew_ref, oa_ref):
    qt = qt_ref[0, 0]
    kt = kt_ref[0, 0]
    dt = jnp.exp(gt_ref[0, 0])
    rows = []
    for bb in range(STEP_SEQS):
        sn = s_ref[bb, 0] * dt[:, bb:bb + 1] + kt[:, bb:bb + 1] * v_ref[bb:bb + 1, :]
        snew_ref[bb, 0] = sn
        rows.append(jnp.sum(sn * qt[:, bb:bb + 1], axis=0, keepdims=True))
    o = jnp.concatenate(rows, axis=0)
    o = o * lax.rsqrt(jnp.mean(o * o, axis=-1, keepdims=True) + RMS_EPS)
    oa_ref[...] = o * gs_ref[...]


def _mixer_step(qT, kT, gT, v, gs, state):
    nseq = v.shape[0]
    nbc = nseq // STEP_SEQS
    cspec = pl.BlockSpec((1, 1, HEAD_DIM, STEP_SEQS), lambda h, c: (h, c, 0, 0))
    rspec = pl.BlockSpec((STEP_SEQS, HEAD_DIM), lambda h, c: (c, h))
    sspec = pl.BlockSpec((STEP_SEQS, 1, HEAD_DIM, HEAD_DIM), lambda h, c: (c, h, 0, 0))
    return pl.pallas_call(
        _mixer_step_kernel,
        out_shape=[jax.ShapeDtypeStruct(state.shape, F32), jax.ShapeDtypeStruct((nseq, HGRN_WIDTH), F32)],
        grid=(HGRN_HEADS, nbc),
        in_specs=[cspec, cspec, cspec, rspec, rspec, sspec],
        out_specs=[sspec, rspec],
        compiler_params=pltpu.CompilerParams(dimension_semantics=("arbitrary", "arbitrary"),
                                             vmem_limit_bytes=VMEM_LIMIT),
        name="mixer_step",
    )(qT, kT, gT, v, gs, state)


def _pool_step_kernel(sp_ref, p_ref, wpool_ref, pscale_ref, ob_ref):
    p_cur = p_ref[...]
    outs = []
    for gi, w in enumerate(POOL_WINDOWS):
        sl = slice(gi * POOL_GROUP_DIM, (gi + 1) * POOL_GROUP_DIM)
        s = p_cur[:, sl]
        for r in range(POOL_BUF - (w - 1), POOL_BUF):
            s = s + sp_ref[r][:, sl]
        pooled = s * (1.0 / w) - p_cur[:, sl]
        outs.append(_bdot(pooled.astype(BF16), wpool_ref[gi]) * pscale_ref[:, sl])
    ob_ref[...] = jnp.concatenate(outs, axis=1)


def _pool_step(spT, p, wpool_bf, pscale):
    nseq = p.shape[0]
    return pl.pallas_call(
        _pool_step_kernel,
        out_shape=jax.ShapeDtypeStruct((nseq, POOL_WIDTH), F32),
        name="pool_step",
    )(spT, p, wpool_bf, pscale)


def _outproj_kernel(mixp_ref, xnp_ref, mixd_ref, xnd_ref, wout_ref, g1_ref, b1_ref, wr_ref, br_ref,
                    x1_ref, x1p_ref, idx_ref, gate_ref, cnt_ref, *, n_prompt_blocks, n_valid_last):
    tm = mixp_ref.shape[0]
    i = pl.program_id(0)
    is_prompt = i < n_prompt_blocks
    mix_in = jnp.where(is_prompt, mixp_ref[...], mixd_ref[...])
    xn = jnp.where(is_prompt, xnp_ref[...], xnd_ref[...])
    mix = _bdot(mix_in, wout_ref[...])
    x1 = _ln(ALPHA * xn + mix, g1_ref[...], b1_ref[...])
    x1_ref[...] = x1
    xb = x1.astype(BF16)
    words = _pack_bf16_pairs(x1)
    for c in range(PACK_ROWS):
        x1p_ref[pl.ds(c, tm, stride=PACK_ROWS), :] = words[:, c * 128:(c + 1) * 128]

    scores = _sigmoid(_bdot(xb, wr_ref[...]))
    sel = scores + br_ref[...]
    lane = lax.broadcasted_iota(I32, (tm, N_EXPERTS), 1).astype(F32)
    lane_o = lax.broadcasted_iota(I32, (tm, 128), 1)
    idx_o = jnp.zeros((tm, 128), F32)
    ssum = jnp.zeros((tm, 1), F32)
    chosen = jnp.zeros((tm, N_EXPERTS), F32)
    s_sel = []
    for j in range(TOP_K):
        m = jnp.max(sel, axis=-1, keepdims=True)
        am = jnp.min(jnp.where(sel == m, lane, float(N_EXPERTS)), axis=-1, keepdims=True)
        hit = lane == am
        sj = jnp.sum(jnp.where(hit, scores, 0.0), axis=-1, keepdims=True)
        sel = jnp.where(hit, -jnp.inf, sel)
        chosen = jnp.where(hit, 1.0, chosen)
        idx_o = jnp.where(lane_o == j, am, idx_o)
        s_sel.append(sj)
        ssum = ssum + sj
    idx_ref[...] = idx_o.astype(I32)
    gates = jnp.zeros((tm, 128), F32)
    for j in range(TOP_K):
        gates = jnp.where(lane_o == j, s_sel[j] / ssum * ROUTED_SCALE, gates)
    gate_ref[...] = gates

    @pl.when(i == 0)
    def _():
        cnt_ref[...] = jnp.zeros_like(cnt_ref)

    row_id = lax.broadcasted_iota(I32, (tm, N_EXPERTS), 0)
    valid = jnp.logical_or(is_prompt, row_id < n_valid_last)
    cnt_ref[...] += jnp.sum(jnp.where(valid, chosen, 0.0), axis=0, keepdims=True)


def _outproj(mix_p, xn_p, mix_d, xn_d, wout_bf, g1, b1, wr_bf, br):
    tm = OUT_TILE
    n_prompt, n_dec = mix_p.shape[0], mix_d.shape[0]
    assert n_prompt % tm == 0 and n_dec <= tm and n_dec % 8 == 0
    nbp = n_prompt // tm
    T = (nbp + 1) * tm
    padrows = lambda a: jnp.pad(a, ((0, tm - n_dec), (0, 0)))
    row = lambda i: (i, 0)
    prow = lambda i: (jnp.minimum(i, nbp - 1), 0)
    const = lambda i: (0, 0)
    return pl.pallas_call(
        functools.partial(_outproj_kernel, n_prompt_blocks=nbp, n_valid_last=n_dec),
        out_shape=[jax.ShapeDtypeStruct((T, D_MODEL), F32), jax.ShapeDtypeStruct((T * PACK_ROWS, 128), I32),
                   jax.ShapeDtypeStruct((T, 128), I32), jax.ShapeDtypeStruct((T, 128), F32),
                   jax.ShapeDtypeStruct((1, N_EXPERTS), F32)],
        grid=(nbp + 1,),
        in_specs=[pl.BlockSpec((tm, D_MODEL), prow), pl.BlockSpec((tm, D_MODEL), prow),
                  pl.BlockSpec((tm, D_MODEL), const), pl.BlockSpec((tm, D_MODEL), const),
                  pl.BlockSpec((D_MODEL, D_MODEL), const), pl.BlockSpec((1, D_MODEL), const),
                  pl.BlockSpec((1, D_MODEL), const), pl.BlockSpec((D_MODEL, N_EXPERTS), const),
                  pl.BlockSpec((1, N_EXPERTS), const)],
        out_specs=[pl.BlockSpec((tm, D_MODEL), row), pl.BlockSpec((tm * PACK_ROWS, 128), row),
                   pl.BlockSpec((tm, 128), row), pl.BlockSpec((tm, 128), row),
                   pl.BlockSpec((1, N_EXPERTS), const)],
        compiler_params=pltpu.CompilerParams(dimension_semantics=("arbitrary",), vmem_limit_bytes=VMEM_LIMIT),
        name="outproj",
    )(mix_p, xn_p, padrows(mix_d), padrows(xn_d), wout_bf, g1, b1, wr_bf, br)


def _moe_kernel(eb0_ref, enb_ref, wsel_ref, tot_ref, bn_ref, bsrc_ref, bdel_ref,
                dst_ref, tokrow_ref, xp_ref, wg_ref, wu_ref, wd_ref, yt_init_ref, yt_ref,
                tile_scr, ybuf, wgb, wub, wdb, dst_smem, tok_smem, sem, sem_o, sem_t):
    del wsel_ref, yt_init_ref
    e = pl.program_id(0)
    g0 = eb0_ref[e]
    nblk = enb_ref[e]
    total = tot_ref[0]
    nb_max = bn_ref.shape[0]
    R = MOE_ROWS
    C = PACK_ROWS
    L = R + ORDER_ALIGN
    spare0 = yt_ref.shape[0] - 2 * TOK_ROWS

    class _Fetch:
        def __init__(self, g):
            gc = jnp.minimum(g, nb_max - 1)
            s = g % 4
            src = pl.multiple_of(bsrc_ref[gc], ORDER_ALIGN)
            dst = pl.multiple_of(s * L, ORDER_ALIGN)
            self.copies = (
                pltpu.make_async_copy(dst_ref.at[pl.ds(src, L)], dst_smem.at[pl.ds(dst, L)], sem_o.at[s]),
                pltpu.make_async_copy(tokrow_ref.at[pl.ds(src, L)], tok_smem.at[pl.ds(dst, L)], sem_t.at[s]))

        def start(self):
            for c in self.copies:
                c.start()

        def wait(self):
            for c in self.copies:
                c.wait()

    fetch = _Fetch

    def list_base(g):
        return (g % 4) * L + bdel_ref[jnp.minimum(g, nb_max - 1)]

    def gather(g):
        base = list_base(g)
        trow = (g % 2) * (PACK_ROWS * TILE_STRIDE)
        for r in range(R):
            t4 = pl.multiple_of(tok_smem[base + r], PACK_ROWS)
            tile_scr[pl.ds(trow + r, PACK_ROWS, stride=TILE_STRIDE), :] = xp_ref[pl.ds(t4, PACK_ROWS), :]

    def rows_sent(cnt):
        return pl.multiple_of(lax.shift_left(lax.shift_right_logical(cnt + 7, 3), 3), 8)

    def wait_rows(s, cnt):
        pltpu.make_async_copy(ybuf.at[s, pl.ds(0, cnt * C), :], yt_ref.at[pl.ds(0, cnt * C), :], sem.at[s]).wait()

    @pl.when(e == 0)
    def _():
        fetch(0).start()
        fetch(1).start()
        fetch(0).wait()
        gather(0)

    @pl.when(nblk > 0)
    def _():
        wgb[...] = wg_ref[0].astype(BF16)
        wub[...] = wu_ref[0].astype(BF16)
        wdb[...] = wd_ref[0].astype(BF16)

    def block(k, carry):
        g = g0 + k
        slot = g % 2
        n = bn_ref[g]
        fetch(g + 1).wait()
        fetch(g + 2).start()

        @pl.when(g >= 2)
        def _():
            wait_rows(slot, rows_sent(bn_ref[jnp.maximum(g - 2, 0)]))

        trow = pl.multiple_of(slot * (PACK_ROWS * TILE_STRIDE), 8)

        def compute(send_prev):
            los, his = [], []
            for j in range(PACK_ROWS):
                lo, hi = _unpack_bf16_pairs(tile_scr[pl.ds(trow + j * TILE_STRIDE, R), :])
                los.append(lo.astype(BF16))
                his.append(hi.astype(BF16))
            xg = jnp.concatenate(los + his, axis=1)
            gather(g + 1)
            if send_prev:
                pbase = list_base(g - 1)
                for r in range(R):
                    send(1 - slot, r, dst_smem[pbase + r], r)
            hg = _bdot(xg, wgb[...])
            hu = _bdot(xg, wub[...])
            hb = (hg * _sigmoid(hg)) * hu
            y = _bdot(hb.astype(BF16), wdb[...])
            words = _pack_bf16_pairs(y)
            yb = ybuf.at[slot]
            for c in range(C):
                yb[pl.ds(c, R, stride=C), :] = words[:, c * 128:(c + 1) * 128]

        prev_full = jnp.logical_and(g >= 1, bn_ref[jnp.maximum(g - 1, 0)] == R)

        @pl.when(prev_full)
        def _():
            compute(True)

        @pl.when(jnp.logical_not(prev_full))
        def _():
            compute(False)

        @pl.when(n < R)
        def _():
            send_rows(g)

        return carry

    def send(s, r, drow, u):
        pltpu.make_async_copy(ybuf.at[s, pl.ds(pl.multiple_of(r * C, C), C), :],
                              yt_ref.at[pl.ds(pl.multiple_of(drow, C), C), :], sem.at[s]).start(priority=u % 2)

    def send_rows(g):
        slot = g % 2
        n = bn_ref[g]
        base = list_base(g)

        def send_group(i, c2):
            for u in range(8):
                send(slot, i * 8 + u, dst_smem[base + i * 8 + u], u)
            return c2

        n_full = lax.shift_right_logical(n, 3)
        lax.fori_loop(0, n_full, send_group, 0)

        @pl.when(n_full * 8 < n)
        def _():
            for u in range(8):
                r = n_full * 8 + u
                send(slot, r, jnp.where(r < n, dst_smem[base + r], spare0 + (slot * 8 + u) * C), u)

    lax.fori_loop(0, nblk, block, 0)

    @pl.when(e == pl.num_programs(0) - 1)
    def _():
        @pl.when(bn_ref[jnp.maximum(total - 1, 0)] == R)
        def _():
            send_rows(total - 1)

        fetch(total + 1).wait()
        wait_rows((total - 1) % 2, rows_sent(bn_ref[jnp.maximum(total - 1, 0)]))

        @pl.when(total >= 2)
        def _():
            wait_rows(total % 2, rows_sent(bn_ref[jnp.maximum(total - 2, 0)]))


def _moe(ex_b0, ex_nb, ex_w, total, blk_n, blk_src, blk_delta, dstrow, tokrow, xp, wg, wu, wd, n_tok):
    R = MOE_ROWS
    wmap_in = lambda e, eb0, enb, wsel, tot, bn, bs, bd: (wsel[e], 0, 0)
    grid_spec = pltpu.PrefetchScalarGridSpec(
        num_scalar_prefetch=7,
        grid=(N_EXPERTS,),
        in_specs=[
            pl.BlockSpec(memory_space=pl.ANY),
            pl.BlockSpec(memory_space=pl.ANY),
            pl.BlockSpec(memory_space=pltpu.VMEM),
            pl.BlockSpec((1, D_MODEL, D_EXPERT), wmap_in),
            pl.BlockSpec((1, D_MODEL, D_EXPERT), wmap_in),
            pl.BlockSpec((1, D_EXPERT, D_MODEL), wmap_in),
            pl.BlockSpec(memory_space=pl.ANY),
        ],
        out_specs=pl.BlockSpec(memory_space=pl.ANY),
        scratch_shapes=[pltpu.VMEM((2 * PACK_ROWS * TILE_STRIDE, 128), I32),
                        pltpu.VMEM((2, R * PACK_ROWS, 128), I32),
                        pltpu.VMEM((D_MODEL, D_EXPERT), BF16),
                        pltpu.VMEM((D_MODEL, D_EXPERT), BF16),
                        pltpu.VMEM((D_EXPERT, D_MODEL), BF16),
                        pltpu.SMEM((4 * (R + ORDER_ALIGN),), I32),
                        pltpu.SMEM((4 * (R + ORDER_ALIGN),), I32),
                        pltpu.SemaphoreType.DMA((2,)),
                        pltpu.SemaphoreType.DMA((4,)),
                        pltpu.SemaphoreType.DMA((4,))],
    )
    yt_init = jnp.zeros(((n_tok + 2) * TOK_ROWS, 128), I32)
    return pl.pallas_call(
        _moe_kernel,
        out_shape=jax.ShapeDtypeStruct(yt_init.shape, I32),
        grid_spec=grid_spec,
        input_output_aliases={7 + 6: 0},
        compiler_params=pltpu.CompilerParams(dimension_semantics=("arbitrary",),
                                             vmem_limit_bytes=58 * 1024 * 1024),
        name="moe",
    )(ex_b0, ex_nb, ex_w, total, blk_n, blk_src, blk_delta, dstrow, tokrow, xp, wg, wu, wd, yt_init)


def _route_plan(idx, counts, n_tok):
    R = MOE_ROWS
    n_assign = n_tok * TOP_K
    nb = (n_assign + N_EXPERTS * (R - 1)) // R
    id_bits = (n_assign - 1).bit_length()
    assert id_bits + (N_EXPERTS - 1).bit_length() < 32
    key = lax.shift_left(idx.reshape(-1), id_bits) | jnp.arange(n_assign, dtype=I32)
    order = lax.sort(key) & ((1 << id_bits) - 1)
    order = jnp.concatenate([order, jnp.zeros((R + ORDER_ALIGN,), I32)])
    nblk_e = (counts + R - 1) // R
    bend = jnp.cumsum(nblk_e)
    bstart = bend - nblk_e
    cstart = jnp.cumsum(counts) - counts
    blk = jnp.arange(nb, dtype=I32)
    blk_e = jnp.minimum(jnp.sum((bend[None, :] <= blk[:, None]).astype(I32), axis=1), N_EXPERTS - 1)
    k = blk - bstart[blk_e]
    active = blk < bend[-1]
    blk_n = jnp.where(active, jnp.clip(counts[blk_e] - k * R, 0, R), 0).astype(I32)
    src = jnp.where(active, cstart[blk_e] + k * R, 0).astype(I32)
    blk_src = (src // ORDER_ALIGN) * ORDER_ALIGN
    ex = jnp.arange(N_EXPERTS, dtype=I32)
    ex_w = jnp.maximum(lax.cummax(jnp.where(nblk_e > 0, ex, -1)), 0).astype(I32)
    tok = order // TOP_K
    tokrow = tok * PACK_ROWS
    dstrow = tok * TOK_ROWS + (order % TOP_K) * PACK_ROWS
    return (bstart.astype(I32), nblk_e.astype(I32), ex_w, bend[-1:].astype(I32),
            blk_n, blk_src, src - blk_src, dstrow, tokrow)


def _combine_kernel(x1_ref, yt_ref, gate_ref, wgs_ref, wus_ref, wds_ref, g2_ref, b2_ref, out_ref):
    tm = x1_ref.shape[0]
    x1 = x1_ref[...]
    xb = x1.astype(BF16)
    hg = _bdot(xb, wgs_ref[...])
    hs = (hg * _sigmoid(hg)) * _bdot(xb, wus_ref[...])
    moe = _bdot(hs.astype(BF16), wds_ref[...])
    gate = gate_ref[...]
    lo_acc = [None] * PACK_ROWS
    hi_acc = [None] * PACK_ROWS
    for j in range(TOP_K):
        gj = gate[:, j:j + 1]
        for c in range(PACK_ROWS):
            lo, hi = _unpack_bf16_pairs(yt_ref[pl.ds(j * PACK_ROWS + c, tm, stride=TOK_ROWS), :])
            lo_acc[c] = lo * gj if j == 0 else lo_acc[c] + lo * gj
            hi_acc[c] = hi * gj if j == 0 else hi_acc[c] + hi * gj
    routed = jnp.concatenate(lo_acc + hi_acc, axis=1)
    out_ref[...] = _ln(ALPHA * x1 + (moe + routed), g2_ref[...], b2_ref[...])


def _combine(x1, yt, gate, wgs_bf, wus_bf, wds_bf, g2, b2, row0, nrows):
    tm = 128
    assert nrows % tm == 0 and row0 % tm == 0
    off = row0 // tm
    row = lambda i: (i + off, 0)
    const = lambda i: (0, 0)
    return pl.pallas_call(
        _combine_kernel,
        out_shape=jax.ShapeDtypeStruct((nrows, D_MODEL), F32),
        grid=(nrows // tm,),
        in_specs=[pl.BlockSpec((tm, D_MODEL), row),
                  pl.BlockSpec((tm * TOK_ROWS, 128), row),
                  pl.BlockSpec((tm, 128), row),
                  pl.BlockSpec((D_MODEL, D_EXPERT), const), pl.BlockSpec((D_MODEL, D_EXPERT), const),
                  pl.BlockSpec((D_EXPERT, D_MODEL), const),
                  pl.BlockSpec((1, D_MODEL), const), pl.BlockSpec((1, D_MODEL), const)],
        out_specs=pl.BlockSpec((tm, D_MODEL), lambda i: (i, 0)),
        compiler_params=pltpu.CompilerParams(dimension_semantics=("arbitrary",), vmem_limit_bytes=VMEM_LIMIT),
        name="combine",
    )(x1, yt, gate, wgs_bf, wus_bf, wds_bf, g2, b2)


def kernel(x_prompt, x_sample, state_hgrn, state_pool, meta_tokens, ln_emb_g, ln_emb_b, w_in, lb_logits, hgrn_norm_g, w_pool, pool_scale, w_out, ln1_g, ln1_b, w_router, b_router, w_gate_e, w_up_e, w_down_e, w_gate_s, w_up_s, w_down_s, ln2_g, ln2_b):
    nseq, seqlen, _ = x_prompt.shape
    ndec = x_sample.shape[0]
    l = 0
    row = lambda a: a.reshape(1, -1)
    w_in_bf = w_in[l].astype(BF16)
    wpool_bf = w_pool[l].astype(BF16)
    lng, lnb = row(ln_emb_g), row(ln_emb_b)
    ng, ps = row(hgrn_norm_g[l]), row(pool_scale[l])
    proj = functools.partial(_ln_proj, ln_g=lng, ln_b=lnb, w_in_bf=w_in_bf, lb_logits=lb_logits, norm_g=ng)

    m_xn, m_q, m_k, m_g, m_v, m_gs, m_p = proj(meta_tokens)
    pad = lambda a: jnp.pad(a, ((0, MIX_BLOCK - N_META), (0, 0)))
    zero_state = jnp.zeros((HGRN_HEADS, HEAD_DIM, HEAD_DIM), F32)
    _, s_meta = _mixer(pad(m_q), pad(m_k), pad(m_g), pad(m_v), pad(m_gs), pad(m_p), zero_state,
                       jnp.zeros((SUB, POOL_WIDTH), F32), wpool_bf, ps, 1, MIX_BLOCK)

    p_xn, p_q, p_k, p_g, p_v, p_gs, p_p = proj(x_prompt.reshape(nseq * seqlen, D_MODEL))
    p_mix, s_prompt = _mixer(p_q, p_k, p_g, p_v, p_gs, p_p, s_meta[0], m_p, wpool_bf, ps, nseq, seqlen)

    d_xn, d_q, d_k, d_g, d_v, d_gs, d_p = proj(x_sample.reshape(ndec, D_MODEL))
    cols = lambda a: a.reshape(ndec // STEP_SEQS, STEP_SEQS, HGRN_HEADS, HEAD_DIM).transpose(2, 0, 3, 1)
    s_dec, d_oa = _mixer_step(cols(d_q), cols(d_k), cols(d_g), d_v, d_gs, state_hgrn[l])
    d_ob = _pool_step(state_pool[l].transpose(1, 0, 2), d_p, wpool_bf, ps)
    d_mix = jnp.concatenate([d_oa, d_ob], axis=1).astype(BF16)

    n_tok = nseq * seqlen + ndec
    x1, xp, idx, gate, cnt = _outproj(p_mix, p_xn, d_mix, d_xn, w_out[l].astype(BF16), row(ln1_g[l]), row(ln1_b[l]),
                                      w_router[l].astype(BF16), row(b_router[l]))
    plan = _route_plan(idx[:n_tok, :TOP_K], cnt[0].astype(I32), n_tok)
    yt = _moe(*plan, xp, w_gate_e[l], w_up_e[l], w_down_e[l], n_tok)
    comb = functools.partial(_combine, x1, yt, gate, w_gate_s[l].astype(BF16), w_up_s[l].astype(BF16),
                             w_down_s[l].astype(BF16), row(ln2_g[l]), row(ln2_b[l]))
    y_prompt = comb(0, nseq * seqlen).reshape(nseq, seqlen, D_MODEL)
    y_sample = comb(nseq * seqlen, ndec).reshape(ndec, 1, D_MODEL)

    state_pool_prompt = p_p.reshape(nseq, seqlen, POOL_WIDTH)[:, seqlen - POOL_BUF:, :]
    state_pool_sample = jnp.concatenate([state_pool[l][:, 1:, :], d_p[:, None, :]], axis=1)
    return (y_prompt, y_sample, s_prompt[None], state_pool_prompt[None], s_dec[None], state_pool_sample[None])
```

```python
import functools

import jax
import jax.numpy as jnp
from jax import lax
from jax.experimental import pallas as pl
from jax.experimental.pallas import tpu as pltpu

F32 = jnp.float32
BF16 = jnp.bfloat16
I32 = jnp.int32

D_MODEL = 1024
N_META = 16
HGRN_WIDTH = 512
HGRN_HEADS = 4
HEAD_DIM = 128
POOL_WIDTH = 512
POOL_WINDOWS = (2, 4, 8, 16)
POOL_GROUP_DIM = 128
POOL_BUF = 15
IN_WIDTH = 4 * HGRN_WIDTH + POOL_WIDTH
N_EXPERTS = 256
TOP_K = 8
D_EXPERT = 256
ROUTED_SCALE = 2.5
DEPTH = 1
ALPHA = (2 * DEPTH) ** 0.25
LN_EPS = 1e-5
RMS_EPS = 1e-6

SUB = 16
MIX_BLOCK = 128
MOE_ROWS = 256
TILE_STRIDE = MOE_ROWS + 8
ORDER_ALIGN = 128
PACK_ROWS = D_MODEL // 2 // 128
OUT_TILE = 512
VMEM_LIMIT = 48 * 1024 * 1024


def _ln(x, g, b):
    mu = jnp.mean(x, axis=-1, keepdims=True)
    xc = x - mu
    var = jnp.mean(xc * xc, axis=-1, keepdims=True)
    return xc * lax.rsqrt(var + LN_EPS) * g + b


def _sigmoid(z):
    return 1.0 / (1.0 + jnp.exp(-z))


def _bdot(a, b):
    return jnp.dot(a, b, preferred_element_type=F32)


def _pack_bf16_pairs(x):
    half = x.shape[1] // 2
    xr = x.astype(BF16).astype(F32)
    lo = lax.shift_right_logical(lax.bitcast_convert_type(xr[:, :half], jnp.uint32), jnp.uint32(16))
    hi = lax.bitcast_convert_type(xr[:, half:], jnp.uint32) & jnp.uint32(0xFFFF0000)
    return lax.bitcast_convert_type(hi | lo, I32)


def _unpack_bf16_pairs(words):
    w = lax.bitcast_convert_type(words, jnp.uint32)
    lo = lax.bitcast_convert_type(lax.shift_left(w, jnp.uint32(16)), F32)
    hi = lax.bitcast_convert_type(w & jnp.uint32(0xFFFF0000), F32)
    return lo, hi


def _ln_proj_kernel(x_ref, g_ref, b_ref, w_ref, lbl_ref, ng_ref,
                    xn_ref, q_ref, k_ref, gl_ref, v_ref, gs_ref, p_ref):
    xn = _ln(x_ref[...], g_ref[...], b_ref[...])
    xn_ref[...] = xn
    proj = _bdot(xn.astype(BF16), w_ref[...])
    lbl = lbl_ref[...]
    e = jnp.exp(lbl - jnp.max(lbl, axis=0, keepdims=True))
    lb = e[0:1] / jnp.sum(e, axis=0, keepdims=True)
    W = HGRN_WIDTH
    q = proj[:, 0:W]
    f = proj[:, W:2 * W]
    q_ref[...] = q * _sigmoid(q)
    k_ref[...] = (1.0 - lb) * _sigmoid(-f)
    gl_ref[...] = jnp.log(lb + (1.0 - lb) * _sigmoid(f))
    v_ref[...] = proj[:, 2 * W:3 * W]
    g = proj[:, 3 * W:4 * W]
    gs_ref[...] = ng_ref[...] * (g * _sigmoid(g))
    p_ref[...] = proj[:, 4 * W:]


def _row_tile(n_rows):
    for tm in (256, 128, 64, 32, 16, 8):
        if n_rows % tm == 0:
            return tm
    raise ValueError(f"row count {n_rows} is not a multiple of 8")


def _ln_proj(x, ln_g, ln_b, w_in_bf, lb_logits, norm_g):
    T = x.shape[0]
    tm = _row_tile(T)
    row = lambda i: (i, 0)
    const = lambda i: (0, 0)
    outs = [jax.ShapeDtypeStruct((T, D_MODEL), F32)] + [jax.ShapeDtypeStruct((T, HGRN_WIDTH), F32)] * 6
    return pl.pallas_call(
        _ln_proj_kernel,
        out_shape=outs,
        grid=(T // tm,),
        in_specs=[
            pl.BlockSpec((tm, D_MODEL), row),
            pl.BlockSpec((1, D_MODEL), const),
            pl.BlockSpec((1, D_MODEL), const),
            pl.BlockSpec((D_MODEL, IN_WIDTH), const),
            pl.BlockSpec((DEPTH + 1, HGRN_WIDTH), const),
            pl.BlockSpec((1, HGRN_WIDTH), const),
        ],
        out_specs=[pl.BlockSpec((tm, D_MODEL), row)] + [pl.BlockSpec((tm, HGRN_WIDTH), row)] * 6,
        compiler_params=pltpu.CompilerParams(dimension_semantics=("arbitrary",), vmem_limit_bytes=VMEM_LIMIT),
        name="ln_proj",
    )(x, ln_g, ln_b, w_in_bf, lb_logits, norm_g)


def _pool_group(pe, p_cur, gi, w):
    sl = slice(gi * POOL_GROUP_DIM, (gi + 1) * POOL_GROUP_DIM)
    s = pe[:, sl]
    sh = 1
    while sh < w:
        s = s + pltpu.roll(s, sh, 0)
        sh *= 2
    return s[SUB:, :] * (1.0 / w) - p_cur[:, sl]


def _mixer_kernel(q_ref, k_ref, g_ref, v_ref, gs_ref, p_ref, s0_ref, pp0_ref, wpool_ref, pscale_ref,
                  mix_ref, sfin_ref, st_scr, pe_scr):
    i = pl.program_id(1)
    nblk = pl.num_programs(1)
    n = MIX_BLOCK

    @pl.when(i == 0)
    def _():
        for h in range(HGRN_HEADS):
            st_scr[h] = s0_ref[h].T
        pe_scr[0:SUB, :] = pp0_ref[...]

    rows = lax.broadcasted_iota(I32, (n, HEAD_DIM), 0)
    r16 = rows & (SUB - 1)
    t8 = lax.broadcasted_iota(I32, (8, HEAD_DIM), 0)
    zero_bf = jnp.zeros((SUB, HEAD_DIM), BF16)

    o_heads = []
    for h in range(HGRN_HEADS):
        hs = slice(h * HEAD_DIM, (h + 1) * HEAD_DIM)
        Q = q_ref[:, hs]
        K = k_ref[:, hs]
        G = g_ref[:, hs]
        V = v_ref[:, hs]
        bf = G
        br = G
        for sh in (1, 2, 4, 8):
            bf = bf + jnp.where(r16 >= sh, pltpu.roll(bf, sh, 0), 0.0)
            br = br + jnp.where(r16 < SUB - sh, pltpu.roll(br, n - sh, 0), 0.0)
        br = br - G
        qt = (Q * jnp.exp(bf)).astype(BF16)
        kt = (K * jnp.exp(br)).astype(BF16)
        vt = V.T.astype(BF16)
        st = st_scr[h]
        o_parts = []
        for c in range(n // SUB):
            r0 = c * SUB
            b_top, b_bot = bf[r0:r0 + 8], bf[r0 + 8:r0 + 16]
            q_top, q_bot = Q[r0:r0 + 8], Q[r0 + 8:r0 + 16]
            acc_top = jnp.zeros((8, HEAD_DIM), F32)
            acc_bot = jnp.zeros((8, HEAD_DIM), F32)
            for s in range(SUB):
                bs = bf[r0 + s:r0 + s + 1]
                ks = K[r0 + s:r0 + s + 1]
                vs = V[r0 + s:r0 + s + 1]
                if s < 8:
                    col = jnp.sum(q_top * jnp.exp(b_top - bs) * ks, axis=-1, keepdims=True)
                    col = jnp.where(t8[:, 0:1] >= s, col, 0.0)
                    acc_top = acc_top + col * vs
                    col = jnp.sum(q_bot * jnp.exp(b_bot - bs) * ks, axis=-1, keepdims=True)
                    acc_bot = acc_bot + col * vs
                else:
                    col = jnp.sum(q_bot * jnp.exp(b_bot - bs) * ks, axis=-1, keepdims=True)
                    col = jnp.where(t8[:, 0:1] + 8 >= s, col, 0.0)
                    acc_bot = acc_bot + col * vs
            o_diag = jnp.concatenate([acc_top, acc_bot], axis=0)
            o_inter = lax.dot_general(qt[r0:r0 + SUB], st.astype(BF16), (((1,), (1,)), ((), ())),
                                      preferred_element_type=F32)
            o_parts.append(o_inter + o_diag)
            kmask = jnp.concatenate([zero_bf] * c + [kt[r0:r0 + SUB]] + [zero_bf] * (n // SUB - 1 - c), axis=0)
            d_st = _bdot(vt, kmask)
            st = st * jnp.exp(bf[r0 + SUB - 1:r0 + SUB]) + d_st
        st_scr[h] = st
        o = jnp.concatenate(o_parts, axis=0)
        o = o * lax.rsqrt(jnp.mean(o * o, axis=-1, keepdims=True) + RMS_EPS)
        o_heads.append(o * gs_ref[:, hs])

    p_cur = p_ref[...]
    pe_scr[SUB:SUB + n, :] = p_cur
    pe = pe_scr[...]
    ob = []
    for gi, w in enumerate(POOL_WINDOWS):
        pooled = _pool_group(pe, p_cur, gi, w)
        sl = slice(gi * POOL_GROUP_DIM, (gi + 1) * POOL_GROUP_DIM)
        ob.append(_bdot(pooled.astype(BF16), wpool_ref[gi]) * pscale_ref[:, sl])
    pe_scr[0:SUB, :] = p_cur[n - SUB:, :]

    mix_ref[...] = jnp.concatenate(o_heads + ob, axis=1).astype(BF16)

    @pl.when(i == nblk - 1)
    def _():
        for h in range(HGRN_HEADS):
            sfin_ref[0, h] = st_scr[h].T


def _mixer(q, k, g, v, gs, p, s0, pp0, wpool_bf, pscale, nseq, seqlen):
    nblk = seqlen // MIX_BLOCK
    tok = lambda b, i: (b * nblk + i, 0)
    tspec = pl.BlockSpec((MIX_BLOCK, HGRN_WIDTH), tok)
    return pl.pallas_call(
        _mixer_kernel,
        out_shape=[jax.ShapeDtypeStruct((nseq * seqlen, D_MODEL), BF16),
                   jax.ShapeDtypeStruct((nseq, HGRN_HEADS, HEAD_DIM, HEAD_DIM), F32)],
        grid=(nseq, nblk),
        in_specs=[tspec] * 6 + [
            pl.BlockSpec((HGRN_HEADS, HEAD_DIM, HEAD_DIM), lambda b, i: (0, 0, 0)),
            pl.BlockSpec((SUB, POOL_WIDTH), lambda b, i: (0, 0)),
            pl.BlockSpec((len(POOL_WINDOWS), POOL_GROUP_DIM, POOL_GROUP_DIM), lambda b, i: (0, 0, 0)),
            pl.BlockSpec((1, POOL_WIDTH), lambda b, i: (0, 0)),
        ],
        out_specs=[pl.BlockSpec((MIX_BLOCK, D_MODEL), tok),
                   pl.BlockSpec((1, HGRN_HEADS, HEAD_DIM, HEAD_DIM), lambda b, i: (b, 0, 0, 0))],
        scratch_shapes=[pltpu.VMEM((HGRN_HEADS, HEAD_DIM, HEAD_DIM), F32),
                        pltpu.VMEM((SUB + MIX_BLOCK, POOL_WIDTH), F32)],
        compiler_params=pltpu.CompilerParams(dimension_semantics=("arbitrary", "arbitrary"),
                                             vmem_limit_bytes=VMEM_LIMIT),
        name="mixer",
    )(q, k, g, v, gs, p, s0, pp0, wpool_bf, pscale)


STEP_SEQS = 32


def _mixer_step_kernel(qt_ref, kt_ref, gt_ref, v_ref, gs_ref, s_ref, snew_ref, oa_ref):
    qt = qt_ref[0, 0]
    kt = kt_ref[0, 0]
    dt = jnp.exp(gt_ref[0, 0])
    rows = []
    for bb in range(STEP_SEQS):
        sn = s_ref[bb, 0] * dt[:, bb:bb + 1] + kt[:, bb:bb + 1] * v_ref[bb:bb + 1, :]
        snew_ref[bb, 0] = sn
        rows.append(jnp.sum(sn * qt[:, bb:bb + 1], axis=0, keepdims=True))
    o = jnp.concatenate(rows, axis=0)
    o = o * lax.rsqrt(jnp.mean(o * o, axis=-1, keepdims=True) + RMS_EPS)
    oa_ref[...] = o * gs_ref[...]


def _mixer_step(qT, kT, gT, v, gs, state):
    nseq = v.shape[0]
    nbc = nseq // STEP_SEQS
    cspec = pl.BlockSpec((1, 1, HEAD_DIM, STEP_SEQS), lambda h, c: (h, c, 0, 0))
    rspec = pl.BlockSpec((STEP_SEQS, HEAD_DIM), lambda h, c: (c, h))
    sspec = pl.BlockSpec((STEP_SEQS, 1, HEAD_DIM, HEAD_DIM), lambda h, c: (c, h, 0, 0))
    return pl.pallas_call(
        _mixer_step_kernel,
        out_shape=[jax.ShapeDtypeStruct(state.shape, F32), jax.ShapeDtypeStruct((nseq, HGRN_WIDTH), F32)],
        grid=(HGRN_HEADS, nbc),
        in_specs=[cspec, cspec, cspec, rspec, rspec, sspec],
        out_specs=[sspec, rspec],
        compiler_params=pltpu.CompilerParams(dimension_semantics=("arbitrary", "arbitrary"),
                                             vmem_limit_bytes=VMEM_LIMIT),
        name="mixer_step",
    )(qT, kT, gT, v, gs, state)


def _pool_step_kernel(sp_ref, p_ref, wpool_ref, pscale_ref, ob_ref):
    p_cur = p_ref[...]
    outs = []
    for gi, w in enumerate(POOL_WINDOWS):
        sl = slice(gi * POOL_GROUP_DIM, (gi + 1) * POOL_GROUP_DIM)
        s = p_cur[:, sl]
        for r in range(POOL_BUF - (w - 1), POOL_BUF):
            s = s + sp_ref[r][:, sl]
        pooled = s * (1.0 / w) - p_cur[:, sl]
        outs.append(_bdot(pooled.astype(BF16), wpool_ref[gi]) * pscale_ref[:, sl])
    ob_ref[...] = jnp.concatenate(outs, axis=1)


def _pool_step(spT, p, wpool_bf, pscale):
    nseq = p.shape[0]
    return pl.pallas_call(
        _pool_step_kernel,
        out_shape=jax.ShapeDtypeStruct((nseq, POOL_WIDTH), F32),
        name="pool_step",
    )(spT, p, wpool_bf, pscale)


def _outproj_kernel(mixp_ref, xnp_ref, mixd_ref, xnd_ref, wout_ref, g1_ref, b1_ref, wr_ref, br_ref,
                    x1_ref, x1p_ref, idx_ref, gate_ref, cnt_ref, *, n_prompt_blocks, n_valid_last):
    tm = mixp_ref.shape[0]
    i = pl.program_id(0)
    is_prompt = i < n_prompt_blocks
    mix_in = jnp.where(is_prompt, mixp_ref[...], mixd_ref[...])
    xn = jnp.where(is_prompt, xnp_ref[...], xnd_ref[...])
    mix = _bdot(mix_in, wout_ref[...])
    x1 = _ln(ALPHA * xn + mix, g1_ref[...], b1_ref[...])
    x1_ref[...] = x1
    xb = x1.astype(BF16)
    words = _pack_bf16_pairs(x1)
    for c in range(PACK_ROWS):
        x1p_ref[pl.ds(c, tm, stride=PACK_ROWS), :] = words[:, c * 128:(c + 1) * 128]

    scores = _sigmoid(_bdot(xb, wr_ref[...]))
    sel = scores + br_ref[...]
    lane = lax.broadcasted_iota(I32, (tm, N_EXPERTS), 1).astype(F32)
    lane_o = lax.broadcasted_iota(I32, (tm, 128), 1)
    idx_o = jnp.zeros((tm, 128), F32)
    ssum = jnp.zeros((tm, 1), F32)
    chosen = jnp.zeros((tm, N_EXPERTS), F32)
    s_sel = []
    for j in range(TOP_K):
        m = jnp.max(sel, axis=-1, keepdims=True)
        am = jnp.min(jnp.where(sel == m, lane, float(N_EXPERTS)), axis=-1, keepdims=True)
        hit = lane == am
        sj = jnp.sum(jnp.where(hit, scores, 0.0), axis=-1, keepdims=True)
        sel = jnp.where(hit, -jnp.inf, sel)
        chosen = jnp.where(hit, 1.0, chosen)
        idx_o = jnp.where(lane_o == j, am, idx_o)
        s_sel.append(sj)
        ssum = ssum + sj
    idx_ref[...] = idx_o.astype(I32)
    gates = jnp.zeros((tm, 128), F32)
    for j in range(TOP_K):
        gates = jnp.where(lane_o == j, s_sel[j] / ssum * ROUTED_SCALE, gates)
    gate_ref[...] = gates

    @pl.when(i == 0)
    def _():
        cnt_ref[...] = jnp.zeros_like(cnt_ref)

    row_id = lax.broadcasted_iota(I32, (tm, N_EXPERTS), 0)
    valid = jnp.logical_or(is_prompt, row_id < n_valid_last)
    cnt_ref[...] += jnp.sum(jnp.where(valid, chosen, 0.0), axis=0, keepdims=True)


def _outproj(mix_p, xn_p, mix_d, xn_d, wout_bf, g1, b1, wr_bf, br):
    tm = OUT_TILE
    n_prompt, n_dec = mix_p.shape[0], mix_d.shape[0]
    assert n_prompt % tm == 0 and n_dec <= tm and n_dec % 8 == 0
    nbp = n_prompt // tm
    T = (nbp + 1) * tm
    padrows = lambda a: jnp.pad(a, ((0, tm - n_dec), (0, 0)))
    row = lambda i: (i, 0)
    prow = lambda i: (jnp.minimum(i, nbp - 1), 0)
    const = lambda i: (0, 0)
    return pl.pallas_call(
        functools.partial(_outproj_kernel, n_prompt_blocks=nbp, n_valid_last=n_dec),
        out_shape=[jax.ShapeDtypeStruct((T, D_MODEL), F32), jax.ShapeDtypeStruct((T * PACK_ROWS, 128), I32),
                   jax.ShapeDtypeStruct((T, 128), I32), jax.ShapeDtypeStruct((T, 128), F32),
                   jax.ShapeDtypeStruct((1, N_EXPERTS), F32)],
        grid=(nbp + 1,),
        in_specs=[pl.BlockSpec((tm, D_MODEL), prow), pl.BlockSpec((tm, D_MODEL), prow),
                  pl.BlockSpec((tm, D_MODEL), const), pl.BlockSpec((tm, D_MODEL), const),
                  pl.BlockSpec((D_MODEL, D_MODEL), const), pl.BlockSpec((1, D_MODEL), const),
                  pl.BlockSpec((1, D_MODEL), const), pl.BlockSpec((D_MODEL, N_EXPERTS), const),
                  pl.BlockSpec((1, N_EXPERTS), const)],
        out_specs=[pl.BlockSpec((tm, D_MODEL), row), pl.BlockSpec((tm * PACK_ROWS, 128), row),
                   pl.BlockSpec((tm, 128), row), pl.BlockSpec((tm, 128), row),
                   pl.BlockSpec((1, N_EXPERTS), const)],
        compiler_params=pltpu.CompilerParams(dimension_semantics=("arbitrary",), vmem_limit_bytes=VMEM_LIMIT),
        name="outproj",
    )(mix_p, xn_p, padrows(mix_d), padrows(xn_d), wout_bf, g1, b1, wr_bf, br)


def _moe_kernel(eb0_ref, enb_ref, wsel_ref, tot_ref, bn_ref, bsrc_ref, bdel_ref,
                order_ref, tokrow_ref, xp_ref, wg_ref, wu_ref, wd_ref, yt_ref,
                tile_scr, ybuf, wgb, wub, wdb, ord_smem, tok_smem, sem, sem_o, sem_t):
    del wsel_ref
    e = pl.program_id(0)
    g0 = eb0_ref[e]
    nblk = enb_ref[e]
    total = tot_ref[0]
    nb_max = bn_ref.shape[0]
    R = MOE_ROWS
    C = PACK_ROWS
    L = R + ORDER_ALIGN
    n_assign = yt_ref.shape[0] // C - 2 * TOP_K

    class _Fetch:
        def __init__(self, g):
            gc = jnp.minimum(g, nb_max - 1)
            s = g % 4
            src = pl.multiple_of(bsrc_ref[gc], ORDER_ALIGN)
            dst = pl.multiple_of(s * L, ORDER_ALIGN)
            self.copies = (
                pltpu.make_async_copy(order_ref.at[pl.ds(src, L)], ord_smem.at[pl.ds(dst, L)], sem_o.at[s]),
                pltpu.make_async_copy(tokrow_ref.at[pl.ds(src, L)], tok_smem.at[pl.ds(dst, L)], sem_t.at[s]))

        def start(self):
            for c in self.copies:
                c.start()

        def wait(self):
            for c in self.copies:
                c.wait()

    fetch = _Fetch

    def list_base(g):
        return (g % 4) * L + bdel_ref[jnp.minimum(g, nb_max - 1)]

    def gather(g):
        base = list_base(g)
        trow = (g % 2) * (PACK_ROWS * TILE_STRIDE)
        for r in range(R):
            t4 = pl.multiple_of(tok_smem[base + r], PACK_ROWS)
            tile_scr[pl.ds(trow + r, PACK_ROWS, stride=TILE_STRIDE), :] = xp_ref[pl.ds(t4, PACK_ROWS), :]

    def rows_sent(cnt):
        return pl.multiple_of(lax.shift_left(lax.shift_right_logical(cnt + 7, 3), 3), 8)

    def wait_rows(s, cnt):
        pltpu.make_async_copy(ybuf.at[s, pl.ds(0, cnt * C), :], yt_ref.at[pl.ds(0, cnt * C), :], sem.at[s]).wait()

    @pl.when(e == 0)
    def _():
        n_spare = 2 * TOP_K * C
        ybuf[0, 0:n_spare, :] = jnp.zeros((n_spare, 128), I32)
        init = pltpu.make_async_copy(ybuf.at[0, pl.ds(0, n_spare), :],
                                     yt_ref.at[pl.ds(yt_ref.shape[0] - n_spare, n_spare), :], sem.at[0])
        init.start()
        init.wait()
        fetch(0).start()
        fetch(1).start()
        fetch(0).wait()
        gather(0)

    @pl.when(nblk > 0)
    def _():
        wgb[...] = wg_ref[0].astype(BF16)
        wub[...] = wu_ref[0].astype(BF16)
        wdb[...] = wd_ref[0].astype(BF16)

    def block(k, carry):
        g = g0 + k
        slot = g % 2
        n = bn_ref[g]
        fetch(g + 1).wait()
        fetch(g + 2).start()

        @pl.when(g >= 2)
        def _():
            wait_rows(slot, rows_sent(bn_ref[jnp.maximum(g - 2, 0)]))

        trow = pl.multiple_of(slot * (PACK_ROWS * TILE_STRIDE), 8)

        def compute(send_prev):
            los, his = [], []
            for j in range(PACK_ROWS):
                lo, hi = _unpack_bf16_pairs(tile_scr[pl.ds(trow + j * TILE_STRIDE, R), :])
                los.append(lo.astype(BF16))
                his.append(hi.astype(BF16))
            xg = jnp.concatenate(los + his, axis=1)
            gather(g + 1)
            if send_prev:
                pbase = list_base(g - 1)
                for r in range(R):
                    send(1 - slot, r, ord_smem[pbase + r], r)
            hg = _bdot(xg, wgb[...])
            hu = _bdot(xg, wub[...])
            hb = (hg * _sigmoid(hg)) * hu
            y = _bdot(hb.astype(BF16), wdb[...])
            words = _pack_bf16_pairs(y)
            yb = ybuf.at[slot]
            for c in range(C):
                yb[pl.ds(c, R, stride=C), :] = words[:, c * 128:(c + 1) * 128]

        prev_full = jnp.logical_and(g >= 1, bn_ref[jnp.maximum(g - 1, 0)] == R)

        @pl.when(prev_full)
        def _():
            compute(True)

        @pl.when(jnp.logical_not(prev_full))
        def _():
            compute(False)

        @pl.when(n < R)
        def _():
            send_rows(g)

        return carry

    def send(s, r, d, u):
        pltpu.make_async_copy(ybuf.at[s, pl.ds(pl.multiple_of(r * C, C), C), :],
                              yt_ref.at[pl.ds(pl.multiple_of(d * C, C), C), :], sem.at[s]).start(priority=u % 2)

    def send_rows(g):
        slot = g % 2
        n = bn_ref[g]
        base = list_base(g)

        def send_group(i, c2):
            for u in range(8):
                send(slot, i * 8 + u, ord_smem[base + i * 8 + u], u)
            return c2

        n_full = lax.shift_right_logical(n, 3)
        lax.fori_loop(0, n_full, send_group, 0)

        @pl.when(n_full * 8 < n)
        def _():
            for u in range(8):
                r = n_full * 8 + u
                send(slot, r, jnp.where(r < n, ord_smem[base + r], n_assign + slot * 8 + u), u)

    lax.fori_loop(0, nblk, block, 0)

    @pl.when(e == pl.num_programs(0) - 1)
    def _():
        @pl.when(bn_ref[jnp.maximum(total - 1, 0)] == R)
        def _():
            send_rows(total - 1)

        fetch(total + 1).wait()
        wait_rows((total - 1) % 2, rows_sent(bn_ref[jnp.maximum(total - 1, 0)]))

        @pl.when(total >= 2)
        def _():
            wait_rows(total % 2, rows_sent(bn_ref[jnp.maximum(total - 2, 0)]))


def _moe(ex_b0, ex_nb, ex_w, total, blk_n, blk_src, blk_delta, order, tokrow, xp, wg, wu, wd, n_tok):
    R = MOE_ROWS
    wmap_in = lambda e, eb0, enb, wsel, tot, bn, bs, bd: (wsel[e], 0, 0)
    grid_spec = pltpu.PrefetchScalarGridSpec(
        num_scalar_prefetch=7,
        grid=(N_EXPERTS,),
        in_specs=[
            pl.BlockSpec(memory_space=pl.ANY),
            pl.BlockSpec(memory_space=pl.ANY),
            pl.BlockSpec(memory_space=pltpu.VMEM),
            pl.BlockSpec((1, D_MODEL, D_EXPERT), wmap_in),
            pl.BlockSpec((1, D_MODEL, D_EXPERT), wmap_in),
            pl.BlockSpec((1, D_EXPERT, D_MODEL), wmap_in),
        ],
        out_specs=pl.BlockSpec(memory_space=pl.ANY),
        scratch_shapes=[pltpu.VMEM((2 * PACK_ROWS * TILE_STRIDE, 128), I32),
                        pltpu.VMEM((2, R * PACK_ROWS, 128), I32),
                        pltpu.VMEM((D_MODEL, D_EXPERT), BF16),
                        pltpu.VMEM((D_MODEL, D_EXPERT), BF16),
                        pltpu.VMEM((D_EXPERT, D_MODEL), BF16),
                        pltpu.SMEM((4 * (R + ORDER_ALIGN),), I32),
                        pltpu.SMEM((4 * (R + ORDER_ALIGN),), I32),
                        pltpu.SemaphoreType.DMA((2,)),
                        pltpu.SemaphoreType.DMA((4,)),
                        pltpu.SemaphoreType.DMA((4,))],
    )
    return pl.pallas_call(
        _moe_kernel,
        out_shape=jax.ShapeDtypeStruct(((n_tok + 2) * TOP_K * PACK_ROWS, 128), I32),
        grid_spec=grid_spec,
        compiler_params=pltpu.CompilerParams(dimension_semantics=("arbitrary",),
                                             vmem_limit_bytes=58 * 1024 * 1024),
        name="moe",
    )(ex_b0, ex_nb, ex_w, total, blk_n, blk_src, blk_delta, order, tokrow, xp, wg, wu, wd)


def _route_plan(idx, counts, n_tok):
    R = MOE_ROWS
    n_assign = n_tok * TOP_K
    nb = (n_assign + N_EXPERTS * (R - 1)) // R
    id_bits = (n_assign - 1).bit_length()
    assert id_bits + (N_EXPERTS - 1).bit_length() < 32
    key = lax.shift_left(idx.reshape(-1), id_bits) | jnp.arange(n_assign, dtype=I32)
    order = lax.sort(key) & ((1 << id_bits) - 1)
    order = jnp.concatenate([order, jnp.zeros((R + ORDER_ALIGN,), I32)])
    nblk_e = (counts + R - 1) // R
    bend = jnp.cumsum(nblk_e)
    bstart = bend - nblk_e
    cstart = jnp.cumsum(counts) - counts
    blk = jnp.arange(nb, dtype=I32)
    blk_e = jnp.minimum(jnp.sum((bend[None, :] <= blk[:, None]).astype(I32), axis=1), N_EXPERTS - 1)
    k = blk - bstart[blk_e]
    active = blk < bend[-1]
    blk_n = jnp.where(active, jnp.clip(counts[blk_e] - k * R, 0, R), 0).astype(I32)
    src = jnp.where(active, cstart[blk_e] + k * R, 0).astype(I32)
    blk_src = (src // ORDER_ALIGN) * ORDER_ALIGN
    ex = jnp.arange(N_EXPERTS, dtype=I32)
    ex_w = jnp.maximum(lax.cummax(jnp.where(nblk_e > 0, ex, -1)), 0).astype(I32)
    tokrow = (order // TOP_K) * PACK_ROWS
    return (bstart.astype(I32), nblk_e.astype(I32), ex_w, bend[-1:].astype(I32),
            blk_n, blk_src, src - blk_src, order, tokrow)


def _combine_kernel(x1_ref, yt_ref, gate_ref, wgs_ref, wus_ref, wds_ref, g2_ref, b2_ref, out_ref):
    tm = x1_ref.shape[0]
    x1 = x1_ref[...]
    xb = x1.astype(BF16)
    hg = _bdot(xb, wgs_ref[...])
    hs = (hg * _sigmoid(hg)) * _bdot(xb, wus_ref[...])
    moe = _bdot(hs.astype(BF16), wds_ref[...])
    gate = gate_ref[...]
    per_tok = TOP_K * PACK_ROWS
    planes = pltpu.einshape("tjl->jtl", yt_ref[...].reshape(tm, per_tok, 128))
    lo_acc = [None] * PACK_ROWS
    hi_acc = [None] * PACK_ROWS
    for j in range(TOP_K):
        gj = gate[:, j:j + 1]
        for c in range(PACK_ROWS):
            lo, hi = _unpack_bf16_pairs(planes[j * PACK_ROWS + c])
            lo_acc[c] = lo * gj if j == 0 else lo_acc[c] + lo * gj
            hi_acc[c] = hi * gj if j == 0 else hi_acc[c] + hi * gj
    routed = jnp.concatenate(lo_acc + hi_acc, axis=1)
    out_ref[...] = _ln(ALPHA * x1 + (moe + routed), g2_ref[...], b2_ref[...])


def _combine(x1, yt, gate, wgs_bf, wus_bf, wds_bf, g2, b2, row0, nrows):
    tm = 128
    assert nrows % tm == 0 and row0 % tm == 0
    off = row0 // tm
    row = lambda i: (i + off, 0)
    const = lambda i: (0, 0)
    return pl.pallas_call(
        _combine_kernel,
        out_shape=jax.ShapeDtypeStruct((nrows, D_MODEL), F32),
        grid=(nrows // tm,),
        in_specs=[pl.BlockSpec((tm, D_MODEL), row),
                  pl.BlockSpec((tm * TOP_K * PACK_ROWS, 128), row),
                  pl.BlockSpec((tm, 128), row),
                  pl.BlockSpec((D_MODEL, D_EXPERT), const), pl.BlockSpec((D_MODEL, D_EXPERT), const),
                  pl.BlockSpec((D_EXPERT, D_MODEL), const),
                  pl.BlockSpec((1, D_MODEL), const), pl.BlockSpec((1, D_MODEL), const)],
        out_specs=pl.BlockSpec((tm, D_MODEL), lambda i: (i, 0)),
        compiler_params=pltpu.CompilerParams(dimension_semantics=("arbitrary",), vmem_limit_bytes=VMEM_LIMIT),
        name="combine",
    )(x1, yt, gate, wgs_bf, wus_bf, wds_bf, g2, b2)


def kernel(x_prompt, x_sample, state_hgrn, state_pool, meta_tokens, ln_emb_g, ln_emb_b, w_in, lb_logits, hgrn_norm_g, w_pool, pool_scale, w_out, ln1_g, ln1_b, w_router, b_router, w_gate_e, w_up_e, w_down_e, w_gate_s, w_up_s, w_down_s, ln2_g, ln2_b):
    nseq, seqlen, _ = x_prompt.shape
    ndec = x_sample.shape[0]
    l = 0
    row = lambda a: a.reshape(1, -1)
    w_in_bf = w_in[l].astype(BF16)
    wpool_bf = w_pool[l].astype(BF16)
    lng, lnb = row(ln_emb_g), row(ln_emb_b)
    ng, ps = row(hgrn_norm_g[l]), row(pool_scale[l])
    proj = functools.partial(_ln_proj, ln_g=lng, ln_b=lnb, w_in_bf=w_in_bf, lb_logits=lb_logits, norm_g=ng)

    m_xn, m_q, m_k, m_g, m_v, m_gs, m_p = proj(meta_tokens)
    pad = lambda a: jnp.pad(a, ((0, MIX_BLOCK - N_META), (0, 0)))
    zero_state = jnp.zeros((HGRN_HEADS, HEAD_DIM, HEAD_DIM), F32)
    _, s_meta = _mixer(pad(m_q), pad(m_k), pad(m_g), pad(m_v), pad(m_gs), pad(m_p), zero_state,
                       jnp.zeros((SUB, POOL_WIDTH), F32), wpool_bf, ps, 1, MIX_BLOCK)

    p_xn, p_q, p_k, p_g, p_v, p_gs, p_p = proj(x_prompt.reshape(nseq * seqlen, D_MODEL))
    p_mix, s_prompt = _mixer(p_q, p_k, p_g, p_v, p_gs, p_p, s_meta[0], m_p, wpool_bf, ps, nseq, seqlen)

    d_xn, d_q, d_k, d_g, d_v, d_gs, d_p = proj(x_sample.reshape(ndec, D_MODEL))
    cols = lambda a: a.reshape(ndec // STEP_SEQS, STEP_SEQS, HGRN_HEADS, HEAD_DIM).transpose(2, 0, 3, 1)
    s_dec, d_oa = _mixer_step(cols(d_q), cols(d_k), cols(d_g), d_v, d_gs, state_hgrn[l])
    d_ob = _pool_step(state_pool[l].transpose(1, 0, 2), d_p, wpool_bf, ps)
    d_mix = jnp.concatenate([d_oa, d_ob], axis=1).astype(BF16)

    n_tok = nseq * seqlen + ndec
    x1, xp, idx, gate, cnt = _outproj(p_mix, p_xn, d_mix, d_xn, w_out[l].astype(BF16), row(ln1_g[l]), row(ln1_b[l]),
                                      w_router[l].astype(BF16), row(b_router[l]))
    plan = _route_plan(idx[:n_tok, :TOP_K], cnt[0].astype(I32), n_tok)
    yt = _moe(*plan, xp, w_gate_e[l], w_up_e[l], w_down_e[l], n_tok)
    comb = functools.partial(_combine, x1, yt, gate, w_gate_s[l].astype(BF16), w_up_s[l].astype(BF16),
                             w_down_s[l].astype(BF16), row(ln2_g[l]), row(ln2_b[l]))
    y_prompt = comb(0, nseq * seqlen).reshape(nseq, seqlen, D_MODEL)
    y_sample = comb(nseq * seqlen, ndec).reshape(ndec, 1, D_MODEL)

    state_pool_prompt = p_p.reshape(nseq, seqlen, POOL_WIDTH)[:, seqlen - POOL_BUF:, :]
    state_pool_sample = jnp.concatenate([state_pool[l][:, 1:, :], d_p[:, None, :]], axis=1)
    return (y_prompt, y_sample, s_prompt[None], state_pool_prompt[None], s_dec[None], state_pool_sample[None])
```

```python
import functools

import jax
import jax.numpy as jnp
from jax import lax
from jax.experimental import pallas as pl
from jax.experimental.pallas import tpu as pltpu

F32 = jnp.float32
BF16 = jnp.bfloat16
I32 = jnp.int32

D_MODEL = 1024
N_META = 16
HGRN_WIDTH = 512
HGRN_HEADS = 4
HEAD_DIM = 128
POOL_WIDTH = 512
POOL_WINDOWS = (2, 4, 8, 16)
POOL_GROUP_DIM = 128
POOL_BUF = 15
IN_WIDTH = 4 * HGRN_WIDTH + POOL_WIDTH
N_EXPERTS = 256
TOP_K = 8
D_EXPERT = 256
ROUTED_SCALE = 2.5
DEPTH = 1
ALPHA = (2 * DEPTH) ** 0.25
LN_EPS = 1e-5
RMS_EPS = 1e-6
LOG2_E = 1.4426950408889634

SUB = 16
MIX_BLOCK = 128
MOE_ROWS = 256
TILE_STRIDE = MOE_ROWS + 8
ORDER_ALIGN = 128
PACK_ROWS = D_MODEL // 2 // 128
OUT_TILE = 512
VMEM_LIMIT = 48 * 1024 * 1024


def _ln(x, g, b):
    mu = jnp.mean(x, axis=-1, keepdims=True)
    xc = x - mu
    var = jnp.mean(xc * xc, axis=-1, keepdims=True)
    return xc * lax.rsqrt(var + LN_EPS) * g + b


def _sigmoid(z):
    return 1.0 / (1.0 + jnp.exp(-z))


def _bdot(a, b):
    return jnp.dot(a, b, preferred_element_type=F32)


def _pack_bf16_pairs(x):
    half = x.shape[1] // 2
    xr = x.astype(BF16).astype(F32)
    lo = lax.shift_right_logical(lax.bitcast_convert_type(xr[:, :half], jnp.uint32), jnp.uint32(16))
    hi = lax.bitcast_convert_type(xr[:, half:], jnp.uint32) & jnp.uint32(0xFFFF0000)
    return lax.bitcast_convert_type(hi | lo, I32)


def _unpack_bf16_pairs(words):
    w = lax.bitcast_convert_type(words, jnp.uint32)
    lo = lax.bitcast_convert_type(lax.shift_left(w, jnp.uint32(16)), F32)
    hi = lax.bitcast_convert_type(w & jnp.uint32(0xFFFF0000), F32)
    return lo, hi


def _ln_proj_kernel(x_ref, g_ref, b_ref, w_ref, lbl_ref, ng_ref,
                    xn_ref, q_ref, k_ref, gl_ref, v_ref, gs_ref, p_ref):
    xn = _ln(x_ref[...], g_ref[...], b_ref[...])
    xn_ref[...] = xn
    proj = _bdot(xn.astype(BF16), w_ref[...])
    lbl = lbl_ref[...]
    e = jnp.exp(lbl - jnp.max(lbl, axis=0, keepdims=True))
    lb = e[0:1] / jnp.sum(e, axis=0, keepdims=True)
    W = HGRN_WIDTH
    q = proj[:, 0:W]
    f = proj[:, W:2 * W]
    q_ref[...] = q * _sigmoid(q)
    k_ref[...] = (1.0 - lb) * _sigmoid(-f)
    gl_ref[...] = jnp.log(lb + (1.0 - lb) * _sigmoid(f))
    v_ref[...] = proj[:, 2 * W:3 * W]
    g = proj[:, 3 * W:4 * W]
    gs_ref[...] = ng_ref[...] * (g * _sigmoid(g))
    p_ref[...] = proj[:, 4 * W:]


def _row_tile(n_rows):
    for tm in (256, 128, 64, 32, 16, 8):
        if n_rows % tm == 0:
            return tm
    raise ValueError(f"row count {n_rows} is not a multiple of 8")


def _ln_proj(x, ln_g, ln_b, w_in_bf, lb_logits, norm_g):
    T = x.shape[0]
    tm = _row_tile(T)
    row = lambda i: (i, 0)
    const = lambda i: (0, 0)
    outs = [jax.ShapeDtypeStruct((T, D_MODEL), F32)] + [jax.ShapeDtypeStruct((T, HGRN_WIDTH), F32)] * 6
    return pl.pallas_call(
        _ln_proj_kernel,
        out_shape=outs,
        grid=(T // tm,),
        in_specs=[
            pl.BlockSpec((tm, D_MODEL), row),
            pl.BlockSpec((1, D_MODEL), const),
            pl.BlockSpec((1, D_MODEL), const),
            pl.BlockSpec((D_MODEL, IN_WIDTH), const),
            pl.BlockSpec((DEPTH + 1, HGRN_WIDTH), const),
            pl.BlockSpec((1, HGRN_WIDTH), const),
        ],
        out_specs=[pl.BlockSpec((tm, D_MODEL), row)] + [pl.BlockSpec((tm, HGRN_WIDTH), row)] * 6,
        compiler_params=pltpu.CompilerParams(dimension_semantics=("arbitrary",), vmem_limit_bytes=VMEM_LIMIT),
        name="ln_proj",
    )(x, ln_g, ln_b, w_in_bf, lb_logits, norm_g)


def _pool_group(pe, p_cur, gi, w):
    sl = slice(gi * POOL_GROUP_DIM, (gi + 1) * POOL_GROUP_DIM)
    s = pe[:, sl]
    sh = 1
    while sh < w:
        s = s + pltpu.roll(s, sh, 0)
        sh *= 2
    return s[SUB:, :] * (1.0 / w) - p_cur[:, sl]


def _mixer_kernel(q_ref, k_ref, g_ref, v_ref, gs_ref, p_ref, s0_ref, pp0_ref, wpool_ref, pscale_ref,
                  mix_ref, sfin_ref, st_scr, pe_scr, bf_scr):
    i = pl.program_id(1)
    nblk = pl.num_programs(1)
    n = MIX_BLOCK

    @pl.when(i == 0)
    def _():
        for h in range(HGRN_HEADS):
            st_scr[h] = s0_ref[h].T
        pe_scr[0:SUB, :] = pp0_ref[...]

    rows = lax.broadcasted_iota(I32, (n, HEAD_DIM), 0)
    r16 = rows & (SUB - 1)
    t8 = lax.broadcasted_iota(I32, (8, HEAD_DIM), 0)
    zero_bf = jnp.zeros((SUB, HEAD_DIM), BF16)

    o_heads = []
    for h in range(HGRN_HEADS):
        hs = slice(h * HEAD_DIM, (h + 1) * HEAD_DIM)
        Q = q_ref[:, hs]
        K = k_ref[:, hs]
        G = g_ref[:, hs]
        V = v_ref[:, hs]
        bf = G
        br = G
        for sh in (1, 2, 4, 8):
            bf = bf + jnp.where(r16 >= sh, pltpu.roll(bf, sh, 0), 0.0)
            br = br + jnp.where(r16 < SUB - sh, pltpu.roll(br, n - sh, 0), 0.0)
        br = br - G
        bf2 = bf * LOG2_E
        bf_scr[:, hs] = bf2
        qt = (Q * jnp.exp(bf)).astype(BF16)
        kt = (K * jnp.exp(br)).astype(BF16)
        vt = V.T.astype(BF16)
        st = st_scr[h]
        o_parts = []
        for c in range(n // SUB):
            r0 = c * SUB
            b_top, b_bot = bf2[r0:r0 + 8], bf2[r0 + 8:r0 + 16]
            q_top, q_bot = Q[r0:r0 + 8], Q[r0 + 8:r0 + 16]
            acc_top = jnp.zeros((8, HEAD_DIM), F32)
            acc_bot = jnp.zeros((8, HEAD_DIM), F32)
            for s in range(SUB):
                bs = bf_scr[r0 + s:r0 + s + 1, hs]
                ks = k_ref[r0 + s:r0 + s + 1, hs]
                vs = v_ref[r0 + s:r0 + s + 1, hs]
                if s < 8:
                    col = jnp.sum(q_top * jnp.exp2(b_top - bs) * ks, axis=-1, keepdims=True)
                    col = jnp.where(t8[:, 0:1] >= s, col, 0.0)
                    acc_top = acc_top + col * vs
                    col = jnp.sum(q_bot * jnp.exp2(b_bot - bs) * ks, axis=-1, keepdims=True)
                    acc_bot = acc_bot + col * vs
                else:
                    col = jnp.sum(q_bot * jnp.exp2(b_bot - bs) * ks, axis=-1, keepdims=True)
                    col = jnp.where(t8[:, 0:1] + 8 >= s, col, 0.0)
                    acc_bot = acc_bot + col * vs
            o_diag = jnp.concatenate([acc_top, acc_bot], axis=0)
            o_inter = lax.dot_general(qt[r0:r0 + SUB], st.astype(BF16), (((1,), (1,)), ((), ())),
                                      preferred_element_type=F32)
            o_parts.append(o_inter + o_diag)
            kmask = jnp.concatenate([zero_bf] * c + [kt[r0:r0 + SUB]] + [zero_bf] * (n // SUB - 1 - c), axis=0)
            d_st = _bdot(vt, kmask)
            st = st * jnp.exp(bf[r0 + SUB - 1:r0 + SUB]) + d_st
        st_scr[h] = st
        o = jnp.concatenate(o_parts, axis=0)
        o = o * lax.rsqrt(jnp.mean(o * o, axis=-1, keepdims=True) + RMS_EPS)
        o_heads.append(o * gs_ref[:, hs])

    p_cur = p_ref[...]
    pe_scr[SUB:SUB + n, :] = p_cur
    pe = pe_scr[...]
    ob = []
    for gi, w in enumerate(POOL_WINDOWS):
        pooled = _pool_group(pe, p_cur, gi, w)
        sl = slice(gi * POOL_GROUP_DIM, (gi + 1) * POOL_GROUP_DIM)
        ob.append(_bdot(pooled.astype(BF16), wpool_ref[gi]) * pscale_ref[:, sl])
    pe_scr[0:SUB, :] = p_cur[n - SUB:, :]

    mix_ref[...] = jnp.concatenate(o_heads + ob, axis=1).astype(BF16)

    @pl.when(i == nblk - 1)
    def _():
        for h in range(HGRN_HEADS):
            sfin_ref[0, h] = st_scr[h].T


def _mixer(q, k, g, v, gs, p, s0, pp0, wpool_bf, pscale, nseq, seqlen):
    nblk = seqlen // MIX_BLOCK
    tok = lambda b, i: (b * nblk + i, 0)
    tspec = pl.BlockSpec((MIX_BLOCK, HGRN_WIDTH), tok)
    return pl.pallas_call(
        _mixer_kernel,
        out_shape=[jax.ShapeDtypeStruct((nseq * seqlen, D_MODEL), BF16),
                   jax.ShapeDtypeStruct((nseq, HGRN_HEADS, HEAD_DIM, HEAD_DIM), F32)],
        grid=(nseq, nblk),
        in_specs=[tspec] * 6 + [
            pl.BlockSpec((HGRN_HEADS, HEAD_DIM, HEAD_DIM), lambda b, i: (0, 0, 0)),
            pl.BlockSpec((SUB, POOL_WIDTH), lambda b, i: (0, 0)),
            pl.BlockSpec((len(POOL_WINDOWS), POOL_GROUP_DIM, POOL_GROUP_DIM), lambda b, i: (0, 0, 0)),
            pl.BlockSpec((1, POOL_WIDTH), lambda b, i: (0, 0)),
        ],
        out_specs=[pl.BlockSpec((MIX_BLOCK, D_MODEL), tok),
                   pl.BlockSpec((1, HGRN_HEADS, HEAD_DIM, HEAD_DIM), lambda b, i: (b, 0, 0, 0))],
        scratch_shapes=[pltpu.VMEM((HGRN_HEADS, HEAD_DIM, HEAD_DIM), F32),
                        pltpu.VMEM((SUB + MIX_BLOCK, POOL_WIDTH), F32),
                        pltpu.VMEM((MIX_BLOCK, HGRN_WIDTH), F32)],
        compiler_params=pltpu.CompilerParams(dimension_semantics=("arbitrary", "arbitrary"),
                                             vmem_limit_bytes=VMEM_LIMIT),
        name="mixer",
    )(q, k, g, v, gs, p, s0, pp0, wpool_bf, pscale)


STEP_SEQS = 32


def _mixer_step_kernel(qt_ref, kt_ref, gt_ref, v_ref, gs_ref, s_ref, snew_ref, oa_ref):
    qt = qt_ref[0, 0]
    kt = kt_ref[0, 0]
    dt = jnp.exp(gt_ref[0, 0])
    rows = []
    for bb in range(STEP_SEQS):
        sn = s_ref[bb, 0] * dt[:, bb:bb + 1] + kt[:, bb:bb + 1] * v_ref[bb:bb + 1, :]
        snew_ref[bb, 0] = sn
        rows.append(jnp.sum(sn * qt[:, bb:bb + 1], axis=0, keepdims=True))
    o = jnp.concatenate(rows, axis=0)
    o = o * lax.rsqrt(jnp.mean(o * o, axis=-1, keepdims=True) + RMS_EPS)
    oa_ref[...] = o * gs_ref[...]


def _mixer_step(qT, kT, gT, v, gs, state):
    nseq = v.shape[0]
    nbc = nseq // STEP_SEQS
    cspec = pl.BlockSpec((1, 1, HEAD_DIM, STEP_SEQS), lambda h, c: (h, c, 0, 0))
    rspec = pl.BlockSpec((STEP_SEQS, HEAD_DIM), lambda h, c: (c, h))
    sspec = pl.BlockSpec((STEP_SEQS, 1, HEAD_DIM, HEAD_DIM), lambda h, c: (c, h, 0, 0))
    return pl.pallas_call(
        _mixer_step_kernel,
        out_shape=[jax.ShapeDtypeStruct(state.shape, F32), jax.ShapeDtypeStruct((nseq, HGRN_WIDTH), F32)],
        grid=(HGRN_HEADS, nbc),
        in_specs=[cspec, cspec, cspec, rspec, rspec, sspec],
        out_specs=[sspec, rspec],
        compiler_params=pltpu.CompilerParams(dimension_semantics=("arbitrary", "arbitrary"),
                                             vmem_limit_bytes=VMEM_LIMIT),
        name="mixer_step",
    )(qT, kT, gT, v, gs, state)


def _pool_step_kernel(sp_ref, p_ref, wpool_ref, pscale_ref, ob_ref):
    p_cur = p_ref[...]
    outs = []
    for gi, w in enumerate(POOL_WINDOWS):
        sl = slice(gi * POOL_GROUP_DIM, (gi + 1) * POOL_GROUP_DIM)
        s = p_cur[:, sl]
        for r in range(POOL_BUF - (w - 1), POOL_BUF):
            s = s + sp_ref[r][:, sl]
        pooled = s * (1.0 / w) - p_cur[:, sl]
        outs.append(_bdot(pooled.astype(BF16), wpool_ref[gi]) * pscale_ref[:, sl])
    ob_ref[...] = jnp.concatenate(outs, axis=1)


def _pool_step(spT, p, wpool_bf, pscale):
    nseq = p.shape[0]
    return pl.pallas_call(
        _pool_step_kernel,
        out_shape=jax.ShapeDtypeStruct((nseq, POOL_WIDTH), F32),
        name="pool_step",
    )(spT, p, wpool_bf, pscale)


def _outproj_kernel(mixp_ref, xnp_ref, mixd_ref, xnd_ref, wout_ref, g1_ref, b1_ref, wr_ref, br_ref,
                    x1_ref, x1p_ref, idx_ref, gate_ref, cnt_ref, *, n_prompt_blocks, n_valid_last):
    tm = mixp_ref.shape[0]
    i = pl.program_id(0)
    is_prompt = i < n_prompt_blocks
    mix_in = jnp.where(is_prompt, mixp_ref[...], mixd_ref[...])
    xn = jnp.where(is_prompt, xnp_ref[...], xnd_ref[...])
    mix = _bdot(mix_in, wout_ref[...])
    x1 = _ln(ALPHA * xn + mix, g1_ref[...], b1_ref[...])
    x1_ref[...] = x1
    xb = x1.astype(BF16)
    words = _pack_bf16_pairs(x1)
    for c in range(PACK_ROWS):
        x1p_ref[pl.ds(c, tm, stride=PACK_ROWS), :] = words[:, c * 128:(c + 1) * 128]

    scores = _sigmoid(_bdot(xb, wr_ref[...]))
    sel = scores + br_ref[...]
    lane = lax.broadcasted_iota(I32, (tm, N_EXPERTS), 1).astype(F32)
    lane_o = lax.broadcasted_iota(I32, (tm, 128), 1)
    idx_o = jnp.zeros((tm, 128), F32)
    ssum = jnp.zeros((tm, 1), F32)
    chosen = jnp.zeros((tm, N_EXPERTS), F32)
    s_sel = []
    for j in range(TOP_K):
        m = jnp.max(sel, axis=-1, keepdims=True)
        am = jnp.min(jnp.where(sel == m, lane, float(N_EXPERTS)), axis=-1, keepdims=True)
        hit = lane == am
        sj = jnp.sum(jnp.where(hit, scores, 0.0), axis=-1, keepdims=True)
        sel = jnp.where(hit, -jnp.inf, sel)
        chosen = jnp.where(hit, 1.0, chosen)
        idx_o = jnp.where(lane_o == j, am, idx_o)
        s_sel.append(sj)
        ssum = ssum + sj
    idx_ref[...] = idx_o.astype(I32)
    gates = jnp.zeros((tm, 128), F32)
    for j in range(TOP_K):
        gates = jnp.where(lane_o == j, s_sel[j] / ssum * ROUTED_SCALE, gates)
    gate_ref[...] = gates

    @pl.when(i == 0)
    def _():
        cnt_ref[...] = jnp.zeros_like(cnt_ref)

    row_id = lax.broadcasted_iota(I32, (tm, N_EXPERTS), 0)
    valid = jnp.logical_or(is_prompt, row_id < n_valid_last)
    cnt_ref[...] += jnp.sum(jnp.where(valid, chosen, 0.0), axis=0, keepdims=True)


def _outproj(mix_p, xn_p, mix_d, xn_d, wout_bf, g1, b1, wr_bf, br):
    tm = OUT_TILE
    n_prompt, n_dec = mix_p.shape[0], mix_d.shape[0]
    assert n_prompt % tm == 0 and n_dec <= tm and n_dec % 8 == 0
    nbp = n_prompt // tm
    T = (nbp + 1) * tm
    padrows = lambda a: jnp.pad(a, ((0, tm - n_dec), (0, 0)))
    row = lambda i: (i, 0)
    prow = lambda i: (jnp.minimum(i, nbp - 1), 0)
    const = lambda i: (0, 0)
    return pl.pallas_call(
        functools.partial(_outproj_kernel, n_prompt_blocks=nbp, n_valid_last=n_dec),
        out_shape=[jax.ShapeDtypeStruct((T, D_MODEL), F32), jax.ShapeDtypeStruct((T * PACK_ROWS, 128), I32),
                   jax.ShapeDtypeStruct((T, 128), I32), jax.ShapeDtypeStruct((T, 128), F32),
                   jax.ShapeDtypeStruct((1, N_EXPERTS), F32)],
        grid=(nbp + 1,),
        in_specs=[pl.BlockSpec((tm, D_MODEL), prow), pl.BlockSpec((tm, D_MODEL), prow),
                  pl.BlockSpec((tm, D_MODEL), const), pl.BlockSpec((tm, D_MODEL), const),
                  pl.BlockSpec((D_MODEL, D_MODEL), const), pl.BlockSpec((1, D_MODEL), const),
                  pl.BlockSpec((1, D_MODEL), const), pl.BlockSpec((D_MODEL, N_EXPERTS), const),
                  pl.BlockSpec((1, N_EXPERTS), const)],
        out_specs=[pl.BlockSpec((tm, D_MODEL), row), pl.BlockSpec((tm * PACK_ROWS, 128), row),
                   pl.BlockSpec((tm, 128), row), pl.BlockSpec((tm, 128), row),
                   pl.BlockSpec((1, N_EXPERTS), const)],
        compiler_params=pltpu.CompilerParams(dimension_semantics=("arbitrary",), vmem_limit_bytes=VMEM_LIMIT),
        name="outproj",
    )(mix_p, xn_p, padrows(mix_d), padrows(xn_d), wout_bf, g1, b1, wr_bf, br)


def _moe_kernel(eb0_ref, enb_ref, wsel_ref, tot_ref, bn_ref, bsrc_ref, bdel_ref,
                order_ref, tokrow_ref, xp_ref, wg_ref, wu_ref, wd_ref, yt_ref,
                tile_scr, ybuf, wgb, wub, wdb, ord_smem, tok_smem, sem, sem_o, sem_t):
    del wsel_ref
    e = pl.program_id(0)
    g0 = eb0_ref[e]
    nblk = enb_ref[e]
    total = tot_ref[0]
    nb_max = bn_ref.shape[0]
    R = MOE_ROWS
    C = PACK_ROWS
    L = R + ORDER_ALIGN
    n_assign = yt_ref.shape[0] // C - 2 * TOP_K

    class _Fetch:
        def __init__(self, g):
            gc = jnp.minimum(g, nb_max - 1)
            s = g % 4
            src = pl.multiple_of(bsrc_ref[gc], ORDER_ALIGN)
            dst = pl.multiple_of(s * L, ORDER_ALIGN)
            self.copies = (
                pltpu.make_async_copy(order_ref.at[pl.ds(src, L)], ord_smem.at[pl.ds(dst, L)], sem_o.at[s]),
                pltpu.make_async_copy(tokrow_ref.at[pl.ds(src, L)], tok_smem.at[pl.ds(dst, L)], sem_t.at[s]))

        def start(self):
            for c in self.copies:
                c.start()

        def wait(self):
            for c in self.copies:
                c.wait()

    fetch = _Fetch

    def list_base(g):
        return (g % 4) * L + bdel_ref[jnp.minimum(g, nb_max - 1)]

    def gather(g):
        base = list_base(g)
        trow = (g % 2) * (PACK_ROWS * TILE_STRIDE)
        for r in range(R):
            t4 = pl.multiple_of(tok_smem[base + r], PACK_ROWS)
            tile_scr[pl.ds(trow + r, PACK_ROWS, stride=TILE_STRIDE), :] = xp_ref[pl.ds(t4, PACK_ROWS), :]

    def rows_sent(cnt):
        return pl.multiple_of(lax.shift_left(lax.shift_right_logical(cnt + 7, 3), 3), 8)

    def wait_rows(s, cnt):
        pltpu.make_async_copy(ybuf.at[s, pl.ds(0, cnt * C), :], yt_ref.at[pl.ds(0, cnt * C), :], sem.at[s]).wait()

    @pl.when(e == 0)
    def _():
        n_spare = 2 * TOP_K * C
        ybuf[0, 0:n_spare, :] = jnp.zeros((n_spare, 128), I32)
        init = pltpu.make_async_copy(ybuf.at[0, pl.ds(0, n_spare), :],
                                     yt_ref.at[pl.ds(yt_ref.shape[0] - n_spare, n_spare), :], sem.at[0])
        init.start()
        init.wait()
        fetch(0).start()
        fetch(1).start()
        fetch(0).wait()
        gather(0)

    @pl.when(nblk > 0)
    def _():
        wgb[...] = wg_ref[0].astype(BF16)
        wub[...] = wu_ref[0].astype(BF16)
        wdb[...] = wd_ref[0].astype(BF16)

    def block(k, carry):
        g = g0 + k
        slot = g % 2
        n = bn_ref[g]
        fetch(g + 1).wait()
        fetch(g + 2).start()

        @pl.when(g >= 2)
        def _():
            wait_rows(slot, rows_sent(bn_ref[jnp.maximum(g - 2, 0)]))

        trow = pl.multiple_of(slot * (PACK_ROWS * TILE_STRIDE), 8)

        def compute(send_prev):
            los, his = [], []
            for j in range(PACK_ROWS):
                lo, hi = _unpack_bf16_pairs(tile_scr[pl.ds(trow + j * TILE_STRIDE, R), :])
                los.append(lo.astype(BF16))
                his.append(hi.astype(BF16))
            xg = jnp.concatenate(los + his, axis=1)
            gather(g + 1)
            if send_prev:
                pbase = list_base(g - 1)
                for r in range(R):
                    send(1 - slot, r, ord_smem[pbase + r], r)
            hg = _bdot(xg, wgb[...])
            hu = _bdot(xg, wub[...])
            hb = (hg * _sigmoid(hg)) * hu
            y = _bdot(hb.astype(BF16), wdb[...])
            words = _pack_bf16_pairs(y)
            yb = ybuf.at[slot]
            for c in range(C):
                yb[pl.ds(c, R, stride=C), :] = words[:, c * 128:(c + 1) * 128]

        prev_full = jnp.logical_and(g >= 1, bn_ref[jnp.maximum(g - 1, 0)] == R)

        @pl.when(prev_full)
        def _():
            compute(True)

        @pl.when(jnp.logical_not(prev_full))
        def _():
            compute(False)

        @pl.when(n < R)
        def _():
            send_rows(g)

        return carry

    def send(s, r, d, u):
        pltpu.make_async_copy(ybuf.at[s, pl.ds(pl.multiple_of(r * C, C), C), :],
                              yt_ref.at[pl.ds(pl.multiple_of(d * C, C), C), :], sem.at[s]).start(priority=u % 2)

    def send_rows(g):
        slot = g % 2
        n = bn_ref[g]
        base = list_base(g)

        def send_group(i, c2):
            for u in range(8):
                send(slot, i * 8 + u, ord_smem[base + i * 8 + u], u)
            return c2

        n_full = lax.shift_right_logical(n, 3)
        lax.fori_loop(0, n_full, send_group, 0)

        @pl.when(n_full * 8 < n)
        def _():
            for u in range(8):
                r = n_full * 8 + u
                send(slot, r, jnp.where(r < n, ord_smem[base + r], n_assign + slot * 8 + u), u)

    lax.fori_loop(0, nblk, block, 0)

    @pl.when(e == pl.num_programs(0) - 1)
    def _():
        @pl.when(bn_ref[jnp.maximum(total - 1, 0)] == R)
        def _():
            send_rows(total - 1)

        fetch(total + 1).wait()
        wait_rows((total - 1) % 2, rows_sent(bn_ref[jnp.maximum(total - 1, 0)]))

        @pl.when(total >= 2)
        def _():
            wait_rows(total % 2, rows_sent(bn_ref[jnp.maximum(total - 2, 0)]))


def _moe(ex_b0, ex_nb, ex_w, total, blk_n, blk_src, blk_delta, order, tokrow, xp, wg, wu, wd, n_tok):
    R = MOE_ROWS
    wmap_in = lambda e, eb0, enb, wsel, tot, bn, bs, bd: (wsel[e], 0, 0)
    grid_spec = pltpu.PrefetchScalarGridSpec(
        num_scalar_prefetch=7,
        grid=(N_EXPERTS,),
        in_specs=[
            pl.BlockSpec(memory_space=pl.ANY),
            pl.BlockSpec(memory_space=pl.ANY),
            pl.BlockSpec(memory_space=pltpu.VMEM),
            pl.BlockSpec((1, D_MODEL, D_EXPERT), wmap_in),
            pl.BlockSpec((1, D_MODEL, D_EXPERT), wmap_in),
            pl.BlockSpec((1, D_EXPERT, D_MODEL), wmap_in),
        ],
        out_specs=pl.BlockSpec(memory_space=pl.ANY),
        scratch_shapes=[pltpu.VMEM((2 * PACK_ROWS * TILE_STRIDE, 128), I32),
                        pltpu.VMEM((2, R * PACK_ROWS, 128), I32),
                        pltpu.VMEM((D_MODEL, D_EXPERT), BF16),
                        pltpu.VMEM((D_MODEL, D_EXPERT), BF16),
                        pltpu.VMEM((D_EXPERT, D_MODEL), BF16),
                        pltpu.SMEM((4 * (R + ORDER_ALIGN),), I32),
                        pltpu.SMEM((4 * (R + ORDER_ALIGN),), I32),
                        pltpu.SemaphoreType.DMA((2,)),
                        pltpu.SemaphoreType.DMA((4,)),
                        pltpu.SemaphoreType.DMA((4,))],
    )
    return pl.pallas_call(
        _moe_kernel,
        out_shape=jax.ShapeDtypeStruct(((n_tok + 2) * TOP_K * PACK_ROWS, 128), I32),
        grid_spec=grid_spec,
        compiler_params=pltpu.CompilerParams(dimension_semantics=("arbitrary",),
                                             vmem_limit_bytes=58 * 1024 * 1024),
        name="moe",
    )(ex_b0, ex_nb, ex_w, total, blk_n, blk_src, blk_delta, order, tokrow, xp, wg, wu, wd)


def _route_plan(idx, counts, n_tok):
    R = MOE_ROWS
    n_assign = n_tok * TOP_K
    nb = (n_assign + N_EXPERTS * (R - 1)) // R
    id_bits = (n_assign - 1).bit_length()
    assert id_bits + (N_EXPERTS - 1).bit_length() < 32
    key = lax.shift_left(idx.reshape(-1), id_bits) | jnp.arange(n_assign, dtype=I32)
    order = lax.sort(key) & ((1 << id_bits) - 1)
    order = jnp.concatenate([order, jnp.zeros((R + ORDER_ALIGN,), I32)])
    nblk_e = (counts + R - 1) // R
    bend = jnp.cumsum(nblk_e)
    bstart = bend - nblk_e
    cstart = jnp.cumsum(counts) - counts
    blk = jnp.arange(nb, dtype=I32)
    blk_e = jnp.minimum(jnp.sum((bend[None, :] <= blk[:, None]).astype(I32), axis=1), N_EXPERTS - 1)
    k = blk - bstart[blk_e]
    active = blk < bend[-1]
    blk_n = jnp.where(active, jnp.clip(counts[blk_e] - k * R, 0, R), 0).astype(I32)
    src = jnp.where(active, cstart[blk_e] + k * R, 0).astype(I32)
    blk_src = (src // ORDER_ALIGN) * ORDER_ALIGN
    ex = jnp.arange(N_EXPERTS, dtype=I32)
    ex_w = jnp.maximum(lax.cummax(jnp.where(nblk_e > 0, ex, -1)), 0).astype(I32)
    tokrow = (order // TOP_K) * PACK_ROWS
    return (bstart.astype(I32), nblk_e.astype(I32), ex_w, bend[-1:].astype(I32),
            blk_n, blk_src, src - blk_src, order, tokrow)


def _combine_kernel(x1_ref, yt_ref, gate_ref, wgs_ref, wus_ref, wds_ref, g2_ref, b2_ref, out_ref):
    tm = x1_ref.shape[0]
    x1 = x1_ref[...]
    xb = x1.astype(BF16)
    hg = _bdot(xb, wgs_ref[...])
    hs = (hg * _sigmoid(hg)) * _bdot(xb, wus_ref[...])
    moe = _bdot(hs.astype(BF16), wds_ref[...])
    gate = gate_ref[...]
    per_tok = TOP_K * PACK_ROWS
    planes = pltpu.einshape("tjl->jtl", yt_ref[...].reshape(tm, per_tok, 128))
    lo_acc = [None] * PACK_ROWS
    hi_acc = [None] * PACK_ROWS
    for j in range(TOP_K):
        gj = gate[:, j:j + 1]
        for c in range(PACK_ROWS):
            lo, hi = _unpack_bf16_pairs(planes[j * PACK_ROWS + c])
            lo_acc[c] = lo * gj if j == 0 else lo_acc[c] + lo * gj
            hi_acc[c] = hi * gj if j == 0 else hi_acc[c] + hi * gj
    routed = jnp.concatenate(lo_acc + hi_acc, axis=1)
    out_ref[...] = _ln(ALPHA * x1 + (moe + routed), g2_ref[...], b2_ref[...])


def _combine(x1, yt, gate, wgs_bf, wus_bf, wds_bf, g2, b2, row0, nrows):
    tm = 128
    assert nrows % tm == 0 and row0 % tm == 0
    off = row0 // tm
    row = lambda i: (i + off, 0)
    const = lambda i: (0, 0)
    return pl.pallas_call(
        _combine_kernel,
        out_shape=jax.ShapeDtypeStruct((nrows, D_MODEL), F32),
        grid=(nrows // tm,),
        in_specs=[pl.BlockSpec((tm, D_MODEL), row),
                  pl.BlockSpec((tm * TOP_K * PACK_ROWS, 128), row),
                  pl.BlockSpec((tm, 128), row),
                  pl.BlockSpec((D_MODEL, D_EXPERT), const), pl.BlockSpec((D_MODEL, D_EXPERT), const),
                  pl.BlockSpec((D_EXPERT, D_MODEL), const),
                  pl.BlockSpec((1, D_MODEL), const), pl.BlockSpec((1, D_MODEL), const)],
        out_specs=pl.BlockSpec((tm, D_MODEL), lambda i: (i, 0)),
        compiler_params=pltpu.CompilerParams(dimension_semantics=("arbitrary",), vmem_limit_bytes=VMEM_LIMIT),
        name="combine",
    )(x1, yt, gate, wgs_bf, wus_bf, wds_bf, g2, b2)


def kernel(x_prompt, x_sample, state_hgrn, state_pool, meta_tokens, ln_emb_g, ln_emb_b, w_in, lb_logits, hgrn_norm_g, w_pool, pool_scale, w_out, ln1_g, ln1_b, w_router, b_router, w_gate_e, w_up_e, w_down_e, w_gate_s, w_up_s, w_down_s, ln2_g, ln2_b):
    nseq, seqlen, _ = x_prompt.shape
    ndec = x_sample.shape[0]
    l = 0
    row = lambda a: a.reshape(1, -1)
    w_in_bf = w_in[l].astype(BF16)
    wpool_bf = w_pool[l].astype(BF16)
    lng, lnb = row(ln_emb_g), row(ln_emb_b)
    ng, ps = row(hgrn_norm_g[l]), row(pool_scale[l])
    proj = functools.partial(_ln_proj, ln_g=lng, ln_b=lnb, w_in_bf=w_in_bf, lb_logits=lb_logits, norm_g=ng)

    m_xn, m_q, m_k, m_g, m_v, m_gs, m_p = proj(meta_tokens)
    pad = lambda a: jnp.pad(a, ((0, MIX_BLOCK - N_META), (0, 0)))
    zero_state = jnp.zeros((HGRN_HEADS, HEAD_DIM, HEAD_DIM), F32)
    _, s_meta = _mixer(pad(m_q), pad(m_k), pad(m_g), pad(m_v), pad(m_gs), pad(m_p), zero_state,
                       jnp.zeros((SUB, POOL_WIDTH), F32), wpool_bf, ps, 1, MIX_BLOCK)

    p_xn, p_q, p_k, p_g, p_v, p_gs, p_p = proj(x_prompt.reshape(nseq * seqlen, D_MODEL))
    p_mix, s_prompt = _mixer(p_q, p_k, p_g, p_v, p_gs, p_p, s_meta[0], m_p, wpool_bf, ps, nseq, seqlen)

    d_xn, d_q, d_k, d_g, d_v, d_gs, d_p = proj(x_sample.reshape(ndec, D_MODEL))
    cols = lambda a: a.reshape(ndec // STEP_SEQS, STEP_SEQS, HGRN_HEADS, HEAD_DIM).transpose(2, 0, 3, 1)
    s_dec, d_oa = _mixer_step(cols(d_q), cols(d_k), cols(d_g), d_v, d_gs, state_hgrn[l])
    d_ob = _pool_step(state_pool[l].transpose(1, 0, 2), d_p, wpool_bf, ps)
    d_mix = jnp.concatenate([d_oa, d_ob], axis=1).astype(BF16)

    n_tok = nseq * seqlen + ndec
    x1, xp, idx, gate, cnt = _outproj(p_mix, p_xn, d_mix, d_xn, w_out[l].astype(BF16), row(ln1_g[l]), row(ln1_b[l]),
                                      w_router[l].astype(BF16), row(b_router[l]))
    plan = _route_plan(idx[:n_tok, :TOP_K], cnt[0].astype(I32), n_tok)
    yt = _moe(*plan, xp, w_gate_e[l], w_up_e[l], w_down_e[l], n_tok)
    comb = functools.partial(_combine, x1, yt, gate, w_gate_s[l].astype(BF16), w_up_s[l].astype(BF16),
                             w_down_s[l].astype(BF16), row(ln2_g[l]), row(ln2_b[l]))
    y_prompt = comb(0, nseq * seqlen).reshape(nseq, seqlen, D_MODEL)
    y_sample = comb(nseq * seqlen, ndec).reshape(ndec, 1, D_MODEL)

    state_pool_prompt = p_p.reshape(nseq, seqlen, POOL_WIDTH)[:, seqlen - POOL_BUF:, :]
    state_pool_sample = jnp.concatenate([state_pool[l][:, 1:, :], d_p[:, None, :]], axis=1)
    return (y_prompt, y_sample, s_prompt[None], state_pool_prompt[None], s_dec[None], state_pool_sample[None])
```

```python
import functools

import jax
import jax.numpy as jnp
from jax import lax
from jax.experimental import pallas as pl
from jax.experimental.pallas import tpu as pltpu

F32 = jnp.float32
BF16 = jnp.bfloat16
I32 = jnp.int32

D_MODEL = 1024
N_META = 16
HGRN_WIDTH = 512
HGRN_HEADS = 4
HEAD_DIM = 128
POOL_WIDTH = 512
POOL_WINDOWS = (2, 4, 8, 16)
POOL_GROUP_DIM = 128
POOL_BUF = 15
IN_WIDTH = 4 * HGRN_WIDTH + POOL_WIDTH
N_EXPERTS = 256
TOP_K = 8
D_EXPERT = 256
ROUTED_SCALE = 2.5
DEPTH = 1
ALPHA = (2 * DEPTH) ** 0.25
LN_EPS = 1e-5
RMS_EPS = 1e-6
LOG2_E = 1.4426950408889634

SUB = 16
MIX_BLOCK = 128
MOE_ROWS = 256
TILE_STRIDE = MOE_ROWS + 8
ORDER_ALIGN = 128
PACK_ROWS = D_MODEL // 2 // 128
OUT_TILE = 512
VMEM_LIMIT = 56 * 1024 * 1024


def _ln(x, g, b):
    mu = jnp.mean(x, axis=-1, keepdims=True)
    xc = x - mu
    var = jnp.mean(xc * xc, axis=-1, keepdims=True)
    return xc * lax.rsqrt(var + LN_EPS) * g + b


def _sigmoid(z):
    return 1.0 / (1.0 + jnp.exp(-z))


def _bdot(a, b):
    return jnp.dot(a, b, preferred_element_type=F32)


def _pack_bf16_pairs(x):
    half = x.shape[1] // 2
    xr = x.astype(BF16).astype(F32)
    lo = lax.shift_right_logical(lax.bitcast_convert_type(xr[:, :half], jnp.uint32), jnp.uint32(16))
    hi = lax.bitcast_convert_type(xr[:, half:], jnp.uint32) & jnp.uint32(0xFFFF0000)
    return lax.bitcast_convert_type(hi | lo, I32)


def _unpack_bf16_pairs(words):
    w = lax.bitcast_convert_type(words, jnp.uint32)
    lo = lax.bitcast_convert_type(lax.shift_left(w, jnp.uint32(16)), F32)
    hi = lax.bitcast_convert_type(w & jnp.uint32(0xFFFF0000), F32)
    return lo, hi


def _ln_proj_kernel(x_ref, g_ref, b_ref, w_ref, lbl_ref, ng_ref,
                    xn_ref, q_ref, k_ref, gl_ref, v_ref, gs_ref, p_ref):
    xn = _ln(x_ref[...], g_ref[...], b_ref[...])
    xn_ref[...] = xn
    proj = _bdot(xn.astype(BF16), w_ref[...])
    lbl = lbl_ref[...]
    e = jnp.exp(lbl - jnp.max(lbl, axis=0, keepdims=True))
    lb = e[0:1] / jnp.sum(e, axis=0, keepdims=True)
    W = HGRN_WIDTH
    q = proj[:, 0:W]
    f = proj[:, W:2 * W]
    q_ref[...] = q * _sigmoid(q)
    k_ref[...] = (1.0 - lb) * _sigmoid(-f)
    gl_ref[...] = jnp.log(lb + (1.0 - lb) * _sigmoid(f))
    v_ref[...] = proj[:, 2 * W:3 * W]
    g = proj[:, 3 * W:4 * W]
    gs_ref[...] = ng_ref[...] * (g * _sigmoid(g))
    p_ref[...] = proj[:, 4 * W:]


def _row_tile(n_rows, largest=512):
    for tm in (512, 256, 128, 64, 32, 16, 8):
        if tm > largest:
            continue
        if n_rows % tm == 0:
            return tm
    raise ValueError(f"row count {n_rows} is not a multiple of 8")


def _ln_proj(x, ln_g, ln_b, w_in_bf, lb_logits, norm_g):
    T = x.shape[0]
    tm = _row_tile(T)
    row = lambda i: (i, 0)
    const = lambda i: (0, 0)
    outs = [jax.ShapeDtypeStruct((T, D_MODEL), F32)] + [jax.ShapeDtypeStruct((T, HGRN_WIDTH), F32)] * 6
    return pl.pallas_call(
        _ln_proj_kernel,
        out_shape=outs,
        grid=(T // tm,),
        in_specs=[
            pl.BlockSpec((tm, D_MODEL), row),
            pl.BlockSpec((1, D_MODEL), const),
            pl.BlockSpec((1, D_MODEL), const),
            pl.BlockSpec((D_MODEL, IN_WIDTH), const),
            pl.BlockSpec((DEPTH + 1, HGRN_WIDTH), const),
            pl.BlockSpec((1, HGRN_WIDTH), const),
        ],
        out_specs=[pl.BlockSpec((tm, D_MODEL), row)] + [pl.BlockSpec((tm, HGRN_WIDTH), row)] * 6,
        compiler_params=pltpu.CompilerParams(dimension_semantics=("arbitrary",), vmem_limit_bytes=VMEM_LIMIT),
        name="ln_proj",
    )(x, ln_g, ln_b, w_in_bf, lb_logits, norm_g)


def _pool_group(pe, p_cur, gi, w):
    sl = slice(gi * POOL_GROUP_DIM, (gi + 1) * POOL_GROUP_DIM)
    s = pe[:, sl]
    sh = 1
    while sh < w:
        s = s + pltpu.roll(s, sh, 0)
        sh *= 2
    return s[SUB:, :] * (1.0 / w) - p_cur[:, sl]


def _mixer_kernel(q_ref, k_ref, g_ref, v_ref, gs_ref, p_ref, s0_ref, pp0_ref, wpool_ref, pscale_ref,
                  mix_ref, sfin_ref, st_scr, pe_scr, bf_scr):
    i = pl.program_id(1)
    nblk = pl.num_programs(1)
    n = MIX_BLOCK

    @pl.when(i == 0)
    def _():
        for h in range(HGRN_HEADS):
            st_scr[h] = s0_ref[h].T
        pe_scr[0:SUB, :] = pp0_ref[...]

    rows = lax.broadcasted_iota(I32, (n, HEAD_DIM), 0)
    r16 = rows & (SUB - 1)
    t8 = lax.broadcasted_iota(I32, (8, HEAD_DIM), 0)
    zero_bf = jnp.zeros((SUB, HEAD_DIM), BF16)

    o_heads = []
    for h in range(HGRN_HEADS):
        hs = slice(h * HEAD_DIM, (h + 1) * HEAD_DIM)
        Q = q_ref[:, hs]
        K = k_ref[:, hs]
        G = g_ref[:, hs]
        V = v_ref[:, hs]
        bf = G
        br = G
        for sh in (1, 2, 4, 8):
            bf = bf + jnp.where(r16 >= sh, pltpu.roll(bf, sh, 0), 0.0)
            br = br + jnp.where(r16 < SUB - sh, pltpu.roll(br, n - sh, 0), 0.0)
        br = br - G
        bf2 = bf * LOG2_E
        bf_scr[:, hs] = bf2
        qt = (Q * jnp.exp(bf)).astype(BF16)
        kt = (K * jnp.exp(br)).astype(BF16)
        vt = V.T.astype(BF16)
        st = st_scr[h]
        o_parts = []
        for c in range(n // SUB):
            r0 = c * SUB
            b_top, b_bot = bf2[r0:r0 + 8], bf2[r0 + 8:r0 + 16]
            q_top, q_bot = Q[r0:r0 + 8], Q[r0 + 8:r0 + 16]
            acc_top = jnp.zeros((8, HEAD_DIM), F32)
            acc_bot = jnp.zeros((8, HEAD_DIM), F32)
            for s in range(SUB):
                bs = bf_scr[r0 + s:r0 + s + 1, hs]
                ks = k_ref[r0 + s:r0 + s + 1, hs]
                vs = v_ref[r0 + s:r0 + s + 1, hs]
                if s < 8:
                    col = jnp.sum(q_top * jnp.exp2(b_top - bs) * ks, axis=-1, keepdims=True)
                    col = jnp.where(t8[:, 0:1] >= s, col, 0.0)
                    acc_top = acc_top + col * vs
                    col = jnp.sum(q_bot * jnp.exp2(b_bot - bs) * ks, axis=-1, keepdims=True)
                    acc_bot = acc_bot + col * vs
                else:
                    col = jnp.sum(q_bot * jnp.exp2(b_bot - bs) * ks, axis=-1, keepdims=True)
                    col = jnp.where(t8[:, 0:1] + 8 >= s, col, 0.0)
                    acc_bot = acc_bot + col * vs
            o_diag = jnp.concatenate([acc_top, acc_bot], axis=0)
            o_inter = lax.dot_general(qt[r0:r0 + SUB], st.astype(BF16), (((1,), (1,)), ((), ())),
                                      preferred_element_type=F32)
            o_parts.append(o_inter + o_diag)
            kmask = jnp.concatenate([zero_bf] * c + [kt[r0:r0 + SUB]] + [zero_bf] * (n // SUB - 1 - c), axis=0)
            d_st = _bdot(vt, kmask)
            st = st * jnp.exp(bf[r0 + SUB - 1:r0 + SUB]) + d_st
        st_scr[h] = st
        o = jnp.concatenate(o_parts, axis=0)
        o = o * lax.rsqrt(jnp.mean(o * o, axis=-1, keepdims=True) + RMS_EPS)
        o_heads.append(o * gs_ref[:, hs])

    p_cur = p_ref[...]
    pe_scr[SUB:SUB + n, :] = p_cur
    pe = pe_scr[...]
    ob = []
    for gi, w in enumerate(POOL_WINDOWS):
        pooled = _pool_group(pe, p_cur, gi, w)
        sl = slice(gi * POOL_GROUP_DIM, (gi + 1) * POOL_GROUP_DIM)
        ob.append(_bdot(pooled.astype(BF16), wpool_ref[gi]) * pscale_ref[:, sl])
    pe_scr[0:SUB, :] = p_cur[n - SUB:, :]

    mix_ref[...] = jnp.concatenate(o_heads + ob, axis=1).astype(BF16)

    @pl.when(i == nblk - 1)
    def _():
        for h in range(HGRN_HEADS):
            sfin_ref[0, h] = st_scr[h].T


def _mixer(q, k, g, v, gs, p, s0, pp0, wpool_bf, pscale, nseq, seqlen):
    nblk = seqlen // MIX_BLOCK
    tok = lambda b, i: (b * nblk + i, 0)
    tspec = pl.BlockSpec((MIX_BLOCK, HGRN_WIDTH), tok)
    return pl.pallas_call(
        _mixer_kernel,
        out_shape=[jax.ShapeDtypeStruct((nseq * seqlen, D_MODEL), BF16),
                   jax.ShapeDtypeStruct((nseq, HGRN_HEADS, HEAD_DIM, HEAD_DIM), F32)],
        grid=(nseq, nblk),
        in_specs=[tspec] * 6 + [
            pl.BlockSpec((HGRN_HEADS, HEAD_DIM, HEAD_DIM), lambda b, i: (0, 0, 0)),
            pl.BlockSpec((SUB, POOL_WIDTH), lambda b, i: (0, 0)),
            pl.BlockSpec((len(POOL_WINDOWS), POOL_GROUP_DIM, POOL_GROUP_DIM), lambda b, i: (0, 0, 0)),
            pl.BlockSpec((1, POOL_WIDTH), lambda b, i: (0, 0)),
        ],
        out_specs=[pl.BlockSpec((MIX_BLOCK, D_MODEL), tok),
                   pl.BlockSpec((1, HGRN_HEADS, HEAD_DIM, HEAD_DIM), lambda b, i: (b, 0, 0, 0))],
        scratch_shapes=[pltpu.VMEM((HGRN_HEADS, HEAD_DIM, HEAD_DIM), F32),
                        pltpu.VMEM((SUB + MIX_BLOCK, POOL_WIDTH), F32),
                        pltpu.VMEM((MIX_BLOCK, HGRN_WIDTH), F32)],
        compiler_params=pltpu.CompilerParams(dimension_semantics=("arbitrary", "arbitrary"),
                                             vmem_limit_bytes=VMEM_LIMIT),
        name="mixer",
    )(q, k, g, v, gs, p, s0, pp0, wpool_bf, pscale)


STEP_SEQS = 32


def _mixer_step_kernel(qt_ref, kt_ref, gt_ref, v_ref, gs_ref, s_ref, snew_ref, oa_ref):
    qt = qt_ref[0, 0]
    kt = kt_ref[0, 0]
    dt = jnp.exp(gt_ref[0, 0])
    rows = []
    for bb in range(STEP_SEQS):
        sn = s_ref[bb, 0] * dt[:, bb:bb + 1] + kt[:, bb:bb + 1] * v_ref[bb:bb + 1, :]
        snew_ref[bb, 0] = sn
        rows.append(jnp.sum(sn * qt[:, bb:bb + 1], axis=0, keepdims=True))
    o = jnp.concatenate(rows, axis=0)
    o = o * lax.rsqrt(jnp.mean(o * o, axis=-1, keepdims=True) + RMS_EPS)
    oa_ref[...] = o * gs_ref[...]


def _mixer_step(qT, kT, gT, v, gs, state):
    nseq = v.shape[0]
    nbc = nseq // STEP_SEQS
    cspec = pl.BlockSpec((1, 1, HEAD_DIM, STEP_SEQS), lambda h, c: (h, c, 0, 0))
    rspec = pl.BlockSpec((STEP_SEQS, HEAD_DIM), lambda h, c: (c, h))
    sspec = pl.BlockSpec((STEP_SEQS, 1, HEAD_DIM, HEAD_DIM), lambda h, c: (c, h, 0, 0))
    return pl.pallas_call(
        _mixer_step_kernel,
        out_shape=[jax.ShapeDtypeStruct(state.shape, F32), jax.ShapeDtypeStruct((nseq, HGRN_WIDTH), F32)],
        grid=(HGRN_HEADS, nbc),
        in_specs=[cspec, cspec, cspec, rspec, rspec, sspec],
        out_specs=[sspec, rspec],
        compiler_params=pltpu.CompilerParams(dimension_semantics=("arbitrary", "arbitrary"),
                                             vmem_limit_bytes=VMEM_LIMIT),
        name="mixer_step",
    )(qT, kT, gT, v, gs, state)


def _pool_step_kernel(sp_ref, p_ref, wpool_ref, pscale_ref, ob_ref):
    p_cur = p_ref[...]
    outs = []
    for gi, w in enumerate(POOL_WINDOWS):
        sl = slice(gi * POOL_GROUP_DIM, (gi + 1) * POOL_GROUP_DIM)
        s = p_cur[:, sl]
        for r in range(POOL_BUF - (w - 1), POOL_BUF):
            s = s + sp_ref[r][:, sl]
        pooled = s * (1.0 / w) - p_cur[:, sl]
        outs.append(_bdot(pooled.astype(BF16), wpool_ref[gi]) * pscale_ref[:, sl])
    ob_ref[...] = jnp.concatenate(outs, axis=1)


def _pool_step(spT, p, wpool_bf, pscale):
    nseq = p.shape[0]
    return pl.pallas_call(
        _pool_step_kernel,
        out_shape=jax.ShapeDtypeStruct((nseq, POOL_WIDTH), F32),
        name="pool_step",
    )(spT, p, wpool_bf, pscale)


def _outproj_kernel(mixp_ref, xnp_ref, mixd_ref, xnd_ref, wout_ref, g1_ref, b1_ref, wr_ref, br_ref,
                    x1_ref, x1p_ref, idx_ref, gate_ref, cnt_ref, *, n_prompt_blocks, n_valid_last):
    tm = mixp_ref.shape[0]
    i = pl.program_id(0)
    is_prompt = i < n_prompt_blocks
    mix_in = jnp.where(is_prompt, mixp_ref[...], mixd_ref[...])
    xn = jnp.where(is_prompt, xnp_ref[...], xnd_ref[...])
    mix = _bdot(mix_in, wout_ref[...])
    x1 = _ln(ALPHA * xn + mix, g1_ref[...], b1_ref[...])
    x1_ref[...] = x1
    xb = x1.astype(BF16)
    words = _pack_bf16_pairs(x1)
    for c in range(PACK_ROWS):
        x1p_ref[pl.ds(c, tm, stride=PACK_ROWS), :] = words[:, c * 128:(c + 1) * 128]

    scores = _sigmoid(_bdot(xb, wr_ref[...]))
    sel = scores + br_ref[...]
    lane = lax.broadcasted_iota(I32, (tm, N_EXPERTS), 1).astype(F32)
    lane_o = lax.broadcasted_iota(I32, (tm, 128), 1)
    idx_o = jnp.zeros((tm, 128), F32)
    ssum = jnp.zeros((tm, 1), F32)
    chosen = jnp.zeros((tm, N_EXPERTS), F32)
    s_sel = []
    for j in range(TOP_K):
        m = jnp.max(sel, axis=-1, keepdims=True)
        am = jnp.min(jnp.where(sel == m, lane, float(N_EXPERTS)), axis=-1, keepdims=True)
        hit = lane == am
        sj = jnp.sum(jnp.where(hit, scores, 0.0), axis=-1, keepdims=True)
        sel = jnp.where(hit, -jnp.inf, sel)
        chosen = jnp.where(hit, 1.0, chosen)
        idx_o = jnp.where(lane_o == j, am, idx_o)
        s_sel.append(sj)
        ssum = ssum + sj
    idx_ref[...] = idx_o.astype(I32)
    gates = jnp.zeros((tm, 128), F32)
    for j in range(TOP_K):
        gates = jnp.where(lane_o == j, s_sel[j] / ssum * ROUTED_SCALE, gates)
    gate_ref[...] = gates

    @pl.when(i == 0)
    def _():
        cnt_ref[...] = jnp.zeros_like(cnt_ref)

    row_id = lax.broadcasted_iota(I32, (tm, N_EXPERTS), 0)
    valid = jnp.logical_or(is_prompt, row_id < n_valid_last)
    cnt_ref[...] += jnp.sum(jnp.where(valid, chosen, 0.0), axis=0, keepdims=True)


def _outproj(mix_p, xn_p, mix_d, xn_d, wout_bf, g1, b1, wr_bf, br):
    tm = OUT_TILE
    n_prompt, n_dec = mix_p.shape[0], mix_d.shape[0]
    assert n_prompt % tm == 0 and n_dec <= tm and n_dec % 8 == 0
    nbp = n_prompt // tm
    T = (nbp + 1) * tm
    padrows = lambda a: jnp.pad(a, ((0, tm - n_dec), (0, 0)))
    row = lambda i: (i, 0)
    prow = lambda i: (jnp.minimum(i, nbp - 1), 0)
    const = lambda i: (0, 0)
    return pl.pallas_call(
        functools.partial(_outproj_kernel, n_prompt_blocks=nbp, n_valid_last=n_dec),
        out_shape=[jax.ShapeDtypeStruct((T, D_MODEL), F32), jax.ShapeDtypeStruct((T * PACK_ROWS, 128), I32),
                   jax.ShapeDtypeStruct((T, 128), I32), jax.ShapeDtypeStruct((T, 128), F32),
                   jax.ShapeDtypeStruct((1, N_EXPERTS), F32)],
        grid=(nbp + 1,),
        in_specs=[pl.BlockSpec((tm, D_MODEL), prow), pl.BlockSpec((tm, D_MODEL), prow),
                  pl.BlockSpec((tm, D_MODEL), const), pl.BlockSpec((tm, D_MODEL), const),
                  pl.BlockSpec((D_MODEL, D_MODEL), const), pl.BlockSpec((1, D_MODEL), const),
                  pl.BlockSpec((1, D_MODEL), const), pl.BlockSpec((D_MODEL, N_EXPERTS), const),
                  pl.BlockSpec((1, N_EXPERTS), const)],
        out_specs=[pl.BlockSpec((tm, D_MODEL), row), pl.BlockSpec((tm * PACK_ROWS, 128), row),
                   pl.BlockSpec((tm, 128), row), pl.BlockSpec((tm, 128), row),
                   pl.BlockSpec((1, N_EXPERTS), const)],
        compiler_params=pltpu.CompilerParams(dimension_semantics=("arbitrary",), vmem_limit_bytes=VMEM_LIMIT),
        name="outproj",
    )(mix_p, xn_p, padrows(mix_d), padrows(xn_d), wout_bf, g1, b1, wr_bf, br)


def _moe_kernel(eb0_ref, enb_ref, wsel_ref, tot_ref, bn_ref, bsrc_ref, bdel_ref,
                order_ref, tokrow_ref, xp_ref, wg_ref, wu_ref, wd_ref, yt_ref,
                tile_scr, ybuf, wgb, wub, wdb, ord_smem, tok_smem, sem, sem_o, sem_t):
    del wsel_ref
    e = pl.program_id(0)
    g0 = eb0_ref[e]
    nblk = enb_ref[e]
    total = tot_ref[0]
    nb_max = bn_ref.shape[0]
    R = MOE_ROWS
    C = PACK_ROWS
    L = R + ORDER_ALIGN
    n_assign = yt_ref.shape[0] // C - 2 * TOP_K

    class _Fetch:
        def __init__(self, g):
            gc = jnp.minimum(g, nb_max - 1)
            s = g % 4
            src = pl.multiple_of(bsrc_ref[gc], ORDER_ALIGN)
            dst = pl.multiple_of(s * L, ORDER_ALIGN)
            self.copies = (
                pltpu.make_async_copy(order_ref.at[pl.ds(src, L)], ord_smem.at[pl.ds(dst, L)], sem_o.at[s]),
                pltpu.make_async_copy(tokrow_ref.at[pl.ds(src, L)], tok_smem.at[pl.ds(dst, L)], sem_t.at[s]))

        def start(self):
            for c in self.copies:
                c.start()

        def wait(self):
            for c in self.copies:
                c.wait()

    fetch = _Fetch

    def list_base(g):
        return (g % 4) * L + bdel_ref[jnp.minimum(g, nb_max - 1)]

    def gather(g):
        base = list_base(g)
        trow = (g % 2) * (PACK_ROWS * TILE_STRIDE)
        for r in range(R):
            t4 = pl.multiple_of(tok_smem[base + r], PACK_ROWS)
            tile_scr[pl.ds(trow + r, PACK_ROWS, stride=TILE_STRIDE), :] = xp_ref[pl.ds(t4, PACK_ROWS), :]

    def rows_sent(cnt):
        return pl.multiple_of(lax.shift_left(lax.shift_right_logical(cnt + 7, 3), 3), 8)

    def wait_rows(s, cnt):
        pltpu.make_async_copy(ybuf.at[s, pl.ds(0, cnt * C), :], yt_ref.at[pl.ds(0, cnt * C), :], sem.at[s]).wait()

    @pl.when(e == 0)
    def _():
        n_spare = 2 * TOP_K * C
        ybuf[0, 0:n_spare, :] = jnp.zeros((n_spare, 128), I32)
        init = pltpu.make_async_copy(ybuf.at[0, pl.ds(0, n_spare), :],
                                     yt_ref.at[pl.ds(yt_ref.shape[0] - n_spare, n_spare), :], sem.at[0])
        init.start()
        init.wait()
        fetch(0).start()
        fetch(1).start()
        fetch(0).wait()
        gather(0)

    @pl.when(nblk > 0)
    def _():
        wgb[...] = wg_ref[0].astype(BF16)
        wub[...] = wu_ref[0].astype(BF16)
        wdb[...] = wd_ref[0].astype(BF16)

    def block(k, carry):
        g = g0 + k
        slot = g % 2
        n = bn_ref[g]
        fetch(g + 1).wait()
        fetch(g + 2).start()

        @pl.when(g >= 2)
        def _():
            wait_rows(slot, rows_sent(bn_ref[jnp.maximum(g - 2, 0)]))

        trow = pl.multiple_of(slot * (PACK_ROWS * TILE_STRIDE), 8)

        def compute(send_prev):
            los, his = [], []
            for j in range(PACK_ROWS):
                lo, hi = _unpack_bf16_pairs(tile_scr[pl.ds(trow + j * TILE_STRIDE, R), :])
                los.append(lo.astype(BF16))
                his.append(hi.astype(BF16))
            xg = jnp.concatenate(los + his, axis=1)
            gather(g + 1)
            if send_prev:
                pbase = list_base(g - 1)
                for r in range(R):
                    send(1 - slot, r, ord_smem[pbase + r], r)
            hg = _bdot(xg, wgb[...])
            hu = _bdot(xg, wub[...])
            hb = (hg * _sigmoid(hg)) * hu
            y = _bdot(hb.astype(BF16), wdb[...])
            words = _pack_bf16_pairs(y)
            yb = ybuf.at[slot]
            for c in range(C):
                yb[pl.ds(c, R, stride=C), :] = words[:, c * 128:(c + 1) * 128]

        prev_full = jnp.logical_and(g >= 1, bn_ref[jnp.maximum(g - 1, 0)] == R)

        @pl.when(prev_full)
        def _():
            compute(True)

        @pl.when(jnp.logical_not(prev_full))
        def _():
            compute(False)

        @pl.when(n < R)
        def _():
            send_rows(g)

        return carry

    def send(s, r, d, u):
        pltpu.make_async_copy(ybuf.at[s, pl.ds(pl.multiple_of(r * C, C), C), :],
                              yt_ref.at[pl.ds(pl.multiple_of(d * C, C), C), :], sem.at[s]).start(priority=u % 2)

    def send_rows(g):
        slot = g % 2
        n = bn_ref[g]
        base = list_base(g)

        def send_group(i, c2):
            for u in range(8):
                send(slot, i * 8 + u, ord_smem[base + i * 8 + u], u)
            return c2

        n_full = lax.shift_right_logical(n, 3)
        lax.fori_loop(0, n_full, send_group, 0)

        @pl.when(n_full * 8 < n)
        def _():
            for u in range(8):
                r = n_full * 8 + u
                send(slot, r, jnp.where(r < n, ord_smem[base + r], n_assign + slot * 8 + u), u)

    lax.fori_loop(0, nblk, block, 0)

    @pl.when(e == pl.num_programs(0) - 1)
    def _():
        @pl.when(bn_ref[jnp.maximum(total - 1, 0)] == R)
        def _():
            send_rows(total - 1)

        fetch(total + 1).wait()
        wait_rows((total - 1) % 2, rows_sent(bn_ref[jnp.maximum(total - 1, 0)]))

        @pl.when(total >= 2)
        def _():
            wait_rows(total % 2, rows_sent(bn_ref[jnp.maximum(total - 2, 0)]))


def _moe(ex_b0, ex_nb, ex_w, total, blk_n, blk_src, blk_delta, order, tokrow, xp, wg, wu, wd, n_tok):
    R = MOE_ROWS
    wmap_in = lambda e, eb0, enb, wsel, tot, bn, bs, bd: (wsel[e], 0, 0)
    grid_spec = pltpu.PrefetchScalarGridSpec(
        num_scalar_prefetch=7,
        grid=(N_EXPERTS,),
        in_specs=[
            pl.BlockSpec(memory_space=pl.ANY),
            pl.BlockSpec(memory_space=pl.ANY),
            pl.BlockSpec(memory_space=pltpu.VMEM),
            pl.BlockSpec((1, D_MODEL, D_EXPERT), wmap_in),
            pl.BlockSpec((1, D_MODEL, D_EXPERT), wmap_in),
            pl.BlockSpec((1, D_EXPERT, D_MODEL), wmap_in),
        ],
        out_specs=pl.BlockSpec(memory_space=pl.ANY),
        scratch_shapes=[pltpu.VMEM((2 * PACK_ROWS * TILE_STRIDE, 128), I32),
                        pltpu.VMEM((2, R * PACK_ROWS, 128), I32),
                        pltpu.VMEM((D_MODEL, D_EXPERT), BF16),
                        pltpu.VMEM((D_MODEL, D_EXPERT), BF16),
                        pltpu.VMEM((D_EXPERT, D_MODEL), BF16),
                        pltpu.SMEM((4 * (R + ORDER_ALIGN),), I32),
                        pltpu.SMEM((4 * (R + ORDER_ALIGN),), I32),
                        pltpu.SemaphoreType.DMA((2,)),
                        pltpu.SemaphoreType.DMA((4,)),
                        pltpu.SemaphoreType.DMA((4,))],
    )
    return pl.pallas_call(
        _moe_kernel,
        out_shape=jax.ShapeDtypeStruct(((n_tok + 2) * TOP_K * PACK_ROWS, 128), I32),
        grid_spec=grid_spec,
        compiler_params=pltpu.CompilerParams(dimension_semantics=("arbitrary",),
                                             vmem_limit_bytes=58 * 1024 * 1024),
        name="moe",
    )(ex_b0, ex_nb, ex_w, total, blk_n, blk_src, blk_delta, order, tokrow, xp, wg, wu, wd)


def _route_plan(idx, counts, n_tok):
    R = MOE_ROWS
    n_assign = n_tok * TOP_K
    nb = (n_assign + N_EXPERTS * (R - 1)) // R
    id_bits = (n_assign - 1).bit_length()
    assert id_bits + (N_EXPERTS - 1).bit_length() < 32
    key = lax.shift_left(idx.reshape(-1), id_bits) | jnp.arange(n_assign, dtype=I32)
    order = lax.sort(key) & ((1 << id_bits) - 1)
    order = jnp.concatenate([order, jnp.zeros((R + ORDER_ALIGN,), I32)])
    nblk_e = (counts + R - 1) // R
    bend = jnp.cumsum(nblk_e)
    bstart = bend - nblk_e
    cstart = jnp.cumsum(counts) - counts
    blk = jnp.arange(nb, dtype=I32)
    blk_e = jnp.minimum(jnp.sum((bend[None, :] <= blk[:, None]).astype(I32), axis=1), N_EXPERTS - 1)
    k = blk - bstart[blk_e]
    active = blk < bend[-1]
    blk_n = jnp.where(active, jnp.clip(counts[blk_e] - k * R, 0, R), 0).astype(I32)
    src = jnp.where(active, cstart[blk_e] + k * R, 0).astype(I32)
    blk_src = (src // ORDER_ALIGN) * ORDER_ALIGN
    ex = jnp.arange(N_EXPERTS, dtype=I32)
    ex_w = jnp.maximum(lax.cummax(jnp.where(nblk_e > 0, ex, -1)), 0).astype(I32)
    tokrow = (order // TOP_K) * PACK_ROWS
    return (bstart.astype(I32), nblk_e.astype(I32), ex_w, bend[-1:].astype(I32),
            blk_n, blk_src, src - blk_src, order, tokrow)


def _combine_kernel(x1_ref, yt_ref, gate_ref, wgs_ref, wus_ref, wds_ref, g2_ref, b2_ref, out_ref):
    tm = x1_ref.shape[0]
    x1 = x1_ref[...]
    xb = x1.astype(BF16)
    hg = _bdot(xb, wgs_ref[...])
    hs = (hg * _sigmoid(hg)) * _bdot(xb, wus_ref[...])
    moe = _bdot(hs.astype(BF16), wds_ref[...])
    gate = gate_ref[...]
    per_tok = TOP_K * PACK_ROWS
    planes = pltpu.einshape("tjl->jtl", yt_ref[...].reshape(tm, per_tok, 128))
    lo_acc = [None] * PACK_ROWS
    hi_acc = [None] * PACK_ROWS
    for j in range(TOP_K):
        gj = gate[:, j:j + 1]
        for c in range(PACK_ROWS):
            lo, hi = _unpack_bf16_pairs(planes[j * PACK_ROWS + c])
            lo_acc[c] = lo * gj if j == 0 else lo_acc[c] + lo * gj
            hi_acc[c] = hi * gj if j == 0 else hi_acc[c] + hi * gj
    routed = jnp.concatenate(lo_acc + hi_acc, axis=1)
    out_ref[...] = _ln(ALPHA * x1 + (moe + routed), g2_ref[...], b2_ref[...])


def _combine(x1, yt, gate, wgs_bf, wus_bf, wds_bf, g2, b2, row0, nrows):
    tm = _row_tile(nrows, largest=256)
    assert row0 % tm == 0
    off = row0 // tm
    row = lambda i: (i + off, 0)
    const = lambda i: (0, 0)
    return pl.pallas_call(
        _combine_kernel,
        out_shape=jax.ShapeDtypeStruct((nrows, D_MODEL), F32),
        grid=(nrows // tm,),
        in_specs=[pl.BlockSpec((tm, D_MODEL), row),
                  pl.BlockSpec((tm * TOP_K * PACK_ROWS, 128), row),
                  pl.BlockSpec((tm, 128), row),
                  pl.BlockSpec((D_MODEL, D_EXPERT), const), pl.BlockSpec((D_MODEL, D_EXPERT), const),
                  pl.BlockSpec((D_EXPERT, D_MODEL), const),
                  pl.BlockSpec((1, D_MODEL), const), pl.BlockSpec((1, D_MODEL), const)],
        out_specs=pl.BlockSpec((tm, D_MODEL), lambda i: (i, 0)),
        compiler_params=pltpu.CompilerParams(dimension_semantics=("arbitrary",), vmem_limit_bytes=VMEM_LIMIT),
        name="combine",
    )(x1, yt, gate, wgs_bf, wus_bf, wds_bf, g2, b2)


def kernel(x_prompt, x_sample, state_hgrn, state_pool, meta_tokens, ln_emb_g, ln_emb_b, w_in, lb_logits, hgrn_norm_g, w_pool, pool_scale, w_out, ln1_g, ln1_b, w_router, b_router, w_gate_e, w_up_e, w_down_e, w_gate_s, w_up_s, w_down_s, ln2_g, ln2_b):
    nseq, seqlen, _ = x_prompt.shape
    ndec = x_sample.shape[0]
    l = 0
    row = lambda a: a.reshape(1, -1)
    w_in_bf = w_in[l].astype(BF16)
    wpool_bf = w_pool[l].astype(BF16)
    lng, lnb = row(ln_emb_g), row(ln_emb_b)
    ng, ps = row(hgrn_norm_g[l]), row(pool_scale[l])
    proj = functools.partial(_ln_proj, ln_g=lng, ln_b=lnb, w_in_bf=w_in_bf, lb_logits=lb_logits, norm_g=ng)

    m_xn, m_q, m_k, m_g, m_v, m_gs, m_p = proj(meta_tokens)
    pad = lambda a: jnp.pad(a, ((0, MIX_BLOCK - N_META), (0, 0)))
    zero_state = jnp.zeros((HGRN_HEADS, HEAD_DIM, HEAD_DIM), F32)
    _, s_meta = _mixer(pad(m_q), pad(m_k), pad(m_g), pad(m_v), pad(m_gs), pad(m_p), zero_state,
                       jnp.zeros((SUB, POOL_WIDTH), F32), wpool_bf, ps, 1, MIX_BLOCK)

    p_xn, p_q, p_k, p_g, p_v, p_gs, p_p = proj(x_prompt.reshape(nseq * seqlen, D_MODEL))
    p_mix, s_prompt = _mixer(p_q, p_k, p_g, p_v, p_gs, p_p, s_meta[0], m_p, wpool_bf, ps, nseq, seqlen)

    d_xn, d_q, d_k, d_g, d_v, d_gs, d_p = proj(x_sample.reshape(ndec, D_MODEL))
    cols = lambda a: a.reshape(ndec // STEP_SEQS, STEP_SEQS, HGRN_HEADS, HEAD_DIM).transpose(2, 0, 3, 1)
    s_dec, d_oa = _mixer_step(cols(d_q), cols(d_k), cols(d_g), d_v, d_gs, state_hgrn[l])
    d_ob = _pool_step(state_pool[l].transpose(1, 0, 2), d_p, wpool_bf, ps)
    d_mix = jnp.concatenate([d_oa, d_ob], axis=1).astype(BF16)

    n_tok = nseq * seqlen + ndec
    x1, xp, idx, gate, cnt = _outproj(p_mix, p_xn, d_mix, d_xn, w_out[l].astype(BF16), row(ln1_g[l]), row(ln1_b[l]),
                                      w_router[l].astype(BF16), row(b_router[l]))
    plan = _route_plan(idx[:n_tok, :TOP_K], cnt[0].astype(I32), n_tok)
    yt = _moe(*plan, xp, w_gate_e[l], w_up_e[l], w_down_e[l], n_tok)
    comb = functools.partial(_combine, x1, yt, gate, w_gate_s[l].astype(BF16), w_up_s[l].astype(BF16),
                             w_down_s[l].astype(BF16), row(ln2_g[l]), row(ln2_b[l]))
    y_prompt = comb(0, nseq * seqlen).reshape(nseq, seqlen, D_MODEL)
    y_sample = comb(nseq * seqlen, ndec).reshape(ndec, 1, D_MODEL)

    state_pool_prompt = p_p.reshape(nseq, seqlen, POOL_WIDTH)[:, seqlen - POOL_BUF:, :]
    state_pool_sample = jnp.concatenate([state_pool[l][:, 1:, :], d_p[:, None, :]], axis=1)
    return (y_prompt, y_sample, s_prompt[None], state_pool_prompt[None], s_dec[None], state_pool_sample[None])
```

```python
import functools

import jax
import jax.numpy as jnp
from jax import lax
from jax.experimental import pallas as pl
from jax.experimental.pallas import tpu as pltpu

F32 = jnp.float32
BF16 = jnp.bfloat16
I32 = jnp.int32

D_MODEL = 1024
N_META = 16
HGRN_WIDTH = 512
HGRN_HEADS = 4
HEAD_DIM = 128
POOL_WIDTH = 512
POOL_WINDOWS = (2, 4, 8, 16)
POOL_GROUP_DIM = 128
POOL_BUF = 15
IN_WIDTH = 4 * HGRN_WIDTH + POOL_WIDTH
N_EXPERTS = 256
TOP_K = 8
D_EXPERT = 256
ROUTED_SCALE = 2.5
DEPTH = 1
ALPHA = (2 * DEPTH) ** 0.25
LN_EPS = 1e-5
RMS_EPS = 1e-6
LOG2_E = 1.4426950408889634

SUB = 16
MIX_BLOCK = 128
MOE_ROWS = 256
TILE_STRIDE = MOE_ROWS + 8
ORDER_ALIGN = 128
PACK_ROWS = D_MODEL // 2 // 128
OUT_TILE = 512
VMEM_LIMIT = 56 * 1024 * 1024


def _ln(x, g, b):
    mu = jnp.mean(x, axis=-1, keepdims=True)
    xc = x - mu
    var = jnp.mean(xc * xc, axis=-1, keepdims=True)
    return xc * lax.rsqrt(var + LN_EPS) * g + b


def _sigmoid(z):
    return 1.0 / (1.0 + jnp.exp(-z))


def _bdot(a, b):
    return jnp.dot(a, b, preferred_element_type=F32)


def _pack_bf16_pairs(x):
    half = x.shape[1] // 2
    xr = x.astype(BF16).astype(F32)
    lo = lax.shift_right_logical(lax.bitcast_convert_type(xr[:, :half], jnp.uint32), jnp.uint32(16))
    hi = lax.bitcast_convert_type(xr[:, half:], jnp.uint32) & jnp.uint32(0xFFFF0000)
    return lax.bitcast_convert_type(hi | lo, I32)


def _unpack_bf16_pairs(words):
    w = lax.bitcast_convert_type(words, jnp.uint32)
    lo = lax.bitcast_convert_type(lax.shift_left(w, jnp.uint32(16)), F32)
    hi = lax.bitcast_convert_type(w & jnp.uint32(0xFFFF0000), F32)
    return lo, hi


def _ln_proj_kernel(x_ref, g_ref, b_ref, w_ref, lbl_ref, ng_ref,
                    xn_ref, q_ref, k_ref, gl_ref, v_ref, gs_ref, p_ref):
    xn = _ln(x_ref[...], g_ref[...], b_ref[...])
    xn_ref[...] = xn
    proj = _bdot(xn.astype(BF16), w_ref[...])
    lbl = lbl_ref[...]
    e = jnp.exp(lbl - jnp.max(lbl, axis=0, keepdims=True))
    lb = e[0:1] / jnp.sum(e, axis=0, keepdims=True)
    W = HGRN_WIDTH
    q = proj[:, 0:W]
    f = proj[:, W:2 * W]
    q_ref[...] = q * _sigmoid(q)
    k_ref[...] = (1.0 - lb) * _sigmoid(-f)
    gl_ref[...] = jnp.log(lb + (1.0 - lb) * _sigmoid(f))
    v_ref[...] = proj[:, 2 * W:3 * W]
    g = proj[:, 3 * W:4 * W]
    gs_ref[...] = ng_ref[...] * (g * _sigmoid(g))
    p_ref[...] = proj[:, 4 * W:]


def _row_tile(n_rows, largest=512):
    for tm in (512, 256, 128, 64, 32, 16, 8):
        if tm > largest:
            continue
        if n_rows % tm == 0:
            return tm
    raise ValueError(f"row count {n_rows} is not a multiple of 8")


def _ln_proj(x, ln_g, ln_b, w_in_bf, lb_logits, norm_g):
    T = x.shape[0]
    tm = _row_tile(T)
    row = lambda i: (i, 0)
    const = lambda i: (0, 0)
    outs = [jax.ShapeDtypeStruct((T, D_MODEL), F32)] + [jax.ShapeDtypeStruct((T, HGRN_WIDTH), F32)] * 6
    return pl.pallas_call(
        _ln_proj_kernel,
        out_shape=outs,
        grid=(T // tm,),
        in_specs=[
            pl.BlockSpec((tm, D_MODEL), row),
            pl.BlockSpec((1, D_MODEL), const),
            pl.BlockSpec((1, D_MODEL), const),
            pl.BlockSpec((D_MODEL, IN_WIDTH), const),
            pl.BlockSpec((DEPTH + 1, HGRN_WIDTH), const),
            pl.BlockSpec((1, HGRN_WIDTH), const),
        ],
        out_specs=[pl.BlockSpec((tm, D_MODEL), row)] + [pl.BlockSpec((tm, HGRN_WIDTH), row)] * 6,
        compiler_params=pltpu.CompilerParams(dimension_semantics=("arbitrary",), vmem_limit_bytes=VMEM_LIMIT),
        name="ln_proj",
    )(x, ln_g, ln_b, w_in_bf, lb_logits, norm_g)


def _pool_group(pe, p_cur, gi, w):
    sl = slice(gi * POOL_GROUP_DIM, (gi + 1) * POOL_GROUP_DIM)
    s = pe[:, sl]
    sh = 1
    while sh < w:
        s = s + pltpu.roll(s, sh, 0)
        sh *= 2
    return s[SUB:, :] * (1.0 / w) - p_cur[:, sl]


def _mixer_kernel(q_ref, k_ref, g_ref, v_ref, gs_ref, p_ref, s0_ref, pp0_ref, wpool_ref, pscale_ref,
                  mix_ref, sfin_ref, st_scr, pe_scr, bf_scr):
    i = pl.program_id(1)
    nblk = pl.num_programs(1)
    n = MIX_BLOCK

    @pl.when(i == 0)
    def _():
        for h in range(HGRN_HEADS):
            st_scr[h] = s0_ref[h].T
        pe_scr[0:SUB, :] = pp0_ref[...]

    rows = lax.broadcasted_iota(I32, (n, HEAD_DIM), 0)
    r16 = rows & (SUB - 1)
    t8 = lax.broadcasted_iota(I32, (8, HEAD_DIM), 0)
    zero_bf = jnp.zeros((SUB, HEAD_DIM), BF16)

    o_heads = []
    for h in range(HGRN_HEADS):
        hs = slice(h * HEAD_DIM, (h + 1) * HEAD_DIM)
        Q = q_ref[:, hs]
        K = k_ref[:, hs]
        G = g_ref[:, hs]
        V = v_ref[:, hs]
        bf = G
        br = G
        for sh in (1, 2, 4, 8):
            bf = bf + jnp.where(r16 >= sh, pltpu.roll(bf, sh, 0), 0.0)
            br = br + jnp.where(r16 < SUB - sh, pltpu.roll(br, n - sh, 0), 0.0)
        br = br - G
        bf2 = bf * LOG2_E
        bf_scr[:, hs] = bf2
        qt = (Q * jnp.exp(bf)).astype(BF16)
        kt = (K * jnp.exp(br)).astype(BF16)
        vt = V.T.astype(BF16)
        st = st_scr[h]
        o_parts = []
        for c in range(n // SUB):
            r0 = c * SUB
            b_top, b_bot = bf2[r0:r0 + 8], bf2[r0 + 8:r0 + 16]
            q_top, q_bot = Q[r0:r0 + 8], Q[r0 + 8:r0 + 16]
            acc_top = jnp.zeros((8, HEAD_DIM), F32)
            acc_bot = jnp.zeros((8, HEAD_DIM), F32)
            for s in range(SUB):
                bs = bf_scr[r0 + s:r0 + s + 1, hs]
                ks = k_ref[r0 + s:r0 + s + 1, hs]
                vs = v_ref[r0 + s:r0 + s + 1, hs]
                if s < 8:
                    col = jnp.sum(q_top * jnp.exp2(b_top - bs) * ks, axis=-1, keepdims=True)
                    col = jnp.where(t8[:, 0:1] >= s, col, 0.0)
                    acc_top = acc_top + col * vs
                    col = jnp.sum(q_bot * jnp.exp2(b_bot - bs) * ks, axis=-1, keepdims=True)
                    acc_bot = acc_bot + col * vs
                else:
                    col = jnp.sum(q_bot * jnp.exp2(b_bot - bs) * ks, axis=-1, keepdims=True)
                    col = jnp.where(t8[:, 0:1] + 8 >= s, col, 0.0)
                    acc_bot = acc_bot + col * vs
            o_diag = jnp.concatenate([acc_top, acc_bot], axis=0)
            o_inter = lax.dot_general(qt[r0:r0 + SUB], st.astype(BF16), (((1,), (1,)), ((), ())),
                                      preferred_element_type=F32)
            o_parts.append(o_inter + o_diag)
            kmask = jnp.concatenate([zero_bf] * c + [kt[r0:r0 + SUB]] + [zero_bf] * (n // SUB - 1 - c), axis=0)
            d_st = _bdot(vt, kmask)
            st = st * jnp.exp(bf[r0 + SUB - 1:r0 + SUB]) + d_st
        st_scr[h] = st
        o = jnp.concatenate(o_parts, axis=0)
        o = o * lax.rsqrt(jnp.mean(o * o, axis=-1, keepdims=True) + RMS_EPS)
        o_heads.append(o * gs_ref[:, hs])

    p_cur = p_ref[...]
    pe_scr[SUB:SUB + n, :] = p_cur
    pe = pe_scr[...]
    ob = []
    for gi, w in enumerate(POOL_WINDOWS):
        pooled = _pool_group(pe, p_cur, gi, w)
        sl = slice(gi * POOL_GROUP_DIM, (gi + 1) * POOL_GROUP_DIM)
        ob.append(_bdot(pooled.astype(BF16), wpool_ref[gi]) * pscale_ref[:, sl])
    pe_scr[0:SUB, :] = p_cur[n - SUB:, :]

    mix_ref[...] = jnp.concatenate(o_heads + ob, axis=1).astype(BF16)

    @pl.when(i == nblk - 1)
    def _():
        for h in range(HGRN_HEADS):
            sfin_ref[0, h] = st_scr[h].T


def _mixer(q, k, g, v, gs, p, s0, pp0, wpool_bf, pscale, nseq, seqlen):
    nblk = seqlen // MIX_BLOCK
    tok = lambda b, i: (b * nblk + i, 0)
    tspec = pl.BlockSpec((MIX_BLOCK, HGRN_WIDTH), tok)
    return pl.pallas_call(
        _mixer_kernel,
        out_shape=[jax.ShapeDtypeStruct((nseq * seqlen, D_MODEL), BF16),
                   jax.ShapeDtypeStruct((nseq, HGRN_HEADS, HEAD_DIM, HEAD_DIM), F32)],
        grid=(nseq, nblk),
        in_specs=[tspec] * 6 + [
            pl.BlockSpec((HGRN_HEADS, HEAD_DIM, HEAD_DIM), lambda b, i: (0, 0, 0)),
            pl.BlockSpec((SUB, POOL_WIDTH), lambda b, i: (0, 0)),
            pl.BlockSpec((len(POOL_WINDOWS), POOL_GROUP_DIM, POOL_GROUP_DIM), lambda b, i: (0, 0, 0)),
            pl.BlockSpec((1, POOL_WIDTH), lambda b, i: (0, 0)),
        ],
        out_specs=[pl.BlockSpec((MIX_BLOCK, D_MODEL), tok),
                   pl.BlockSpec((1, HGRN_HEADS, HEAD_DIM, HEAD_DIM), lambda b, i: (b, 0, 0, 0))],
        scratch_shapes=[pltpu.VMEM((HGRN_HEADS, HEAD_DIM, HEAD_DIM), F32),
                        pltpu.VMEM((SUB + MIX_BLOCK, POOL_WIDTH), F32),
                        pltpu.VMEM((MIX_BLOCK, HGRN_WIDTH), F32)],
        compiler_params=pltpu.CompilerParams(dimension_semantics=("arbitrary", "arbitrary"),
                                             vmem_limit_bytes=VMEM_LIMIT),
        name="mixer",
    )(q, k, g, v, gs, p, s0, pp0, wpool_bf, pscale)


STEP_SEQS = 64


def _mixer_step_kernel(qt_ref, kt_ref, gt_ref, v_ref, gs_ref, s_ref, snew_ref, oa_ref):
    qt = qt_ref[0, 0]
    kt = kt_ref[0, 0]
    dt = jnp.exp(gt_ref[0, 0])
    rows = []
    for bb in range(STEP_SEQS):
        sn = s_ref[bb, 0] * dt[:, bb:bb + 1] + kt[:, bb:bb + 1] * v_ref[bb:bb + 1, :]
        snew_ref[bb, 0] = sn
        rows.append(jnp.sum(sn * qt[:, bb:bb + 1], axis=0, keepdims=True))
    o = jnp.concatenate(rows, axis=0)
    o = o * lax.rsqrt(jnp.mean(o * o, axis=-1, keepdims=True) + RMS_EPS)
    oa_ref[...] = o * gs_ref[...]


def _mixer_step(qT, kT, gT, v, gs, state):
    nseq = v.shape[0]
    nbc = nseq // STEP_SEQS
    cspec = pl.BlockSpec((1, 1, HEAD_DIM, STEP_SEQS), lambda h, c: (h, c, 0, 0))
    rspec = pl.BlockSpec((STEP_SEQS, HEAD_DIM), lambda h, c: (c, h))
    sspec = pl.BlockSpec((STEP_SEQS, 1, HEAD_DIM, HEAD_DIM), lambda h, c: (c, h, 0, 0))
    return pl.pallas_call(
        _mixer_step_kernel,
        out_shape=[jax.ShapeDtypeStruct(state.shape, F32), jax.ShapeDtypeStruct((nseq, HGRN_WIDTH), F32)],
        grid=(HGRN_HEADS, nbc),
        in_specs=[cspec, cspec, cspec, rspec, rspec, sspec],
        out_specs=[sspec, rspec],
        compiler_params=pltpu.CompilerParams(dimension_semantics=("arbitrary", "arbitrary"),
                                             vmem_limit_bytes=VMEM_LIMIT),
        name="mixer_step",
    )(qT, kT, gT, v, gs, state)


def _pool_step_kernel(sp_ref, p_ref, wpool_ref, pscale_ref, ob_ref):
    p_cur = p_ref[...]
    outs = []
    for gi, w in enumerate(POOL_WINDOWS):
        sl = slice(gi * POOL_GROUP_DIM, (gi + 1) * POOL_GROUP_DIM)
        s = p_cur[:, sl]
        for r in range(POOL_BUF - (w - 1), POOL_BUF):
            s = s + sp_ref[r][:, sl]
        pooled = s * (1.0 / w) - p_cur[:, sl]
        outs.append(_bdot(pooled.astype(BF16), wpool_ref[gi]) * pscale_ref[:, sl])
    ob_ref[...] = jnp.concatenate(outs, axis=1)


def _pool_step(spT, p, wpool_bf, pscale):
    nseq = p.shape[0]
    return pl.pallas_call(
        _pool_step_kernel,
        out_shape=jax.ShapeDtypeStruct((nseq, POOL_WIDTH), F32),
        name="pool_step",
    )(spT, p, wpool_bf, pscale)


def _outproj_kernel(mixp_ref, xnp_ref, mixd_ref, xnd_ref, wout_ref, g1_ref, b1_ref, wr_ref, br_ref,
                    x1_ref, x1p_ref, idx_ref, gate_ref, cnt_ref, *, n_prompt_blocks, n_valid_last):
    tm = mixp_ref.shape[0]
    i = pl.program_id(0)
    is_prompt = i < n_prompt_blocks
    mix_in = jnp.where(is_prompt, mixp_ref[...], mixd_ref[...])
    xn = jnp.where(is_prompt, xnp_ref[...], xnd_ref[...])
    mix = _bdot(mix_in, wout_ref[...])
    x1 = _ln(ALPHA * xn + mix, g1_ref[...], b1_ref[...])
    x1_ref[...] = x1
    xb = x1.astype(BF16)
    words = _pack_bf16_pairs(x1)
    for c in range(PACK_ROWS):
        x1p_ref[pl.ds(c, tm, stride=PACK_ROWS), :] = words[:, c * 128:(c + 1) * 128]

    scores = _sigmoid(_bdot(xb, wr_ref[...]))
    sel = scores + br_ref[...]
    lane = lax.broadcasted_iota(I32, (tm, N_EXPERTS), 1).astype(F32)
    lane_o = lax.broadcasted_iota(I32, (tm, 128), 1)
    idx_o = jnp.zeros((tm, 128), F32)
    ssum = jnp.zeros((tm, 1), F32)
    chosen = jnp.zeros((tm, N_EXPERTS), F32)
    s_sel = []
    for j in range(TOP_K):
        m = jnp.max(sel, axis=-1, keepdims=True)
        am = jnp.min(jnp.where(sel == m, lane, float(N_EXPERTS)), axis=-1, keepdims=True)
        hit = lane == am
        sj = jnp.sum(jnp.where(hit, scores, 0.0), axis=-1, keepdims=True)
        sel = jnp.where(hit, -jnp.inf, sel)
        chosen = jnp.where(hit, 1.0, chosen)
        idx_o = jnp.where(lane_o == j, am, idx_o)
        s_sel.append(sj)
        ssum = ssum + sj
    idx_ref[...] = idx_o.astype(I32)
    gates = jnp.zeros((tm, 128), F32)
    for j in range(TOP_K):
        gates = jnp.where(lane_o == j, s_sel[j] / ssum * ROUTED_SCALE, gates)
    gate_ref[...] = gates

    @pl.when(i == 0)
    def _():
        cnt_ref[...] = jnp.zeros_like(cnt_ref)

    row_id = lax.broadcasted_iota(I32, (tm, N_EXPERTS), 0)
    valid = jnp.logical_or(is_prompt, row_id < n_valid_last)
    cnt_ref[...] += jnp.sum(jnp.where(valid, chosen, 0.0), axis=0, keepdims=True)


def _outproj(mix_p, xn_p, mix_d, xn_d, wout_bf, g1, b1, wr_bf, br):
    tm = OUT_TILE
    n_prompt, n_dec = mix_p.shape[0], mix_d.shape[0]
    assert n_prompt % tm == 0 and n_dec <= tm and n_dec % 8 == 0
    nbp = n_prompt // tm
    T = (nbp + 1) * tm
    padrows = lambda a: jnp.pad(a, ((0, tm - n_dec), (0, 0)))
    row = lambda i: (i, 0)
    prow = lambda i: (jnp.minimum(i, nbp - 1), 0)
    const = lambda i: (0, 0)
    return pl.pallas_call(
        functools.partial(_outproj_kernel, n_prompt_blocks=nbp, n_valid_last=n_dec),
        out_shape=[jax.ShapeDtypeStruct((T, D_MODEL), F32), jax.ShapeDtypeStruct((T * PACK_ROWS, 128), I32),
                   jax.ShapeDtypeStruct((T, 128), I32), jax.ShapeDtypeStruct((T, 128), F32),
                   jax.ShapeDtypeStruct((1, N_EXPERTS), F32)],
        grid=(nbp + 1,),
        in_specs=[pl.BlockSpec((tm, D_MODEL), prow), pl.BlockSpec((tm, D_MODEL), prow),
                  pl.BlockSpec((tm, D_MODEL), const), pl.BlockSpec((tm, D_MODEL), const),
                  pl.BlockSpec((D_MODEL, D_MODEL), const), pl.BlockSpec((1, D_MODEL), const),
                  pl.BlockSpec((1, D_MODEL), const), pl.BlockSpec((D_MODEL, N_EXPERTS), const),
                  pl.BlockSpec((1, N_EXPERTS), const)],
        out_specs=[pl.BlockSpec((tm, D_MODEL), row), pl.BlockSpec((tm * PACK_ROWS, 128), row),
                   pl.BlockSpec((tm, 128), row), pl.BlockSpec((tm, 128), row),
                   pl.BlockSpec((1, N_EXPERTS), const)],
        compiler_params=pltpu.CompilerParams(dimension_semantics=("arbitrary",), vmem_limit_bytes=VMEM_LIMIT),
        name="outproj",
    )(mix_p, xn_p, padrows(mix_d), padrows(xn_d), wout_bf, g1, b1, wr_bf, br)


def _moe_kernel(eb0_ref, enb_ref, wsel_ref, tot_ref, bn_ref, bsrc_ref, bdel_ref,
                order_ref, tokrow_ref, xp_ref, wg_ref, wu_ref, wd_ref, yt_ref,
                tile_scr, ybuf, wgb, wub, wdb, ord_smem, tok_smem, sem, sem_o, sem_t):
    del wsel_ref
    e = pl.program_id(0)
    g0 = eb0_ref[e]
    nblk = enb_ref[e]
    total = tot_ref[0]
    nb_max = bn_ref.shape[0]
    R = MOE_ROWS
    C = PACK_ROWS
    L = R + ORDER_ALIGN
    n_assign = yt_ref.shape[0] // C - 2 * TOP_K

    class _Fetch:
        def __init__(self, g):
            gc = jnp.minimum(g, nb_max - 1)
            s = g % 4
            src = pl.multiple_of(bsrc_ref[gc], ORDER_ALIGN)
            dst = pl.multiple_of(s * L, ORDER_ALIGN)
            self.copies = (
                pltpu.make_async_copy(order_ref.at[pl.ds(src, L)], ord_smem.at[pl.ds(dst, L)], sem_o.at[s]),
                pltpu.make_async_copy(tokrow_ref.at[pl.ds(src, L)], tok_smem.at[pl.ds(dst, L)], sem_t.at[s]))

        def start(self):
            for c in self.copies:
                c.start()

        def wait(self):
            for c in self.copies:
                c.wait()

    fetch = _Fetch

    def list_base(g):
        return (g % 4) * L + bdel_ref[jnp.minimum(g, nb_max - 1)]

    def gather(g):
        base = list_base(g)
        trow = (g % 2) * (PACK_ROWS * TILE_STRIDE)
        for r in range(R):
            t4 = pl.multiple_of(tok_smem[base + r], PACK_ROWS)
            tile_scr[pl.ds(trow + r, PACK_ROWS, stride=TILE_STRIDE), :] = xp_ref[pl.ds(t4, PACK_ROWS), :]

    def rows_sent(cnt):
        return pl.multiple_of(lax.shift_left(lax.shift_right_logical(cnt + 7, 3), 3), 8)

    def wait_rows(s, cnt):
        pltpu.make_async_copy(ybuf.at[s, pl.ds(0, cnt * C), :], yt_ref.at[pl.ds(0, cnt * C), :], sem.at[s]).wait()

    @pl.when(e == 0)
    def _():
        n_spare = 2 * TOP_K * C
        ybuf[0, 0:n_spare, :] = jnp.zeros((n_spare, 128), I32)
        init = pltpu.make_async_copy(ybuf.at[0, pl.ds(0, n_spare), :],
                                     yt_ref.at[pl.ds(yt_ref.shape[0] - n_spare, n_spare), :], sem.at[0])
        init.start()
        init.wait()
        fetch(0).start()
        fetch(1).start()
        fetch(0).wait()
        gather(0)

    @pl.when(nblk > 0)
    def _():
        wgb[...] = wg_ref[0].astype(BF16)
        wub[...] = wu_ref[0].astype(BF16)
        wdb[...] = wd_ref[0].astype(BF16)

    def block(k, carry):
        g = g0 + k
        slot = g % 2
        n = bn_ref[g]
        fetch(g + 1).wait()
        fetch(g + 2).start()

        @pl.when(g >= 2)
        def _():
            wait_rows(slot, rows_sent(bn_ref[jnp.maximum(g - 2, 0)]))

        trow = pl.multiple_of(slot * (PACK_ROWS * TILE_STRIDE), 8)

        def compute(send_prev):
            los, his = [], []
            for j in range(PACK_ROWS):
                lo, hi = _unpack_bf16_pairs(tile_scr[pl.ds(trow + j * TILE_STRIDE, R), :])
                los.append(lo.astype(BF16))
                his.append(hi.astype(BF16))
            xg = jnp.concatenate(los + his, axis=1)
            gather(g + 1)
            if send_prev:
                pbase = list_base(g - 1)
                for r in range(R):
                    send(1 - slot, r, ord_smem[pbase + r], r)
            hg = _bdot(xg, wgb[...])
            hu = _bdot(xg, wub[...])
            hb = (hg * _sigmoid(hg)) * hu
            y = _bdot(hb.astype(BF16), wdb[...])
            words = _pack_bf16_pairs(y)
            yb = ybuf.at[slot]
            for c in range(C):
                yb[pl.ds(c, R, stride=C), :] = words[:, c * 128:(c + 1) * 128]

        prev_full = jnp.logical_and(g >= 1, bn_ref[jnp.maximum(g - 1, 0)] == R)

        @pl.when(prev_full)
        def _():
            compute(True)

        @pl.when(jnp.logical_not(prev_full))
        def _():
            compute(False)

        @pl.when(n < R)
        def _():
            send_rows(g)

        return carry

    def send(s, r, d, u):
        pltpu.make_async_copy(ybuf.at[s, pl.ds(pl.multiple_of(r * C, C), C), :],
                              yt_ref.at[pl.ds(pl.multiple_of(d * C, C), C), :], sem.at[s]).start(priority=u % 2)

    def send_rows(g):
        slot = g % 2
        n = bn_ref[g]
        base = list_base(g)

        def send_group(i, c2):
            for u in range(8):
                send(slot, i * 8 + u, ord_smem[base + i * 8 + u], u)
            return c2

        n_full = lax.shift_right_logical(n, 3)
        lax.fori_loop(0, n_full, send_group, 0)

        @pl.when(n_full * 8 < n)
        def _():
            for u in range(8):
                r = n_full * 8 + u
                send(slot, r, jnp.where(r < n, ord_smem[base + r], n_assign + slot * 8 + u), u)

    lax.fori_loop(0, nblk, block, 0)

    @pl.when(e == pl.num_programs(0) - 1)
    def _():
        @pl.when(bn_ref[jnp.maximum(total - 1, 0)] == R)
        def _():
            send_rows(total - 1)

        fetch(total + 1).wait()
        wait_rows((total - 1) % 2, rows_sent(bn_ref[jnp.maximum(total - 1, 0)]))

        @pl.when(total >= 2)
        def _():
            wait_rows(total % 2, rows_sent(bn_ref[jnp.maximum(total - 2, 0)]))


def _moe(ex_b0, ex_nb, ex_w, total, blk_n, blk_src, blk_delta, order, tokrow, xp, wg, wu, wd, n_tok):
    R = MOE_ROWS
    wmap_in = lambda e, eb0, enb, wsel, tot, bn, bs, bd: (wsel[e], 0, 0)
    grid_spec = pltpu.PrefetchScalarGridSpec(
        num_scalar_prefetch=7,
        grid=(N_EXPERTS,),
        in_specs=[
            pl.BlockSpec(memory_space=pl.ANY),
            pl.BlockSpec(memory_space=pl.ANY),
            pl.BlockSpec(memory_space=pltpu.VMEM),
            pl.BlockSpec((1, D_MODEL, D_EXPERT), wmap_in),
            pl.BlockSpec((1, D_MODEL, D_EXPERT), wmap_in),
            pl.BlockSpec((1, D_EXPERT, D_MODEL), wmap_in),
        ],
        out_specs=pl.BlockSpec(memory_space=pl.ANY),
        scratch_shapes=[pltpu.VMEM((2 * PACK_ROWS * TILE_STRIDE, 128), I32),
                        pltpu.VMEM((2, R * PACK_ROWS, 128), I32),
                        pltpu.VMEM((D_MODEL, D_EXPERT), BF16),
                        pltpu.VMEM((D_MODEL, D_EXPERT), BF16),
                        pltpu.VMEM((D_EXPERT, D_MODEL), BF16),
                        pltpu.SMEM((4 * (R + ORDER_ALIGN),), I32),
                        pltpu.SMEM((4 * (R + ORDER_ALIGN),), I32),
                        pltpu.SemaphoreType.DMA((2,)),
                        pltpu.SemaphoreType.DMA((4,)),
                        pltpu.SemaphoreType.DMA((4,))],
    )
    return pl.pallas_call(
        _moe_kernel,
        out_shape=jax.ShapeDtypeStruct(((n_tok + 2) * TOP_K * PACK_ROWS, 128), I32),
        grid_spec=grid_spec,
        compiler_params=pltpu.CompilerParams(dimension_semantics=("arbitrary",),
                                             vmem_limit_bytes=58 * 1024 * 1024),
        name="moe",
    )(ex_b0, ex_nb, ex_w, total, blk_n, blk_src, blk_delta, order, tokrow, xp, wg, wu, wd)


def _route_plan(idx, counts, n_tok):
    R = MOE_ROWS
    n_assign = n_tok * TOP_K
    nb = (n_assign + N_EXPERTS * (R - 1)) // R
    id_bits = (n_assign - 1).bit_length()
    assert id_bits + (N_EXPERTS - 1).bit_length() < 32
    key = lax.shift_left(idx.reshape(-1), id_bits) | jnp.arange(n_assign, dtype=I32)
    order = lax.sort(key) & ((1 << id_bits) - 1)
    order = jnp.concatenate([order, jnp.zeros((R + ORDER_ALIGN,), I32)])
    nblk_e = (counts + R - 1) // R
    bend = jnp.cumsum(nblk_e)
    bstart = bend - nblk_e
    cstart = jnp.cumsum(counts) - counts
    blk = jnp.arange(nb, dtype=I32)
    blk_e = jnp.minimum(jnp.sum((bend[None, :] <= blk[:, None]).astype(I32), axis=1), N_EXPERTS - 1)
    k = blk - bstart[blk_e]
    active = blk < bend[-1]
    blk_n = jnp.where(active, jnp.clip(counts[blk_e] - k * R, 0, R), 0).astype(I32)
    src = jnp.where(active, cstart[blk_e] + k * R, 0).astype(I32)
    blk_src = (src // ORDER_ALIGN) * ORDER_ALIGN
    ex = jnp.arange(N_EXPERTS, dtype=I32)
    ex_w = jnp.maximum(lax.cummax(jnp.where(nblk_e > 0, ex, -1)), 0).astype(I32)
    tokrow = (order // TOP_K) * PACK_ROWS
    return (bstart.astype(I32), nblk_e.astype(I32), ex_w, bend[-1:].astype(I32),
            blk_n, blk_src, src - blk_src, order, tokrow)


def _combine_kernel(x1_ref, yt_ref, gate_ref, wgs_ref, wus_ref, wds_ref, g2_ref, b2_ref, out_ref):
    tm = x1_ref.shape[0]
    x1 = x1_ref[...]
    xb = x1.astype(BF16)
    hg = _bdot(xb, wgs_ref[...])
    hs = (hg * _sigmoid(hg)) * _bdot(xb, wus_ref[...])
    moe = _bdot(hs.astype(BF16), wds_ref[...])
    gate = gate_ref[...]
    per_tok = TOP_K * PACK_ROWS
    planes = pltpu.einshape("tjl->jtl", yt_ref[...].reshape(tm, per_tok, 128))
    lo_acc = [None] * PACK_ROWS
    hi_acc = [None] * PACK_ROWS
    for j in range(TOP_K):
        gj = gate[:, j:j + 1]
        for c in range(PACK_ROWS):
            lo, hi = _unpack_bf16_pairs(planes[j * PACK_ROWS + c])
            lo_acc[c] = lo * gj if j == 0 else lo_acc[c] + lo * gj
            hi_acc[c] = hi * gj if j == 0 else hi_acc[c] + hi * gj
    routed = jnp.concatenate(lo_acc + hi_acc, axis=1)
    out_ref[...] = _ln(ALPHA * x1 + (moe + routed), g2_ref[...], b2_ref[...])


def _combine(x1, yt, gate, wgs_bf, wus_bf, wds_bf, g2, b2, row0, nrows):
    tm = _row_tile(nrows)
    assert row0 % tm == 0
    off = row0 // tm
    row = lambda i: (i + off, 0)
    const = lambda i: (0, 0)
    return pl.pallas_call(
        _combine_kernel,
        out_shape=jax.ShapeDtypeStruct((nrows, D_MODEL), F32),
        grid=(nrows // tm,),
        in_specs=[pl.BlockSpec((tm, D_MODEL), row),
                  pl.BlockSpec((tm * TOP_K * PACK_ROWS, 128), row),
                  pl.BlockSpec((tm, 128), row),
                  pl.BlockSpec((D_MODEL, D_EXPERT), const), pl.BlockSpec((D_MODEL, D_EXPERT), const),
                  pl.BlockSpec((D_EXPERT, D_MODEL), const),
                  pl.BlockSpec((1, D_MODEL), const), pl.BlockSpec((1, D_MODEL), const)],
        out_specs=pl.BlockSpec((tm, D_MODEL), lambda i: (i, 0)),
        compiler_params=pltpu.CompilerParams(dimension_semantics=("arbitrary",), vmem_limit_bytes=VMEM_LIMIT),
        name="combine",
    )(x1, yt, gate, wgs_bf, wus_bf, wds_bf, g2, b2)


def kernel(x_prompt, x_sample, state_hgrn, state_pool, meta_tokens, ln_emb_g, ln_emb_b, w_in, lb_logits, hgrn_norm_g, w_pool, pool_scale, w_out, ln1_g, ln1_b, w_router, b_router, w_gate_e, w_up_e, w_down_e, w_gate_s, w_up_s, w_down_s, ln2_g, ln2_b):
    nseq, seqlen, _ = x_prompt.shape
    ndec = x_sample.shape[0]
    l = 0
    row = lambda a: a.reshape(1, -1)
    w_in_bf = w_in[l].astype(BF16)
    wpool_bf = w_pool[l].astype(BF16)
    lng, lnb = row(ln_emb_g), row(ln_emb_b)
    ng, ps = row(hgrn_norm_g[l]), row(pool_scale[l])
    proj = functools.partial(_ln_proj, ln_g=lng, ln_b=lnb, w_in_bf=w_in_bf, lb_logits=lb_logits, norm_g=ng)

    m_xn, m_q, m_k, m_g, m_v, m_gs, m_p = proj(meta_tokens)
    pad = lambda a: jnp.pad(a, ((0, MIX_BLOCK - N_META), (0, 0)))
    zero_state = jnp.zeros((HGRN_HEADS, HEAD_DIM, HEAD_DIM), F32)
    _, s_meta = _mixer(pad(m_q), pad(m_k), pad(m_g), pad(m_v), pad(m_gs), pad(m_p), zero_state,
                       jnp.zeros((SUB, POOL_WIDTH), F32), wpool_bf, ps, 1, MIX_BLOCK)

    p_xn, p_q, p_k, p_g, p_v, p_gs, p_p = proj(x_prompt.reshape(nseq * seqlen, D_MODEL))
    p_mix, s_prompt = _mixer(p_q, p_k, p_g, p_v, p_gs, p_p, s_meta[0], m_p, wpool_bf, ps, nseq, seqlen)

    d_xn, d_q, d_k, d_g, d_v, d_gs, d_p = proj(x_sample.reshape(ndec, D_MODEL))
    cols = lambda a: a.reshape(ndec // STEP_SEQS, STEP_SEQS, HGRN_HEADS, HEAD_DIM).transpose(2, 0, 3, 1)
    s_dec, d_oa = _mixer_step(cols(d_q), cols(d_k), cols(d_g), d_v, d_gs, state_hgrn[l])
    d_ob = _pool_step(state_pool[l].transpose(1, 0, 2), d_p, wpool_bf, ps)
    d_mix = jnp.concatenate([d_oa, d_ob], axis=1).astype(BF16)

    n_tok = nseq * seqlen + ndec
    x1, xp, idx, gate, cnt = _outproj(p_mix, p_xn, d_mix, d_xn, w_out[l].astype(BF16), row(ln1_g[l]), row(ln1_b[l]),
                                      w_router[l].astype(BF16), row(b_router[l]))
    plan = _route_plan(idx[:n_tok, :TOP_K], cnt[0].astype(I32), n_tok)
    yt = _moe(*plan, xp, w_gate_e[l], w_up_e[l], w_down_e[l], n_tok)
    comb = functools.partial(_combine, x1, yt, gate, w_gate_s[l].astype(BF16), w_up_s[l].astype(BF16),
                             w_down_s[l].astype(BF16), row(ln2_g[l]), row(ln2_b[l]))
    y_prompt = comb(0, nseq * seqlen).reshape(nseq, seqlen, D_MODEL)
    y_sample = comb(nseq * seqlen, ndec).reshape(ndec, 1, D_MODEL)

    state_pool_prompt = p_p.reshape(nseq, seqlen, POOL_WIDTH)[:, seqlen - POOL_BUF:, :]
    state_pool_sample = jnp.concatenate([state_pool[l][:, 1:, :], d_p[:, None, :]], axis=1)
    return (y_prompt, y_sample, s_prompt[None], state_pool_prompt[None], s_dec[None], state_pool_sample[None])
```

```python
import functools

import jax
import jax.numpy as jnp
from jax import lax
from jax.experimental import pallas as pl
from jax.experimental.pallas import tpu as pltpu

F32 = jnp.float32
BF16 = jnp.bfloat16
I32 = jnp.int32

D_MODEL = 1024
N_META = 16
HGRN_WIDTH = 512
HGRN_HEADS = 4
HEAD_DIM = 128
POOL_WIDTH = 512
POOL_WINDOWS = (2, 4, 8, 16)
POOL_GROUP_DIM = 128
POOL_BUF = 15
IN_WIDTH = 4 * HGRN_WIDTH + POOL_WIDTH
N_EXPERTS = 256
TOP_K = 8
D_EXPERT = 256
ROUTED_SCALE = 2.5
DEPTH = 1
ALPHA = (2 * DEPTH) ** 0.25
LN_EPS = 1e-5
RMS_EPS = 1e-6
LOG2_E = 1.4426950408889634

SUB = 16
MIX_BLOCK = 128
MOE_ROWS = 256
TILE_STRIDE = MOE_ROWS + 8
ORDER_ALIGN = 128
PACK_ROWS = D_MODEL // 2 // 128
OUT_TILE = 512
V7X_VMEM_BYTES = 64 * 1024 * 1024
VMEM_LIMIT = V7X_VMEM_BYTES - 8 * 1024 * 1024
MOE_VMEM_LIMIT = V7X_VMEM_BYTES - 6 * 1024 * 1024


def _ln(x, g, b):
    mu = jnp.mean(x, axis=-1, keepdims=True)
    xc = x - mu
    var = jnp.mean(xc * xc, axis=-1, keepdims=True)
    return xc * lax.rsqrt(var + LN_EPS) * g + b


def _sigmoid(z):
    return 1.0 / (1.0 + jnp.exp(-z))


def _bdot(a, b):
    return jnp.dot(a, b, preferred_element_type=F32)


def _pack_bf16_pairs(x):
    half = x.shape[1] // 2
    xr = x.astype(BF16).astype(F32)
    lo = lax.shift_right_logical(lax.bitcast_convert_type(xr[:, :half], jnp.uint32), jnp.uint32(16))
    hi = lax.bitcast_convert_type(xr[:, half:], jnp.uint32) & jnp.uint32(0xFFFF0000)
    return lax.bitcast_convert_type(hi | lo, I32)


def _unpack_bf16_pairs(words):
    w = lax.bitcast_convert_type(words, jnp.uint32)
    lo = lax.bitcast_convert_type(lax.shift_left(w, jnp.uint32(16)), F32)
    hi = lax.bitcast_convert_type(w & jnp.uint32(0xFFFF0000), F32)
    return lo, hi


def _ln_proj_kernel(x_ref, g_ref, b_ref, w_ref, lbl_ref, ng_ref,
                    xn_ref, q_ref, k_ref, gl_ref, v_ref, gs_ref, p_ref):
    xn = _ln(x_ref[...], g_ref[...], b_ref[...])
    xn_ref[...] = xn
    proj = _bdot(xn.astype(BF16), w_ref[...])
    lbl = lbl_ref[...]
    e = jnp.exp(lbl - jnp.max(lbl, axis=0, keepdims=True))
    lb = e[0:1] / jnp.sum(e, axis=0, keepdims=True)
    W = HGRN_WIDTH
    q = proj[:, 0:W]
    f = proj[:, W:2 * W]
    q_ref[...] = q * _sigmoid(q)
    k_ref[...] = (1.0 - lb) * _sigmoid(-f)
    gl_ref[...] = jnp.log(lb + (1.0 - lb) * _sigmoid(f))
    v_ref[...] = proj[:, 2 * W:3 * W]
    g = proj[:, 3 * W:4 * W]
    gs_ref[...] = ng_ref[...] * (g * _sigmoid(g))
    p_ref[...] = proj[:, 4 * W:]


def _row_tile(n_rows):
    for tm in (512, 256, 128, 64, 32, 16, 8):
        if n_rows % tm == 0:
            return tm
    raise ValueError(f"row count {n_rows} is not a multiple of 8")


def _ln_proj(x, ln_g, ln_b, w_in_bf, lb_logits, norm_g):
    T = x.shape[0]
    tm = _row_tile(T)
    row = lambda i: (i, 0)
    const = lambda i: (0, 0)
    outs = [jax.ShapeDtypeStruct((T, D_MODEL), F32)] + [jax.ShapeDtypeStruct((T, HGRN_WIDTH), F32)] * 6
    return pl.pallas_call(
        _ln_proj_kernel,
        out_shape=outs,
        grid=(T // tm,),
        in_specs=[
            pl.BlockSpec((tm, D_MODEL), row),
            pl.BlockSpec((1, D_MODEL), const),
            pl.BlockSpec((1, D_MODEL), const),
            pl.BlockSpec((D_MODEL, IN_WIDTH), const),
            pl.BlockSpec((DEPTH + 1, HGRN_WIDTH), const),
            pl.BlockSpec((1, HGRN_WIDTH), const),
        ],
        out_specs=[pl.BlockSpec((tm, D_MODEL), row)] + [pl.BlockSpec((tm, HGRN_WIDTH), row)] * 6,
        compiler_params=pltpu.CompilerParams(dimension_semantics=("arbitrary",), vmem_limit_bytes=VMEM_LIMIT),
        name="ln_proj",
    )(x, ln_g, ln_b, w_in_bf, lb_logits, norm_g)


def _pool_group(pe, p_cur, gi, w):
    sl = slice(gi * POOL_GROUP_DIM, (gi + 1) * POOL_GROUP_DIM)
    s = pe[:, sl]
    sh = 1
    while sh < w:
        s = s + pltpu.roll(s, sh, 0)
        sh *= 2
    return s[SUB:, :] * (1.0 / w) - p_cur[:, sl]


def _mixer_kernel(q_ref, k_ref, g_ref, v_ref, gs_ref, p_ref, s0_ref, pp0_ref, wpool_ref, pscale_ref,
                  mix_ref, sfin_ref, st_scr, pe_scr, bf_scr):
    i = pl.program_id(1)
    nblk = pl.num_programs(1)
    n = MIX_BLOCK

    @pl.when(i == 0)
    def _():
        for h in range(HGRN_HEADS):
            st_scr[h] = s0_ref[h].T
        pe_scr[0:SUB, :] = pp0_ref[...]

    rows = lax.broadcasted_iota(I32, (n, HEAD_DIM), 0)
    r16 = rows & (SUB - 1)
    t8 = lax.broadcasted_iota(I32, (8, HEAD_DIM), 0)
    zero_bf = jnp.zeros((SUB, HEAD_DIM), BF16)

    o_heads = []
    for h in range(HGRN_HEADS):
        hs = slice(h * HEAD_DIM, (h + 1) * HEAD_DIM)
        Q = q_ref[:, hs]
        K = k_ref[:, hs]
        G = g_ref[:, hs]
        V = v_ref[:, hs]
        bf = G
        br = G
        for sh in (1, 2, 4, 8):
            bf = bf + jnp.where(r16 >= sh, pltpu.roll(bf, sh, 0), 0.0)
            br = br + jnp.where(r16 < SUB - sh, pltpu.roll(br, n - sh, 0), 0.0)
        br = br - G
        bf2 = bf * LOG2_E
        bf_scr[:, hs] = bf2
        qt = (Q * jnp.exp(bf)).astype(BF16)
        kt = (K * jnp.exp(br)).astype(BF16)
        vt = V.T.astype(BF16)
        st = st_scr[h]
        o_parts = []
        for c in range(n // SUB):
            r0 = c * SUB
            b_top, b_bot = bf2[r0:r0 + 8], bf2[r0 + 8:r0 + 16]
            q_top, q_bot = Q[r0:r0 + 8], Q[r0 + 8:r0 + 16]
            acc_top = jnp.zeros((8, HEAD_DIM), F32)
            acc_bot = jnp.zeros((8, HEAD_DIM), F32)
            for s in range(SUB):
                bs = bf_scr[r0 + s:r0 + s + 1, hs]
                ks = k_ref[r0 + s:r0 + s + 1, hs]
                vs = v_ref[r0 + s:r0 + s + 1, hs]
                if s < 8:
                    col = jnp.sum(q_top * jnp.exp2(b_top - bs) * ks, axis=-1, keepdims=True)
                    col = jnp.where(t8[:, 0:1] >= s, col, 0.0)
                    acc_top = acc_top + col * vs
                    col = jnp.sum(q_bot * jnp.exp2(b_bot - bs) * ks, axis=-1, keepdims=True)
                    acc_bot = acc_bot + col * vs
                else:
                    col = jnp.sum(q_bot * jnp.exp2(b_bot - bs) * ks, axis=-1, keepdims=True)
                    col = jnp.where(t8[:, 0:1] + 8 >= s, col, 0.0)
                    acc_bot = acc_bot + col * vs
            o_diag = jnp.concatenate([acc_top, acc_bot], axis=0)
            o_inter = lax.dot_general(qt[r0:r0 + SUB], st.astype(BF16), (((1,), (1,)), ((), ())),
                                      preferred_element_type=F32)
            o_parts.append(o_inter + o_diag)
            kmask = jnp.concatenate([zero_bf] * c + [kt[r0:r0 + SUB]] + [zero_bf] * (n // SUB - 1 - c), axis=0)
            d_st = _bdot(vt, kmask)
            st = st * jnp.exp(bf[r0 + SUB - 1:r0 + SUB]) + d_st
        st_scr[h] = st
        o = jnp.concatenate(o_parts, axis=0)
        o = o * lax.rsqrt(jnp.mean(o * o, axis=-1, keepdims=True) + RMS_EPS)
        o_heads.append(o * gs_ref[:, hs])

    p_cur = p_ref[...]
    pe_scr[SUB:SUB + n, :] = p_cur
    pe = pe_scr[...]
    ob = []
    for gi, w in enumerate(POOL_WINDOWS):
        pooled = _pool_group(pe, p_cur, gi, w)
        sl = slice(gi * POOL_GROUP_DIM, (gi + 1) * POOL_GROUP_DIM)
        ob.append(_bdot(pooled.astype(BF16), wpool_ref[gi]) * pscale_ref[:, sl])
    pe_scr[0:SUB, :] = p_cur[n - SUB:, :]

    mix_ref[...] = jnp.concatenate(o_heads + ob, axis=1).astype(BF16)

    @pl.when(i == nblk - 1)
    def _():
        for h in range(HGRN_HEADS):
            sfin_ref[0, h] = st_scr[h].T


def _mixer(q, k, g, v, gs, p, s0, pp0, wpool_bf, pscale, nseq, seqlen):
    nblk = seqlen // MIX_BLOCK
    tok = lambda b, i: (b * nblk + i, 0)
    tspec = pl.BlockSpec((MIX_BLOCK, HGRN_WIDTH), tok)
    return pl.pallas_call(
        _mixer_kernel,
        out_shape=[jax.ShapeDtypeStruct((nseq * seqlen, D_MODEL), BF16),
                   jax.ShapeDtypeStruct((nseq, HGRN_HEADS, HEAD_DIM, HEAD_DIM), F32)],
        grid=(nseq, nblk),
        in_specs=[tspec] * 6 + [
            pl.BlockSpec((HGRN_HEADS, HEAD_DIM, HEAD_DIM), lambda b, i: (0, 0, 0)),
            pl.BlockSpec((SUB, POOL_WIDTH), lambda b, i: (0, 0)),
            pl.BlockSpec((len(POOL_WINDOWS), POOL_GROUP_DIM, POOL_GROUP_DIM), lambda b, i: (0, 0, 0)),
            pl.BlockSpec((1, POOL_WIDTH), lambda b, i: (0, 0)),
        ],
        out_specs=[pl.BlockSpec((MIX_BLOCK, D_MODEL), tok),
                   pl.BlockSpec((1, HGRN_HEADS, HEAD_DIM, HEAD_DIM), lambda b, i: (b, 0, 0, 0))],
        scratch_shapes=[pltpu.VMEM((HGRN_HEADS, HEAD_DIM, HEAD_DIM), F32),
                        pltpu.VMEM((SUB + MIX_BLOCK, POOL_WIDTH), F32),
                        pltpu.VMEM((MIX_BLOCK, HGRN_WIDTH), F32)],
        compiler_params=pltpu.CompilerParams(dimension_semantics=("arbitrary", "arbitrary"),
                                             vmem_limit_bytes=VMEM_LIMIT),
        name="mixer",
    )(q, k, g, v, gs, p, s0, pp0, wpool_bf, pscale)


STEP_SEQS = 64


def _mixer_step_kernel(qt_ref, kt_ref, gt_ref, v_ref, gs_ref, s_ref, snew_ref, oa_ref):
    qt = qt_ref[0, 0]
    kt = kt_ref[0, 0]
    dt = jnp.exp(gt_ref[0, 0])
    rows = []
    for bb in range(STEP_SEQS):
        sn = s_ref[bb, 0] * dt[:, bb:bb + 1] + kt[:, bb:bb + 1] * v_ref[bb:bb + 1, :]
        snew_ref[bb, 0] = sn
        rows.append(jnp.sum(sn * qt[:, bb:bb + 1], axis=0, keepdims=True))
    o = jnp.concatenate(rows, axis=0)
    o = o * lax.rsqrt(jnp.mean(o * o, axis=-1, keepdims=True) + RMS_EPS)
    oa_ref[...] = o * gs_ref[...]


def _mixer_step(qT, kT, gT, v, gs, state):
    nseq = v.shape[0]
    nbc = nseq // STEP_SEQS
    cspec = pl.BlockSpec((1, 1, HEAD_DIM, STEP_SEQS), lambda h, c: (h, c, 0, 0))
    rspec = pl.BlockSpec((STEP_SEQS, HEAD_DIM), lambda h, c: (c, h))
    sspec = pl.BlockSpec((STEP_SEQS, 1, HEAD_DIM, HEAD_DIM), lambda h, c: (c, h, 0, 0))
    return pl.pallas_call(
        _mixer_step_kernel,
        out_shape=[jax.ShapeDtypeStruct(state.shape, F32), jax.ShapeDtypeStruct((nseq, HGRN_WIDTH), F32)],
        grid=(HGRN_HEADS, nbc),
        in_specs=[cspec, cspec, cspec, rspec, rspec, sspec],
        out_specs=[sspec, rspec],
        compiler_params=pltpu.CompilerParams(dimension_semantics=("arbitrary", "arbitrary"),
                                             vmem_limit_bytes=VMEM_LIMIT),
        name="mixer_step",
    )(qT, kT, gT, v, gs, state)


def _pool_step_kernel(sp_ref, p_ref, wpool_ref, pscale_ref, ob_ref):
    p_cur = p_ref[...]
    outs = []
    for gi, w in enumerate(POOL_WINDOWS):
        sl = slice(gi * POOL_GROUP_DIM, (gi + 1) * POOL_GROUP_DIM)
        s = p_cur[:, sl]
        for r in range(POOL_BUF - (w - 1), POOL_BUF):
            s = s + sp_ref[r][:, sl]
        pooled = s * (1.0 / w) - p_cur[:, sl]
        outs.append(_bdot(pooled.astype(BF16), wpool_ref[gi]) * pscale_ref[:, sl])
    ob_ref[...] = jnp.concatenate(outs, axis=1)


def _pool_step(spT, p, wpool_bf, pscale):
    nseq = p.shape[0]
    return pl.pallas_call(
        _pool_step_kernel,
        out_shape=jax.ShapeDtypeStruct((nseq, POOL_WIDTH), F32),
        name="pool_step",
    )(spT, p, wpool_bf, pscale)


def _outproj_kernel(mixp_ref, xnp_ref, mixd_ref, xnd_ref, wout_ref, g1_ref, b1_ref, wr_ref, br_ref,
                    x1_ref, x1p_ref, idx_ref, gate_ref, cnt_ref, *, n_prompt_blocks, n_valid_last):
    tm = mixp_ref.shape[0]
    i = pl.program_id(0)
    is_prompt = i < n_prompt_blocks
    mix_in = jnp.where(is_prompt, mixp_ref[...], mixd_ref[...])
    xn = jnp.where(is_prompt, xnp_ref[...], xnd_ref[...])
    mix = _bdot(mix_in, wout_ref[...])
    x1 = _ln(ALPHA * xn + mix, g1_ref[...], b1_ref[...])
    x1_ref[...] = x1
    xb = x1.astype(BF16)
    words = _pack_bf16_pairs(x1)
    for c in range(PACK_ROWS):
        x1p_ref[pl.ds(c, tm, stride=PACK_ROWS), :] = words[:, c * 128:(c + 1) * 128]

    scores = _sigmoid(_bdot(xb, wr_ref[...]))
    sel = scores + br_ref[...]
    lane = lax.broadcasted_iota(I32, (tm, N_EXPERTS), 1).astype(F32)
    lane_o = lax.broadcasted_iota(I32, (tm, 128), 1)
    idx_o = jnp.zeros((tm, 128), F32)
    ssum = jnp.zeros((tm, 1), F32)
    chosen = jnp.zeros((tm, N_EXPERTS), F32)
    s_sel = []
    for j in range(TOP_K):
        m = jnp.max(sel, axis=-1, keepdims=True)
        am = jnp.min(jnp.where(sel == m, lane, float(N_EXPERTS)), axis=-1, keepdims=True)
        hit = lane == am
        sj = jnp.sum(jnp.where(hit, scores, 0.0), axis=-1, keepdims=True)
        sel = jnp.where(hit, -jnp.inf, sel)
        chosen = jnp.where(hit, 1.0, chosen)
        idx_o = jnp.where(lane_o == j, am, idx_o)
        s_sel.append(sj)
        ssum = ssum + sj
    idx_ref[...] = idx_o.astype(I32)
    gates = jnp.zeros((tm, 128), F32)
    for j in range(TOP_K):
        gates = jnp.where(lane_o == j, s_sel[j] / ssum * ROUTED_SCALE, gates)
    gate_ref[...] = gates

    @pl.when(i == 0)
    def _():
        cnt_ref[...] = jnp.zeros_like(cnt_ref)

    row_id = lax.broadcasted_iota(I32, (tm, N_EXPERTS), 0)
    valid = jnp.logical_or(is_prompt, row_id < n_valid_last)
    cnt_ref[...] += jnp.sum(jnp.where(valid, chosen, 0.0), axis=0, keepdims=True)


def _outproj(mix_p, xn_p, mix_d, xn_d, wout_bf, g1, b1, wr_bf, br):
    tm = OUT_TILE
    n_prompt, n_dec = mix_p.shape[0], mix_d.shape[0]
    assert n_prompt % tm == 0 and n_dec <= tm and n_dec % 8 == 0
    nbp = n_prompt // tm
    T = (nbp + 1) * tm
    padrows = lambda a: jnp.pad(a, ((0, tm - n_dec), (0, 0)))
    row = lambda i: (i, 0)
    prow = lambda i: (jnp.minimum(i, nbp - 1), 0)
    const = lambda i: (0, 0)
    return pl.pallas_call(
        functools.partial(_outproj_kernel, n_prompt_blocks=nbp, n_valid_last=n_dec),
        out_shape=[jax.ShapeDtypeStruct((T, D_MODEL), F32), jax.ShapeDtypeStruct((T * PACK_ROWS, 128), I32),
                   jax.ShapeDtypeStruct((T, 128), I32), jax.ShapeDtypeStruct((T, 128), F32),
                   jax.ShapeDtypeStruct((1, N_EXPERTS), F32)],
        grid=(nbp + 1,),
        in_specs=[pl.BlockSpec((tm, D_MODEL), prow), pl.BlockSpec((tm, D_MODEL), prow),
                  pl.BlockSpec((tm, D_MODEL), const), pl.BlockSpec((tm, D_MODEL), const),
                  pl.BlockSpec((D_MODEL, D_MODEL), const), pl.BlockSpec((1, D_MODEL), const),
                  pl.BlockSpec((1, D_MODEL), const), pl.BlockSpec((D_MODEL, N_EXPERTS), const),
                  pl.BlockSpec((1, N_EXPERTS), const)],
        out_specs=[pl.BlockSpec((tm, D_MODEL), row), pl.BlockSpec((tm * PACK_ROWS, 128), row),
                   pl.BlockSpec((tm, 128), row), pl.BlockSpec((tm, 128), row),
                   pl.BlockSpec((1, N_EXPERTS), const)],
        compiler_params=pltpu.CompilerParams(dimension_semantics=("arbitrary",), vmem_limit_bytes=VMEM_LIMIT),
        name="outproj",
    )(mix_p, xn_p, padrows(mix_d), padrows(xn_d), wout_bf, g1, b1, wr_bf, br)


def _moe_kernel(eb0_ref, enb_ref, wsel_ref, tot_ref, bn_ref, bsrc_ref, bdel_ref,
                order_ref, tokrow_ref, xp_ref, wg_ref, wu_ref, wd_ref, yt_ref,
                tile_scr, ybuf, wgb, wub, wdb, ord_smem, tok_smem, sem, sem_o, sem_t):
    del wsel_ref
    e = pl.program_id(0)
    g0 = eb0_ref[e]
    nblk = enb_ref[e]
    total = tot_ref[0]
    nb_max = bn_ref.shape[0]
    R = MOE_ROWS
    C = PACK_ROWS
    L = R + ORDER_ALIGN
    n_assign = yt_ref.shape[0] // C - 2 * TOP_K

    class _Fetch:
        def __init__(self, g):
            gc = jnp.minimum(g, nb_max - 1)
            s = g % 4
            src = pl.multiple_of(bsrc_ref[gc], ORDER_ALIGN)
            dst = pl.multiple_of(s * L, ORDER_ALIGN)
            self.copies = (
                pltpu.make_async_copy(order_ref.at[pl.ds(src, L)], ord_smem.at[pl.ds(dst, L)], sem_o.at[s]),
                pltpu.make_async_copy(tokrow_ref.at[pl.ds(src, L)], tok_smem.at[pl.ds(dst, L)], sem_t.at[s]))

        def start(self):
            for c in self.copies:
                c.start()

        def wait(self):
            for c in self.copies:
                c.wait()

    fetch = _Fetch

    def list_base(g):
        return (g % 4) * L + bdel_ref[jnp.minimum(g, nb_max - 1)]

    def gather(g):
        base = list_base(g)
        trow = (g % 2) * (PACK_ROWS * TILE_STRIDE)
        for r in range(R):
            t4 = pl.multiple_of(tok_smem[base + r], PACK_ROWS)
            tile_scr[pl.ds(trow + r, PACK_ROWS, stride=TILE_STRIDE), :] = xp_ref[pl.ds(t4, PACK_ROWS), :]

    def rows_sent(cnt):
        return pl.multiple_of(lax.shift_left(lax.shift_right_logical(cnt + 7, 3), 3), 8)

    def wait_rows(s, cnt):
        pltpu.make_async_copy(ybuf.at[s, pl.ds(0, cnt * C), :], yt_ref.at[pl.ds(0, cnt * C), :], sem.at[s]).wait()

    @pl.when(e == 0)
    def _():
        n_spare = 2 * TOP_K * C
        ybuf[0, 0:n_spare, :] = jnp.zeros((n_spare, 128), I32)
        init = pltpu.make_async_copy(ybuf.at[0, pl.ds(0, n_spare), :],
                                     yt_ref.at[pl.ds(yt_ref.shape[0] - n_spare, n_spare), :], sem.at[0])
        init.start()
        init.wait()
        fetch(0).start()
        fetch(1).start()
        fetch(0).wait()
        gather(0)

    @pl.when(nblk > 0)
    def _():
        wgb[...] = wg_ref[0].astype(BF16)
        wub[...] = wu_ref[0].astype(BF16)
        wdb[...] = wd_ref[0].astype(BF16)

    def block(k, carry):
        g = g0 + k
        slot = g % 2
        n = bn_ref[g]
        fetch(g + 1).wait()
        fetch(g + 2).start()

        @pl.when(g >= 2)
        def _():
            wait_rows(slot, rows_sent(bn_ref[jnp.maximum(g - 2, 0)]))

        trow = pl.multiple_of(slot * (PACK_ROWS * TILE_STRIDE), 8)

        def compute(send_prev):
            los, his = [], []
            for j in range(PACK_ROWS):
                lo, hi = _unpack_bf16_pairs(tile_scr[pl.ds(trow + j * TILE_STRIDE, R), :])
                los.append(lo.astype(BF16))
                his.append(hi.astype(BF16))
            xg = jnp.concatenate(los + his, axis=1)
            gather(g + 1)
            if send_prev:
                pbase = list_base(g - 1)
                for r in range(R):
                    send(1 - slot, r, ord_smem[pbase + r], r)
            hg = _bdot(xg, wgb[...])
            hu = _bdot(xg, wub[...])
            hb = (hg * _sigmoid(hg)) * hu
            y = _bdot(hb.astype(BF16), wdb[...])
            words = _pack_bf16_pairs(y)
            yb = ybuf.at[slot]
            for c in range(C):
                yb[pl.ds(c, R, stride=C), :] = words[:, c * 128:(c + 1) * 128]

        prev_full = jnp.logical_and(g >= 1, bn_ref[jnp.maximum(g - 1, 0)] == R)

        @pl.when(prev_full)
        def _():
            compute(True)

        @pl.when(jnp.logical_not(prev_full))
        def _():
            compute(False)

        @pl.when(n < R)
        def _():
            send_rows(g)

        return carry

    def send(s, r, d, u):
        pltpu.make_async_copy(ybuf.at[s, pl.ds(pl.multiple_of(r * C, C), C), :],
                              yt_ref.at[pl.ds(pl.multiple_of(d * C, C), C), :], sem.at[s]).start(priority=u % 2)

    def send_rows(g):
        slot = g % 2
        n = bn_ref[g]
        base = list_base(g)

        def send_group(i, c2):
            for u in range(8):
                send(slot, i * 8 + u, ord_smem[base + i * 8 + u], u)
            return c2

        n_full = lax.shift_right_logical(n, 3)
        lax.fori_loop(0, n_full, send_group, 0)

        @pl.when(n_full * 8 < n)
        def _():
            for u in range(8):
                r = n_full * 8 + u
                send(slot, r, jnp.where(r < n, ord_smem[base + r], n_assign + slot * 8 + u), u)

    lax.fori_loop(0, nblk, block, 0)

    @pl.when(e == pl.num_programs(0) - 1)
    def _():
        @pl.when(bn_ref[jnp.maximum(total - 1, 0)] == R)
        def _():
            send_rows(total - 1)

        fetch(total + 1).wait()
        wait_rows((total - 1) % 2, rows_sent(bn_ref[jnp.maximum(total - 1, 0)]))

        @pl.when(total >= 2)
        def _():
            wait_rows(total % 2, rows_sent(bn_ref[jnp.maximum(total - 2, 0)]))


def _moe(ex_b0, ex_nb, ex_w, total, blk_n, blk_src, blk_delta, order, tokrow, xp, wg, wu, wd, n_tok):
    R = MOE_ROWS
    wmap_in = lambda e, eb0, enb, wsel, tot, bn, bs, bd: (wsel[e], 0, 0)
    grid_spec = pltpu.PrefetchScalarGridSpec(
        num_scalar_prefetch=7,
        grid=(N_EXPERTS,),
        in_specs=[
            pl.BlockSpec(memory_space=pl.ANY),
            pl.BlockSpec(memory_space=pl.ANY),
            pl.BlockSpec(memory_space=pltpu.VMEM),
            pl.BlockSpec((1, D_MODEL, D_EXPERT), wmap_in),
            pl.BlockSpec((1, D_MODEL, D_EXPERT), wmap_in),
            pl.BlockSpec((1, D_EXPERT, D_MODEL), wmap_in),
        ],
        out_specs=pl.BlockSpec(memory_space=pl.ANY),
        scratch_shapes=[pltpu.VMEM((2 * PACK_ROWS * TILE_STRIDE, 128), I32),
                        pltpu.VMEM((2, R * PACK_ROWS, 128), I32),
                        pltpu.VMEM((D_MODEL, D_EXPERT), BF16),
                        pltpu.VMEM((D_MODEL, D_EXPERT), BF16),
                        pltpu.VMEM((D_EXPERT, D_MODEL), BF16),
                        pltpu.SMEM((4 * (R + ORDER_ALIGN),), I32),
                        pltpu.SMEM((4 * (R + ORDER_ALIGN),), I32),
                        pltpu.SemaphoreType.DMA((2,)),
                        pltpu.SemaphoreType.DMA((4,)),
                        pltpu.SemaphoreType.DMA((4,))],
    )
    return pl.pallas_call(
        _moe_kernel,
        out_shape=jax.ShapeDtypeStruct(((n_tok + 2) * TOP_K * PACK_ROWS, 128), I32),
        grid_spec=grid_spec,
        compiler_params=pltpu.CompilerParams(dimension_semantics=("arbitrary",),
                                             vmem_limit_bytes=MOE_VMEM_LIMIT),
        name="moe",
    )(ex_b0, ex_nb, ex_w, total, blk_n, blk_src, blk_delta, order, tokrow, xp, wg, wu, wd)


def _route_plan(idx, counts, n_tok):
    R = MOE_ROWS
    n_assign = n_tok * TOP_K
    nb = (n_assign + N_EXPERTS * (R - 1)) // R
    id_bits = (n_assign - 1).bit_length()
    assert id_bits + (N_EXPERTS - 1).bit_length() < 32
    key = lax.shift_left(idx.reshape(-1), id_bits) | jnp.arange(n_assign, dtype=I32)
    order = lax.sort(key) & ((1 << id_bits) - 1)
    order = jnp.concatenate([order, jnp.zeros((R + ORDER_ALIGN,), I32)])
    nblk_e = (counts + R - 1) // R
    bend = jnp.cumsum(nblk_e)
    bstart = bend - nblk_e
    cstart = jnp.cumsum(counts) - counts
    blk = jnp.arange(nb, dtype=I32)
    blk_e = jnp.minimum(jnp.sum((bend[None, :] <= blk[:, None]).astype(I32), axis=1), N_EXPERTS - 1)
    k = blk - bstart[blk_e]
    active = blk < bend[-1]
    blk_n = jnp.where(active, jnp.clip(counts[blk_e] - k * R, 0, R), 0).astype(I32)
    src = jnp.where(active, cstart[blk_e] + k * R, 0).astype(I32)
    blk_src = (src // ORDER_ALIGN) * ORDER_ALIGN
    ex = jnp.arange(N_EXPERTS, dtype=I32)
    ex_w = jnp.maximum(lax.cummax(jnp.where(nblk_e > 0, ex, -1)), 0).astype(I32)
    tokrow = (order // TOP_K) * PACK_ROWS
    return (bstart.astype(I32), nblk_e.astype(I32), ex_w, bend[-1:].astype(I32),
            blk_n, blk_src, src - blk_src, order, tokrow)


def _combine_kernel(x1_ref, yt_ref, gate_ref, wgs_ref, wus_ref, wds_ref, g2_ref, b2_ref, out_ref):
    tm = x1_ref.shape[0]
    x1 = x1_ref[...]
    xb = x1.astype(BF16)
    hg = _bdot(xb, wgs_ref[...])
    hs = (hg * _sigmoid(hg)) * _bdot(xb, wus_ref[...])
    moe = _bdot(hs.astype(BF16), wds_ref[...])
    gate = gate_ref[...]
    per_tok = TOP_K * PACK_ROWS
    planes = pltpu.einshape("tjl->jtl", yt_ref[...].reshape(tm, per_tok, 128))
    lo_acc = [None] * PACK_ROWS
    hi_acc = [None] * PACK_ROWS
    for j in range(TOP_K):
        gj = gate[:, j:j + 1]
        for c in range(PACK_ROWS):
            lo, hi = _unpack_bf16_pairs(planes[j * PACK_ROWS + c])
            lo_acc[c] = lo * gj if j == 0 else lo_acc[c] + lo * gj
            hi_acc[c] = hi * gj if j == 0 else hi_acc[c] + hi * gj
    routed = jnp.concatenate(lo_acc + hi_acc, axis=1)
    out_ref[...] = _ln(ALPHA * x1 + (moe + routed), g2_ref[...], b2_ref[...])


def _combine(x1, yt, gate, wgs_bf, wus_bf, wds_bf, g2, b2, row0, nrows):
    tm = _row_tile(nrows)
    assert row0 % tm == 0
    off = row0 // tm
    row = lambda i: (i + off, 0)
    const = lambda i: (0, 0)
    return pl.pallas_call(
        _combine_kernel,
        out_shape=jax.ShapeDtypeStruct((nrows, D_MODEL), F32),
        grid=(nrows // tm,),
        in_specs=[pl.BlockSpec((tm, D_MODEL), row),
                  pl.BlockSpec((tm * TOP_K * PACK_ROWS, 128), row),
                  pl.BlockSpec((tm, 128), row),
                  pl.BlockSpec((D_MODEL, D_EXPERT), const), pl.BlockSpec((D_MODEL, D_EXPERT), const),
                  pl.BlockSpec((D_EXPERT, D_MODEL), const),
                  pl.BlockSpec((1, D_MODEL), const), pl.BlockSpec((1, D_MODEL), const)],
        out_specs=pl.BlockSpec((tm, D_MODEL), lambda i: (i, 0)),
        compiler_params=pltpu.CompilerParams(dimension_semantics=("arbitrary",), vmem_limit_bytes=VMEM_LIMIT),
        name="combine",
    )(x1, yt, gate, wgs_bf, wus_bf, wds_bf, g2, b2)


def kernel(x_prompt, x_sample, state_hgrn, state_pool, meta_tokens, ln_emb_g, ln_emb_b, w_in, lb_logits, hgrn_norm_g, w_pool, pool_scale, w_out, ln1_g, ln1_b, w_router, b_router, w_gate_e, w_up_e, w_down_e, w_gate_s, w_up_s, w_down_s, ln2_g, ln2_b):
    nseq, seqlen, _ = x_prompt.shape
    ndec = x_sample.shape[0]
    l = 0
    row = lambda a: a.reshape(1, -1)
    w_in_bf = w_in[l].astype(BF16)
    wpool_bf = w_pool[l].astype(BF16)
    lng, lnb = row(ln_emb_g), row(ln_emb_b)
    ng, ps = row(hgrn_norm_g[l]), row(pool_scale[l])
    proj = functools.partial(_ln_proj, ln_g=lng, ln_b=lnb, w_in_bf=w_in_bf, lb_logits=lb_logits, norm_g=ng)

    m_xn, m_q, m_k, m_g, m_v, m_gs, m_p = proj(meta_tokens)
    pad = lambda a: jnp.pad(a, ((0, MIX_BLOCK - N_META), (0, 0)))
    zero_state = jnp.zeros((HGRN_HEADS, HEAD_DIM, HEAD_DIM), F32)
    _, s_meta = _mixer(pad(m_q), pad(m_k), pad(m_g), pad(m_v), pad(m_gs), pad(m_p), zero_state,
                       jnp.zeros((SUB, POOL_WIDTH), F32), wpool_bf, ps, 1, MIX_BLOCK)

    p_xn, p_q, p_k, p_g, p_v, p_gs, p_p = proj(x_prompt.reshape(nseq * seqlen, D_MODEL))
    p_mix, s_prompt = _mixer(p_q, p_k, p_g, p_v, p_gs, p_p, s_meta[0], m_p, wpool_bf, ps, nseq, seqlen)

    d_xn, d_q, d_k, d_g, d_v, d_gs, d_p = proj(x_sample.reshape(ndec, D_MODEL))
    cols = lambda a: a.reshape(ndec // STEP_SEQS, STEP_SEQS, HGRN_HEADS, HEAD_DIM).transpose(2, 0, 3, 1)
    s_dec, d_oa = _mixer_step(cols(d_q), cols(d_k), cols(d_g), d_v, d_gs, state_hgrn[l])
    d_ob = _pool_step(state_pool[l].transpose(1, 0, 2), d_p, wpool_bf, ps)
    d_mix = jnp.concatenate([d_oa, d_ob], axis=1).astype(BF16)

    n_tok = nseq * seqlen + ndec
    x1, xp, idx, gate, cnt = _outproj(p_mix, p_xn, d_mix, d_xn, w_out[l].astype(BF16), row(ln1_g[l]), row(ln1_b[l]),
                                      w_router[l].astype(BF16), row(b_router[l]))
    plan = _route_plan(idx[:n_tok, :TOP_K], cnt[0].astype(I32), n_tok)
    yt = _moe(*plan, xp, w_gate_e[l], w_up_e[l], w_down_e[l], n_tok)
    comb = functools.partial(_combine, x1, yt, gate, w_gate_s[l].astype(BF16), w_up_s[l].astype(BF16),
                             w_down_s[l].astype(BF16), row(ln2_g[l]), row(ln2_b[l]))
    y_prompt = comb(0, nseq * seqlen).reshape(nseq, seqlen, D_MODEL)
    y_sample = comb(nseq * seqlen, ndec).reshape(ndec, 1, D_MODEL)

    state_pool_prompt = p_p.reshape(nseq, seqlen, POOL_WIDTH)[:, seqlen - POOL_BUF:, :]
    state_pool_sample = jnp.concatenate([state_pool[l][:, 1:, :], d_p[:, None, :]], axis=1)
    return (y_prompt, y_sample, s_prompt[None], state_pool_prompt[None], s_dec[None], state_pool_sample[None])
```

```python
import functools

import jax
import jax.numpy as jnp
from jax import lax
from jax.experimental import pallas as pl
from jax.experimental.pallas import tpu as pltpu

F32 = jnp.float32
BF16 = jnp.bfloat16
I32 = jnp.int32

D_MODEL = 1024
N_META = 16
HGRN_WIDTH = 512
HGRN_HEADS = 4
HEAD_DIM = 128
POOL_WIDTH = 512
POOL_WINDOWS = (2, 4, 8, 16)
POOL_GROUP_DIM = 128
POOL_BUF = 15
IN_WIDTH = 4 * HGRN_WIDTH + POOL_WIDTH
N_EXPERTS = 256
TOP_K = 8
D_EXPERT = 256
ROUTED_SCALE = 2.5
DEPTH = 1
ALPHA = (2 * DEPTH) ** 0.25
LN_EPS = 1e-5
RMS_EPS = 1e-6
LOG2_E = 1.4426950408889634

SUB = 16
MIX_BLOCK = 128
MOE_ROWS = 256
TILE_STRIDE = MOE_ROWS + 8
ORDER_ALIGN = 128
PACK_ROWS = D_MODEL // 2 // 128
OUT_TILE = 512
V7X_VMEM_BYTES = 64 * 1024 * 1024
VMEM_LIMIT = V7X_VMEM_BYTES - 8 * 1024 * 1024
MOE_VMEM_LIMIT = V7X_VMEM_BYTES - 6 * 1024 * 1024


def _ln(x, g, b):
    mu = jnp.mean(x, axis=-1, keepdims=True)
    xc = x - mu
    var = jnp.mean(xc * xc, axis=-1, keepdims=True)
    return xc * lax.rsqrt(var + LN_EPS) * g + b


def _sigmoid(z):
    return 1.0 / (1.0 + jnp.exp(-z))


def _bdot(a, b):
    return jnp.dot(a, b, preferred_element_type=F32)


def _pack_bf16_pairs(x):
    half = x.shape[1] // 2
    xr = x.astype(BF16).astype(F32)
    lo = lax.shift_right_logical(lax.bitcast_convert_type(xr[:, :half], jnp.uint32), jnp.uint32(16))
    hi = lax.bitcast_convert_type(xr[:, half:], jnp.uint32) & jnp.uint32(0xFFFF0000)
    return lax.bitcast_convert_type(hi | lo, I32)


def _unpack_bf16_pairs(words):
    w = lax.bitcast_convert_type(words, jnp.uint32)
    lo = lax.bitcast_convert_type(lax.shift_left(w, jnp.uint32(16)), F32)
    hi = lax.bitcast_convert_type(w & jnp.uint32(0xFFFF0000), F32)
    return lo, hi


def _ln_proj_kernel(x_ref, g_ref, b_ref, w_ref, lbl_ref, ng_ref,
                    xn_ref, q_ref, k_ref, gl_ref, v_ref, gs_ref, p_ref):
    xn = _ln(x_ref[...], g_ref[...], b_ref[...])
    xn_ref[...] = xn
    proj = _bdot(xn.astype(BF16), w_ref[...])
    lbl = lbl_ref[...]
    e = jnp.exp(lbl - jnp.max(lbl, axis=0, keepdims=True))
    lb = e[0:1] / jnp.sum(e, axis=0, keepdims=True)
    W = HGRN_WIDTH
    q = proj[:, 0:W]
    f = proj[:, W:2 * W]
    q_ref[...] = q * _sigmoid(q)
    k_ref[...] = (1.0 - lb) * _sigmoid(-f)
    gl_ref[...] = jnp.log(lb + (1.0 - lb) * _sigmoid(f))
    v_ref[...] = proj[:, 2 * W:3 * W]
    g = proj[:, 3 * W:4 * W]
    gs_ref[...] = ng_ref[...] * (g * _sigmoid(g))
    p_ref[...] = proj[:, 4 * W:]


def _row_tile(n_rows):
    for tm in (512, 256, 128, 64, 32, 16, 8):
        if n_rows % tm == 0:
            return tm
    raise ValueError(f"row count {n_rows} is not a multiple of 8")


def _ln_proj(x, ln_g, ln_b, w_in_bf, lb_logits, norm_g):
    T = x.shape[0]
    tm = _row_tile(T)
    row = lambda i: (i, 0)
    const = lambda i: (0, 0)
    outs = [jax.ShapeDtypeStruct((T, D_MODEL), F32)] + [jax.ShapeDtypeStruct((T, HGRN_WIDTH), F32)] * 6
    return pl.pallas_call(
        _ln_proj_kernel,
        out_shape=outs,
        grid=(T // tm,),
        in_specs=[
            pl.BlockSpec((tm, D_MODEL), row),
            pl.BlockSpec((1, D_MODEL), const),
            pl.BlockSpec((1, D_MODEL), const),
            pl.BlockSpec((D_MODEL, IN_WIDTH), const),
            pl.BlockSpec((DEPTH + 1, HGRN_WIDTH), const),
            pl.BlockSpec((1, HGRN_WIDTH), const),
        ],
        out_specs=[pl.BlockSpec((tm, D_MODEL), row)] + [pl.BlockSpec((tm, HGRN_WIDTH), row)] * 6,
        compiler_params=pltpu.CompilerParams(dimension_semantics=("arbitrary",), vmem_limit_bytes=VMEM_LIMIT),
        name="ln_proj",
    )(x, ln_g, ln_b, w_in_bf, lb_logits, norm_g)


def _pool_group(pe, p_cur, gi, w):
    sl = slice(gi * POOL_GROUP_DIM, (gi + 1) * POOL_GROUP_DIM)
    s = pe[:, sl]
    sh = 1
    while sh < w:
        s = s + pltpu.roll(s, sh, 0)
        sh *= 2
    return s[SUB:, :] * (1.0 / w) - p_cur[:, sl]


def _mixer_kernel(q_ref, k_ref, g_ref, v_ref, gs_ref, p_ref, s0_ref, pp0_ref, wpool_ref, pscale_ref,
                  mix_ref, sfin_ref, st_scr, pe_scr, bf_scr):
    i = pl.program_id(1)
    nblk = pl.num_programs(1)
    n = MIX_BLOCK

    @pl.when(i == 0)
    def _():
        for h in range(HGRN_HEADS):
            st_scr[h] = s0_ref[h].T
        pe_scr[0:SUB, :] = pp0_ref[...]

    rows = lax.broadcasted_iota(I32, (n, HEAD_DIM), 0)
    r16 = rows & (SUB - 1)
    t8 = lax.broadcasted_iota(I32, (8, HEAD_DIM), 0)
    zero_bf = jnp.zeros((SUB, HEAD_DIM), BF16)

    o_heads = []
    for h in range(HGRN_HEADS):
        hs = slice(h * HEAD_DIM, (h + 1) * HEAD_DIM)
        Q = q_ref[:, hs]
        K = k_ref[:, hs]
        G = g_ref[:, hs]
        V = v_ref[:, hs]
        bf = G
        br = G
        for sh in (1, 2, 4, 8):
            bf = bf + jnp.where(r16 >= sh, pltpu.roll(bf, sh, 0), 0.0)
            br = br + jnp.where(r16 < SUB - sh, pltpu.roll(br, n - sh, 0), 0.0)
        br = br - G
        bf2 = bf * LOG2_E
        bf_scr[:, hs] = bf2
        qt = (Q * jnp.exp(bf)).astype(BF16)
        kt = (K * jnp.exp(br)).astype(BF16)
        vt = V.T.astype(BF16)
        st = st_scr[h]
        o_parts = []
        for c in range(n // SUB):
            r0 = c * SUB
            b_top, b_bot = bf2[r0:r0 + 8], bf2[r0 + 8:r0 + 16]
            q_top, q_bot = Q[r0:r0 + 8], Q[r0 + 8:r0 + 16]
            acc_top = jnp.zeros((8, HEAD_DIM), F32)
            acc_bot = jnp.zeros((8, HEAD_DIM), F32)
            for s in range(SUB):
                bs = bf_scr[r0 + s:r0 + s + 1, hs]
                ks = k_ref[r0 + s:r0 + s + 1, hs]
                vs = v_ref[r0 + s:r0 + s + 1, hs]
                if s < 8:
                    col = jnp.sum(q_top * jnp.exp2(b_top - bs) * ks, axis=-1, keepdims=True)
                    col = jnp.where(t8[:, 0:1] >= s, col, 0.0)
                    acc_top = acc_top + col * vs
                    col = jnp.sum(q_bot * jnp.exp2(b_bot - bs) * ks, axis=-1, keepdims=True)
                    acc_bot = acc_bot + col * vs
                else:
                    col = jnp.sum(q_bot * jnp.exp2(b_bot - bs) * ks, axis=-1, keepdims=True)
                    col = jnp.where(t8[:, 0:1] + 8 >= s, col, 0.0)
                    acc_bot = acc_bot + col * vs
            o_diag = jnp.concatenate([acc_top, acc_bot], axis=0)
            o_inter = lax.dot_general(qt[r0:r0 + SUB], st.astype(BF16), (((1,), (1,)), ((), ())),
                                      preferred_element_type=F32)
            o_parts.append(o_inter + o_diag)
            kmask = jnp.concatenate([zero_bf] * c + [kt[r0:r0 + SUB]] + [zero_bf] * (n // SUB - 1 - c), axis=0)
            d_st = _bdot(vt, kmask)
            st = st * jnp.exp(bf[r0 + SUB - 1:r0 + SUB]) + d_st
        st_scr[h] = st
        o = jnp.concatenate(o_parts, axis=0)
        o = o * lax.rsqrt(jnp.mean(o * o, axis=-1, keepdims=True) + RMS_EPS)
        o_heads.append(o * gs_ref[:, hs])

    p_cur = p_ref[...]
    pe_scr[SUB:SUB + n, :] = p_cur
    pe = pe_scr[...]
    ob = []
    for gi, w in enumerate(POOL_WINDOWS):
        pooled = _pool_group(pe, p_cur, gi, w)
        sl = slice(gi * POOL_GROUP_DIM, (gi + 1) * POOL_GROUP_DIM)
        ob.append(_bdot(pooled.astype(BF16), wpool_ref[gi]) * pscale_ref[:, sl])
    pe_scr[0:SUB, :] = p_cur[n - SUB:, :]

    mix_ref[...] = jnp.concatenate(o_heads + ob, axis=1).astype(BF16)

    @pl.when(i == nblk - 1)
    def _():
        for h in range(HGRN_HEADS):
            sfin_ref[0, h] = st_scr[h].T


def _mixer(q, k, g, v, gs, p, s0, pp0, wpool_bf, pscale, nseq, seqlen):
    nblk = seqlen // MIX_BLOCK
    tok = lambda b, i: (b * nblk + i, 0)
    tspec = pl.BlockSpec((MIX_BLOCK, HGRN_WIDTH), tok)
    return pl.pallas_call(
        _mixer_kernel,
        out_shape=[jax.ShapeDtypeStruct((nseq * seqlen, D_MODEL), BF16),
                   jax.ShapeDtypeStruct((nseq, HGRN_HEADS, HEAD_DIM, HEAD_DIM), F32)],
        grid=(nseq, nblk),
        in_specs=[tspec] * 6 + [
            pl.BlockSpec((HGRN_HEADS, HEAD_DIM, HEAD_DIM), lambda b, i: (0, 0, 0)),
            pl.BlockSpec((SUB, POOL_WIDTH), lambda b, i: (0, 0)),
            pl.BlockSpec((len(POOL_WINDOWS), POOL_GROUP_DIM, POOL_GROUP_DIM), lambda b, i: (0, 0, 0)),
            pl.BlockSpec((1, POOL_WIDTH), lambda b, i: (0, 0)),
        ],
        out_specs=[pl.BlockSpec((MIX_BLOCK, D_MODEL), tok),
                   pl.BlockSpec((1, HGRN_HEADS, HEAD_DIM, HEAD_DIM), lambda b, i: (b, 0, 0, 0))],
        scratch_shapes=[pltpu.VMEM((HGRN_HEADS, HEAD_DIM, HEAD_DIM), F32),
                        pltpu.VMEM((SUB + MIX_BLOCK, POOL_WIDTH), F32),
                        pltpu.VMEM((MIX_BLOCK, HGRN_WIDTH), F32)],
        compiler_params=pltpu.CompilerParams(dimension_semantics=("arbitrary", "arbitrary"),
                                             vmem_limit_bytes=VMEM_LIMIT),
        name="mixer",
    )(q, k, g, v, gs, p, s0, pp0, wpool_bf, pscale)


STEP_SEQS = 64


def _mixer_step_kernel(qt_ref, kt_ref, gt_ref, v_ref, gs_ref, s_ref, snew_ref, oa_ref):
    qt = qt_ref[0, 0]
    kt = kt_ref[0, 0]
    dt = jnp.exp(gt_ref[0, 0])
    rows = []
    for bb in range(STEP_SEQS):
        sn = s_ref[bb, 0] * dt[:, bb:bb + 1] + kt[:, bb:bb + 1] * v_ref[bb:bb + 1, :]
        snew_ref[bb, 0] = sn
        rows.append(jnp.sum(sn * qt[:, bb:bb + 1], axis=0, keepdims=True))
    o = jnp.concatenate(rows, axis=0)
    o = o * lax.rsqrt(jnp.mean(o * o, axis=-1, keepdims=True) + RMS_EPS)
    oa_ref[...] = o * gs_ref[...]


def _mixer_step(qT, kT, gT, v, gs, state):
    nseq = v.shape[0]
    nbc = nseq // STEP_SEQS
    cspec = pl.BlockSpec((1, 1, HEAD_DIM, STEP_SEQS), lambda h, c: (h, c, 0, 0))
    rspec = pl.BlockSpec((STEP_SEQS, HEAD_DIM), lambda h, c: (c, h))
    sspec = pl.BlockSpec((STEP_SEQS, 1, HEAD_DIM, HEAD_DIM), lambda h, c: (c, h, 0, 0))
    return pl.pallas_call(
        _mixer_step_kernel,
        out_shape=[jax.ShapeDtypeStruct(state.shape, F32), jax.ShapeDtypeStruct((nseq, HGRN_WIDTH), F32)],
        grid=(HGRN_HEADS, nbc),
        in_specs=[cspec, cspec, cspec, rspec, rspec, sspec],
        out_specs=[sspec, rspec],
        compiler_params=pltpu.CompilerParams(dimension_semantics=("arbitrary", "arbitrary"),
                                             vmem_limit_bytes=VMEM_LIMIT),
        name="mixer_step",
    )(qT, kT, gT, v, gs, state)


def _pool_step_kernel(sp_ref, p_ref, wpool_ref, pscale_ref, ob_ref):
    p_cur = p_ref[...]
    outs = []
    for gi, w in enumerate(POOL_WINDOWS):
        sl = slice(gi * POOL_GROUP_DIM, (gi + 1) * POOL_GROUP_DIM)
        s = p_cur[:, sl]
        for r in range(POOL_BUF - (w - 1), POOL_BUF):
            s = s + sp_ref[r][:, sl]
        pooled = s * (1.0 / w) - p_cur[:, sl]
        outs.append(_bdot(pooled.astype(BF16), wpool_ref[gi]) * pscale_ref[:, sl])
    ob_ref[...] = jnp.concatenate(outs, axis=1)


def _pool_step(spT, p, wpool_bf, pscale):
    nseq = p.shape[0]
    return pl.pallas_call(
        _pool_step_kernel,
        out_shape=jax.ShapeDtypeStruct((nseq, POOL_WIDTH), F32),
        name="pool_step",
    )(spT, p, wpool_bf, pscale)


def _outproj_kernel(mixp_ref, xnp_ref, mixd_ref, xnd_ref, wout_ref, g1_ref, b1_ref, wrt_ref, brt_ref,
                    x1_ref, x1p_ref, idx_ref, gate_ref, cnt_ref, *, n_prompt_blocks, n_valid_last):
    tm = mixp_ref.shape[0]
    i = pl.program_id(0)
    is_prompt = i < n_prompt_blocks
    mix_in = jnp.where(is_prompt, mixp_ref[...], mixd_ref[...])
    xn = jnp.where(is_prompt, xnp_ref[...], xnd_ref[...])
    mix = _bdot(mix_in, wout_ref[...])
    x1 = _ln(ALPHA * xn + mix, g1_ref[...], b1_ref[...])
    x1_ref[...] = x1
    xb = x1.astype(BF16)
    words = _pack_bf16_pairs(x1)
    for c in range(PACK_ROWS):
        x1p_ref[pl.ds(c, tm, stride=PACK_ROWS), :] = words[:, c * 128:(c + 1) * 128]

    scores = _sigmoid(lax.dot_general(wrt_ref[...], xb, (((1,), (1,)), ((), ())),
                                      preferred_element_type=F32))
    sel = scores + brt_ref[...]
    erow = lax.broadcasted_iota(I32, (N_EXPERTS, tm), 0).astype(F32)
    ssum = jnp.zeros((1, tm), F32)
    chosen = jnp.zeros((N_EXPERTS, tm), F32)
    idx_rows, s_rows = [], []
    for j in range(TOP_K):
        m = jnp.max(sel, axis=0, keepdims=True)
        am = jnp.min(jnp.where(sel == m, erow, float(N_EXPERTS)), axis=0, keepdims=True)
        hit = erow == am
        sj = jnp.sum(jnp.where(hit, scores, 0.0), axis=0, keepdims=True)
        sel = jnp.where(hit, -jnp.inf, sel)
        chosen = jnp.where(hit, 1.0, chosen)
        idx_rows.append(am)
        s_rows.append(sj)
        ssum = ssum + sj
    idx_ref[...] = jnp.concatenate(idx_rows, axis=0).astype(I32)
    gate_ref[...] = jnp.concatenate([s / ssum * ROUTED_SCALE for s in s_rows], axis=0)

    @pl.when(i == 0)
    def _():
        cnt_ref[...] = jnp.zeros_like(cnt_ref)

    col_id = lax.broadcasted_iota(I32, (N_EXPERTS, tm), 1)
    counted = jnp.where(jnp.logical_or(is_prompt, col_id < n_valid_last), chosen, 0.0)
    part = counted[:, 0:128]
    for c in range(1, tm // 128):
        part = part + counted[:, c * 128:(c + 1) * 128]
    cnt_ref[...] += part


def _outproj(mix_p, xn_p, mix_d, xn_d, wout_bf, g1, b1, wrt_bf, brt):
    tm = OUT_TILE
    n_prompt, n_dec = mix_p.shape[0], mix_d.shape[0]
    assert n_prompt % tm == 0 and n_dec <= tm and n_dec % 8 == 0
    nbp = n_prompt // tm
    T = (nbp + 1) * tm
    padrows = lambda a: jnp.pad(a, ((0, tm - n_dec), (0, 0)))
    row = lambda i: (i, 0)
    prow = lambda i: (jnp.minimum(i, nbp - 1), 0)
    const = lambda i: (0, 0)
    return pl.pallas_call(
        functools.partial(_outproj_kernel, n_prompt_blocks=nbp, n_valid_last=n_dec),
        out_shape=[jax.ShapeDtypeStruct((T, D_MODEL), F32), jax.ShapeDtypeStruct((T * PACK_ROWS, 128), I32),
                   jax.ShapeDtypeStruct((TOP_K, T), I32), jax.ShapeDtypeStruct((TOP_K, T), F32),
                   jax.ShapeDtypeStruct((N_EXPERTS, 128), F32)],
        grid=(nbp + 1,),
        in_specs=[pl.BlockSpec((tm, D_MODEL), prow), pl.BlockSpec((tm, D_MODEL), prow),
                  pl.BlockSpec((tm, D_MODEL), const), pl.BlockSpec((tm, D_MODEL), const),
                  pl.BlockSpec((D_MODEL, D_MODEL), const), pl.BlockSpec((1, D_MODEL), const),
                  pl.BlockSpec((1, D_MODEL), const), pl.BlockSpec((N_EXPERTS, D_MODEL), const),
                  pl.BlockSpec((N_EXPERTS, 1), const)],
        out_specs=[pl.BlockSpec((tm, D_MODEL), row), pl.BlockSpec((tm * PACK_ROWS, 128), row),
                   pl.BlockSpec((TOP_K, tm), lambda i: (0, i)), pl.BlockSpec((TOP_K, tm), lambda i: (0, i)),
                   pl.BlockSpec((N_EXPERTS, 128), const)],
        compiler_params=pltpu.CompilerParams(dimension_semantics=("arbitrary",), vmem_limit_bytes=VMEM_LIMIT),
        name="outproj",
    )(mix_p, xn_p, padrows(mix_d), padrows(xn_d), wout_bf, g1, b1, wrt_bf, brt)


def _moe_kernel(eb0_ref, enb_ref, wsel_ref, tot_ref, bn_ref, bsrc_ref, bdel_ref,
                order_ref, tokrow_ref, xp_ref, wg_ref, wu_ref, wd_ref, yt_ref,
                tile_scr, ybuf, wgb, wub, wdb, ord_smem, tok_smem, sem, sem_o, sem_t):
    del wsel_ref
    e = pl.program_id(0)
    g0 = eb0_ref[e]
    nblk = enb_ref[e]
    total = tot_ref[0]
    nb_max = bn_ref.shape[0]
    R = MOE_ROWS
    C = PACK_ROWS
    L = R + ORDER_ALIGN
    n_assign = yt_ref.shape[0] // C - 2 * TOP_K

    class _Fetch:
        def __init__(self, g):
            gc = jnp.minimum(g, nb_max - 1)
            s = g % 4
            src = pl.multiple_of(bsrc_ref[gc], ORDER_ALIGN)
            dst = pl.multiple_of(s * L, ORDER_ALIGN)
            self.copies = (
                pltpu.make_async_copy(order_ref.at[pl.ds(src, L)], ord_smem.at[pl.ds(dst, L)], sem_o.at[s]),
                pltpu.make_async_copy(tokrow_ref.at[pl.ds(src, L)], tok_smem.at[pl.ds(dst, L)], sem_t.at[s]))

        def start(self):
            for c in self.copies:
                c.start()

        def wait(self):
            for c in self.copies:
                c.wait()

    fetch = _Fetch

    def list_base(g):
        return (g % 4) * L + bdel_ref[jnp.minimum(g, nb_max - 1)]

    def gather(g):
        base = list_base(g)
        trow = (g % 2) * (PACK_ROWS * TILE_STRIDE)
        for r in range(R):
            t4 = pl.multiple_of(tok_smem[base + r], PACK_ROWS)
            tile_scr[pl.ds(trow + r, PACK_ROWS, stride=TILE_STRIDE), :] = xp_ref[pl.ds(t4, PACK_ROWS), :]

    def rows_sent(cnt):
        return pl.multiple_of(lax.shift_left(lax.shift_right_logical(cnt + 7, 3), 3), 8)

    def wait_rows(s, cnt):
        pltpu.make_async_copy(ybuf.at[s, pl.ds(0, cnt * C), :], yt_ref.at[pl.ds(0, cnt * C), :], sem.at[s]).wait()

    @pl.when(e == 0)
    def _():
        n_spare = 2 * TOP_K * C
        ybuf[0, 0:n_spare, :] = jnp.zeros((n_spare, 128), I32)
        init = pltpu.make_async_copy(ybuf.at[0, pl.ds(0, n_spare), :],
                                     yt_ref.at[pl.ds(yt_ref.shape[0] - n_spare, n_spare), :], sem.at[0])
        init.start()
        init.wait()
        fetch(0).start()
        fetch(1).start()
        fetch(0).wait()
        gather(0)

    @pl.when(nblk > 0)
    def _():
        wgb[...] = wg_ref[0].astype(BF16)
        wub[...] = wu_ref[0].astype(BF16)
        wdb[...] = wd_ref[0].astype(BF16)

    def block(k, carry):
        g = g0 + k
        slot = g % 2
        n = bn_ref[g]
        fetch(g + 1).wait()
        fetch(g + 2).start()

        @pl.when(g >= 2)
        def _():
            wait_rows(slot, rows_sent(bn_ref[jnp.maximum(g - 2, 0)]))

        trow = pl.multiple_of(slot * (PACK_ROWS * TILE_STRIDE), 8)

        def compute(send_prev):
            los, his = [], []
            for j in range(PACK_ROWS):
                lo, hi = _unpack_bf16_pairs(tile_scr[pl.ds(trow + j * TILE_STRIDE, R), :])
                los.append(lo.astype(BF16))
                his.append(hi.astype(BF16))
            xg = jnp.concatenate(los + his, axis=1)
            gather(g + 1)
            if send_prev:
                pbase = list_base(g - 1)
                for r in range(R):
                    send(1 - slot, r, ord_smem[pbase + r], r)
            hg = _bdot(xg, wgb[...])
            hu = _bdot(xg, wub[...])
            hb = (hg * _sigmoid(hg)) * hu
            y = _bdot(hb.astype(BF16), wdb[...])
            words = _pack_bf16_pairs(y)
            yb = ybuf.at[slot]
            for c in range(C):
                yb[pl.ds(c, R, stride=C), :] = words[:, c * 128:(c + 1) * 128]

        prev_full = jnp.logical_and(g >= 1, bn_ref[jnp.maximum(g - 1, 0)] == R)

        @pl.when(prev_full)
        def _():
            compute(True)

        @pl.when(jnp.logical_not(prev_full))
        def _():
            compute(False)

        @pl.when(n < R)
        def _():
            send_rows(g)

        return carry

    def send(s, r, d, u):
        pltpu.make_async_copy(ybuf.at[s, pl.ds(pl.multiple_of(r * C, C), C), :],
                              yt_ref.at[pl.ds(pl.multiple_of(d * C, C), C), :], sem.at[s]).start(priority=u % 2)

    def send_rows(g):
        slot = g % 2
        n = bn_ref[g]
        base = list_base(g)

        def send_group(i, c2):
            for u in range(8):
                send(slot, i * 8 + u, ord_smem[base + i * 8 + u], u)
            return c2

        n_full = lax.shift_right_logical(n, 3)
        lax.fori_loop(0, n_full, send_group, 0)

        @pl.when(n_full * 8 < n)
        def _():
            for u in range(8):
                r = n_full * 8 + u
                send(slot, r, jnp.where(r < n, ord_smem[base + r], n_assign + slot * 8 + u), u)

    lax.fori_loop(0, nblk, block, 0)

    @pl.when(e == pl.num_programs(0) - 1)
    def _():
        @pl.when(bn_ref[jnp.maximum(total - 1, 0)] == R)
        def _():
            send_rows(total - 1)

        fetch(total + 1).wait()
        wait_rows((total - 1) % 2, rows_sent(bn_ref[jnp.maximum(total - 1, 0)]))

        @pl.when(total >= 2)
        def _():
            wait_rows(total % 2, rows_sent(bn_ref[jnp.maximum(total - 2, 0)]))


def _moe(ex_b0, ex_nb, ex_w, total, blk_n, blk_src, blk_delta, order, tokrow, xp, wg, wu, wd, n_tok):
    R = MOE_ROWS
    wmap_in = lambda e, eb0, enb, wsel, tot, bn, bs, bd: (wsel[e], 0, 0)
    grid_spec = pltpu.PrefetchScalarGridSpec(
        num_scalar_prefetch=7,
        grid=(N_EXPERTS,),
        in_specs=[
            pl.BlockSpec(memory_space=pl.ANY),
            pl.BlockSpec(memory_space=pl.ANY),
            pl.BlockSpec(memory_space=pltpu.VMEM),
            pl.BlockSpec((1, D_MODEL, D_EXPERT), wmap_in),
            pl.BlockSpec((1, D_MODEL, D_EXPERT), wmap_in),
            pl.BlockSpec((1, D_EXPERT, D_MODEL), wmap_in),
        ],
        out_specs=pl.BlockSpec(memory_space=pl.ANY),
        scratch_shapes=[pltpu.VMEM((2 * PACK_ROWS * TILE_STRIDE, 128), I32),
                        pltpu.VMEM((2, R * PACK_ROWS, 128), I32),
                        pltpu.VMEM((D_MODEL, D_EXPERT), BF16),
                        pltpu.VMEM((D_MODEL, D_EXPERT), BF16),
                        pltpu.VMEM((D_EXPERT, D_MODEL), BF16),
                        pltpu.SMEM((4 * (R + ORDER_ALIGN),), I32),
                        pltpu.SMEM((4 * (R + ORDER_ALIGN),), I32),
                        pltpu.SemaphoreType.DMA((2,)),
                        pltpu.SemaphoreType.DMA((4,)),
                        pltpu.SemaphoreType.DMA((4,))],
    )
    return pl.pallas_call(
        _moe_kernel,
        out_shape=jax.ShapeDtypeStruct(((n_tok + 2) * TOP_K * PACK_ROWS, 128), I32),
        grid_spec=grid_spec,
        compiler_params=pltpu.CompilerParams(dimension_semantics=("arbitrary",),
                                             vmem_limit_bytes=MOE_VMEM_LIMIT),
        name="moe",
    )(ex_b0, ex_nb, ex_w, total, blk_n, blk_src, blk_delta, order, tokrow, xp, wg, wu, wd)


def _route_plan(idx, counts, n_tok):
    R = MOE_ROWS
    n_assign = n_tok * TOP_K
    nb = (n_assign + N_EXPERTS * (R - 1)) // R
    id_bits = (n_assign - 1).bit_length()
    assert id_bits + (N_EXPERTS - 1).bit_length() < 32
    key = lax.shift_left(idx.reshape(-1), id_bits) | jnp.arange(n_assign, dtype=I32)
    order = lax.sort(key) & ((1 << id_bits) - 1)
    order = jnp.concatenate([order, jnp.zeros((R + ORDER_ALIGN,), I32)])
    nblk_e = (counts + R - 1) // R
    bend = jnp.cumsum(nblk_e)
    bstart = bend - nblk_e
    cstart = jnp.cumsum(counts) - counts
    blk = jnp.arange(nb, dtype=I32)
    blk_e = jnp.minimum(jnp.sum((bend[None, :] <= blk[:, None]).astype(I32), axis=1), N_EXPERTS - 1)
    k = blk - bstart[blk_e]
    active = blk < bend[-1]
    blk_n = jnp.where(active, jnp.clip(counts[blk_e] - k * R, 0, R), 0).astype(I32)
    src = jnp.where(active, cstart[blk_e] + k * R, 0).astype(I32)
    blk_src = (src // ORDER_ALIGN) * ORDER_ALIGN
    ex = jnp.arange(N_EXPERTS, dtype=I32)
    ex_w = jnp.maximum(lax.cummax(jnp.where(nblk_e > 0, ex, -1)), 0).astype(I32)
    tokrow = (order // TOP_K) * PACK_ROWS
    return (bstart.astype(I32), nblk_e.astype(I32), ex_w, bend[-1:].astype(I32),
            blk_n, blk_src, src - blk_src, order, tokrow)


def _combine_kernel(x1_ref, yt_ref, gate_ref, wgs_ref, wus_ref, wds_ref, g2_ref, b2_ref, out_ref):
    tm = x1_ref.shape[0]
    x1 = x1_ref[...]
    xb = x1.astype(BF16)
    hg = _bdot(xb, wgs_ref[...])
    hs = (hg * _sigmoid(hg)) * _bdot(xb, wus_ref[...])
    moe = _bdot(hs.astype(BF16), wds_ref[...])
    gate = gate_ref[...]
    per_tok = TOP_K * PACK_ROWS
    planes = pltpu.einshape("tjl->jtl", yt_ref[...].reshape(tm, per_tok, 128))
    lo_acc = [None] * PACK_ROWS
    hi_acc = [None] * PACK_ROWS
    for j in range(TOP_K):
        gj = gate[:, j:j + 1]
        for c in range(PACK_ROWS):
            lo, hi = _unpack_bf16_pairs(planes[j * PACK_ROWS + c])
            lo_acc[c] = lo * gj if j == 0 else lo_acc[c] + lo * gj
            hi_acc[c] = hi * gj if j == 0 else hi_acc[c] + hi * gj
    routed = jnp.concatenate(lo_acc + hi_acc, axis=1)
    out_ref[...] = _ln(ALPHA * x1 + (moe + routed), g2_ref[...], b2_ref[...])


def _combine(x1, yt, gate, wgs_bf, wus_bf, wds_bf, g2, b2, row0, nrows):
    tm = _row_tile(nrows)
    assert row0 % tm == 0
    off = row0 // tm
    row = lambda i: (i + off, 0)
    const = lambda i: (0, 0)
    return pl.pallas_call(
        _combine_kernel,
        out_shape=jax.ShapeDtypeStruct((nrows, D_MODEL), F32),
        grid=(nrows // tm,),
        in_specs=[pl.BlockSpec((tm, D_MODEL), row),
                  pl.BlockSpec((tm * TOP_K * PACK_ROWS, 128), row),
                  pl.BlockSpec((tm, 128), row),
                  pl.BlockSpec((D_MODEL, D_EXPERT), const), pl.BlockSpec((D_MODEL, D_EXPERT), const),
                  pl.BlockSpec((D_EXPERT, D_MODEL), const),
                  pl.BlockSpec((1, D_MODEL), const), pl.BlockSpec((1, D_MODEL), const)],
        out_specs=pl.BlockSpec((tm, D_MODEL), lambda i: (i, 0)),
        compiler_params=pltpu.CompilerParams(dimension_semantics=("arbitrary",), vmem_limit_bytes=VMEM_LIMIT),
        name="combine",
    )(x1, yt, gate, wgs_bf, wus_bf, wds_bf, g2, b2)


def kernel(x_prompt, x_sample, state_hgrn, state_pool, meta_tokens, ln_emb_g, ln_emb_b, w_in, lb_logits, hgrn_norm_g, w_pool, pool_scale, w_out, ln1_g, ln1_b, w_router, b_router, w_gate_e, w_up_e, w_down_e, w_gate_s, w_up_s, w_down_s, ln2_g, ln2_b):
    nseq, seqlen, _ = x_prompt.shape
    ndec = x_sample.shape[0]
    l = 0
    row = lambda a: a.reshape(1, -1)
    w_in_bf = w_in[l].astype(BF16)
    wpool_bf = w_pool[l].astype(BF16)
    lng, lnb = row(ln_emb_g), row(ln_emb_b)
    ng, ps = row(hgrn_norm_g[l]), row(pool_scale[l])
    proj = functools.partial(_ln_proj, ln_g=lng, ln_b=lnb, w_in_bf=w_in_bf, lb_logits=lb_logits, norm_g=ng)

    m_xn, m_q, m_k, m_g, m_v, m_gs, m_p = proj(meta_tokens)
    pad = lambda a: jnp.pad(a, ((0, MIX_BLOCK - N_META), (0, 0)))
    zero_state = jnp.zeros((HGRN_HEADS, HEAD_DIM, HEAD_DIM), F32)
    _, s_meta = _mixer(pad(m_q), pad(m_k), pad(m_g), pad(m_v), pad(m_gs), pad(m_p), zero_state,
                       jnp.zeros((SUB, POOL_WIDTH), F32), wpool_bf, ps, 1, MIX_BLOCK)

    p_xn, p_q, p_k, p_g, p_v, p_gs, p_p = proj(x_prompt.reshape(nseq * seqlen, D_MODEL))
    p_mix, s_prompt = _mixer(p_q, p_k, p_g, p_v, p_gs, p_p, s_meta[0], m_p, wpool_bf, ps, nseq, seqlen)

    d_xn, d_q, d_k, d_g, d_v, d_gs, d_p = proj(x_sample.reshape(ndec, D_MODEL))
    cols = lambda a: a.reshape(ndec // STEP_SEQS, STEP_SEQS, HGRN_HEADS, HEAD_DIM).transpose(2, 0, 3, 1)
    s_dec, d_oa = _mixer_step(cols(d_q), cols(d_k), cols(d_g), d_v, d_gs, state_hgrn[l])
    d_ob = _pool_step(state_pool[l].transpose(1, 0, 2), d_p, wpool_bf, ps)
    d_mix = jnp.concatenate([d_oa, d_ob], axis=1).astype(BF16)

    n_tok = nseq * seqlen + ndec
    x1, xp, idx_t, gate_t, cnt = _outproj(p_mix, p_xn, d_mix, d_xn, w_out[l].astype(BF16), row(ln1_g[l]),
                                          row(ln1_b[l]), w_router[l].T.astype(BF16), b_router[l].reshape(-1, 1))
    idx = idx_t[:, :n_tok].T
    gate = jnp.pad(gate_t.T, ((0, 0), (0, 128 - TOP_K)))
    plan = _route_plan(idx, jnp.sum(cnt, axis=1).astype(I32), n_tok)
    yt = _moe(*plan, xp, w_gate_e[l], w_up_e[l], w_down_e[l], n_tok)
    comb = functools.partial(_combine, x1, yt, gate, w_gate_s[l].astype(BF16), w_up_s[l].astype(BF16),
                             w_down_s[l].astype(BF16), row(ln2_g[l]), row(ln2_b[l]))
    y_prompt = comb(0, nseq * seqlen).reshape(nseq, seqlen, D_MODEL)
    y_sample = comb(nseq * seqlen, ndec).reshape(ndec, 1, D_MODEL)

    state_pool_prompt = p_p.reshape(nseq, seqlen, POOL_WIDTH)[:, seqlen - POOL_BUF:, :]
    state_pool_sample = jnp.concatenate([state_pool[l][:, 1:, :], d_p[:, None, :]], axis=1)
    return (y_prompt, y_sample, s_prompt[None], state_pool_prompt[None], s_dec[None], state_pool_sample[None])
```

```python
import functools

import jax
import jax.numpy as jnp
from jax import lax
from jax.experimental import pallas as pl
from jax.experimental.pallas import tpu as pltpu

F32 = jnp.float32
BF16 = jnp.bfloat16
I32 = jnp.int32

D_MODEL = 1024
N_META = 16
HGRN_WIDTH = 512
HGRN_HEADS = 4
HEAD_DIM = 128
POOL_WIDTH = 512
POOL_WINDOWS = (2, 4, 8, 16)
POOL_GROUP_DIM = 128
POOL_BUF = 15
IN_WIDTH = 4 * HGRN_WIDTH + POOL_WIDTH
N_EXPERTS = 256
TOP_K = 8
D_EXPERT = 256
ROUTED_SCALE = 2.5
DEPTH = 1
ALPHA = (2 * DEPTH) ** 0.25
LN_EPS = 1e-5
RMS_EPS = 1e-6
LOG2_E = 1.4426950408889634

SUB = 16
MIX_BLOCK = 128
MOE_ROWS = 256
TILE_STRIDE = MOE_ROWS + 8
ORDER_ALIGN = 128
PACK_ROWS = D_MODEL // 2 // 128
OUT_TILE = 512
V7X_VMEM_BYTES = 64 * 1024 * 1024
VMEM_LIMIT = V7X_VMEM_BYTES - 8 * 1024 * 1024
MOE_VMEM_LIMIT = V7X_VMEM_BYTES - 6 * 1024 * 1024


def _ln(x, g, b):
    mu = jnp.mean(x, axis=-1, keepdims=True)
    xc = x - mu
    var = jnp.mean(xc * xc, axis=-1, keepdims=True)
    return xc * lax.rsqrt(var + LN_EPS) * g + b


def _sigmoid(z):
    return 1.0 / (1.0 + jnp.exp(-z))


def _bdot(a, b):
    return jnp.dot(a, b, preferred_element_type=F32)


def _pack_bf16_pairs(x):
    half = x.shape[1] // 2
    xr = x.astype(BF16).astype(F32)
    lo = lax.shift_right_logical(lax.bitcast_convert_type(xr[:, :half], jnp.uint32), jnp.uint32(16))
    hi = lax.bitcast_convert_type(xr[:, half:], jnp.uint32) & jnp.uint32(0xFFFF0000)
    return lax.bitcast_convert_type(hi | lo, I32)


def _unpack_bf16_pairs(words):
    w = lax.bitcast_convert_type(words, jnp.uint32)
    lo = lax.bitcast_convert_type(lax.shift_left(w, jnp.uint32(16)), F32)
    hi = lax.bitcast_convert_type(w & jnp.uint32(0xFFFF0000), F32)
    return lo, hi


def _ln_proj_kernel(x_ref, g_ref, b_ref, w_ref, lbl_ref, ng_ref,
                    xn_ref, q_ref, k_ref, gl_ref, v_ref, gs_ref, p_ref):
    xn = _ln(x_ref[...], g_ref[...], b_ref[...])
    xn_ref[...] = xn
    proj = _bdot(xn.astype(BF16), w_ref[...])
    lbl = lbl_ref[...]
    e = jnp.exp(lbl - jnp.max(lbl, axis=0, keepdims=True))
    lb = e[0:1] / jnp.sum(e, axis=0, keepdims=True)
    W = HGRN_WIDTH
    q = proj[:, 0:W]
    f = proj[:, W:2 * W]
    q_ref[...] = q * _sigmoid(q)
    k_ref[...] = (1.0 - lb) * _sigmoid(-f)
    gl_ref[...] = jnp.log(lb + (1.0 - lb) * _sigmoid(f))
    v_ref[...] = proj[:, 2 * W:3 * W]
    g = proj[:, 3 * W:4 * W]
    gs_ref[...] = ng_ref[...] * (g * _sigmoid(g))
    p_ref[...] = proj[:, 4 * W:]


def _row_tile(n_rows):
    for tm in (512, 256, 128, 64, 32, 16, 8):
        if n_rows % tm == 0:
            return tm
    raise ValueError(f"row count {n_rows} is not a multiple of 8")


def _ln_proj(x, ln_g, ln_b, w_in_bf, lb_logits, norm_g):
    T = x.shape[0]
    tm = _row_tile(T)
    row = lambda i: (i, 0)
    const = lambda i: (0, 0)
    outs = [jax.ShapeDtypeStruct((T, D_MODEL), F32)] + [jax.ShapeDtypeStruct((T, HGRN_WIDTH), F32)] * 6
    return pl.pallas_call(
        _ln_proj_kernel,
        out_shape=outs,
        grid=(T // tm,),
        in_specs=[
            pl.BlockSpec((tm, D_MODEL), row),
            pl.BlockSpec((1, D_MODEL), const),
            pl.BlockSpec((1, D_MODEL), const),
            pl.BlockSpec((D_MODEL, IN_WIDTH), const),
            pl.BlockSpec((DEPTH + 1, HGRN_WIDTH), const),
            pl.BlockSpec((1, HGRN_WIDTH), const),
        ],
        out_specs=[pl.BlockSpec((tm, D_MODEL), row)] + [pl.BlockSpec((tm, HGRN_WIDTH), row)] * 6,
        compiler_params=pltpu.CompilerParams(dimension_semantics=("arbitrary",), vmem_limit_bytes=VMEM_LIMIT),
        name="ln_proj",
    )(x, ln_g, ln_b, w_in_bf, lb_logits, norm_g)


def _pool_group(pe, p_cur, gi, w):
    sl = slice(gi * POOL_GROUP_DIM, (gi + 1) * POOL_GROUP_DIM)
    s = pe[:, sl]
    sh = 1
    while sh < w:
        s = s + pltpu.roll(s, sh, 0)
        sh *= 2
    return s[SUB:, :] * (1.0 / w) - p_cur[:, sl]


def _mixer_kernel(q_ref, k_ref, g_ref, v_ref, gs_ref, p_ref, s0_ref, pp0_ref, wpool_ref, pscale_ref,
                  mix_ref, sfin_ref, st_scr, pe_scr, bf_scr):
    i = pl.program_id(1)
    nblk = pl.num_programs(1)
    n = MIX_BLOCK

    @pl.when(i == 0)
    def _():
        for h in range(HGRN_HEADS):
            st_scr[h] = s0_ref[h].T
        pe_scr[0:SUB, :] = pp0_ref[...]

    rows = lax.broadcasted_iota(I32, (n, HEAD_DIM), 0)
    r16 = rows & (SUB - 1)
    t8 = lax.broadcasted_iota(I32, (8, HEAD_DIM), 0)
    zero_bf = jnp.zeros((SUB, HEAD_DIM), BF16)

    o_heads = []
    for h in range(HGRN_HEADS):
        hs = slice(h * HEAD_DIM, (h + 1) * HEAD_DIM)
        Q = q_ref[:, hs]
        K = k_ref[:, hs]
        G = g_ref[:, hs]
        V = v_ref[:, hs]
        bf = G
        br = G
        for sh in (1, 2, 4, 8):
            bf = bf + jnp.where(r16 >= sh, pltpu.roll(bf, sh, 0), 0.0)
            br = br + jnp.where(r16 < SUB - sh, pltpu.roll(br, n - sh, 0), 0.0)
        br = br - G
        bf2 = bf * LOG2_E
        bf_scr[:, hs] = bf2
        qt = (Q * jnp.exp(bf)).astype(BF16)
        kt = (K * jnp.exp(br)).astype(BF16)
        vt = V.T.astype(BF16)
        st = st_scr[h]
        o_parts = []
        for c in range(n // SUB):
            r0 = c * SUB
            b_top, b_bot = bf2[r0:r0 + 8], bf2[r0 + 8:r0 + 16]
            q_top, q_bot = Q[r0:r0 + 8], Q[r0 + 8:r0 + 16]
            acc_top = jnp.zeros((8, HEAD_DIM), F32)
            acc_bot = jnp.zeros((8, HEAD_DIM), F32)
            for s in range(SUB):
                bs = bf_scr[r0 + s:r0 + s + 1, hs]
                ks = k_ref[r0 + s:r0 + s + 1, hs]
                vs = v_ref[r0 + s:r0 + s + 1, hs]
                if s < 8:
                    col = jnp.sum(q_top * jnp.exp2(b_top - bs) * ks, axis=-1, keepdims=True)
                    col = jnp.where(t8[:, 0:1] >= s, col, 0.0)
                    acc_top = acc_top + col * vs
                    col = jnp.sum(q_bot * jnp.exp2(b_bot - bs) * ks, axis=-1, keepdims=True)
                    acc_bot = acc_bot + col * vs
                else:
                    col = jnp.sum(q_bot * jnp.exp2(b_bot - bs) * ks, axis=-1, keepdims=True)
                    col = jnp.where(t8[:, 0:1] + 8 >= s, col, 0.0)
                    acc_bot = acc_bot + col * vs
            o_diag = jnp.concatenate([acc_top, acc_bot], axis=0)
            o_inter = lax.dot_general(qt[r0:r0 + SUB], st.astype(BF16), (((1,), (1,)), ((), ())),
                                      preferred_element_type=F32)
            o_parts.append(o_inter + o_diag)
            kmask = jnp.concatenate([zero_bf] * c + [kt[r0:r0 + SUB]] + [zero_bf] * (n // SUB - 1 - c), axis=0)
            d_st = _bdot(vt, kmask)
            st = st * jnp.exp(bf[r0 + SUB - 1:r0 + SUB]) + d_st
        st_scr[h] = st
        o = jnp.concatenate(o_parts, axis=0)
        o = o * lax.rsqrt(jnp.mean(o * o, axis=-1, keepdims=True) + RMS_EPS)
        o_heads.append(o * gs_ref[:, hs])

    p_cur = p_ref[...]
    pe_scr[SUB:SUB + n, :] = p_cur
    pe = pe_scr[...]
    ob = []
    for gi, w in enumerate(POOL_WINDOWS):
        pooled = _pool_group(pe, p_cur, gi, w)
        sl = slice(gi * POOL_GROUP_DIM, (gi + 1) * POOL_GROUP_DIM)
        ob.append(_bdot(pooled.astype(BF16), wpool_ref[gi]) * pscale_ref[:, sl])
    pe_scr[0:SUB, :] = p_cur[n - SUB:, :]

    mix_ref[...] = jnp.concatenate(o_heads + ob, axis=1).astype(BF16)

    @pl.when(i == nblk - 1)
    def _():
        for h in range(HGRN_HEADS):
            sfin_ref[0, h] = st_scr[h].T


def _mixer(q, k, g, v, gs, p, s0, pp0, wpool_bf, pscale, nseq, seqlen):
    nblk = seqlen // MIX_BLOCK
    tok = lambda b, i: (b * nblk + i, 0)
    tspec = pl.BlockSpec((MIX_BLOCK, HGRN_WIDTH), tok)
    return pl.pallas_call(
        _mixer_kernel,
        out_shape=[jax.ShapeDtypeStruct((nseq * seqlen, D_MODEL), BF16),
                   jax.ShapeDtypeStruct((nseq, HGRN_HEADS, HEAD_DIM, HEAD_DIM), F32)],
        grid=(nseq, nblk),
        in_specs=[tspec] * 6 + [
            pl.BlockSpec((HGRN_HEADS, HEAD_DIM, HEAD_DIM), lambda b, i: (0, 0, 0)),
            pl.BlockSpec((SUB, POOL_WIDTH), lambda b, i: (0, 0)),
            pl.BlockSpec((len(POOL_WINDOWS), POOL_GROUP_DIM, POOL_GROUP_DIM), lambda b, i: (0, 0, 0)),
            pl.BlockSpec((1, POOL_WIDTH), lambda b, i: (0, 0)),
        ],
        out_specs=[pl.BlockSpec((MIX_BLOCK, D_MODEL), tok),
                   pl.BlockSpec((1, HGRN_HEADS, HEAD_DIM, HEAD_DIM), lambda b, i: (b, 0, 0, 0))],
        scratch_shapes=[pltpu.VMEM((HGRN_HEADS, HEAD_DIM, HEAD_DIM), F32),
                        pltpu.VMEM((SUB + MIX_BLOCK, POOL_WIDTH), F32),
                        pltpu.VMEM((MIX_BLOCK, HGRN_WIDTH), F32)],
        compiler_params=pltpu.CompilerParams(dimension_semantics=("arbitrary", "arbitrary"),
                                             vmem_limit_bytes=VMEM_LIMIT),
        name="mixer",
    )(q, k, g, v, gs, p, s0, pp0, wpool_bf, pscale)


STEP_SEQS = 64


def _mixer_step_kernel(qt_ref, kt_ref, gt_ref, v_ref, gs_ref, s_ref, snew_ref, oa_ref):
    qt = qt_ref[0, 0]
    kt = kt_ref[0, 0]
    dt = jnp.exp(gt_ref[0, 0])
    rows = []
    for bb in range(STEP_SEQS):
        sn = s_ref[bb, 0] * dt[:, bb:bb + 1] + kt[:, bb:bb + 1] * v_ref[bb:bb + 1, :]
        snew_ref[bb, 0] = sn
        rows.append(jnp.sum(sn * qt[:, bb:bb + 1], axis=0, keepdims=True))
    o = jnp.concatenate(rows, axis=0)
    o = o * lax.rsqrt(jnp.mean(o * o, axis=-1, keepdims=True) + RMS_EPS)
    oa_ref[...] = o * gs_ref[...]


def _mixer_step(qT, kT, gT, v, gs, state):
    nseq = v.shape[0]
    nbc = nseq // STEP_SEQS
    cspec = pl.BlockSpec((1, 1, HEAD_DIM, STEP_SEQS), lambda h, c: (h, c, 0, 0))
    rspec = pl.BlockSpec((STEP_SEQS, HEAD_DIM), lambda h, c: (c, h))
    sspec = pl.BlockSpec((STEP_SEQS, 1, HEAD_DIM, HEAD_DIM), lambda h, c: (c, h, 0, 0))
    return pl.pallas_call(
        _mixer_step_kernel,
        out_shape=[jax.ShapeDtypeStruct(state.shape, F32), jax.ShapeDtypeStruct((nseq, HGRN_WIDTH), F32)],
        grid=(HGRN_HEADS, nbc),
        in_specs=[cspec, cspec, cspec, rspec, rspec, sspec],
        out_specs=[sspec, rspec],
        compiler_params=pltpu.CompilerParams(dimension_semantics=("arbitrary", "arbitrary"),
                                             vmem_limit_bytes=VMEM_LIMIT),
        name="mixer_step",
    )(qT, kT, gT, v, gs, state)


def _pool_step_kernel(sp_ref, p_ref, wpool_ref, pscale_ref, ob_ref):
    p_cur = p_ref[...]
    outs = []
    for gi, w in enumerate(POOL_WINDOWS):
        sl = slice(gi * POOL_GROUP_DIM, (gi + 1) * POOL_GROUP_DIM)
        s = p_cur[:, sl]
        for r in range(POOL_BUF - (w - 1), POOL_BUF):
            s = s + sp_ref[r][:, sl]
        pooled = s * (1.0 / w) - p_cur[:, sl]
        outs.append(_bdot(pooled.astype(BF16), wpool_ref[gi]) * pscale_ref[:, sl])
    ob_ref[...] = jnp.concatenate(outs, axis=1)


def _pool_step(spT, p, wpool_bf, pscale):
    nseq = p.shape[0]
    return pl.pallas_call(
        _pool_step_kernel,
        out_shape=jax.ShapeDtypeStruct((nseq, POOL_WIDTH), F32),
        name="pool_step",
    )(spT, p, wpool_bf, pscale)


def _outproj_kernel(mixp_ref, xnp_ref, mixd_ref, xnd_ref, wout_ref, g1_ref, b1_ref, wrt_ref, brt_ref,
                    x1_ref, x1p_ref, idx_ref, gate_ref, cnt_ref, *, n_prompt_blocks, n_valid_last):
    tm = mixp_ref.shape[0]
    i = pl.program_id(0)
    is_prompt = i < n_prompt_blocks
    mix_in = jnp.where(is_prompt, mixp_ref[...], mixd_ref[...])
    xn = jnp.where(is_prompt, xnp_ref[...], xnd_ref[...])
    mix = _bdot(mix_in, wout_ref[...])
    x1 = _ln(ALPHA * xn + mix, g1_ref[...], b1_ref[...])
    x1_ref[...] = x1
    xb = x1.astype(BF16)
    words = _pack_bf16_pairs(x1)
    for c in range(PACK_ROWS):
        x1p_ref[pl.ds(c, tm, stride=PACK_ROWS), :] = words[:, c * 128:(c + 1) * 128]

    scores = _sigmoid(lax.dot_general(wrt_ref[...], xb, (((1,), (1,)), ((), ())),
                                      preferred_element_type=F32))
    sel = scores + brt_ref[...]
    erow = lax.broadcasted_iota(I32, (N_EXPERTS, tm), 0).astype(F32)
    ssum = jnp.zeros((1, tm), F32)
    chosen = jnp.zeros((N_EXPERTS, tm), F32)
    idx_rows, s_rows = [], []
    for j in range(TOP_K):
        m = jnp.max(sel, axis=0, keepdims=True)
        am = jnp.min(jnp.where(sel == m, erow, float(N_EXPERTS)), axis=0, keepdims=True)
        hit = erow == am
        sj = jnp.sum(jnp.where(hit, scores, 0.0), axis=0, keepdims=True)
        sel = jnp.where(hit, -jnp.inf, sel)
        chosen = jnp.where(hit, 1.0, chosen)
        idx_rows.append(am)
        s_rows.append(sj)
        ssum = ssum + sj
    idx_ref[...] = jnp.concatenate(idx_rows, axis=0).astype(I32)
    gate_ref[...] = jnp.concatenate([s / ssum * ROUTED_SCALE for s in s_rows], axis=0)

    @pl.when(i == 0)
    def _():
        cnt_ref[...] = jnp.zeros_like(cnt_ref)

    col_id = lax.broadcasted_iota(I32, (N_EXPERTS, tm), 1)
    counted = jnp.where(jnp.logical_or(is_prompt, col_id < n_valid_last), chosen, 0.0)
    part = counted[:, 0:128]
    for c in range(1, tm // 128):
        part = part + counted[:, c * 128:(c + 1) * 128]
    cnt_ref[...] += part


def _outproj(mix_p, xn_p, mix_d, xn_d, wout_bf, g1, b1, wrt_bf, brt):
    tm = OUT_TILE
    n_prompt, n_dec = mix_p.shape[0], mix_d.shape[0]
    assert n_prompt % tm == 0 and n_dec <= tm and n_dec % 8 == 0
    nbp = n_prompt // tm
    T = (nbp + 1) * tm
    padrows = lambda a: jnp.pad(a, ((0, tm - n_dec), (0, 0)))
    row = lambda i: (i, 0)
    prow = lambda i: (jnp.minimum(i, nbp - 1), 0)
    const = lambda i: (0, 0)
    return pl.pallas_call(
        functools.partial(_outproj_kernel, n_prompt_blocks=nbp, n_valid_last=n_dec),
        out_shape=[jax.ShapeDtypeStruct((T, D_MODEL), F32), jax.ShapeDtypeStruct((T * PACK_ROWS, 128), I32),
                   jax.ShapeDtypeStruct((TOP_K, T), I32), jax.ShapeDtypeStruct((TOP_K, T), F32),
                   jax.ShapeDtypeStruct((N_EXPERTS, 128), F32)],
        grid=(nbp + 1,),
        in_specs=[pl.BlockSpec((tm, D_MODEL), prow), pl.BlockSpec((tm, D_MODEL), prow),
                  pl.BlockSpec((tm, D_MODEL), const), pl.BlockSpec((tm, D_MODEL), const),
                  pl.BlockSpec((D_MODEL, D_MODEL), const), pl.BlockSpec((1, D_MODEL), const),
                  pl.BlockSpec((1, D_MODEL), const), pl.BlockSpec((N_EXPERTS, D_MODEL), const),
                  pl.BlockSpec((N_EXPERTS, 1), const)],
        out_specs=[pl.BlockSpec((tm, D_MODEL), row), pl.BlockSpec((tm * PACK_ROWS, 128), row),
                   pl.BlockSpec((TOP_K, tm), lambda i: (0, i)), pl.BlockSpec((TOP_K, tm), lambda i: (0, i)),
                   pl.BlockSpec((N_EXPERTS, 128), const)],
        compiler_params=pltpu.CompilerParams(dimension_semantics=("arbitrary",), vmem_limit_bytes=VMEM_LIMIT),
        name="outproj",
    )(mix_p, xn_p, padrows(mix_d), padrows(xn_d), wout_bf, g1, b1, wrt_bf, brt)


def _moe_kernel(eb0_ref, enb_ref, wsel_ref, tot_ref, bn_ref, bsrc_ref, bdel_ref,
                order_ref, tokrow_ref, xp_ref, wg_ref, wu_ref, wd_ref, yt_ref,
                tile_scr, ybuf, wgb, wub, wdb, ord_smem, tok_smem, sem, sem_o, sem_t):
    del wsel_ref
    e = pl.program_id(0)
    g0 = eb0_ref[e]
    nblk = enb_ref[e]
    total = tot_ref[0]
    nb_max = bn_ref.shape[0]
    R = MOE_ROWS
    C = PACK_ROWS
    L = R + ORDER_ALIGN
    n_assign = yt_ref.shape[0] // C - 2 * TOP_K

    class _Fetch:
        def __init__(self, g):
            gc = jnp.minimum(g, nb_max - 1)
            s = g % 4
            src = pl.multiple_of(bsrc_ref[gc], ORDER_ALIGN)
            dst = pl.multiple_of(s * L, ORDER_ALIGN)
            self.copies = (
                pltpu.make_async_copy(order_ref.at[pl.ds(src, L)], ord_smem.at[pl.ds(dst, L)], sem_o.at[s]),
                pltpu.make_async_copy(tokrow_ref.at[pl.ds(src, L)], tok_smem.at[pl.ds(dst, L)], sem_t.at[s]))

        def start(self):
            for c in self.copies:
                c.start()

        def wait(self):
            for c in self.copies:
                c.wait()

    fetch = _Fetch

    def list_base(g):
        return (g % 4) * L + bdel_ref[jnp.minimum(g, nb_max - 1)]

    def gather(g):
        base = list_base(g)
        trow = (g % 2) * (PACK_ROWS * TILE_STRIDE)
        for r in range(R):
            t4 = pl.multiple_of(tok_smem[base + r], PACK_ROWS)
            tile_scr[pl.ds(trow + r, PACK_ROWS, stride=TILE_STRIDE), :] = xp_ref[pl.ds(t4, PACK_ROWS), :]

    def rows_sent(cnt):
        return pl.multiple_of(lax.shift_left(lax.shift_right_logical(cnt + 7, 3), 3), 8)

    def wait_rows(s, cnt):
        pltpu.make_async_copy(ybuf.at[s, pl.ds(0, cnt * C), :], yt_ref.at[pl.ds(0, cnt * C), :], sem.at[s]).wait()

    @pl.when(e == 0)
    def _():
        n_spare = 2 * TOP_K * C
        ybuf[0, 0:n_spare, :] = jnp.zeros((n_spare, 128), I32)
        init = pltpu.make_async_copy(ybuf.at[0, pl.ds(0, n_spare), :],
                                     yt_ref.at[pl.ds(yt_ref.shape[0] - n_spare, n_spare), :], sem.at[0])
        init.start()
        init.wait()
        fetch(0).start()
        fetch(1).start()
        fetch(0).wait()
        gather(0)

    @pl.when(nblk > 0)
    def _():
        wgb[...] = wg_ref[0].astype(BF16)
        wub[...] = wu_ref[0].astype(BF16)
        wdb[...] = wd_ref[0].astype(BF16)

    def block(k, carry):
        g = g0 + k
        slot = g % 2
        n = bn_ref[g]
        fetch(g + 1).wait()
        fetch(g + 2).start()

        @pl.when(g >= 2)
        def _():
            wait_rows(slot, rows_sent(bn_ref[jnp.maximum(g - 2, 0)]))

        trow = pl.multiple_of(slot * (PACK_ROWS * TILE_STRIDE), 8)

        def compute(send_prev):
            los, his = [], []
            for j in range(PACK_ROWS):
                lo, hi = _unpack_bf16_pairs(tile_scr[pl.ds(trow + j * TILE_STRIDE, R), :])
                los.append(lo.astype(BF16))
                his.append(hi.astype(BF16))
            xg = jnp.concatenate(los + his, axis=1)
            gather(g + 1)
            if send_prev:
                pbase = list_base(g - 1)
                for r in range(R):
                    send(1 - slot, r, ord_smem[pbase + r], r)
            hg = _bdot(xg, wgb[...])
            hu = _bdot(xg, wub[...])
            hb = (hg * _sigmoid(hg)) * hu
            y = _bdot(hb.astype(BF16), wdb[...])
            words = _pack_bf16_pairs(y)
            yb = ybuf.at[slot]
            for c in range(C):
                yb[pl.ds(c, R, stride=C), :] = words[:, c * 128:(c + 1) * 128]

        prev_full = jnp.logical_and(g >= 1, bn_ref[jnp.maximum(g - 1, 0)] == R)

        @pl.when(prev_full)
        def _():
            compute(True)

        @pl.when(jnp.logical_not(prev_full))
        def _():
            compute(False)

        @pl.when(n < R)
        def _():
            send_rows(g)

        return carry

    def send(s, r, d, u):
        pltpu.make_async_copy(ybuf.at[s, pl.ds(pl.multiple_of(r * C, C), C), :],
                              yt_ref.at[pl.ds(pl.multiple_of(d * C, C), C), :], sem.at[s]).start(priority=u % 2)

    def send_rows(g):
        slot = g % 2
        n = bn_ref[g]
        base = list_base(g)

        def send_group(i, c2):
            for u in range(8):
                send(slot, i * 8 + u, ord_smem[base + i * 8 + u], u)
            return c2

        n_full = lax.shift_right_logical(n, 3)
        lax.fori_loop(0, n_full, send_group, 0)

        @pl.when(n_full * 8 < n)
        def _():
            for u in range(8):
                r = n_full * 8 + u
                send(slot, r, jnp.where(r < n, ord_smem[base + r], n_assign + slot * 8 + u), u)

    lax.fori_loop(0, nblk, block, 0)

    @pl.when(e == pl.num_programs(0) - 1)
    def _():
        @pl.when(bn_ref[jnp.maximum(total - 1, 0)] == R)
        def _():
            send_rows(total - 1)

        fetch(total + 1).wait()
        wait_rows((total - 1) % 2, rows_sent(bn_ref[jnp.maximum(total - 1, 0)]))

        @pl.when(total >= 2)
        def _():
            wait_rows(total % 2, rows_sent(bn_ref[jnp.maximum(total - 2, 0)]))


def _moe(ex_b0, ex_nb, ex_w, total, blk_n, blk_src, blk_delta, order, tokrow, xp, wg, wu, wd, n_tok):
    R = MOE_ROWS
    wmap_in = lambda e, eb0, enb, wsel, tot, bn, bs, bd: (wsel[e], 0, 0)
    grid_spec = pltpu.PrefetchScalarGridSpec(
        num_scalar_prefetch=7,
        grid=(N_EXPERTS,),
        in_specs=[
            pl.BlockSpec(memory_space=pl.ANY),
            pl.BlockSpec(memory_space=pl.ANY),
            pl.BlockSpec(memory_space=pltpu.VMEM),
            pl.BlockSpec((1, D_MODEL, D_EXPERT), wmap_in),
            pl.BlockSpec((1, D_MODEL, D_EXPERT), wmap_in),
            pl.BlockSpec((1, D_EXPERT, D_MODEL), wmap_in),
        ],
        out_specs=pl.BlockSpec(memory_space=pl.ANY),
        scratch_shapes=[pltpu.VMEM((2 * PACK_ROWS * TILE_STRIDE, 128), I32),
                        pltpu.VMEM((2, R * PACK_ROWS, 128), I32),
                        pltpu.VMEM((D_MODEL, D_EXPERT), BF16),
                        pltpu.VMEM((D_MODEL, D_EXPERT), BF16),
                        pltpu.VMEM((D_EXPERT, D_MODEL), BF16),
                        pltpu.SMEM((4 * (R + ORDER_ALIGN),), I32),
                        pltpu.SMEM((4 * (R + ORDER_ALIGN),), I32),
                        pltpu.SemaphoreType.DMA((2,)),
                        pltpu.SemaphoreType.DMA((4,)),
                        pltpu.SemaphoreType.DMA((4,))],
    )
    return pl.pallas_call(
        _moe_kernel,
        out_shape=jax.ShapeDtypeStruct(((n_tok + 2) * TOP_K * PACK_ROWS, 128), I32),
        grid_spec=grid_spec,
        compiler_params=pltpu.CompilerParams(dimension_semantics=("arbitrary",),
                                             vmem_limit_bytes=MOE_VMEM_LIMIT),
        name="moe",
    )(ex_b0, ex_nb, ex_w, total, blk_n, blk_src, blk_delta, order, tokrow, xp, wg, wu, wd)


def _route_plan(idx, counts, n_tok):
    R = MOE_ROWS
    n_assign = n_tok * TOP_K
    nb = (n_assign + N_EXPERTS * (R - 1)) // R
    id_bits = (n_assign - 1).bit_length()
    assert id_bits + (N_EXPERTS - 1).bit_length() < 32
    key = lax.shift_left(idx.reshape(-1), id_bits) | jnp.arange(n_assign, dtype=I32)
    order = lax.sort(key) & ((1 << id_bits) - 1)
    order = jnp.concatenate([order, jnp.zeros((R + ORDER_ALIGN,), I32)])
    nblk_e = (counts + R - 1) // R
    bend = jnp.cumsum(nblk_e)
    bstart = bend - nblk_e
    cstart = jnp.cumsum(counts) - counts
    blk = jnp.arange(nb, dtype=I32)
    blk_e = jnp.minimum(jnp.sum((bend[None, :] <= blk[:, None]).astype(I32), axis=1), N_EXPERTS - 1)
    k = blk - bstart[blk_e]
    active = blk < bend[-1]
    blk_n = jnp.where(active, jnp.clip(counts[blk_e] - k * R, 0, R), 0).astype(I32)
    src = jnp.where(active, cstart[blk_e] + k * R, 0).astype(I32)
    blk_src = (src // ORDER_ALIGN) * ORDER_ALIGN
    ex = jnp.arange(N_EXPERTS, dtype=I32)
    ex_w = jnp.maximum(lax.cummax(jnp.where(nblk_e > 0, ex, -1)), 0).astype(I32)
    tokrow = (order // TOP_K) * PACK_ROWS
    return (bstart.astype(I32), nblk_e.astype(I32), ex_w, bend[-1:].astype(I32),
            blk_n, blk_src, src - blk_src, order, tokrow)


def _combine_kernel(x1_ref, yt_ref, gate_ref, wgs_ref, wus_ref, wds_ref, g2_ref, b2_ref, out_ref):
    tm = x1_ref.shape[0]
    x1 = x1_ref[...]
    xb = x1.astype(BF16)
    hg = _bdot(xb, wgs_ref[...])
    hs = (hg * _sigmoid(hg)) * _bdot(xb, wus_ref[...])
    moe = _bdot(hs.astype(BF16), wds_ref[...])
    gate = gate_ref[...].T
    per_tok = TOP_K * PACK_ROWS
    planes = pltpu.einshape("tjl->jtl", yt_ref[...].reshape(tm, per_tok, 128))
    lo_acc = [None] * PACK_ROWS
    hi_acc = [None] * PACK_ROWS
    for j in range(TOP_K):
        gj = gate[:, j:j + 1]
        for c in range(PACK_ROWS):
            lo, hi = _unpack_bf16_pairs(planes[j * PACK_ROWS + c])
            lo_acc[c] = lo * gj if j == 0 else lo_acc[c] + lo * gj
            hi_acc[c] = hi * gj if j == 0 else hi_acc[c] + hi * gj
    routed = jnp.concatenate(lo_acc + hi_acc, axis=1)
    out_ref[...] = _ln(ALPHA * x1 + (moe + routed), g2_ref[...], b2_ref[...])


def _combine(x1, yt, gate, wgs_bf, wus_bf, wds_bf, g2, b2, row0, nrows):
    tm = _row_tile(nrows)
    assert row0 % tm == 0
    off = row0 // tm
    row = lambda i: (i + off, 0)
    const = lambda i: (0, 0)
    return pl.pallas_call(
        _combine_kernel,
        out_shape=jax.ShapeDtypeStruct((nrows, D_MODEL), F32),
        grid=(nrows // tm,),
        in_specs=[pl.BlockSpec((tm, D_MODEL), row),
                  pl.BlockSpec((tm * TOP_K * PACK_ROWS, 128), row),
                  pl.BlockSpec((TOP_K, tm), lambda i: (0, i + off)),
                  pl.BlockSpec((D_MODEL, D_EXPERT), const), pl.BlockSpec((D_MODEL, D_EXPERT), const),
                  pl.BlockSpec((D_EXPERT, D_MODEL), const),
                  pl.BlockSpec((1, D_MODEL), const), pl.BlockSpec((1, D_MODEL), const)],
        out_specs=pl.BlockSpec((tm, D_MODEL), lambda i: (i, 0)),
        compiler_params=pltpu.CompilerParams(dimension_semantics=("arbitrary",), vmem_limit_bytes=VMEM_LIMIT),
        name="combine",
    )(x1, yt, gate, wgs_bf, wus_bf, wds_bf, g2, b2)


def kernel(x_prompt, x_sample, state_hgrn, state_pool, meta_tokens, ln_emb_g, ln_emb_b, w_in, lb_logits, hgrn_norm_g, w_pool, pool_scale, w_out, ln1_g, ln1_b, w_router, b_router, w_gate_e, w_up_e, w_down_e, w_gate_s, w_up_s, w_down_s, ln2_g, ln2_b):
    nseq, seqlen, _ = x_prompt.shape
    ndec = x_sample.shape[0]
    l = 0
    row = lambda a: a.reshape(1, -1)
    w_in_bf = w_in[l].astype(BF16)
    wpool_bf = w_pool[l].astype(BF16)
    lng, lnb = row(ln_emb_g), row(ln_emb_b)
    ng, ps = row(hgrn_norm_g[l]), row(pool_scale[l])
    proj = functools.partial(_ln_proj, ln_g=lng, ln_b=lnb, w_in_bf=w_in_bf, lb_logits=lb_logits, norm_g=ng)

    m_xn, m_q, m_k, m_g, m_v, m_gs, m_p = proj(meta_tokens)
    pad = lambda a: jnp.pad(a, ((0, MIX_BLOCK - N_META), (0, 0)))
    zero_state = jnp.zeros((HGRN_HEADS, HEAD_DIM, HEAD_DIM), F32)
    _, s_meta = _mixer(pad(m_q), pad(m_k), pad(m_g), pad(m_v), pad(m_gs), pad(m_p), zero_state,
                       jnp.zeros((SUB, POOL_WIDTH), F32), wpool_bf, ps, 1, MIX_BLOCK)

    p_xn, p_q, p_k, p_g, p_v, p_gs, p_p = proj(x_prompt.reshape(nseq * seqlen, D_MODEL))
    p_mix, s_prompt = _mixer(p_q, p_k, p_g, p_v, p_gs, p_p, s_meta[0], m_p, wpool_bf, ps, nseq, seqlen)

    d_xn, d_q, d_k, d_g, d_v, d_gs, d_p = proj(x_sample.reshape(ndec, D_MODEL))
    cols = lambda a: a.reshape(ndec // STEP_SEQS, STEP_SEQS, HGRN_HEADS, HEAD_DIM).transpose(2, 0, 3, 1)
    s_dec, d_oa = _mixer_step(cols(d_q), cols(d_k), cols(d_g), d_v, d_gs, state_hgrn[l])
    d_ob = _pool_step(state_pool[l].transpose(1, 0, 2), d_p, wpool_bf, ps)
    d_mix = jnp.concatenate([d_oa, d_ob], axis=1).astype(BF16)

    n_tok = nseq * seqlen + ndec
    x1, xp, idx_t, gate, cnt = _outproj(p_mix, p_xn, d_mix, d_xn, w_out[l].astype(BF16), row(ln1_g[l]),
                                        row(ln1_b[l]), w_router[l].T.astype(BF16), b_router[l].reshape(-1, 1))
    plan = _route_plan(idx_t[:, :n_tok].T, jnp.sum(cnt, axis=1).astype(I32), n_tok)
    yt = _moe(*plan, xp, w_gate_e[l], w_up_e[l], w_down_e[l], n_tok)
    comb = functools.partial(_combine, x1, yt, gate, w_gate_s[l].astype(BF16), w_up_s[l].astype(BF16),
                             w_down_s[l].astype(BF16), row(ln2_g[l]), row(ln2_b[l]))
    y_prompt = comb(0, nseq * seqlen).reshape(nseq, seqlen, D_MODEL)
    y_sample = comb(nseq * seqlen, ndec).reshape(ndec, 1, D_MODEL)

    state_pool_prompt = p_p.reshape(nseq, seqlen, POOL_WIDTH)[:, seqlen - POOL_BUF:, :]
    state_pool_sample = jnp.concatenate([state_pool[l][:, 1:, :], d_p[:, None, :]], axis=1)
    return (y_prompt, y_sample, s_prompt[None], state_pool_prompt[None], s_dec[None], state_pool_sample[None])
```
